```python
import math
import jax, jax.numpy as jnp
from jax import lax
import numpy as np

D_MODEL = 2048
BATCH = 1
SEQ = 16384
DEPTH = 1
DEC_BATCH = 128
DEC_SEQ = 8
PAST_LEN = 16384
PAGE_SIZE = 128

HEAD_DIM = 128
MIX_WIDTH = D_MODEL
GDN_HEADS = (MIX_WIDTH // 2) // HEAD_DIM
GDN_WIDTH = GDN_HEADS * HEAD_DIM
SWA_HEADS = (MIX_WIDTH - GDN_WIDTH) // HEAD_DIM
SWA_WIDTH = SWA_HEADS * HEAD_DIM
SWA_KV_HEADS = 2
SWA_GROUP = SWA_HEADS // SWA_KV_HEADS
WINDOW = 128
BLOCK = 128
GDN_CONV = 4
GDN_CHUNK = 64
N_META = 16
N_BUCKETS = 32
MAX_DISTANCE = 128
FFN_CONV = 3
D_FF = 11 * D_MODEL // 4
EPS = 1e-6
GDN_QKV_DIM = 3 * GDN_WIDTH
PROJ_SIZES = (GDN_QKV_DIM, GDN_WIDTH, GDN_HEADS, GDN_HEADS, SWA_WIDTH, SWA_KV_HEADS * HEAD_DIM, SWA_KV_HEADS * HEAD_DIM)
PROJ_DIM = sum(PROJ_SIZES)
PROJ_SPLITS = [sum(PROJ_SIZES[:i + 1]) for i in range(len(PROJ_SIZES) - 1)]

kernel_name = "hymba_gdn_swa_sink_convffn_step"


def rmsnorm(x, w):
    x32 = x.astype(jnp.float32)
    y = x32 * lax.rsqrt(jnp.mean(x32 * x32, -1, keepdims=True) + EPS)
    return (y * w.astype(jnp.float32)).astype(x.dtype)


def l2norm(x):
    return x * lax.rsqrt(jnp.sum(x * x, -1, keepdims=True) + EPS)


def causal_dwconv(x, hist, w):
    width, t_len = w.shape[0], x.shape[1]
    xe = jnp.concatenate([hist.astype(x.dtype), x], axis=1)
    return sum(xe[:, i:i + t_len] * w[i].astype(x.dtype) for i in range(width))


def gdn_prepare(qkv, b, a, a_log, dt_bias):
    bsz, t_len, _ = qkv.shape
    q, k, v = jnp.split(qkv.astype(jnp.float32), 3, axis=-1)
    q = l2norm(q.reshape(bsz, t_len, GDN_HEADS, HEAD_DIM)) * HEAD_DIM ** -0.5
    k = l2norm(k.reshape(bsz, t_len, GDN_HEADS, HEAD_DIM))
    v = v.reshape(bsz, t_len, GDN_HEADS, HEAD_DIM)
    beta = jax.nn.sigmoid(b.astype(jnp.float32))
    g = -jnp.exp(a_log.astype(jnp.float32)) * jax.nn.softplus(a.astype(jnp.float32) + dt_bias.astype(jnp.float32))
    return q, k, v, g, beta


def gdn_chunked(q, k, v, g, beta, s0, chunk):
    bsz, t_len, n_h, _ = q.shape
    dv = v.shape[-1]
    n = t_len // chunk

    def blocks(t):
        t = jnp.moveaxis(t, 2, 1)
        return t.reshape(bsz, n_h, n, chunk, *t.shape[3:])

    q, k, v, g, beta = (blocks(t) for t in (q, k, v, g, beta))
    gc = jnp.cumsum(g, axis=-1)
    pos = jnp.arange(chunk)
    causal = pos[:, None] >= pos[None, :]
    strict = pos[:, None] > pos[None, :]
    decay = jnp.exp(jnp.where(causal, gc[..., :, None] - gc[..., None, :], -jnp.inf))
    kb = k * beta[..., None]
    lower = jnp.where(strict, jnp.einsum('bhncd,bhnsd->bhncs', kb, k) * decay, 0.0)
    rhs = jnp.concatenate([v * beta[..., None], kb * jnp.exp(gc)[..., None]], axis=-1)
    sol = lax.linalg.triangular_solve(jnp.eye(chunk, dtype=jnp.float32) + lower, rhs, left_side=True, lower=True)
    u, w = sol[..., :dv], sol[..., dv:]
    qk = jnp.einsum('bhncd,bhnsd->bhncs', q, k) * decay
    qg = q * jnp.exp(gc)[..., None]
    kd = k * jnp.exp(gc[..., -1:] - gc)[..., None]
    gl = jnp.exp(gc[..., -1])

    def step(s, xs):
        u_c, w_c, qk_c, qg_c, kd_c, gl_c = xs
        v_new = u_c - jnp.einsum('bhcd,bhde->bhce', w_c, s)
        o_c = jnp.einsum('bhcd,bhde->bhce', qg_c, s) + jnp.einsum('bhcs,bhse->bhce', qk_c, v_new)
        s = s * gl_c[..., None, None] + jnp.einsum('bhcd,bhce->bhde', kd_c, v_new)
        return s, o_c

    xs = tuple(jnp.moveaxis(t, 2, 0) for t in (u, w, qk, qg, kd, gl))
    s_fin, o = lax.scan(step, s0.astype(jnp.float32), xs)
    o = jnp.moveaxis(o, 0, 2).reshape(bsz, n_h, t_len, dv)
    return jnp.moveaxis(o, 1, 2), s_fin


def gated_rmsnorm(o, z, w):
    bsz, t_len = o.shape[:2]
    zz = z.astype(jnp.float32).reshape(bsz, t_len, GDN_HEADS, HEAD_DIM)
    y = o * lax.rsqrt(jnp.mean(o * o, -1, keepdims=True) + EPS) * w.astype(jnp.float32) * jax.nn.silu(zz)
    return y.reshape(bsz, t_len, GDN_WIDTH)


def t5_bucket(dist):
    n = jnp.maximum(dist, 0)
    exact = N_BUCKETS // 2
    large = exact + (jnp.log(jnp.maximum(n, 1).astype(jnp.float32) / exact)
                     / math.log(MAX_DISTANCE / exact) * (N_BUCKETS - exact)).astype(jnp.int32)
    return jnp.where(n < exact, n, jnp.minimum(large, N_BUCKETS - 1))


def swa_attend(q, k, v, qpos, kpos, is_meta, rel_table, sinks):
    bsz, nblk, nq, n_h, d = q.shape
    ns = k.shape[2]
    qg = q.astype(jnp.float32).reshape(bsz, nblk, nq, SWA_KV_HEADS, SWA_GROUP, d)
    s = jnp.einsum('bnqkgd,bnskd->bnkgqs', qg, k.astype(jnp.float32)) * d ** -0.5
    dist = qpos[:, :, None] - kpos[:, None, :]
    valid = (dist >= 0) & (is_meta[None, None, :] | ((dist < WINDOW) & (kpos[:, None, :] >= N_META)))
    bias = rel_table.astype(jnp.float32)[t5_bucket(dist)]
    bias = bias.reshape(nblk, nq, ns, SWA_KV_HEADS, SWA_GROUP).transpose(0, 3, 4, 1, 2)
    s = jnp.where(valid[:, None, None], s + bias, -jnp.inf)
    sink = sinks.astype(jnp.float32).reshape(SWA_KV_HEADS, SWA_GROUP)[None, None, :, :, None, None]
    m = jnp.maximum(jnp.max(s, -1, keepdims=True), sink)
    p = jnp.exp(s - m)
    p = p / (jnp.sum(p, -1, keepdims=True) + jnp.exp(sink - m))
    o = jnp.einsum('bnkgqs,bnskd->bnqkgd', p, v.astype(jnp.float32))
    return o.reshape(bsz, nblk, nq, n_h * d)


def swa_prompt(q, k, v, rel_table, sinks):
    bsz, seq_len, n_h, d = q.shape
    nb = -(-seq_len // BLOCK)
    lp = nb * BLOCK
    qb = jnp.pad(q, ((0, 0), (0, lp - seq_len), (0, 0), (0, 0))).reshape(bsz, nb, BLOCK, n_h, d)

    def with_meta(t):
        tp = jnp.pad(t, ((0, 0), (BLOCK, lp - seq_len), (0, 0), (0, 0))).reshape(bsz, nb + 1, BLOCK, SWA_KV_HEADS, d)
        band = jnp.concatenate([tp[:, :-1], tp[:, 1:]], axis=2)
        meta = jnp.broadcast_to(t[:, None, :N_META], (bsz, nb, N_META, SWA_KV_HEADS, d))
        return jnp.concatenate([meta, band], axis=2)

    qpos = jnp.arange(lp).reshape(nb, BLOCK)
    band_pos = (jnp.arange(nb)[:, None] - 1) * BLOCK + jnp.arange(2 * BLOCK)[None, :]
    kpos = jnp.concatenate([jnp.broadcast_to(jnp.arange(N_META)[None, :], (nb, N_META)), band_pos], axis=1)
    is_meta = jnp.arange(N_META + 2 * BLOCK) < N_META
    o = swa_attend(qb, with_meta(k), with_meta(v), qpos, kpos, is_meta, rel_table, sinks)
    o = o.reshape(bsz, lp, n_h * d)[:, :seq_len]
    new_meta = jnp.stack([k[:, :N_META], v[:, :N_META]], axis=2)
    new_window = jnp.stack([k[:, seq_len - WINDOW:], v[:, seq_len - WINDOW:]], axis=2)
    return o, (new_meta, new_window)


def swa_sample(q, k, v, meta_kv, window_kv, rel_table, sinks):
    t_len = q.shape[1]
    kv_new = jnp.stack([k, v], axis=2)
    kv_all = jnp.concatenate([meta_kv.astype(kv_new.dtype), window_kv.astype(kv_new.dtype), kv_new], axis=1)
    qpos = (PAST_LEN + jnp.arange(t_len))[None, :]
    kpos = jnp.concatenate([jnp.arange(N_META), PAST_LEN - WINDOW + jnp.arange(WINDOW), PAST_LEN + jnp.arange(t_len)])[None, :]
    is_meta = jnp.arange(N_META + WINDOW + t_len) < N_META
    o = swa_attend(q[:, None], kv_all[:, None, :, 0], kv_all[:, None, :, 1], qpos, kpos, is_meta, rel_table, sinks)
    new_window = jnp.concatenate([window_kv.astype(kv_new.dtype), kv_new], axis=1)[:, -WINDOW:]
    return o[:, 0], (new_window,)


def layer(h, gdn_hist, gdn_s0, ffn_hist, swa_cache, segments, lw, rel_table):
    (w_in, gdn_conv_w, a_log, dt_bias, gdn_norm_w, sinks, w_out,
     n_mix_pre, n_mix_post, n_ffn_pre, n_ffn_post, w_gate, w_up, ffn_conv_w, w_down) = lw
    bsz, t_len, _ = h.shape
    xn = rmsnorm(h, n_mix_pre)
    qkv_raw, z, b, a, sq, sk, sv = jnp.split(xn @ w_in, PROJ_SPLITS, axis=-1)
    qkv = jax.nn.silu(causal_dwconv(qkv_raw, gdn_hist, gdn_conv_w))
    q, k, v, g, beta = gdn_prepare(qkv, b, a, a_log, dt_bias)
    s = gdn_s0
    outs = []
    start = 0
    for length, chunk in segments:
        sl = slice(start, start + length)
        o_seg, s = gdn_chunked(q[:, sl], k[:, sl], v[:, sl], g[:, sl], beta[:, sl], s, chunk)
        outs.append(o_seg)
        start += length
    gdn_out = gated_rmsnorm(jnp.concatenate(outs, axis=1), z, gdn_norm_w).astype(h.dtype)
    new_gdn_conv = jnp.concatenate([gdn_hist.astype(qkv_raw.dtype), qkv_raw], axis=1)[:, -(GDN_CONV - 1):]
    sq = sq.reshape(bsz, t_len, SWA_HEADS, HEAD_DIM)
    sk = sk.reshape(bsz, t_len, SWA_KV_HEADS, HEAD_DIM)
    sv = sv.reshape(bsz, t_len, SWA_KV_HEADS, HEAD_DIM)
    if swa_cache is None:
        swa_out, swa_new = swa_prompt(sq, sk, sv, rel_table, sinks)
    else:
        swa_out, swa_new = swa_sample(sq, sk, sv, swa_cache[0], swa_cache[1], rel_table, sinks)
    mix = jnp.concatenate([gdn_out, swa_out.astype(h.dtype)], axis=-1) @ w_out
    h = h + rmsnorm(mix, n_mix_post)
    xn = rmsnorm(h, n_ffn_pre)
    gate_raw = xn @ w_gate
    gate = causal_dwconv(gate_raw, ffn_hist, ffn_conv_w)
    y = (jax.nn.silu(gate) * (xn @ w_up)) @ w_down
    h = h + rmsnorm(y, n_ffn_post)
    new_ffn_conv = jnp.concatenate([ffn_hist.astype(gate_raw.dtype), gate_raw], axis=1)[:, -(FFN_CONV - 1):]
    return h, swa_new, new_gdn_conv, s.astype(gdn_s0.dtype), new_ffn_conv


def setup_inputs(seed: int = 0) -> dict:
    key = jax.random.key(seed)
    ks = jax.random.split(key, 24)
    f32 = jnp.float32
    nrm = lambda k_, shape, scale=1.0: jax.random.normal(k_, shape, f32) * scale
    dt = jnp.exp(jax.random.uniform(ks[11], (DEPTH, GDN_HEADS), f32, math.log(1e-3), math.log(1e-1)))
    return {
        'x_prompt': nrm(ks[0], (BATCH, SEQ, D_MODEL)),
        'x_sample': nrm(ks[1], (DEC_BATCH, DEC_SEQ, D_MODEL)),
        'cache_swa_meta_kv': nrm(ks[2], (DEPTH, DEC_BATCH, N_META, 2, SWA_KV_HEADS, HEAD_DIM)),
        'cache_swa_window_kv': nrm(ks[3], (DEPTH, DEC_BATCH, WINDOW, 2, SWA_KV_HEADS, HEAD_DIM)),
        'state_gdn_conv': nrm(ks[4], (DEPTH, DEC_BATCH, GDN_CONV - 1, GDN_QKV_DIM)),
        'state_gdn': nrm(ks[5], (DEPTH, DEC_BATCH, GDN_HEADS, HEAD_DIM, HEAD_DIM), HEAD_DIM ** -0.5),
        'state_ffn_conv': nrm(ks[6], (DEPTH, DEC_BATCH, FFN_CONV - 1, D_FF)),
        'meta_tokens': nrm(ks[7], (N_META, D_MODEL)),
        'rel_bias_table': nrm(ks[8], (N_BUCKETS, SWA_HEADS), 0.5),
        'w_in': nrm(ks[9], (DEPTH, D_MODEL, PROJ_DIM), D_MODEL ** -0.5),
        'gdn_conv_w': nrm(ks[10], (DEPTH, GDN_CONV, GDN_QKV_DIM), GDN_CONV ** -0.5),
        'gdn_a_log': jnp.log(jax.random.uniform(ks[12], (DEPTH, GDN_HEADS), f32, 1.0, 16.0)),
        'gdn_dt_bias': dt + jnp.log(-jnp.expm1(-dt)),
        'gdn_norm_w': 1.0 + nrm(ks[13], (DEPTH, HEAD_DIM), 0.01),
        'swa_sinks': nrm(ks[14], (DEPTH, SWA_HEADS)),
        'w_out': nrm(ks[15], (DEPTH, MIX_WIDTH, D_MODEL), MIX_WIDTH ** -0.5),
        'norm_mix_pre': 1.0 + nrm(ks[16], (DEPTH, D_MODEL), 0.01),
        'norm_mix_post': 1.0 + nrm(ks[17], (DEPTH, D_MODEL), 0.01),
        'norm_ffn_pre': 1.0 + nrm(ks[18], (DEPTH, D_MODEL), 0.01),
        'norm_ffn_post': 1.0 + nrm(ks[19], (DEPTH, D_MODEL), 0.01),
        'ffn_w_gate': nrm(ks[20], (DEPTH, D_MODEL, D_FF), D_MODEL ** -0.5),
        'ffn_w_up': nrm(ks[21], (DEPTH, D_MODEL, D_FF), D_MODEL ** -0.5),
        'ffn_conv_w': nrm(ks[22], (DEPTH, FFN_CONV, D_FF), FFN_CONV ** -0.5),
        'ffn_w_down': nrm(ks[23], (DEPTH, D_FF, D_MODEL), D_FF ** -0.5),
    }


def reference(x_prompt, x_sample, cache_swa_meta_kv, cache_swa_window_kv, state_gdn_conv, state_gdn, state_ffn_conv,
              meta_tokens, rel_bias_table, w_in, gdn_conv_w, gdn_a_log, gdn_dt_bias, gdn_norm_w, swa_sinks, w_out,
              norm_mix_pre, norm_mix_post, norm_ffn_pre, norm_ffn_post, ffn_w_gate, ffn_w_up, ffn_conv_w, ffn_w_down):
    bsz = x_prompt.shape[0]
    hp = jnp.concatenate([jnp.broadcast_to(meta_tokens.astype(x_prompt.dtype)[None], (bsz, N_META, D_MODEL)), x_prompt], axis=1)
    hs = x_sample
    prompt_segments = ((N_META, N_META), (x_prompt.shape[1], GDN_CHUNK))
    sample_segments = ((x_sample.shape[1], x_sample.shape[1]),)
    p_states, s_states = [], []
    for l in range(DEPTH):
        lw = (w_in[l], gdn_conv_w[l], gdn_a_log[l], gdn_dt_bias[l], gdn_norm_w[l], swa_sinks[l], w_out[l],
              norm_mix_pre[l], norm_mix_post[l], norm_ffn_pre[l], norm_ffn_post[l],
              ffn_w_gate[l], ffn_w_up[l], ffn_conv_w[l], ffn_w_down[l])
        hp, p_swa, p_gc, p_gs, p_fc = layer(
            hp, jnp.zeros((bsz, GDN_CONV - 1, GDN_QKV_DIM), hp.dtype),
            jnp.zeros((bsz, GDN_HEADS, HEAD_DIM, HEAD_DIM), hp.dtype),
            jnp.zeros((bsz, FFN_CONV - 1, D_FF), hp.dtype), None, prompt_segments, lw, rel_bias_table)
        hs, s_swa, s_gc, s_gs, s_fc = layer(
            hs, state_gdn_conv[l], state_gdn[l], state_ffn_conv[l],
            (cache_swa_meta_kv[l], cache_swa_window_kv[l]), sample_segments, lw, rel_bias_table)
        p_states.append((p_swa[0], p_swa[1], p_gc, p_gs, p_fc))
        s_states.append((s_swa[0], s_gc, s_gs, s_fc))
    p_meta_kv, p_window_kv, p_gdn_conv, p_gdn_state, p_ffn_conv = [jnp.stack(t) for t in zip(*p_states)]
    s_window_kv, s_gdn_conv, s_gdn_state, s_ffn_conv = [jnp.stack(t) for t in zip(*s_states)]
    y_prompt = hp[:, N_META:]
    return (y_prompt, hs, p_meta_kv, p_window_kv, p_gdn_conv, p_gdn_state, p_ffn_conv,
            s_window_kv, s_gdn_conv, s_gdn_state, s_ffn_conv)
```

```python
import functools
import math

import numpy as np
import jax
import jax.numpy as jnp
from jax import lax
from jax.experimental import pallas as pl
from jax.experimental.pallas import tpu as pltpu

D_MODEL = 2048
HEAD_DIM = 128
GDN_HEADS = 8
GDN_WIDTH = GDN_HEADS * HEAD_DIM
GDN_QKV = 3 * GDN_WIDTH
SWA_HEADS = 8
SWA_KV_HEADS = 2
SWA_GROUP = SWA_HEADS // SWA_KV_HEADS
SWA_WIDTH = SWA_HEADS * HEAD_DIM
KV_WIDTH = 2 * SWA_KV_HEADS * HEAD_DIM
WINDOW = 128
N_META = 16
N_BUCKETS = 32
MAX_DISTANCE = 128
GDN_CONV = 4
FFN_CONV = 3
D_FF = 5632
EPS = 1e-6
PAST_LEN = 16384

SUBLANES = 8
LANES = 128

CHUNK = 128
INV_BASE = 16

COL_Z = GDN_QKV
COL_SQ = COL_Z + GDN_WIDTH
COL_KV = COL_SQ + SWA_WIDTH
COL_BA = COL_KV + KV_WIDTH
PROJ_COLS = COL_BA + LANES
GDN_COLS = COL_SQ

SMALL_KEYS = 32
NEG = -1e30
VMEM_LIMIT = 56 * 1024 * 1024

_NT = (((1,), (1,)), ((), ()))


def _dot(a, b):
    return jnp.dot(a.astype(jnp.bfloat16), b.astype(jnp.bfloat16), preferred_element_type=jnp.float32)


def _dot_nt(a, b):
    return lax.dot_general(a.astype(jnp.bfloat16), b.astype(jnp.bfloat16), _NT,
                           preferred_element_type=jnp.float32)


def _split(a):
    hi = a.astype(jnp.bfloat16)
    lo = (a - hi.astype(jnp.float32)).astype(jnp.bfloat16)
    return hi, lo


def _dot3(a, b):
    ah, al = _split(a)
    bh, bl = _split(b)
    d = functools.partial(jnp.dot, preferred_element_type=jnp.float32)
    return d(ah, bh) + (d(ah, bl) + d(al, bh))


def _dot_exact(a, b, dims=None):
    if dims is None:
        return jnp.dot(a, b, precision=lax.Precision.HIGHEST, preferred_element_type=jnp.float32)
    return lax.dot_general(a, b, dims, precision=lax.Precision.HIGHEST,
                           preferred_element_type=jnp.float32)


def _rms_scale(x):
    return lax.rsqrt(jnp.mean(x * x, axis=-1, keepdims=True) + EPS)


def _silu(x):
    return x * jax.nn.sigmoid(x)


def _inproj_kernel(x_ref, nw_ref, w_ref, o_ref, xn_ref, *, row_chunk):
    @pl.when(pl.program_id(1) == 0)
    def _():
        def body(c, carry):
            rows = pl.ds(pl.multiple_of(c * row_chunk, row_chunk), row_chunk)
            x = x_ref[rows, :]
            xn_ref[rows, :] = (x * _rms_scale(x) * nw_ref[...]).astype(jnp.bfloat16)
            return carry
        lax.fori_loop(0, x_ref.shape[0] // row_chunk, body, 0)

    o_ref[...] = jnp.dot(xn_ref[...], w_ref[...], preferred_element_type=jnp.float32)


def _inproj(x, nw, w, *, tm, tn, row_chunk):
    rows = x.shape[0]
    return pl.pallas_call(
        functools.partial(_inproj_kernel, row_chunk=row_chunk),
        grid=(rows // tm, PROJ_COLS // tn),
        in_specs=[
            pl.BlockSpec((tm, D_MODEL), lambda i, j: (i, 0)),
            pl.BlockSpec((1, D_MODEL), lambda i, j: (0, 0)),
            pl.BlockSpec((D_MODEL, tn), lambda i, j: (0, j)),
        ],
        out_specs=pl.BlockSpec((tm, tn), lambda i, j: (i, j)),
        out_shape=jax.ShapeDtypeStruct((rows, PROJ_COLS), jnp.float32),
        scratch_shapes=[pltpu.VMEM((tm, D_MODEL), jnp.bfloat16)],
        compiler_params=pltpu.CompilerParams(
            dimension_semantics=("arbitrary", "arbitrary"), vmem_limit_bytes=VMEM_LIMIT),
        name="inproj",
    )(x, nw, w)


def _tri_inverse(lm, ri, ci):
    shift = INV_BASE.bit_length() - 1
    eye = (ri == ci).astype(jnp.float32)
    ld = jnp.where((ri >> shift) == (ci >> shift), lm, 0.0)
    t = eye - ld
    p = ld
    for _ in range(shift - 1):
        p = _dot3(p, p)
        t = t + _dot3(t, p)
    size = INV_BASE
    while size < CHUNK:
        shift += 1
        in_pair = (ri >> shift) == (ci >> shift)
        in_block = (ri >> (shift - 1)) == (ci >> (shift - 1))
        off = jnp.where(in_pair & jnp.logical_not(in_block), lm, 0.0)
        t = t - _dot3(_dot3(t, off), t)
        size *= 2
    return t


def _gdn_kernel(x_ref, ba_ref, hist_ref, s0_ref, cw_ref, alog_ref, dtb_ref, gnw_ref,
                o_ref, sout_ref, xe_ref, s_ref, *, nb, seq, group, carry, pad_rows):
    step = pl.program_id(0)
    rows = nb * seq
    n_chunks = rows // CHUNK
    n_groups = CHUNK // group
    gshift = group.bit_length() - 1

    if carry:
        @pl.when(step == 0)
        def _():
            xe_ref[:, 0:SUBLANES, :] = hist_ref[...]
            s_ref[...] = s0_ref[0]
    else:
        xe_ref[:, 0:SUBLANES, :] = hist_ref[...]
    xe_ref[:, SUBLANES:SUBLANES + seq, :] = x_ref[:, :, 0:GDN_QKV]

    seq_rows = min(seq, CHUNK)
    seqs_per_chunk = CHUNK // seq_rows

    def chunk_rows(ref, c, row_off, cols):
        if seq >= CHUNK:
            start = row_off + c * CHUNK
            return ref[0:1, start:start + CHUNK, cols]
        b0 = c * seqs_per_chunk
        return ref[b0:b0 + seqs_per_chunk, row_off:row_off + seq, cols]

    def conv_chunk(col, c):
        cols = slice(col, col + HEAD_DIM)
        acc = None
        for s in range(GDN_CONV):
            term = chunk_rows(xe_ref, c, SUBLANES - s, cols) * cw_ref[GDN_CONV - 1 - s:GDN_CONV - s, cols]
            acc = term if acc is None else acc + term
        return _silu(acc).reshape(CHUNK, HEAD_DIM)

    ri = lax.broadcasted_iota(jnp.int32, (CHUNK, CHUNK), 0)
    ci = lax.broadcasted_iota(jnp.int32, (CHUNK, CHUNK), 1)
    same = (ri >> gshift) == (ci >> gshift)
    m_incl = same & (ri >= ci)
    m_strict = same & (ri > ci)
    f_incl = m_incl.astype(jnp.float32)
    f_same = same.astype(jnp.float32)
    lane = lax.broadcasted_iota(jnp.int32, (CHUNK, LANES), 1)
    row_in_chunk = lax.broadcasted_iota(jnp.int32, (CHUNK, LANES), 0)

    for c in range(n_chunks):
        r0 = c * CHUNK
        bac = chunk_rows(ba_ref, c, 0, slice(0, LANES)).reshape(CHUNK, LANES)
        beta_all = jax.nn.sigmoid(bac)
        sp_in = bac + dtb_ref[...]
        softplus = jnp.maximum(sp_in, 0.0) + jnp.log1p(jnp.exp(-jnp.abs(sp_in)))
        g_all = -jnp.exp(alog_ref[...]) * softplus
        if pad_rows and c == 0:
            valid = row_in_chunk >= pad_rows
            beta_all = jnp.where(valid, beta_all, 0.0)
            g_all = jnp.where(valid, g_all, 0.0)
        g_all = jnp.where((lane >= GDN_HEADS) & (lane < 2 * GDN_HEADS), g_all, 0.0)
        gc_col = _dot_exact(f_incl, g_all)
        gtot_col = _dot_exact(f_same, g_all)
        gc_row = _dot_exact(g_all.T, f_incl, _NT)

        for h in range(GDN_HEADS):
            q = conv_chunk(h * HEAD_DIM, c)
            k = conv_chunk(GDN_WIDTH + h * HEAD_DIM, c)
            v = conv_chunk(2 * GDN_WIDTH + h * HEAD_DIM, c)
            q = q * lax.rsqrt(jnp.sum(q * q, -1, keepdims=True) + EPS) * (HEAD_DIM ** -0.5)
            k = k * lax.rsqrt(jnp.sum(k * k, -1, keepdims=True) + EPS)

            gcc = gc_col[:, GDN_HEADS + h:GDN_HEADS + h + 1]
            gcr = gc_row[GDN_HEADS + h:GDN_HEADS + h + 1, :]
            gtc = gtot_col[:, GDN_HEADS + h:GDN_HEADS + h + 1]
            beta = beta_all[:, h:h + 1]
            decay = jnp.exp(jnp.where(m_incl, gcc - gcr, NEG))
            kb = k * beta
            lm = jnp.where(m_strict, _dot_nt(kb, k) * decay, 0.0)
            qk = _dot_nt(q, k) * decay
            egc = jnp.exp(gcc)
            rhs = jnp.concatenate([v * beta, kb * egc], axis=1)
            sol = _dot3(_tri_inverse(lm, ri, ci), rhs)
            u = sol[:, :HEAD_DIM]
            w = sol[:, HEAD_DIM:]
            qg = q * egc
            kd_t = (k * jnp.exp(gtc - gcc)).T
            gl = jnp.exp(gtc)

            state = lambda b: s_ref[h] if carry else s0_ref[c * n_groups + b, h]
            ws_parts, qs_parts = [], []
            for b in range(n_groups):
                g0 = b * group
                wq = jnp.concatenate([w[g0:g0 + group], qg[g0:g0 + group]], axis=0)
                res = _dot(wq, state(b))
                ws_parts.append(res[:group])
                qs_parts.append(res[group:])
            ws = ws_parts[0] if n_groups == 1 else jnp.concatenate(ws_parts, axis=0)
            qs = qs_parts[0] if n_groups == 1 else jnp.concatenate(qs_parts, axis=0)
            v_new = u - ws
            o = qs + _dot(qk, v_new)
            for b in range(n_groups):
                g0 = b * group
                kd_b = kd_t if n_groups == 1 else jnp.where((ci >> gshift) == b, kd_t, 0.0)
                st = state(b) * gl[g0:g0 + 1, :] + _dot(kd_b, v_new)
                if carry:
                    s_ref[h] = st
                else:
                    sout_ref[c * n_groups + b, h] = st

            z = chunk_rows(x_ref, c, 0, slice(COL_Z + h * HEAD_DIM, COL_Z + (h + 1) * HEAD_DIM)).reshape(CHUNK, HEAD_DIM)
            y = o * lax.rsqrt(jnp.mean(o * o, -1, keepdims=True) + EPS) * gnw_ref[...] * _silu(z)
            o_ref[r0:r0 + CHUNK, h * HEAD_DIM:(h + 1) * HEAD_DIM] = y.astype(o_ref.dtype)

    if carry:
        xe_ref[:, 0:SUBLANES, :] = xe_ref[:, seq:seq + SUBLANES, :]

        @pl.when(step == pl.num_programs(0) - 1)
        def _():
            sout_ref[0] = s_ref[...]


def _gdn(x3, x_idx, ba_idx, hist, hist_idx, s0, cw, alog_row, dtb_row, gnw, *,
         n_steps, nb, seq, group, carry, pad_rows):
    rows = nb * seq
    if carry:
        state_spec = pl.BlockSpec((1, GDN_HEADS, HEAD_DIM, HEAD_DIM), lambda s: (0, 0, 0, 0))
        state_shape = (1, GDN_HEADS, HEAD_DIM, HEAD_DIM)
    else:
        n_states = rows // group
        state_spec = pl.BlockSpec((n_states, GDN_HEADS, HEAD_DIM, HEAD_DIM), lambda s: (s, 0, 0, 0))
        state_shape = (n_steps * n_states, GDN_HEADS, HEAD_DIM, HEAD_DIM)
    full = lambda shape: pl.BlockSpec(shape, lambda s: (0,) * len(shape))
    return pl.pallas_call(
        functools.partial(_gdn_kernel, nb=nb, seq=seq, group=group, carry=carry, pad_rows=pad_rows),
        grid=(n_steps,),
        in_specs=[
            pl.BlockSpec((nb, seq, GDN_COLS), x_idx),
            pl.BlockSpec((nb, seq, LANES), ba_idx),
            pl.BlockSpec((nb, SUBLANES, GDN_QKV), hist_idx),
            state_spec,
            full((GDN_CONV, GDN_QKV)),
            full((1, LANES)),
            full((1, LANES)),
            full((1, HEAD_DIM)),
        ],
        out_specs=[
            pl.BlockSpec((rows, GDN_WIDTH), lambda s: (s, 0)),
            state_spec,
        ],
        out_shape=[
            jax.ShapeDtypeStruct((n_steps * rows, GDN_WIDTH), jnp.bfloat16),
            jax.ShapeDtypeStruct(state_shape, jnp.float32),
        ],
        scratch_shapes=[
            pltpu.VMEM((nb, SUBLANES + seq, GDN_QKV), jnp.float32),
            pltpu.VMEM((GDN_HEADS, HEAD_DIM, HEAD_DIM), jnp.float32),
        ],
        compiler_params=pltpu.CompilerParams(
            dimension_semantics=("arbitrary",), vmem_limit_bytes=VMEM_LIMIT),
        name="gdn_seq" if carry else "gdn_batch",
    )(x3, x3, hist, s0, cw, alog_row, dtb_row, gnw)


def _t5_bucket_np(dist):
    n = np.maximum(dist, 0)
    exact = N_BUCKETS // 2
    large = exact + (np.log(np.maximum(n, 1).astype(np.float32) / exact)
                     / math.log(MAX_DISTANCE / exact) * (N_BUCKETS - exact)).astype(np.int32)
    return np.where(n < exact, n, np.minimum(large, N_BUCKETS - 1)).astype(np.int32)


def _bucket_ids(dist, valid):
    return np.where(valid, _t5_bucket_np(dist), -1).astype(np.int32)


def _bias_kernel(table_ref, *refs):
    n = len(refs) // 2
    for ids_ref, out_ref in zip(refs[:n], refs[n:]):
        ids = ids_ref[...]
        nq = ids.shape[0]
        for head in range(SWA_HEADS):
            def body(b, acc):
                return jnp.where(ids == b, table_ref[b, head], acc)
            acc = lax.fori_loop(0, N_BUCKETS, body, jnp.full(ids.shape, NEG, jnp.float32))
            kh, g = divmod(head, SWA_GROUP)
            out_ref[kh, g * nq:(g + 1) * nq, :] = acc


def _bias_tables(rel_table, id_arrays):
    out_shapes = [jax.ShapeDtypeStruct((SWA_KV_HEADS, SWA_GROUP * a.shape[0], a.shape[1]), jnp.float32)
                  for a in id_arrays]
    vmem = pl.BlockSpec(memory_space=pltpu.VMEM)
    return pl.pallas_call(
        _bias_kernel,
        in_specs=[pl.BlockSpec(memory_space=pltpu.SMEM)] + [vmem] * len(id_arrays),
        out_specs=[vmem] * len(id_arrays),
        out_shape=out_shapes,
        name="swa_bias",
    )(rel_table, *[jnp.asarray(a) for a in id_arrays])


def _attend(q, keys, values, biases, sink):
    scale = HEAD_DIM ** -0.5
    scores = [_dot_nt(q, k) * scale + b for k, b in zip(keys, biases)]
    m = sink
    for s in scores:
        m = jnp.maximum(m, jnp.max(s, axis=-1, keepdims=True))
    den = jnp.exp(sink - m)
    acc = None
    for s, v in zip(scores, values):
        p = jnp.exp(s - m)
        den = den + jnp.sum(p, axis=-1, keepdims=True)
        pv = _dot(p, v)
        acc = pv if acc is None else acc + pv
    return acc / den


def _swa_prompt_kernel(q_ref, kvc_ref, kvp_ref, kvm_ref, bcur_ref, bprev_ref, bm0_ref, bfar_ref,
                       sink_ref, o_ref):
    first = pl.program_id(0) == 0
    for kh in range(SWA_KV_HEADS):
        ks = slice(kh * HEAD_DIM, (kh + 1) * HEAD_DIM)
        vs = slice((SWA_KV_HEADS + kh) * HEAD_DIM, (SWA_KV_HEADS + kh + 1) * HEAD_DIM)
        q = jnp.concatenate(
            [q_ref[:, (kh * SWA_GROUP + g) * HEAD_DIM:(kh * SWA_GROUP + g + 1) * HEAD_DIM]
             for g in range(SWA_GROUP)], axis=0)
        b_prev = jnp.where(first, NEG, bprev_ref[kh])
        b_meta = jnp.where(first, bm0_ref[kh], bfar_ref[kh])
        o = _attend(q,
                    [kvc_ref[:, ks], kvp_ref[:, ks], kvm_ref[:, ks]],
                    [kvc_ref[:, vs], kvp_ref[:, vs], kvm_ref[:, vs]],
                    [bcur_ref[kh], b_prev, b_meta], sink_ref[kh])
        for g in range(SWA_GROUP):
            head = kh * SWA_GROUP + g
            o_ref[:, head * HEAD_DIM:(head + 1) * HEAD_DIM] = (
                o[g * WINDOW:(g + 1) * WINDOW].astype(o_ref.dtype))


def _swa_meta_kernel(q_ref, kv_ref, bias_ref, sink_ref, o_ref):
    for kh in range(SWA_KV_HEADS):
        ks = slice(kh * HEAD_DIM, (kh + 1) * HEAD_DIM)
        vs = slice((SWA_KV_HEADS + kh) * HEAD_DIM, (SWA_KV_HEADS + kh + 1) * HEAD_DIM)
        q = jnp.concatenate(
            [q_ref[:, (kh * SWA_GROUP + g) * HEAD_DIM:(kh * SWA_GROUP + g + 1) * HEAD_DIM]
             for g in range(SWA_GROUP)], axis=0)
        o = _attend(q, [kv_ref[:, ks]], [kv_ref[:, vs]], [bias_ref[kh]], sink_ref[kh])
        for g in range(SWA_GROUP):
            head = kh * SWA_GROUP + g
            o_ref[:, head * HEAD_DIM:(head + 1) * HEAD_DIM] = (
                o[g * N_META:(g + 1) * N_META].astype(o_ref.dtype))


def _swa_sample_kernel(q_ref, kvn_ref, win_ref, meta_ref, bwin_ref, bsmall_ref, sink_ref, o_ref, *, nb, seq):
    pad = jnp.zeros((SMALL_KEYS - N_META - seq, HEAD_DIM), jnp.float32)
    outs = [[] for _ in range(SWA_HEADS)]
    for b in range(nb):
        for kh in range(SWA_KV_HEADS):
            ks = slice(kh * HEAD_DIM, (kh + 1) * HEAD_DIM)
            vs = slice((SWA_KV_HEADS + kh) * HEAD_DIM, (SWA_KV_HEADS + kh + 1) * HEAD_DIM)
            q = jnp.concatenate(
                [q_ref[b, :, (kh * SWA_GROUP + g) * HEAD_DIM:(kh * SWA_GROUP + g + 1) * HEAD_DIM]
                 for g in range(SWA_GROUP)], axis=0)
            k_small = jnp.concatenate([meta_ref[b, :, ks], kvn_ref[b, :, ks], pad], axis=0)
            v_small = jnp.concatenate([meta_ref[b, :, vs], kvn_ref[b, :, vs], pad], axis=0)
            o = _attend(q, [win_ref[b, :, ks], k_small], [win_ref[b, :, vs], v_small],
                        [bwin_ref[kh], bsmall_ref[kh]], sink_ref[kh])
            for g in range(SWA_GROUP):
                outs[kh * SWA_GROUP + g].append(o[g * seq:(g + 1) * seq])
    for head in range(SWA_HEADS):
        o_ref[:, head * HEAD_DIM:(head + 1) * HEAD_DIM] = (
            jnp.concatenate(outs[head], axis=0).astype(o_ref.dtype))


def _outproj_kernel(g_ref, s_ref, h_ref, wo_ref, nw_ref, o_ref):
    mix = (jnp.dot(g_ref[...], wo_ref[0:GDN_WIDTH, :], preferred_element_type=jnp.float32)
           + jnp.dot(s_ref[...], wo_ref[GDN_WIDTH:, :], preferred_element_type=jnp.float32))
    o_ref[...] = h_ref[...] + mix * _rms_scale(mix) * nw_ref[...]


def _outproj(g, s, h, wo, nw, *, tm):
    rows = h.shape[0]
    return pl.pallas_call(
        _outproj_kernel,
        grid=(rows // tm,),
        in_specs=[
            pl.BlockSpec((tm, GDN_WIDTH), lambda i: (i, 0)),
            pl.BlockSpec((tm, SWA_WIDTH), lambda i: (i, 0)),
            pl.BlockSpec((tm, D_MODEL), lambda i: (i, 0)),
            pl.BlockSpec((D_MODEL, D_MODEL), lambda i: (0, 0)),
            pl.BlockSpec((1, D_MODEL), lambda i: (0, 0)),
        ],
        out_specs=pl.BlockSpec((tm, D_MODEL), lambda i: (i, 0)),
        out_shape=jax.ShapeDtypeStruct((rows, D_MODEL), jnp.float32),
        compiler_params=pltpu.CompilerParams(
            dimension_semantics=("arbitrary",), vmem_limit_bytes=VMEM_LIMIT),
        name="outproj",
    )(g, s, h, wo, nw)


def _ffn_kernel(*refs, batch, tm, tf):
    if batch:
        (h_ref, nw_pre_ref, wg_ref, wu_ref, cw_ref, wd_ref, nw_post_ref, hist_ref,
         y_ref, graw_ref, xn_ref, acc_ref, xe_ref) = refs
    else:
        (h_ref, nw_pre_ref, wg_ref, wu_ref, cw_ref, wd_ref, nw_post_ref, hist_ref,
         y_ref, graw_ref, xn_ref, acc_ref, xe_ref, carry_ref) = refs
    i = pl.program_id(0)
    j = pl.program_id(1)
    seq = SUBLANES if batch else tm

    @pl.when(j == 0)
    def _():
        h = h_ref[...]
        xn_ref[...] = (h * _rms_scale(h) * nw_pre_ref[...]).astype(jnp.bfloat16)

    xn = xn_ref[...]
    gate = jnp.dot(xn, wg_ref[...], preferred_element_type=jnp.float32)
    up = jnp.dot(xn, wu_ref[...], preferred_element_type=jnp.float32)

    if batch:
        xe_ref[:, 0:SUBLANES, :] = hist_ref[...]
        graw_ref[...] = gate
    else:
        @pl.when(i == 0)
        def _():
            xe_ref[:, 0:SUBLANES, :] = hist_ref[...].reshape(1, SUBLANES, tf)

        @pl.when(i > 0)
        def _():
            xe_ref[:, 0:SUBLANES, :] = carry_ref[pl.ds(j, 1)]
        carry_ref[pl.ds(j, 1)] = gate[tm - SUBLANES:].reshape(1, SUBLANES, tf)
        graw_ref[...] = gate[tm - SUBLANES:]
    xe_ref[:, SUBLANES:SUBLANES + seq, :] = gate.reshape(xe_ref.shape[0], seq, tf)
    conv = None
    for s in range(FFN_CONV):
        term = xe_ref[:, SUBLANES - s:SUBLANES - s + seq, :] * cw_ref[FFN_CONV - 1 - s:FFN_CONV - s, :]
        conv = term if conv is None else conv + term
    act = (_silu(conv.reshape(tm, tf)) * up).astype(jnp.bfloat16)
    contrib = jnp.dot(act, wd_ref[...], preferred_element_type=jnp.float32)

    @pl.when(j == 0)
    def _():
        acc_ref[...] = contrib

    @pl.when(j > 0)
    def _():
        acc_ref[...] += contrib

    @pl.when(j == pl.num_programs(1) - 1)
    def _():
        y = acc_ref[...]
        y_ref[...] = h_ref[...] + y * _rms_scale(y) * nw_post_ref[...]


def _ffn(h, nw_pre, wg, wu, cw, wd, nw_post, hist, *, batch, tm, tf):
    rows = h.shape[0]
    nj = D_FF // tf
    in_specs = [
        pl.BlockSpec((tm, D_MODEL), lambda i, j: (i, 0)),
        pl.BlockSpec((1, D_MODEL), lambda i, j: (0, 0)),
        pl.BlockSpec((D_MODEL, tf), lambda i, j: (0, j)),
        pl.BlockSpec((D_MODEL, tf), lambda i, j: (0, j)),
        pl.BlockSpec((FFN_CONV, tf), lambda i, j: (0, j)),
        pl.BlockSpec((tf, D_MODEL), lambda i, j: (j, 0)),
        pl.BlockSpec((1, D_MODEL), lambda i, j: (0, 0)),
    ]
    args = [h, nw_pre, wg, wu, cw, wd, nw_post, hist]
    scratch = [pltpu.VMEM((tm, D_MODEL), jnp.bfloat16), pltpu.VMEM((tm, D_MODEL), jnp.float32)]
    if batch:
        in_specs.append(pl.BlockSpec((tm // SUBLANES, SUBLANES, tf), lambda i, j: (i, 0, j)))
        graw_spec = pl.BlockSpec((tm, tf), lambda i, j: (i, j))
        graw_shape = jax.ShapeDtypeStruct((rows, D_FF), jnp.float32)
        scratch.append(pltpu.VMEM((tm // SUBLANES, 2 * SUBLANES, tf), jnp.float32))
    else:
        in_specs.append(pl.BlockSpec((SUBLANES, tf), lambda i, j: (0, j)))
        graw_spec = pl.BlockSpec((SUBLANES, tf), lambda i, j: (i, j))
        graw_shape = jax.ShapeDtypeStruct((rows // tm * SUBLANES, D_FF), jnp.float32)
        scratch.append(pltpu.VMEM((1, SUBLANES + tm, tf), jnp.float32))
        scratch.append(pltpu.VMEM((nj, SUBLANES, tf), jnp.float32))
    return pl.pallas_call(
        functools.partial(_ffn_kernel, batch=batch, tm=tm, tf=tf),
        grid=(rows // tm, nj),
        in_specs=in_specs,
        out_specs=[pl.BlockSpec((tm, D_MODEL), lambda i, j: (i, 0)), graw_spec],
        out_shape=[jax.ShapeDtypeStruct((rows, D_MODEL), jnp.float32), graw_shape],
        scratch_shapes=scratch,
        compiler_params=pltpu.CompilerParams(
            dimension_semantics=("arbitrary", "arbitrary"), vmem_limit_bytes=VMEM_LIMIT),
        name="ffn_batch" if batch else "ffn_seq",
    )(*args)


def kernel(x_prompt, x_sample, cache_swa_meta_kv, cache_swa_window_kv, state_gdn_conv, state_gdn, state_ffn_conv, meta_tokens, rel_bias_table, w_in, gdn_conv_w, gdn_a_log, gdn_dt_bias, gdn_norm_w, swa_sinks, w_out, norm_mix_pre, norm_mix_post, norm_ffn_pre, norm_ffn_post, ffn_w_gate, ffn_w_up, ffn_conv_w, ffn_w_down):
    f32, bf16 = jnp.float32, jnp.bfloat16
    seq = x_prompt.shape[1]
    dec_b, dec_t = x_sample.shape[0], x_sample.shape[1]
    n_dec = dec_b * dec_t
    assert x_prompt.shape[0] == 1 and seq % CHUNK == 0 and dec_t == SUBLANES and n_dec % CHUNK == 0

    wi = w_in[0]
    n_ba = 2 * GDN_HEADS
    w_in_p = jnp.concatenate(
        [wi[:, :COL_SQ], wi[:, COL_SQ + n_ba:], wi[:, COL_SQ:COL_SQ + n_ba],
         jnp.zeros((D_MODEL, LANES - n_ba), f32)], axis=1).astype(bf16)
    wo = w_out[0].astype(bf16)
    wg = ffn_w_gate[0].astype(bf16)
    wu = ffn_w_up[0].astype(bf16)
    wd = ffn_w_down[0].astype(bf16)
    lane_pad = lambda v: jnp.pad(v.reshape(1, GDN_HEADS), ((0, 0), (GDN_HEADS, LANES - 2 * GDN_HEADS)))
    alog_row = lane_pad(gdn_a_log[0])
    dtb_row = lane_pad(gdn_dt_bias[0])
    gnw = gdn_norm_w[0].reshape(1, HEAD_DIM)

    pad_rows = CHUNK - N_META
    n_small = n_dec + CHUNK
    x_big = x_prompt.reshape(seq, D_MODEL)
    x_small = jnp.concatenate(
        [x_sample.reshape(n_dec, D_MODEL), jnp.zeros((pad_rows, D_MODEL), f32), meta_tokens.astype(f32)], axis=0)
    nw = norm_mix_pre[0].reshape(1, D_MODEL)
    proj_big = _inproj(x_big, nw, w_in_p, tm=1024, tn=640, row_chunk=128)
    proj_small = _inproj(x_small, nw, w_in_p, tm=n_small, tn=640, row_chunk=128)

    cw = gdn_conv_w[0]
    small_chunks = proj_small.reshape(n_small // CHUNK, CHUNK, PROJ_COLS)
    small_groups = proj_small.reshape(n_small // SUBLANES, SUBLANES, PROJ_COLS)
    last_chunk = n_small // CHUNK - 1
    ba_blk = COL_BA // LANES
    gdn_meta, s_meta = _gdn(
        small_chunks, lambda s: (last_chunk, 0, 0), lambda s: (last_chunk, 0, ba_blk),
        jnp.zeros((1, SUBLANES, GDN_QKV), f32), lambda s: (0, 0, 0),
        jnp.zeros((1, GDN_HEADS, HEAD_DIM, HEAD_DIM), f32), cw, alog_row, dtb_row, gnw,
        n_steps=1, nb=1, seq=CHUNK, group=CHUNK, carry=True, pad_rows=pad_rows)
    last_group = n_small // SUBLANES - 1
    gdn_big, s_prompt = _gdn(
        proj_big.reshape(1, seq, PROJ_COLS), lambda s: (0, s, 0), lambda s: (0, s, ba_blk),
        small_groups, lambda s: (last_group, 0, 0),
        s_meta, cw, alog_row, dtb_row, gnw,
        n_steps=seq // CHUNK, nb=1, seq=CHUNK, group=CHUNK, carry=True, pad_rows=0)
    hist_gdn = jnp.pad(state_gdn_conv[0], ((0, 0), (SUBLANES - (GDN_CONV - 1), 0), (0, 0)))
    nb_gdn = CHUNK // dec_t
    gdn_small, s_sample = _gdn(
        small_groups, lambda s: (s, 0, 0), lambda s: (s, 0, ba_blk),
        hist_gdn, lambda s: (s, 0, 0),
        state_gdn[0], cw, alog_row, dtb_row, gnw,
        n_steps=dec_b // nb_gdn, nb=nb_gdn, seq=dec_t, group=dec_t, carry=False, pad_rows=0)

    qi = np.arange(WINDOW)[:, None]
    kj = np.arange(WINDOW)[None, :]
    mi = np.arange(N_META)[None, :]
    ti = np.arange(dec_t)[:, None]
    small_keys = np.arange(SMALL_KEYS)[None, :]
    new_keys = small_keys - N_META
    id_arrays = [
        _bucket_ids(qi - kj, qi >= kj),
        _bucket_ids(qi - kj + WINDOW, kj > qi),
        _bucket_ids(qi + N_META - mi, np.ones((WINDOW, N_META), bool)),
        _bucket_ids(qi + N_META - mi + WINDOW, np.ones((WINDOW, N_META), bool)),
        _bucket_ids(ti + WINDOW - kj, kj > ti),
        _bucket_ids(np.where(new_keys < 0, PAST_LEN + ti - small_keys, ti - new_keys),
                    (new_keys < 0) | ((new_keys <= ti) & (new_keys < dec_t))),
        _bucket_ids(mi.T - mi, mi.T >= mi),
    ]
    bcur, bprev, bm0, bfar, bwin, bsmall, bmm = _bias_tables(rel_bias_table, id_arrays)
    sink_rows = lambda q: jnp.repeat(swa_sinks[0].reshape(SWA_KV_HEADS, SWA_GROUP), q, axis=1)[..., None]

    sq_blk = COL_SQ // SWA_WIDTH
    kv_blk = COL_KV // KV_WIDTH
    meta_blk = (n_small - N_META) // N_META
    full3 = lambda a: pl.BlockSpec(a.shape, lambda j: (0, 0, 0))
    sink_p = sink_rows(WINDOW)
    swa_big = pl.pallas_call(
        _swa_prompt_kernel,
        grid=(seq // WINDOW,),
        in_specs=[
            pl.BlockSpec((WINDOW, SWA_WIDTH), lambda j: (j, sq_blk)),
            pl.BlockSpec((WINDOW, KV_WIDTH), lambda j: (j, kv_blk)),
            pl.BlockSpec((WINDOW, KV_WIDTH), lambda j: (jnp.maximum(j - 1, 0), kv_blk)),
            pl.BlockSpec((N_META, KV_WIDTH), lambda j: (meta_blk, kv_blk)),
            full3(bcur), full3(bprev), full3(bm0), full3(bfar), full3(sink_p),
        ],
        out_specs=pl.BlockSpec((WINDOW, SWA_WIDTH), lambda j: (j, 0)),
        out_shape=jax.ShapeDtypeStruct((seq, SWA_WIDTH), bf16),
        compiler_params=pltpu.CompilerParams(
            dimension_semantics=("arbitrary",), vmem_limit_bytes=VMEM_LIMIT),
        name="swa_prompt",
    )(proj_big, proj_big, proj_big, proj_small, bcur, bprev, bm0, bfar, sink_p)

    nb_swa = 8
    sink_s = sink_rows(dec_t)
    win = cache_swa_window_kv[0].reshape(dec_b, WINDOW, KV_WIDTH)
    meta_kv = cache_swa_meta_kv[0].reshape(dec_b, N_META, KV_WIDTH)
    swa_small = pl.pallas_call(
        functools.partial(_swa_sample_kernel, nb=nb_swa, seq=dec_t),
        grid=(dec_b // nb_swa,),
        in_specs=[
            pl.BlockSpec((nb_swa, dec_t, SWA_WIDTH), lambda j: (j, 0, sq_blk)),
            pl.BlockSpec((nb_swa, dec_t, KV_WIDTH), lambda j: (j, 0, kv_blk)),
            pl.BlockSpec((nb_swa, WINDOW, KV_WIDTH), lambda j: (j, 0, 0)),
            pl.BlockSpec((nb_swa, N_META, KV_WIDTH), lambda j: (j, 0, 0)),
            full3(bwin), full3(bsmall), full3(sink_s),
        ],
        out_specs=pl.BlockSpec((nb_swa * dec_t, SWA_WIDTH), lambda j: (j, 0)),
        out_shape=jax.ShapeDtypeStruct((n_dec, SWA_WIDTH), bf16),
        compiler_params=pltpu.CompilerParams(
            dimension_semantics=("arbitrary",), vmem_limit_bytes=VMEM_LIMIT),
        name="swa_sample",
    )(small_groups, small_groups, win, meta_kv, bwin, bsmall, sink_s)

    sink_m = sink_rows(N_META)
    swa_meta = pl.pallas_call(
        _swa_meta_kernel,
        grid=(1,),
        in_specs=[
            pl.BlockSpec((N_META, SWA_WIDTH), lambda j: (meta_blk, sq_blk)),
            pl.BlockSpec((N_META, KV_WIDTH), lambda j: (meta_blk, kv_blk)),
            full3(bmm), full3(sink_m),
        ],
        out_specs=pl.BlockSpec((N_META, SWA_WIDTH), lambda j: (0, 0)),
        out_shape=jax.ShapeDtypeStruct((N_META, SWA_WIDTH), bf16),
        name="swa_meta",
    )(proj_small, proj_small, bmm, sink_m)

    nw_post = norm_mix_post[0].reshape(1, D_MODEL)
    nf_pre = norm_ffn_pre[0].reshape(1, D_MODEL)
    nf_post = norm_ffn_post[0].reshape(1, D_MODEL)
    fcw = ffn_conv_w[0]
    gdn_small_all = jnp.concatenate([gdn_small, gdn_meta], axis=0)
    swa_small_all = jnp.concatenate([swa_small, jnp.zeros((pad_rows, SWA_WIDTH), bf16), swa_meta], axis=0)
    h_small = _outproj(gdn_small_all, swa_small_all, x_small, wo, nw_post, tm=n_small // 2)
    hist_ffn = jnp.pad(state_ffn_conv[0], ((0, CHUNK // SUBLANES), (SUBLANES - (FFN_CONV - 1), 0), (0, 0)))
    y_small, g_small = _ffn(h_small, nf_pre, wg, wu, fcw, wd, nf_post, hist_ffn,
                            batch=True, tm=n_small // 2, tf=512)
    h_big = _outproj(gdn_big, swa_big, x_big, wo, nw_post, tm=512)
    y_big, g_tail = _ffn(h_big, nf_pre, wg, wu, fcw, wd, nf_post, g_small[n_small - SUBLANES:],
                         batch=False, tm=512, tf=512)

    kv_shape = lambda n: (1, n, 2, SWA_KV_HEADS, HEAD_DIM)
    kv_small = proj_small[:, COL_KV:COL_KV + KV_WIDTH]
    y_prompt = y_big.reshape(1, seq, D_MODEL)
    y_sample = y_small[:n_dec].reshape(dec_b, dec_t, D_MODEL)
    p_meta_kv = kv_small[n_small - N_META:].reshape(kv_shape(N_META))[None]
    p_window_kv = proj_big[seq - WINDOW:, COL_KV:COL_KV + KV_WIDTH].reshape(kv_shape(WINDOW))[None]
    p_gdn_conv = proj_big[seq - (GDN_CONV - 1):, :GDN_QKV].reshape(1, 1, GDN_CONV - 1, GDN_QKV)
    p_gdn_state = s_prompt[None]
    p_ffn_conv = g_tail[g_tail.shape[0] - (FFN_CONV - 1):].reshape(1, 1, FFN_CONV - 1, D_FF)
    kv_new = kv_small[:n_dec].reshape(dec_b, dec_t, 2, SWA_KV_HEADS, HEAD_DIM)
    s_window_kv = jnp.concatenate([cache_swa_window_kv[0][:, dec_t:], kv_new], axis=1)[None]
    s_gdn_conv = proj_small[:n_dec, :GDN_QKV].reshape(dec_b, dec_t, GDN_QKV)[:, dec_t - (GDN_CONV - 1):][None]
    s_gdn_state = s_sample[None]
    s_ffn_conv = g_small[:n_dec].reshape(dec_b, dec_t, D_FF)[:, dec_t - (FFN_CONV - 1):][None]
    return (y_prompt, y_sample, p_meta_kv, p_window_kv, p_gdn_conv, p_gdn_state, p_ffn_conv,
            s_window_kv, s_gdn_conv, s_gdn_state, s_ffn_conv)
```

```python
import functools
import math

import numpy as np
import jax
import jax.numpy as jnp
from jax import lax
from jax.experimental import pallas as pl
from jax.experimental.pallas import tpu as pltpu

D_MODEL = 2048
HEAD_DIM = 128
GDN_HEADS = 8
GDN_WIDTH = GDN_HEADS * HEAD_DIM
GDN_QKV = 3 * GDN_WIDTH
SWA_HEADS = 8
SWA_KV_HEADS = 2
SWA_GROUP = SWA_HEADS // SWA_KV_HEADS
SWA_WIDTH = SWA_HEADS * HEAD_DIM
KV_WIDTH = 2 * SWA_KV_HEADS * HEAD_DIM
WINDOW = 128
N_META = 16
N_BUCKETS = 32
MAX_DISTANCE = 128
GDN_CONV = 4
FFN_CONV = 3
D_FF = 5632
EPS = 1e-6
PAST_LEN = 16384

SUBLANES = 8
LANES = 128

CHUNK = 128
INV_BASE = 16
GDN_HEAD_GROUP = 8

COL_Z = GDN_QKV
COL_SQ = COL_Z + GDN_WIDTH
COL_KV = COL_SQ + SWA_WIDTH
COL_BA = COL_KV + KV_WIDTH
PROJ_COLS = COL_BA + LANES
GDN_COLS = COL_SQ

SMALL_KEYS = 32
NEG = -1e30
VMEM_LIMIT = 56 * 1024 * 1024

_NT = (((1,), (1,)), ((), ()))


def _dot(a, b):
    return jnp.dot(a.astype(jnp.bfloat16), b.astype(jnp.bfloat16), preferred_element_type=jnp.float32)


def _dot_nt(a, b):
    return lax.dot_general(a.astype(jnp.bfloat16), b.astype(jnp.bfloat16), _NT,
                           preferred_element_type=jnp.float32)


def _split(a):
    hi = a.astype(jnp.bfloat16)
    lo = (a - hi.astype(jnp.float32)).astype(jnp.bfloat16)
    return hi, lo


def _dot3(a, b):
    ah, al = _split(a)
    bh, bl = _split(b)
    d = functools.partial(jnp.dot, preferred_element_type=jnp.float32)
    return d(ah, bh) + (d(ah, bl) + d(al, bh))


def _dot_exact(a, b, dims=None):
    if dims is None:
        return jnp.dot(a, b, precision=lax.Precision.HIGHEST, preferred_element_type=jnp.float32)
    return lax.dot_general(a, b, dims, precision=lax.Precision.HIGHEST,
                           preferred_element_type=jnp.float32)


def _rms_scale(x):
    return lax.rsqrt(jnp.mean(x * x, axis=-1, keepdims=True) + EPS)


def _silu(x):
    return x * jax.nn.sigmoid(x)


def _inproj_kernel(x_ref, nw_ref, w_ref, o_ref, xn_ref, *, row_chunk):
    @pl.when(pl.program_id(1) == 0)
    def _():
        def body(c, carry):
            rows = pl.ds(pl.multiple_of(c * row_chunk, row_chunk), row_chunk)
            x = x_ref[rows, :]
            xn_ref[rows, :] = (x * _rms_scale(x) * nw_ref[...]).astype(jnp.bfloat16)
            return carry
        lax.fori_loop(0, x_ref.shape[0] // row_chunk, body, 0)

    o_ref[...] = jnp.dot(xn_ref[...], w_ref[...], preferred_element_type=jnp.float32)


def _inproj(x, nw, w, *, tm, tn, row_chunk):
    rows = x.shape[0]
    return pl.pallas_call(
        functools.partial(_inproj_kernel, row_chunk=row_chunk),
        grid=(rows // tm, PROJ_COLS // tn),
        in_specs=[
            pl.BlockSpec((tm, D_MODEL), lambda i, j: (i, 0)),
            pl.BlockSpec((1, D_MODEL), lambda i, j: (0, 0)),
            pl.BlockSpec((D_MODEL, tn), lambda i, j: (0, j)),
        ],
        out_specs=pl.BlockSpec((tm, tn), lambda i, j: (i, j)),
        out_shape=jax.ShapeDtypeStruct((rows, PROJ_COLS), jnp.float32),
        scratch_shapes=[pltpu.VMEM((tm, D_MODEL), jnp.bfloat16)],
        compiler_params=pltpu.CompilerParams(
            dimension_semantics=("arbitrary", "arbitrary"), vmem_limit_bytes=VMEM_LIMIT),
        name="inproj",
    )(x, nw, w)


def _tri_inverse(lms, ri, ci):
    shift = INV_BASE.bit_length() - 1
    eye = (ri == ci).astype(jnp.float32)
    in_block = (ri >> shift) == (ci >> shift)
    ps = [jnp.where(in_block, lm, 0.0) for lm in lms]
    ts = [eye - p for p in ps]
    for _ in range(shift - 1):
        ps = [_dot3(p, p) for p in ps]
        ts = [t + _dot3(t, p) for t, p in zip(ts, ps)]
    size = INV_BASE
    while size < CHUNK:
        shift += 1
        in_pair = (ri >> shift) == (ci >> shift)
        off_mask = in_pair & jnp.logical_not(in_block)
        tos = [_dot3(t, jnp.where(off_mask, lm, 0.0)) for t, lm in zip(ts, lms)]
        ts = [t - _dot3(to, t) for t, to in zip(ts, tos)]
        in_block = in_pair
        size *= 2
    return ts


def _gdn_kernel(x_ref, ba_ref, hist_ref, s0_ref, cw_ref, alog_ref, dtb_ref, gnw_ref,
                o_ref, sout_ref, xe_ref, s_ref, *, nb, seq, group, carry, pad_rows):
    head_group = GDN_HEAD_GROUP
    step = pl.program_id(0)
    rows = nb * seq
    n_chunks = rows // CHUNK
    n_groups = CHUNK // group
    gshift = group.bit_length() - 1

    if carry:
        @pl.when(step == 0)
        def _():
            xe_ref[:, 0:SUBLANES, :] = hist_ref[...]
            s_ref[...] = s0_ref[0]
    else:
        xe_ref[:, 0:SUBLANES, :] = hist_ref[...]
    xe_ref[:, SUBLANES:SUBLANES + seq, :] = x_ref[:, :, 0:GDN_QKV]

    seq_rows = min(seq, CHUNK)
    seqs_per_chunk = CHUNK // seq_rows

    def chunk_rows(ref, c, row_off, cols):
        if seq >= CHUNK:
            start = row_off + c * CHUNK
            return ref[0:1, start:start + CHUNK, cols]
        b0 = c * seqs_per_chunk
        return ref[b0:b0 + seqs_per_chunk, row_off:row_off + seq, cols]

    def conv_chunk(col, c):
        cols = slice(col, col + HEAD_DIM)
        acc = None
        for s in range(GDN_CONV):
            term = chunk_rows(xe_ref, c, SUBLANES - s, cols) * cw_ref[GDN_CONV - 1 - s:GDN_CONV - s, cols]
            acc = term if acc is None else acc + term
        return _silu(acc).reshape(CHUNK, HEAD_DIM)

    ri = lax.broadcasted_iota(jnp.int32, (CHUNK, CHUNK), 0)
    ci = lax.broadcasted_iota(jnp.int32, (CHUNK, CHUNK), 1)
    same = (ri >> gshift) == (ci >> gshift)
    m_incl = same & (ri >= ci)
    m_strict = same & (ri > ci)
    f_incl = m_incl.astype(jnp.float32)
    f_same = same.astype(jnp.float32)
    lane = lax.broadcasted_iota(jnp.int32, (CHUNK, LANES), 1)
    row_in_chunk = lax.broadcasted_iota(jnp.int32, (CHUNK, LANES), 0)

    for c in range(n_chunks):
        r0 = c * CHUNK
        bac = chunk_rows(ba_ref, c, 0, slice(0, LANES)).reshape(CHUNK, LANES)
        beta_all = jax.nn.sigmoid(bac)
        sp_in = bac + dtb_ref[...]
        softplus = jnp.maximum(sp_in, 0.0) + jnp.log1p(jnp.exp(-jnp.abs(sp_in)))
        g_all = -jnp.exp(alog_ref[...]) * softplus
        if pad_rows and c == 0:
            valid = row_in_chunk >= pad_rows
            beta_all = jnp.where(valid, beta_all, 0.0)
            g_all = jnp.where(valid, g_all, 0.0)
        g_all = jnp.where((lane >= GDN_HEADS) & (lane < 2 * GDN_HEADS), g_all, 0.0)
        gc_col = _dot_exact(f_incl, g_all)
        gtot_col = _dot_exact(f_same, g_all)
        gc_row = _dot_exact(g_all.T, f_incl, _NT)

        for h0 in range(0, GDN_HEADS, head_group):
            heads = range(h0, h0 + head_group)
            q, k, v, lm, qk, rhs, egc, kd_t, gl = [], [], [], [], [], [], [], [], []
            for h in heads:
                qh = conv_chunk(h * HEAD_DIM, c)
                kh = conv_chunk(GDN_WIDTH + h * HEAD_DIM, c)
                vh = conv_chunk(2 * GDN_WIDTH + h * HEAD_DIM, c)
                qh = qh * lax.rsqrt(jnp.sum(qh * qh, -1, keepdims=True) + EPS) * (HEAD_DIM ** -0.5)
                kh = kh * lax.rsqrt(jnp.sum(kh * kh, -1, keepdims=True) + EPS)
                gcc = gc_col[:, GDN_HEADS + h:GDN_HEADS + h + 1]
                gcr = gc_row[GDN_HEADS + h:GDN_HEADS + h + 1, :]
                gtc = gtot_col[:, GDN_HEADS + h:GDN_HEADS + h + 1]
                beta = beta_all[:, h:h + 1]
                decay = jnp.exp(jnp.where(m_incl, gcc - gcr, NEG))
                kb = kh * beta
                e = jnp.exp(gcc)
                lm.append(jnp.where(m_strict, _dot_nt(kb, kh) * decay, 0.0))
                qk.append(_dot_nt(qh, kh) * decay)
                rhs.append(jnp.concatenate([vh * beta, kb * e], axis=1))
                egc.append(e)
                kd_t.append((kh * jnp.exp(gtc - gcc)).T)
                gl.append(jnp.exp(gtc))
                q.append(qh)
            sol = [_dot3(t, r) for t, r in zip(_tri_inverse(lm, ri, ci), rhs)]

            state = lambda h, b: s_ref[h] if carry else s0_ref[c * n_groups + b, h]
            ws, qs = [], []
            for i, h in enumerate(heads):
                w = sol[i][:, HEAD_DIM:]
                qg = q[i] * egc[i]
                ws_parts, qs_parts = [], []
                for b in range(n_groups):
                    g0 = b * group
                    wq = jnp.concatenate([w[g0:g0 + group], qg[g0:g0 + group]], axis=0)
                    res = _dot(wq, state(h, b))
                    ws_parts.append(res[:group])
                    qs_parts.append(res[group:])
                ws.append(ws_parts[0] if n_groups == 1 else jnp.concatenate(ws_parts, axis=0))
                qs.append(qs_parts[0] if n_groups == 1 else jnp.concatenate(qs_parts, axis=0))
            v_new = [s[:, :HEAD_DIM] - w for s, w in zip(sol, ws)]
            o = [a + _dot(b, vn) for a, b, vn in zip(qs, qk, v_new)]
            for i, h in enumerate(heads):
                for b in range(n_groups):
                    g0 = b * group
                    kd_b = kd_t[i] if n_groups == 1 else jnp.where((ci >> gshift) == b, kd_t[i], 0.0)
                    st = state(h, b) * gl[i][g0:g0 + 1, :] + _dot(kd_b, v_new[i])
                    if carry:
                        s_ref[h] = st
                    else:
                        sout_ref[c * n_groups + b, h] = st
            for i, h in enumerate(heads):
                z = chunk_rows(x_ref, c, 0, slice(COL_Z + h * HEAD_DIM, COL_Z + (h + 1) * HEAD_DIM)).reshape(CHUNK, HEAD_DIM)
                y = o[i] * lax.rsqrt(jnp.mean(o[i] * o[i], -1, keepdims=True) + EPS) * gnw_ref[...] * _silu(z)
                o_ref[r0:r0 + CHUNK, h * HEAD_DIM:(h + 1) * HEAD_DIM] = y.astype(o_ref.dtype)

    if carry:
        xe_ref[:, 0:SUBLANES, :] = xe_ref[:, seq:seq + SUBLANES, :]

        @pl.when(step == pl.num_programs(0) - 1)
        def _():
            sout_ref[0] = s_ref[...]


def _gdn(x3, x_idx, ba_idx, hist, hist_idx, s0, cw, alog_row, dtb_row, gnw, *,
         n_steps, nb, seq, group, carry, pad_rows):
    rows = nb * seq
    if carry:
        state_spec = pl.BlockSpec((1, GDN_HEADS, HEAD_DIM, HEAD_DIM), lambda s: (0, 0, 0, 0))
        state_shape = (1, GDN_HEADS, HEAD_DIM, HEAD_DIM)
    else:
        n_states = rows // group
        state_spec = pl.BlockSpec((n_states, GDN_HEADS, HEAD_DIM, HEAD_DIM), lambda s: (s, 0, 0, 0))
        state_shape = (n_steps * n_states, GDN_HEADS, HEAD_DIM, HEAD_DIM)
    full = lambda shape: pl.BlockSpec(shape, lambda s: (0,) * len(shape))
    return pl.pallas_call(
        functools.partial(_gdn_kernel, nb=nb, seq=seq, group=group, carry=carry, pad_rows=pad_rows),
        grid=(n_steps,),
        in_specs=[
            pl.BlockSpec((nb, seq, GDN_COLS), x_idx),
            pl.BlockSpec((nb, seq, LANES), ba_idx),
            pl.BlockSpec((nb, SUBLANES, GDN_QKV), hist_idx),
            state_spec,
            full((GDN_CONV, GDN_QKV)),
            full((1, LANES)),
            full((1, LANES)),
            full((1, HEAD_DIM)),
        ],
        out_specs=[
            pl.BlockSpec((rows, GDN_WIDTH), lambda s: (s, 0)),
            state_spec,
        ],
        out_shape=[
            jax.ShapeDtypeStruct((n_steps * rows, GDN_WIDTH), jnp.bfloat16),
            jax.ShapeDtypeStruct(state_shape, jnp.float32),
        ],
        scratch_shapes=[
            pltpu.VMEM((nb, SUBLANES + seq, GDN_QKV), jnp.float32),
            pltpu.VMEM((GDN_HEADS, HEAD_DIM, HEAD_DIM), jnp.float32),
        ],
        compiler_params=pltpu.CompilerParams(
            dimension_semantics=("arbitrary",), vmem_limit_bytes=VMEM_LIMIT),
        name="gdn_seq" if carry else "gdn_batch",
    )(x3, x3, hist, s0, cw, alog_row, dtb_row, gnw)


def _t5_bucket_np(dist):
    n = np.maximum(dist, 0)
    exact = N_BUCKETS // 2
    large = exact + (np.log(np.maximum(n, 1).astype(np.float32) / exact)
                     / math.log(MAX_DISTANCE / exact) * (N_BUCKETS - exact)).astype(np.int32)
    return np.where(n < exact, n, np.minimum(large, N_BUCKETS - 1)).astype(np.int32)


def _bucket_ids(dist, valid):
    return np.where(valid, _t5_bucket_np(dist), -1).astype(np.int32)


def _bias_kernel(table_ref, *refs):
    n = len(refs) // 2
    for ids_ref, out_ref in zip(refs[:n], refs[n:]):
        ids = ids_ref[...]
        nq = ids.shape[0]
        for head in range(SWA_HEADS):
            def body(b, acc):
                return jnp.where(ids == b, table_ref[b, head], acc)
            acc = lax.fori_loop(0, N_BUCKETS, body, jnp.full(ids.shape, NEG, jnp.float32))
            kh, g = divmod(head, SWA_GROUP)
            out_ref[kh, g * nq:(g + 1) * nq, :] = acc


def _bias_tables(rel_table, id_arrays):
    out_shapes = [jax.ShapeDtypeStruct((SWA_KV_HEADS, SWA_GROUP * a.shape[0], a.shape[1]), jnp.float32)
                  for a in id_arrays]
    vmem = pl.BlockSpec(memory_space=pltpu.VMEM)
    return pl.pallas_call(
        _bias_kernel,
        in_specs=[pl.BlockSpec(memory_space=pltpu.SMEM)] + [vmem] * len(id_arrays),
        out_specs=[vmem] * len(id_arrays),
        out_shape=out_shapes,
        name="swa_bias",
    )(rel_table, *[jnp.asarray(a) for a in id_arrays])


def _attend(q, keys, values, biases, sink):
    scale = HEAD_DIM ** -0.5
    scores = [_dot_nt(q, k) * scale + b for k, b in zip(keys, biases)]
    m = sink
    for s in scores:
        m = jnp.maximum(m, jnp.max(s, axis=-1, keepdims=True))
    den = jnp.exp(sink - m)
    acc = None
    for s, v in zip(scores, values):
        p = jnp.exp(s - m)
        den = den + jnp.sum(p, axis=-1, keepdims=True)
        pv = _dot(p, v)
        acc = pv if acc is None else acc + pv
    return acc / den


def _swa_prompt_kernel(q_ref, kvc_ref, kvp_ref, kvm_ref, bcur_ref, bprev_ref, bm0_ref, bfar_ref,
                       sink_ref, o_ref):
    first = pl.program_id(0) == 0
    for kh in range(SWA_KV_HEADS):
        ks = slice(kh * HEAD_DIM, (kh + 1) * HEAD_DIM)
        vs = slice((SWA_KV_HEADS + kh) * HEAD_DIM, (SWA_KV_HEADS + kh + 1) * HEAD_DIM)
        q = jnp.concatenate(
            [q_ref[:, (kh * SWA_GROUP + g) * HEAD_DIM:(kh * SWA_GROUP + g + 1) * HEAD_DIM]
             for g in range(SWA_GROUP)], axis=0)
        b_prev = jnp.where(first, NEG, bprev_ref[kh])
        b_meta = jnp.where(first, bm0_ref[kh], bfar_ref[kh])
        o = _attend(q,
                    [kvc_ref[:, ks], kvp_ref[:, ks], kvm_ref[:, ks]],
                    [kvc_ref[:, vs], kvp_ref[:, vs], kvm_ref[:, vs]],
                    [bcur_ref[kh], b_prev, b_meta], sink_ref[kh])
        for g in range(SWA_GROUP):
            head = kh * SWA_GROUP + g
            o_ref[:, head * HEAD_DIM:(head + 1) * HEAD_DIM] = (
                o[g * WINDOW:(g + 1) * WINDOW].astype(o_ref.dtype))


def _swa_meta_kernel(q_ref, kv_ref, bias_ref, sink_ref, o_ref):
    for kh in range(SWA_KV_HEADS):
        ks = slice(kh * HEAD_DIM, (kh + 1) * HEAD_DIM)
        vs = slice((SWA_KV_HEADS + kh) * HEAD_DIM, (SWA_KV_HEADS + kh + 1) * HEAD_DIM)
        q = jnp.concatenate(
            [q_ref[:, (kh * SWA_GROUP + g) * HEAD_DIM:(kh * SWA_GROUP + g + 1) * HEAD_DIM]
             for g in range(SWA_GROUP)], axis=0)
        o = _attend(q, [kv_ref[:, ks]], [kv_ref[:, vs]], [bias_ref[kh]], sink_ref[kh])
        for g in range(SWA_GROUP):
            head = kh * SWA_GROUP + g
            o_ref[:, head * HEAD_DIM:(head + 1) * HEAD_DIM] = (
                o[g * N_META:(g + 1) * N_META].astype(o_ref.dtype))


def _swa_sample_kernel(q_ref, kvn_ref, win_ref, meta_ref, bwin_ref, bsmall_ref, sink_ref, o_ref, *, nb, seq):
    pad = jnp.zeros((SMALL_KEYS - N_META - seq, HEAD_DIM), jnp.float32)
    outs = [[] for _ in range(SWA_HEADS)]
    for b in range(nb):
        for kh in range(SWA_KV_HEADS):
            ks = slice(kh * HEAD_DIM, (kh + 1) * HEAD_DIM)
            vs = slice((SWA_KV_HEADS + kh) * HEAD_DIM, (SWA_KV_HEADS + kh + 1) * HEAD_DIM)
            q = jnp.concatenate(
                [q_ref[b, :, (kh * SWA_GROUP + g) * HEAD_DIM:(kh * SWA_GROUP + g + 1) * HEAD_DIM]
                 for g in range(SWA_GROUP)], axis=0)
            k_small = jnp.concatenate([meta_ref[b, :, ks], kvn_ref[b, :, ks], pad], axis=0)
            v_small = jnp.concatenate([meta_ref[b, :, vs], kvn_ref[b, :, vs], pad], axis=0)
            o = _attend(q, [win_ref[b, :, ks], k_small], [win_ref[b, :, vs], v_small],
                        [bwin_ref[kh], bsmall_ref[kh]], sink_ref[kh])
            for g in range(SWA_GROUP):
                outs[kh * SWA_GROUP + g].append(o[g * seq:(g + 1) * seq])
    for head in range(SWA_HEADS):
        o_ref[:, head * HEAD_DIM:(head + 1) * HEAD_DIM] = (
            jnp.concatenate(outs[head], axis=0).astype(o_ref.dtype))


def _outproj_kernel(g_ref, s_ref, h_ref, wo_ref, nw_ref, o_ref):
    mix = (jnp.dot(g_ref[...], wo_ref[0:GDN_WIDTH, :], preferred_element_type=jnp.float32)
           + jnp.dot(s_ref[...], wo_ref[GDN_WIDTH:, :], preferred_element_type=jnp.float32))
    o_ref[...] = h_ref[...] + mix * _rms_scale(mix) * nw_ref[...]


def _outproj(g, s, h, wo, nw, *, tm):
    rows = h.shape[0]
    return pl.pallas_call(
        _outproj_kernel,
        grid=(rows // tm,),
        in_specs=[
            pl.BlockSpec((tm, GDN_WIDTH), lambda i: (i, 0)),
            pl.BlockSpec((tm, SWA_WIDTH), lambda i: (i, 0)),
            pl.BlockSpec((tm, D_MODEL), lambda i: (i, 0)),
            pl.BlockSpec((D_MODEL, D_MODEL), lambda i: (0, 0)),
            pl.BlockSpec((1, D_MODEL), lambda i: (0, 0)),
        ],
        out_specs=pl.BlockSpec((tm, D_MODEL), lambda i: (i, 0)),
        out_shape=jax.ShapeDtypeStruct((rows, D_MODEL), jnp.float32),
        compiler_params=pltpu.CompilerParams(
            dimension_semantics=("arbitrary",), vmem_limit_bytes=VMEM_LIMIT),
        name="outproj",
    )(g, s, h, wo, nw)


def _ffn_kernel(*refs, batch, tm, tf):
    if batch:
        (h_ref, nw_pre_ref, wg_ref, wu_ref, cw_ref, wd_ref, nw_post_ref, hist_ref,
         y_ref, graw_ref, xn_ref, acc_ref, xe_ref) = refs
    else:
        (h_ref, nw_pre_ref, wg_ref, wu_ref, cw_ref, wd_ref, nw_post_ref, hist_ref,
         y_ref, graw_ref, xn_ref, acc_ref, xe_ref, carry_ref) = refs
    i = pl.program_id(0)
    j = pl.program_id(1)
    seq = SUBLANES if batch else tm

    @pl.when(j == 0)
    def _():
        h = h_ref[...]
        xn_ref[...] = (h * _rms_scale(h) * nw_pre_ref[...]).astype(jnp.bfloat16)

    xn = xn_ref[...]
    gate = jnp.dot(xn, wg_ref[...], preferred_element_type=jnp.float32)
    up = jnp.dot(xn, wu_ref[...], preferred_element_type=jnp.float32)

    if batch:
        xe_ref[:, 0:SUBLANES, :] = hist_ref[...]
        graw_ref[...] = gate
    else:
        @pl.when(i == 0)
        def _():
            xe_ref[:, 0:SUBLANES, :] = hist_ref[...].reshape(1, SUBLANES, tf)

        @pl.when(i > 0)
        def _():
            xe_ref[:, 0:SUBLANES, :] = carry_ref[pl.ds(j, 1)]
        carry_ref[pl.ds(j, 1)] = gate[tm - SUBLANES:].reshape(1, SUBLANES, tf)
        graw_ref[...] = gate[tm - SUBLANES:]
    xe_ref[:, SUBLANES:SUBLANES + seq, :] = gate.reshape(xe_ref.shape[0], seq, tf)
    conv = None
    for s in range(FFN_CONV):
        term = xe_ref[:, SUBLANES - s:SUBLANES - s + seq, :] * cw_ref[FFN_CONV - 1 - s:FFN_CONV - s, :]
        conv = term if conv is None else conv + term
    act = (_silu(conv.reshape(tm, tf)) * up).astype(jnp.bfloat16)
    contrib = jnp.dot(act, wd_ref[...], preferred_element_type=jnp.float32)

    @pl.when(j == 0)
    def _():
        acc_ref[...] = contrib

    @pl.when(j > 0)
    def _():
        acc_ref[...] += contrib

    @pl.when(j == pl.num_programs(1) - 1)
    def _():
        y = acc_ref[...]
        y_ref[...] = h_ref[...] + y * _rms_scale(y) * nw_post_ref[...]


def _ffn(h, nw_pre, wg, wu, cw, wd, nw_post, hist, *, batch, tm, tf):
    rows = h.shape[0]
    nj = D_FF // tf
    in_specs = [
        pl.BlockSpec((tm, D_MODEL), lambda i, j: (i, 0)),
        pl.BlockSpec((1, D_MODEL), lambda i, j: (0, 0)),
        pl.BlockSpec((D_MODEL, tf), lambda i, j: (0, j)),
        pl.BlockSpec((D_MODEL, tf), lambda i, j: (0, j)),
        pl.BlockSpec((FFN_CONV, tf), lambda i, j: (0, j)),
        pl.BlockSpec((tf, D_MODEL), lambda i, j: (j, 0)),
        pl.BlockSpec((1, D_MODEL), lambda i, j: (0, 0)),
    ]
    args = [h, nw_pre, wg, wu, cw, wd, nw_post, hist]
    scratch = [pltpu.VMEM((tm, D_MODEL), jnp.bfloat16), pltpu.VMEM((tm, D_MODEL), jnp.float32)]
    if batch:
        in_specs.append(pl.BlockSpec((tm // SUBLANES, SUBLANES, tf), lambda i, j: (i, 0, j)))
        graw_spec = pl.BlockSpec((tm, tf), lambda i, j: (i, j))
        graw_shape = jax.ShapeDtypeStruct((rows, D_FF), jnp.float32)
        scratch.append(pltpu.VMEM((tm // SUBLANES, 2 * SUBLANES, tf), jnp.float32))
    else:
        in_specs.append(pl.BlockSpec((SUBLANES, tf), lambda i, j: (0, j)))
        graw_spec = pl.BlockSpec((SUBLANES, tf), lambda i, j: (i, j))
        graw_shape = jax.ShapeDtypeStruct((rows // tm * SUBLANES, D_FF), jnp.float32)
        scratch.append(pltpu.VMEM((1, SUBLANES + tm, tf), jnp.float32))
        scratch.append(pltpu.VMEM((nj, SUBLANES, tf), jnp.float32))
    return pl.pallas_call(
        functools.partial(_ffn_kernel, batch=batch, tm=tm, tf=tf),
        grid=(rows // tm, nj),
        in_specs=in_specs,
        out_specs=[pl.BlockSpec((tm, D_MODEL), lambda i, j: (i, 0)), graw_spec],
        out_shape=[jax.ShapeDtypeStruct((rows, D_MODEL), jnp.float32), graw_shape],
        scratch_shapes=scratch,
        compiler_params=pltpu.CompilerParams(
            dimension_semantics=("arbitrary", "arbitrary"), vmem_limit_bytes=VMEM_LIMIT),
        name="ffn_batch" if batch else "ffn_seq",
    )(*args)


def kernel(x_prompt, x_sample, cache_swa_meta_kv, cache_swa_window_kv, state_gdn_conv, state_gdn, state_ffn_conv, meta_tokens, rel_bias_table, w_in, gdn_conv_w, gdn_a_log, gdn_dt_bias, gdn_norm_w, swa_sinks, w_out, norm_mix_pre, norm_mix_post, norm_ffn_pre, norm_ffn_post, ffn_w_gate, ffn_w_up, ffn_conv_w, ffn_w_down):
    f32, bf16 = jnp.float32, jnp.bfloat16
    seq = x_prompt.shape[1]
    dec_b, dec_t = x_sample.shape[0], x_sample.shape[1]
    n_dec = dec_b * dec_t
    assert x_prompt.shape[0] == 1 and seq % CHUNK == 0 and dec_t == SUBLANES and n_dec % CHUNK == 0

    wi = w_in[0]
    n_ba = 2 * GDN_HEADS
    w_in_p = jnp.concatenate(
        [wi[:, :COL_SQ], wi[:, COL_SQ + n_ba:], wi[:, COL_SQ:COL_SQ + n_ba],
         jnp.zeros((D_MODEL, LANES - n_ba), f32)], axis=1).astype(bf16)
    wo = w_out[0].astype(bf16)
    wg = ffn_w_gate[0].astype(bf16)
    wu = ffn_w_up[0].astype(bf16)
    wd = ffn_w_down[0].astype(bf16)
    lane_pad = lambda v: jnp.pad(v.reshape(1, GDN_HEADS), ((0, 0), (GDN_HEADS, LANES - 2 * GDN_HEADS)))
    alog_row = lane_pad(gdn_a_log[0])
    dtb_row = lane_pad(gdn_dt_bias[0])
    gnw = gdn_norm_w[0].reshape(1, HEAD_DIM)

    pad_rows = CHUNK - N_META
    n_small = n_dec + CHUNK
    x_big = x_prompt.reshape(seq, D_MODEL)
    x_small = jnp.concatenate(
        [x_sample.reshape(n_dec, D_MODEL), jnp.zeros((pad_rows, D_MODEL), f32), meta_tokens.astype(f32)], axis=0)
    nw = norm_mix_pre[0].reshape(1, D_MODEL)
    proj_big = _inproj(x_big, nw, w_in_p, tm=1024, tn=640, row_chunk=128)
    proj_small = _inproj(x_small, nw, w_in_p, tm=n_small, tn=640, row_chunk=128)

    cw = gdn_conv_w[0]
    small_chunks = proj_small.reshape(n_small // CHUNK, CHUNK, PROJ_COLS)
    small_groups = proj_small.reshape(n_small // SUBLANES, SUBLANES, PROJ_COLS)
    last_chunk = n_small // CHUNK - 1
    ba_blk = COL_BA // LANES
    gdn_meta, s_meta = _gdn(
        small_chunks, lambda s: (last_chunk, 0, 0), lambda s: (last_chunk, 0, ba_blk),
        jnp.zeros((1, SUBLANES, GDN_QKV), f32), lambda s: (0, 0, 0),
        jnp.zeros((1, GDN_HEADS, HEAD_DIM, HEAD_DIM), f32), cw, alog_row, dtb_row, gnw,
        n_steps=1, nb=1, seq=CHUNK, group=CHUNK, carry=True, pad_rows=pad_rows)
    last_group = n_small // SUBLANES - 1
    gdn_big, s_prompt = _gdn(
        proj_big.reshape(1, seq, PROJ_COLS), lambda s: (0, s, 0), lambda s: (0, s, ba_blk),
        small_groups, lambda s: (last_group, 0, 0),
        s_meta, cw, alog_row, dtb_row, gnw,
        n_steps=seq // CHUNK, nb=1, seq=CHUNK, group=CHUNK, carry=True, pad_rows=0)
    hist_gdn = jnp.pad(state_gdn_conv[0], ((0, 0), (SUBLANES - (GDN_CONV - 1), 0), (0, 0)))
    nb_gdn = CHUNK // dec_t
    gdn_small, s_sample = _gdn(
        small_groups, lambda s: (s, 0, 0), lambda s: (s, 0, ba_blk),
        hist_gdn, lambda s: (s, 0, 0),
        state_gdn[0], cw, alog_row, dtb_row, gnw,
        n_steps=dec_b // nb_gdn, nb=nb_gdn, seq=dec_t, group=dec_t, carry=False, pad_rows=0)

    qi = np.arange(WINDOW)[:, None]
    kj = np.arange(WINDOW)[None, :]
    mi = np.arange(N_META)[None, :]
    ti = np.arange(dec_t)[:, None]
    small_keys = np.arange(SMALL_KEYS)[None, :]
    new_keys = small_keys - N_META
    id_arrays = [
        _bucket_ids(qi - kj, qi >= kj),
        _bucket_ids(qi - kj + WINDOW, kj > qi),
        _bucket_ids(qi + N_META - mi, np.ones((WINDOW, N_META), bool)),
        _bucket_ids(qi + N_META - mi + WINDOW, np.ones((WINDOW, N_META), bool)),
        _bucket_ids(ti + WINDOW - kj, kj > ti),
        _bucket_ids(np.where(new_keys < 0, PAST_LEN + ti - small_keys, ti - new_keys),
                    (new_keys < 0) | ((new_keys <= ti) & (new_keys < dec_t))),
        _bucket_ids(mi.T - mi, mi.T >= mi),
    ]
    bcur, bprev, bm0, bfar, bwin, bsmall, bmm = _bias_tables(rel_bias_table, id_arrays)
    sink_rows = lambda q: jnp.repeat(swa_sinks[0].reshape(SWA_KV_HEADS, SWA_GROUP), q, axis=1)[..., None]

    sq_blk = COL_SQ // SWA_WIDTH
    kv_blk = COL_KV // KV_WIDTH
    meta_blk = (n_small - N_META) // N_META
    full3 = lambda a: pl.BlockSpec(a.shape, lambda j: (0, 0, 0))
    sink_p = sink_rows(WINDOW)
    swa_big = pl.pallas_call(
        _swa_prompt_kernel,
        grid=(seq // WINDOW,),
        in_specs=[
            pl.BlockSpec((WINDOW, SWA_WIDTH), lambda j: (j, sq_blk)),
            pl.BlockSpec((WINDOW, KV_WIDTH), lambda j: (j, kv_blk)),
            pl.BlockSpec((WINDOW, KV_WIDTH), lambda j: (jnp.maximum(j - 1, 0), kv_blk)),
            pl.BlockSpec((N_META, KV_WIDTH), lambda j: (meta_blk, kv_blk)),
            full3(bcur), full3(bprev), full3(bm0), full3(bfar), full3(sink_p),
        ],
        out_specs=pl.BlockSpec((WINDOW, SWA_WIDTH), lambda j: (j, 0)),
        out_shape=jax.ShapeDtypeStruct((seq, SWA_WIDTH), bf16),
        compiler_params=pltpu.CompilerParams(
            dimension_semantics=("arbitrary",), vmem_limit_bytes=VMEM_LIMIT),
        name="swa_prompt",
    )(proj_big, proj_big, proj_big, proj_small, bcur, bprev, bm0, bfar, sink_p)

    nb_swa = 8
    sink_s = sink_rows(dec_t)
    win = cache_swa_window_kv[0].reshape(dec_b, WINDOW, KV_WIDTH)
    meta_kv = cache_swa_meta_kv[0].reshape(dec_b, N_META, KV_WIDTH)
    swa_small = pl.pallas_call(
        functools.partial(_swa_sample_kernel, nb=nb_swa, seq=dec_t),
        grid=(dec_b // nb_swa,),
        in_specs=[
            pl.BlockSpec((nb_swa, dec_t, SWA_WIDTH), lambda j: (j, 0, sq_blk)),
            pl.BlockSpec((nb_swa, dec_t, KV_WIDTH), lambda j: (j, 0, kv_blk)),
            pl.BlockSpec((nb_swa, WINDOW, KV_WIDTH), lambda j: (j, 0, 0)),
            pl.BlockSpec((nb_swa, N_META, KV_WIDTH), lambda j: (j, 0, 0)),
            full3(bwin), full3(bsmall), full3(sink_s),
        ],
        out_specs=pl.BlockSpec((nb_swa * dec_t, SWA_WIDTH), lambda j: (j, 0)),
        out_shape=jax.ShapeDtypeStruct((n_dec, SWA_WIDTH), bf16),
        compiler_params=pltpu.CompilerParams(
            dimension_semantics=("arbitrary",), vmem_limit_bytes=VMEM_LIMIT),
        name="swa_sample",
    )(small_groups, small_groups, win, meta_kv, bwin, bsmall, sink_s)

    sink_m = sink_rows(N_META)
    swa_meta = pl.pallas_call(
        _swa_meta_kernel,
        grid=(1,),
        in_specs=[
            pl.BlockSpec((N_META, SWA_WIDTH), lambda j: (meta_blk, sq_blk)),
            pl.BlockSpec((N_META, KV_WIDTH), lambda j: (meta_blk, kv_blk)),
            full3(bmm), full3(sink_m),
        ],
        out_specs=pl.BlockSpec((N_META, SWA_WIDTH), lambda j: (0, 0)),
        out_shape=jax.ShapeDtypeStruct((N_META, SWA_WIDTH), bf16),
        name="swa_meta",
    )(proj_small, proj_small, bmm, sink_m)

    nw_post = norm_mix_post[0].reshape(1, D_MODEL)
    nf_pre = norm_ffn_pre[0].reshape(1, D_MODEL)
    nf_post = norm_ffn_post[0].reshape(1, D_MODEL)
    fcw = ffn_conv_w[0]
    gdn_small_all = jnp.concatenate([gdn_small, gdn_meta], axis=0)
    swa_small_all = jnp.concatenate([swa_small, jnp.zeros((pad_rows, SWA_WIDTH), bf16), swa_meta], axis=0)
    h_small = _outproj(gdn_small_all, swa_small_all, x_small, wo, nw_post, tm=n_small // 2)
    hist_ffn = jnp.pad(state_ffn_conv[0], ((0, CHUNK // SUBLANES), (SUBLANES - (FFN_CONV - 1), 0), (0, 0)))
    y_small, g_small = _ffn(h_small, nf_pre, wg, wu, fcw, wd, nf_post, hist_ffn,
                            batch=True, tm=n_small // 2, tf=512)
    h_big = _outproj(gdn_big, swa_big, x_big, wo, nw_post, tm=512)
    y_big, g_tail = _ffn(h_big, nf_pre, wg, wu, fcw, wd, nf_post, g_small[n_small - SUBLANES:],
                         batch=False, tm=512, tf=512)

    kv_shape = lambda n: (1, n, 2, SWA_KV_HEADS, HEAD_DIM)
    kv_small = proj_small[:, COL_KV:COL_KV + KV_WIDTH]
    y_prompt = y_big.reshape(1, seq, D_MODEL)
    y_sample = y_small[:n_dec].reshape(dec_b, dec_t, D_MODEL)
    p_meta_kv = kv_small[n_small - N_META:].reshape(kv_shape(N_META))[None]
    p_window_kv = proj_big[seq - WINDOW:, COL_KV:COL_KV + KV_WIDTH].reshape(kv_shape(WINDOW))[None]
    p_gdn_conv = proj_big[seq - (GDN_CONV - 1):, :GDN_QKV].reshape(1, 1, GDN_CONV - 1, GDN_QKV)
    p_gdn_state = s_prompt[None]
    p_ffn_conv = g_tail[g_tail.shape[0] - (FFN_CONV - 1):].reshape(1, 1, FFN_CONV - 1, D_FF)
    kv_new = kv_small[:n_dec].reshape(dec_b, dec_t, 2, SWA_KV_HEADS, HEAD_DIM)
    s_window_kv = jnp.concatenate([cache_swa_window_kv[0][:, dec_t:], kv_new], axis=1)[None]
    s_gdn_conv = proj_small[:n_dec, :GDN_QKV].reshape(dec_b, dec_t, GDN_QKV)[:, dec_t - (GDN_CONV - 1):][None]
    s_gdn_state = s_sample[None]
    s_ffn_conv = g_small[:n_dec].reshape(dec_b, dec_t, D_FF)[:, dec_t - (FFN_CONV - 1):][None]
    return (y_prompt, y_sample, p_meta_kv, p_window_kv, p_gdn_conv, p_gdn_state, p_ffn_conv,
            s_window_kv, s_gdn_conv, s_gdn_state, s_ffn_conv)
```

```python
import functools
import math

import numpy as np
import jax
import jax.numpy as jnp
from jax import lax
from jax.experimental import pallas as pl
from jax.experimental.pallas import tpu as pltpu

D_MODEL = 2048
HEAD_DIM = 128
GDN_HEADS = 8
GDN_WIDTH = GDN_HEADS * HEAD_DIM
GDN_QKV = 3 * GDN_WIDTH
SWA_HEADS = 8
SWA_KV_HEADS = 2
SWA_GROUP = SWA_HEADS // SWA_KV_HEADS
SWA_WIDTH = SWA_HEADS * HEAD_DIM
KV_WIDTH = 2 * SWA_KV_HEADS * HEAD_DIM
WINDOW = 128
N_META = 16
N_BUCKETS = 32
MAX_DISTANCE = 128
GDN_CONV = 4
FFN_CONV = 3
D_FF = 5632
EPS = 1e-6
PAST_LEN = 16384

SUBLANES = 8
LANES = 128

CHUNK = 128
INV_BASE = 16
GDN_HEAD_GROUP = 8
FFN_ROW_BLOCKS = 4

COL_Z = GDN_QKV
COL_SQ = COL_Z + GDN_WIDTH
COL_KV = COL_SQ + SWA_WIDTH
PROJ_COLS = COL_KV + KV_WIDTH
GDN_COLS = COL_SQ

SMALL_KEYS = 32
NEG = -1e30
VMEM_LIMIT = 56 * 1024 * 1024

_NT = (((1,), (1,)), ((), ()))


def _dot(a, b):
    return jnp.dot(a.astype(jnp.bfloat16), b.astype(jnp.bfloat16), preferred_element_type=jnp.float32)


def _dot_nt(a, b):
    return lax.dot_general(a.astype(jnp.bfloat16), b.astype(jnp.bfloat16), _NT,
                           preferred_element_type=jnp.float32)


def _split(a):
    hi = a.astype(jnp.bfloat16)
    lo = (a - hi.astype(jnp.float32)).astype(jnp.bfloat16)
    return hi, lo


def _dot3(a, b):
    ah, al = _split(a)
    bh, bl = _split(b)
    d = functools.partial(jnp.dot, preferred_element_type=jnp.float32)
    return d(ah, bh) + (d(ah, bl) + d(al, bh))


def _dot_exact(a, b, dims=None):
    if dims is None:
        return jnp.dot(a, b, precision=lax.Precision.HIGHEST, preferred_element_type=jnp.float32)
    return lax.dot_general(a, b, dims, precision=lax.Precision.HIGHEST,
                           preferred_element_type=jnp.float32)


def _rms_scale(x):
    return lax.rsqrt(jnp.mean(x * x, axis=-1, keepdims=True) + EPS)


def _silu(x):
    return x * jax.nn.sigmoid(x)


def _inproj_kernel(x_ref, nw_ref, w_ref, wba_ref, o_ref, ba_ref, xn_ref, *, row_chunk):
    @pl.when(pl.program_id(1) == 0)
    def _():
        def body(c, carry):
            rows = pl.ds(pl.multiple_of(c * row_chunk, row_chunk), row_chunk)
            x = x_ref[rows, :]
            xn_ref[rows, :] = (x * _rms_scale(x) * nw_ref[...]).astype(jnp.bfloat16)
            return carry
        lax.fori_loop(0, x_ref.shape[0] // row_chunk, body, 0)
        ba_ref[...] = jnp.dot(xn_ref[...], wba_ref[...], preferred_element_type=jnp.float32)

    o_ref[...] = jnp.dot(xn_ref[...], w_ref[...], preferred_element_type=jnp.float32)


def _inproj(x, nw, w, wba, *, tm, tn, row_chunk):
    rows = x.shape[0]
    return pl.pallas_call(
        functools.partial(_inproj_kernel, row_chunk=row_chunk),
        grid=(rows // tm, PROJ_COLS // tn),
        in_specs=[
            pl.BlockSpec((tm, D_MODEL), lambda i, j: (i, 0)),
            pl.BlockSpec((1, D_MODEL), lambda i, j: (0, 0)),
            pl.BlockSpec((D_MODEL, tn), lambda i, j: (0, j)),
            pl.BlockSpec((D_MODEL, LANES), lambda i, j: (0, 0)),
        ],
        out_specs=[pl.BlockSpec((tm, tn), lambda i, j: (i, j)),
                   pl.BlockSpec((tm, LANES), lambda i, j: (i, 0))],
        out_shape=[jax.ShapeDtypeStruct((rows, PROJ_COLS), jnp.float32),
                   jax.ShapeDtypeStruct((rows, LANES), jnp.float32)],
        scratch_shapes=[pltpu.VMEM((tm, D_MODEL), jnp.bfloat16)],
        compiler_params=pltpu.CompilerParams(
            dimension_semantics=("arbitrary", "arbitrary"), vmem_limit_bytes=VMEM_LIMIT),
        name="inproj",
    )(x, nw, w, wba)


def _tri_inverse(lms, ri, ci):
    shift = INV_BASE.bit_length() - 1
    eye = (ri == ci).astype(jnp.float32)
    in_block = (ri >> shift) == (ci >> shift)
    ps = [jnp.where(in_block, lm, 0.0) for lm in lms]
    ts = [eye - p for p in ps]
    for _ in range(shift - 1):
        ps = [_dot3(p, p) for p in ps]
        ts = [t + _dot3(t, p) for t, p in zip(ts, ps)]
    size = INV_BASE
    while size < CHUNK:
        shift += 1
        in_pair = (ri >> shift) == (ci >> shift)
        off_mask = in_pair & jnp.logical_not(in_block)
        tos = [_dot3(t, jnp.where(off_mask, lm, 0.0)) for t, lm in zip(ts, lms)]
        ts = [t - _dot3(to, t) for t, to in zip(ts, tos)]
        in_block = in_pair
        size *= 2
    return ts


def _gdn_kernel(x_ref, ba_ref, hist_ref, s0_ref, cw_ref, alog_ref, dtb_ref, gnw_ref,
                o_ref, sout_ref, xe_ref, s_ref, *, nb, seq, group, carry, pad_rows):
    head_group = GDN_HEAD_GROUP
    step = pl.program_id(0)
    rows = nb * seq
    n_chunks = rows // CHUNK
    n_groups = CHUNK // group
    gshift = group.bit_length() - 1

    if carry:
        @pl.when(step == 0)
        def _():
            xe_ref[:, 0:SUBLANES, :] = hist_ref[...]
            s_ref[...] = s0_ref[0]
    else:
        xe_ref[:, 0:SUBLANES, :] = hist_ref[...]
    xe_ref[:, SUBLANES:SUBLANES + seq, :] = x_ref[:, :, 0:GDN_QKV]

    seq_rows = min(seq, CHUNK)
    seqs_per_chunk = CHUNK // seq_rows

    def chunk_rows(ref, c, row_off, cols):
        if seq >= CHUNK:
            start = row_off + c * CHUNK
            return ref[0:1, start:start + CHUNK, cols]
        b0 = c * seqs_per_chunk
        return ref[b0:b0 + seqs_per_chunk, row_off:row_off + seq, cols]

    def conv_chunk(col, c):
        cols = slice(col, col + HEAD_DIM)
        acc = None
        for s in range(GDN_CONV):
            term = chunk_rows(xe_ref, c, SUBLANES - s, cols) * cw_ref[GDN_CONV - 1 - s:GDN_CONV - s, cols]
            acc = term if acc is None else acc + term
        return _silu(acc).reshape(CHUNK, HEAD_DIM)

    ri = lax.broadcasted_iota(jnp.int32, (CHUNK, CHUNK), 0)
    ci = lax.broadcasted_iota(jnp.int32, (CHUNK, CHUNK), 1)
    same = (ri >> gshift) == (ci >> gshift)
    m_incl = same & (ri >= ci)
    m_strict = same & (ri > ci)
    f_incl = m_incl.astype(jnp.float32)
    f_same = same.astype(jnp.float32)
    lane = lax.broadcasted_iota(jnp.int32, (CHUNK, LANES), 1)
    row_in_chunk = lax.broadcasted_iota(jnp.int32, (CHUNK, LANES), 0)

    for c in range(n_chunks):
        r0 = c * CHUNK
        bac = chunk_rows(ba_ref, c, 0, slice(0, LANES)).reshape(CHUNK, LANES)
        beta_all = jax.nn.sigmoid(bac)
        sp_in = bac + dtb_ref[...]
        softplus = jnp.maximum(sp_in, 0.0) + jnp.log1p(jnp.exp(-jnp.abs(sp_in)))
        g_all = -jnp.exp(alog_ref[...]) * softplus
        if pad_rows and c == 0:
            valid = row_in_chunk >= pad_rows
            beta_all = jnp.where(valid, beta_all, 0.0)
            g_all = jnp.where(valid, g_all, 0.0)
        g_all = jnp.where((lane >= GDN_HEADS) & (lane < 2 * GDN_HEADS), g_all, 0.0)
        gc_col = _dot_exact(f_incl, g_all)
        gtot_col = _dot_exact(f_same, g_all)
        gc_row = _dot_exact(g_all.T, f_incl, _NT)

        for h0 in range(0, GDN_HEADS, head_group):
            heads = range(h0, h0 + head_group)
            q, k, v, lm, qk, rhs, egc, kd_t, gl = [], [], [], [], [], [], [], [], []
            for h in heads:
                qh = conv_chunk(h * HEAD_DIM, c)
                kh = conv_chunk(GDN_WIDTH + h * HEAD_DIM, c)
                vh = conv_chunk(2 * GDN_WIDTH + h * HEAD_DIM, c)
                qh = qh * lax.rsqrt(jnp.sum(qh * qh, -1, keepdims=True) + EPS) * (HEAD_DIM ** -0.5)
                kh = kh * lax.rsqrt(jnp.sum(kh * kh, -1, keepdims=True) + EPS)
                gcc = gc_col[:, GDN_HEADS + h:GDN_HEADS + h + 1]
                gcr = gc_row[GDN_HEADS + h:GDN_HEADS + h + 1, :]
                gtc = gtot_col[:, GDN_HEADS + h:GDN_HEADS + h + 1]
                beta = beta_all[:, h:h + 1]
                decay = jnp.exp(jnp.where(m_incl, gcc - gcr, NEG))
                kb = kh * beta
                e = jnp.exp(gcc)
                lm.append(jnp.where(m_strict, _dot_nt(kb, kh) * decay, 0.0))
                qk.append(_dot_nt(qh, kh) * decay)
                rhs.append(jnp.concatenate([vh * beta, kb * e], axis=1))
                egc.append(e)
                kd_t.append((kh * jnp.exp(gtc - gcc)).T)
                gl.append(jnp.exp(gtc))
                q.append(qh)
            sol = [_dot3(t, r) for t, r in zip(_tri_inverse(lm, ri, ci), rhs)]

            state = lambda h, b: s_ref[h] if carry else s0_ref[c * n_groups + b, h]
            ws, qs = [], []
            for i, h in enumerate(heads):
                w = sol[i][:, HEAD_DIM:]
                qg = q[i] * egc[i]
                ws_parts, qs_parts = [], []
                for b in range(n_groups):
                    g0 = b * group
                    wq = jnp.concatenate([w[g0:g0 + group], qg[g0:g0 + group]], axis=0)
                    res = _dot(wq, state(h, b))
                    ws_parts.append(res[:group])
                    qs_parts.append(res[group:])
                ws.append(ws_parts[0] if n_groups == 1 else jnp.concatenate(ws_parts, axis=0))
                qs.append(qs_parts[0] if n_groups == 1 else jnp.concatenate(qs_parts, axis=0))
            v_new = [s[:, :HEAD_DIM] - w for s, w in zip(sol, ws)]
            o = [a + _dot(b, vn) for a, b, vn in zip(qs, qk, v_new)]
            for i, h in enumerate(heads):
                for b in range(n_groups):
                    g0 = b * group
                    kd_b = kd_t[i] if n_groups == 1 else jnp.where((ci >> gshift) == b, kd_t[i], 0.0)
                    st = state(h, b) * gl[i][g0:g0 + 1, :] + _dot(kd_b, v_new[i])
                    if carry:
                        s_ref[h] = st
                    else:
                        sout_ref[c * n_groups + b, h] = st
            for i, h in enumerate(heads):
                z = chunk_rows(x_ref, c, 0, slice(COL_Z + h * HEAD_DIM, COL_Z + (h + 1) * HEAD_DIM)).reshape(CHUNK, HEAD_DIM)
                y = o[i] * lax.rsqrt(jnp.mean(o[i] * o[i], -1, keepdims=True) + EPS) * gnw_ref[...] * _silu(z)
                o_ref[r0:r0 + CHUNK, h * HEAD_DIM:(h + 1) * HEAD_DIM] = y.astype(o_ref.dtype)

    if carry:
        xe_ref[:, 0:SUBLANES, :] = xe_ref[:, seq:seq + SUBLANES, :]

        @pl.when(step == pl.num_programs(0) - 1)
        def _():
            sout_ref[0] = s_ref[...]


def _gdn(x3, ba3, x_idx, hist, hist_idx, s0, cw, alog_row, dtb_row, gnw, *,
         n_steps, nb, seq, group, carry, pad_rows):
    rows = nb * seq
    if carry:
        state_spec = pl.BlockSpec((1, GDN_HEADS, HEAD_DIM, HEAD_DIM), lambda s: (0, 0, 0, 0))
        state_shape = (1, GDN_HEADS, HEAD_DIM, HEAD_DIM)
    else:
        n_states = rows // group
        state_spec = pl.BlockSpec((None, n_states, GDN_HEADS, HEAD_DIM, HEAD_DIM), lambda s: (0, s, 0, 0, 0))
        state_shape = (1, n_steps * n_states, GDN_HEADS, HEAD_DIM, HEAD_DIM)
    full = lambda shape: pl.BlockSpec(shape, lambda s: (0,) * len(shape))
    return pl.pallas_call(
        functools.partial(_gdn_kernel, nb=nb, seq=seq, group=group, carry=carry, pad_rows=pad_rows),
        grid=(n_steps,),
        in_specs=[
            pl.BlockSpec((nb, seq, GDN_COLS), x_idx),
            pl.BlockSpec((nb, seq, LANES), x_idx),
            pl.BlockSpec((nb, SUBLANES, GDN_QKV), hist_idx),
            state_spec,
            full((GDN_CONV, GDN_QKV)),
            full((1, LANES)),
            full((1, LANES)),
            full((1, HEAD_DIM)),
        ],
        out_specs=[
            pl.BlockSpec((rows, GDN_WIDTH), lambda s: (s, 0)),
            state_spec,
        ],
        out_shape=[
            jax.ShapeDtypeStruct((n_steps * rows, GDN_WIDTH), jnp.bfloat16),
            jax.ShapeDtypeStruct(state_shape, jnp.float32),
        ],
        scratch_shapes=[
            pltpu.VMEM((nb, SUBLANES + seq, GDN_QKV), jnp.float32),
            pltpu.VMEM((GDN_HEADS, HEAD_DIM, HEAD_DIM), jnp.float32),
        ],
        compiler_params=pltpu.CompilerParams(
            dimension_semantics=("arbitrary",), vmem_limit_bytes=VMEM_LIMIT),
        name="gdn_seq" if carry else "gdn_batch",
    )(x3, ba3, hist, s0, cw, alog_row, dtb_row, gnw)


def _t5_bucket_np(dist):
    n = np.maximum(dist, 0)
    exact = N_BUCKETS // 2
    large = exact + (np.log(np.maximum(n, 1).astype(np.float32) / exact)
                     / math.log(MAX_DISTANCE / exact) * (N_BUCKETS - exact)).astype(np.int32)
    return np.where(n < exact, n, np.minimum(large, N_BUCKETS - 1)).astype(np.int32)


def _bucket_ids(dist, valid):
    return np.where(valid, _t5_bucket_np(dist), -1).astype(np.int32)


def _bias_kernel(table_ref, *refs):
    n = len(refs) // 2
    for ids_ref, out_ref in zip(refs[:n], refs[n:]):
        ids = ids_ref[...]
        nq = ids.shape[0]
        for head in range(SWA_HEADS):
            def body(b, acc):
                return jnp.where(ids == b, table_ref[b, head], acc)
            acc = lax.fori_loop(0, N_BUCKETS, body, jnp.full(ids.shape, NEG, jnp.float32))
            kh, g = divmod(head, SWA_GROUP)
            out_ref[kh, g * nq:(g + 1) * nq, :] = acc


def _bias_tables(rel_table, id_arrays):
    out_shapes = [jax.ShapeDtypeStruct((SWA_KV_HEADS, SWA_GROUP * a.shape[0], a.shape[1]), jnp.float32)
                  for a in id_arrays]
    vmem = pl.BlockSpec(memory_space=pltpu.VMEM)
    return pl.pallas_call(
        _bias_kernel,
        in_specs=[pl.BlockSpec(memory_space=pltpu.SMEM)] + [vmem] * len(id_arrays),
        out_specs=[vmem] * len(id_arrays),
        out_shape=out_shapes,
        name="swa_bias",
    )(rel_table, *[jnp.asarray(a) for a in id_arrays])


def _attend(q, keys, values, biases, sink):
    scale = HEAD_DIM ** -0.5
    scores = [_dot_nt(q, k) * scale + b for k, b in zip(keys, biases)]
    m = sink
    for s in scores:
        m = jnp.maximum(m, jnp.max(s, axis=-1, keepdims=True))
    den = jnp.exp(sink - m)
    acc = None
    for s, v in zip(scores, values):
        p = jnp.exp(s - m)
        den = den + jnp.sum(p, axis=-1, keepdims=True)
        pv = _dot(p, v)
        acc = pv if acc is None else acc + pv
    return acc / den


def _swa_prompt_kernel(q_ref, kvc_ref, kvp_ref, kvm_ref, bcur_ref, bprev_ref, bm0_ref, bfar_ref,
                       sink_ref, o_ref):
    first = pl.program_id(0) == 0
    for kh in range(SWA_KV_HEADS):
        ks = slice(kh * HEAD_DIM, (kh + 1) * HEAD_DIM)
        vs = slice((SWA_KV_HEADS + kh) * HEAD_DIM, (SWA_KV_HEADS + kh + 1) * HEAD_DIM)
        q = jnp.concatenate(
            [q_ref[:, (kh * SWA_GROUP + g) * HEAD_DIM:(kh * SWA_GROUP + g + 1) * HEAD_DIM]
             for g in range(SWA_GROUP)], axis=0)
        b_prev = jnp.where(first, NEG, bprev_ref[kh])
        b_meta = jnp.where(first, bm0_ref[kh], bfar_ref[kh])
        o = _attend(q,
                    [kvc_ref[:, ks], kvp_ref[:, ks], kvm_ref[:, ks]],
                    [kvc_ref[:, vs], kvp_ref[:, vs], kvm_ref[:, vs]],
                    [bcur_ref[kh], b_prev, b_meta], sink_ref[kh])
        for g in range(SWA_GROUP):
            head = kh * SWA_GROUP + g
            o_ref[:, head * HEAD_DIM:(head + 1) * HEAD_DIM] = (
                o[g * WINDOW:(g + 1) * WINDOW].astype(o_ref.dtype))


def _swa_meta_kernel(q_ref, kv_ref, bias_ref, sink_ref, o_ref):
    for kh in range(SWA_KV_HEADS):
        ks = slice(kh * HEAD_DIM, (kh + 1) * HEAD_DIM)
        vs = slice((SWA_KV_HEADS + kh) * HEAD_DIM, (SWA_KV_HEADS + kh + 1) * HEAD_DIM)
        q = jnp.concatenate(
            [q_ref[:, (kh * SWA_GROUP + g) * HEAD_DIM:(kh * SWA_GROUP + g + 1) * HEAD_DIM]
             for g in range(SWA_GROUP)], axis=0)
        o = _attend(q, [kv_ref[:, ks]], [kv_ref[:, vs]], [bias_ref[kh]], sink_ref[kh])
        for g in range(SWA_GROUP):
            head = kh * SWA_GROUP + g
            o_ref[:, head * HEAD_DIM:(head + 1) * HEAD_DIM] = (
                o[g * N_META:(g + 1) * N_META].astype(o_ref.dtype))


def _swa_sample_kernel(q_ref, kvn_ref, win_ref, meta_ref, bwin_ref, bsmall_ref, sink_ref,
                       o_ref, wout_ref, *, nb, seq):
    wout_ref[:, 0:WINDOW - seq, :] = win_ref[:, seq:WINDOW, :]
    wout_ref[:, WINDOW - seq:WINDOW, :] = kvn_ref[...]
    pad = jnp.zeros((SMALL_KEYS - N_META - seq, HEAD_DIM), jnp.float32)
    outs = [[] for _ in range(SWA_HEADS)]
    for b in range(nb):
        for kh in range(SWA_KV_HEADS):
            ks = slice(kh * HEAD_DIM, (kh + 1) * HEAD_DIM)
            vs = slice((SWA_KV_HEADS + kh) * HEAD_DIM, (SWA_KV_HEADS + kh + 1) * HEAD_DIM)
            q = jnp.concatenate(
                [q_ref[b, :, (kh * SWA_GROUP + g) * HEAD_DIM:(kh * SWA_GROUP + g + 1) * HEAD_DIM]
                 for g in range(SWA_GROUP)], axis=0)
            k_small = jnp.concatenate([meta_ref[b, :, ks], kvn_ref[b, :, ks], pad], axis=0)
            v_small = jnp.concatenate([meta_ref[b, :, vs], kvn_ref[b, :, vs], pad], axis=0)
            o = _attend(q, [win_ref[b, :, ks], k_small], [win_ref[b, :, vs], v_small],
                        [bwin_ref[kh], bsmall_ref[kh]], sink_ref[kh])
            for g in range(SWA_GROUP):
                outs[kh * SWA_GROUP + g].append(o[g * seq:(g + 1) * seq])
    for head in range(SWA_HEADS):
        o_ref[:, head * HEAD_DIM:(head + 1) * HEAD_DIM] = (
            jnp.concatenate(outs[head], axis=0).astype(o_ref.dtype))


def _outproj_kernel(g_ref, s_ref, h_ref, wo_ref, nw_ref, o_ref):
    mix = (jnp.dot(g_ref[...], wo_ref[0:GDN_WIDTH, :], preferred_element_type=jnp.float32)
           + jnp.dot(s_ref[...], wo_ref[GDN_WIDTH:, :], preferred_element_type=jnp.float32))
    o_ref[...] = h_ref[...] + mix * _rms_scale(mix) * nw_ref[...]


def _outproj(g, s, h, wo, nw, *, tm):
    rows = h.shape[0]
    return pl.pallas_call(
        _outproj_kernel,
        grid=(rows // tm,),
        in_specs=[
            pl.BlockSpec((tm, GDN_WIDTH), lambda i: (i, 0)),
            pl.BlockSpec((tm, SWA_WIDTH), lambda i: (i, 0)),
            pl.BlockSpec((tm, D_MODEL), lambda i: (i, 0)),
            pl.BlockSpec((D_MODEL, D_MODEL), lambda i: (0, 0)),
            pl.BlockSpec((1, D_MODEL), lambda i: (0, 0)),
        ],
        out_specs=pl.BlockSpec((tm, D_MODEL), lambda i: (i, 0)),
        out_shape=jax.ShapeDtypeStruct((rows, D_MODEL), jnp.float32),
        compiler_params=pltpu.CompilerParams(
            dimension_semantics=("arbitrary",), vmem_limit_bytes=VMEM_LIMIT),
        name="outproj",
    )(g, s, h, wo, nw)


def _ffn_kernel(*refs, batch, tm, tf):
    if batch:
        (h_ref, nw_pre_ref, wg_ref, wu_ref, cw_ref, wd_ref, nw_post_ref, hist_ref,
         y_ref, graw_ref, xn_ref, acc_ref, xe_ref) = refs
    else:
        (h_ref, nw_pre_ref, wg_ref, wu_ref, cw_ref, wd_ref, nw_post_ref, hist_ref,
         y_ref, graw_ref, xn_ref, acc_ref, xe_ref, carry_ref) = refs
    i = pl.program_id(0)
    j = pl.program_id(1)
    rb = tm // FFN_ROW_BLOCKS

    @pl.when(j == 0)
    def _():
        h = h_ref[...]
        xn_ref[...] = (h * _rms_scale(h) * nw_pre_ref[...]).astype(jnp.bfloat16)
        acc_ref[...] = jnp.zeros_like(acc_ref)

    if batch:
        xe_ref[:, 0:SUBLANES, :] = hist_ref[...]
    else:
        @pl.when(i == 0)
        def _():
            carry_ref[pl.ds(j, 1)] = hist_ref[...].reshape(1, SUBLANES, tf)
        xe_ref[:, 0:SUBLANES, :] = carry_ref[pl.ds(j, 1)]

    def gate_up(r):
        xn = xn_ref[r * rb:(r + 1) * rb, :]
        return (jnp.dot(xn, wg_ref[...], preferred_element_type=jnp.float32),
                jnp.dot(xn, wu_ref[...], preferred_element_type=jnp.float32))

    nxt = gate_up(0)
    for r in range(FFN_ROW_BLOCKS):
        rows = slice(r * rb, (r + 1) * rb)
        gate, up = nxt
        if r + 1 < FFN_ROW_BLOCKS:
            nxt = gate_up(r + 1)
        if batch:
            seqs = slice(r * rb // SUBLANES, (r + 1) * rb // SUBLANES)
            graw_ref[rows, :] = gate
            xe_ref[seqs, SUBLANES:2 * SUBLANES, :] = gate.reshape(rb // SUBLANES, SUBLANES, tf)
            taps = [xe_ref[seqs, SUBLANES - s:2 * SUBLANES - s, :] for s in range(FFN_CONV)]
        else:
            base = SUBLANES + r * rb
            xe_ref[:, base:base + rb, :] = gate.reshape(1, rb, tf)
            taps = [xe_ref[:, base - s:base - s + rb, :] for s in range(FFN_CONV)]
        conv = None
        for s, tap in enumerate(taps):
            term = tap * cw_ref[FFN_CONV - 1 - s:FFN_CONV - s, :]
            conv = term if conv is None else conv + term
        act = (_silu(conv.reshape(rb, tf)) * up).astype(jnp.bfloat16)
        acc_ref[rows, :] += jnp.dot(act, wd_ref[...], preferred_element_type=jnp.float32)
    if not batch:
        tail = xe_ref[:, tm:tm + SUBLANES, :]
        carry_ref[pl.ds(j, 1)] = tail
        graw_ref[...] = tail.reshape(SUBLANES, tf)

    @pl.when(j == pl.num_programs(1) - 1)
    def _():
        y = acc_ref[...]
        y_ref[...] = h_ref[...] + y * _rms_scale(y) * nw_post_ref[...]


def _ffn(h, nw_pre, wg, wu, cw, wd, nw_post, hist, *, batch, tm, tf):
    rows = h.shape[0]
    nj = D_FF // tf
    in_specs = [
        pl.BlockSpec((tm, D_MODEL), lambda i, j: (i, 0)),
        pl.BlockSpec((1, D_MODEL), lambda i, j: (0, 0)),
        pl.BlockSpec((D_MODEL, tf), lambda i, j: (0, j)),
        pl.BlockSpec((D_MODEL, tf), lambda i, j: (0, j)),
        pl.BlockSpec((FFN_CONV, tf), lambda i, j: (0, j)),
        pl.BlockSpec((tf, D_MODEL), lambda i, j: (j, 0)),
        pl.BlockSpec((1, D_MODEL), lambda i, j: (0, 0)),
    ]
    args = [h, nw_pre, wg, wu, cw, wd, nw_post, hist]
    scratch = [pltpu.VMEM((tm, D_MODEL), jnp.bfloat16), pltpu.VMEM((tm, D_MODEL), jnp.float32)]
    if batch:
        in_specs.append(pl.BlockSpec((tm // SUBLANES, SUBLANES, tf), lambda i, j: (i, 0, j)))
        graw_spec = pl.BlockSpec((tm, tf), lambda i, j: (i, j))
        graw_shape = jax.ShapeDtypeStruct((rows, D_FF), jnp.float32)
        scratch.append(pltpu.VMEM((tm // SUBLANES, 2 * SUBLANES, tf), jnp.float32))
    else:
        in_specs.append(pl.BlockSpec((SUBLANES, tf), lambda i, j: (0, j)))
        graw_spec = pl.BlockSpec((SUBLANES, tf), lambda i, j: (i, j))
        graw_shape = jax.ShapeDtypeStruct((rows // tm * SUBLANES, D_FF), jnp.float32)
        scratch.append(pltpu.VMEM((1, SUBLANES + tm, tf), jnp.float32))
        scratch.append(pltpu.VMEM((nj, SUBLANES, tf), jnp.float32))
    return pl.pallas_call(
        functools.partial(_ffn_kernel, batch=batch, tm=tm, tf=tf),
        grid=(rows // tm, nj),
        in_specs=in_specs,
        out_specs=[pl.BlockSpec((tm, D_MODEL), lambda i, j: (i, 0)), graw_spec],
        out_shape=[jax.ShapeDtypeStruct((rows, D_MODEL), jnp.float32), graw_shape],
        scratch_shapes=scratch,
        compiler_params=pltpu.CompilerParams(
            dimension_semantics=("arbitrary", "arbitrary"), vmem_limit_bytes=VMEM_LIMIT),
        name="ffn_batch" if batch else "ffn_seq",
    )(*args)


def kernel(x_prompt, x_sample, cache_swa_meta_kv, cache_swa_window_kv, state_gdn_conv, state_gdn, state_ffn_conv, meta_tokens, rel_bias_table, w_in, gdn_conv_w, gdn_a_log, gdn_dt_bias, gdn_norm_w, swa_sinks, w_out, norm_mix_pre, norm_mix_post, norm_ffn_pre, norm_ffn_post, ffn_w_gate, ffn_w_up, ffn_conv_w, ffn_w_down):
    f32, bf16 = jnp.float32, jnp.bfloat16
    seq = x_prompt.shape[1]
    dec_b, dec_t = x_sample.shape[0], x_sample.shape[1]
    n_dec = dec_b * dec_t
    assert x_prompt.shape[0] == 1 and seq % CHUNK == 0 and dec_t == SUBLANES and n_dec % CHUNK == 0

    wi = w_in[0]
    n_ba = 2 * GDN_HEADS
    w_in_p = jnp.concatenate([wi[:, :COL_SQ], wi[:, COL_SQ + n_ba:]], axis=1).astype(bf16)
    w_ba = jnp.pad(wi[:, COL_SQ:COL_SQ + n_ba], ((0, 0), (0, LANES - n_ba))).astype(bf16)
    wo = w_out[0].astype(bf16)
    wg = ffn_w_gate[0].astype(bf16)
    wu = ffn_w_up[0].astype(bf16)
    wd = ffn_w_down[0].astype(bf16)
    lane_pad = lambda v: jnp.pad(v.reshape(1, GDN_HEADS), ((0, 0), (GDN_HEADS, LANES - 2 * GDN_HEADS)))
    alog_row = lane_pad(gdn_a_log[0])
    dtb_row = lane_pad(gdn_dt_bias[0])
    gnw = gdn_norm_w[0].reshape(1, HEAD_DIM)

    pad_rows = CHUNK - N_META
    n_small = n_dec + CHUNK
    x_big = x_prompt.reshape(seq, D_MODEL)
    x_small = jnp.concatenate(
        [x_sample.reshape(n_dec, D_MODEL), jnp.zeros((pad_rows, D_MODEL), f32), meta_tokens.astype(f32)], axis=0)
    nw = norm_mix_pre[0].reshape(1, D_MODEL)
    proj_big, ba_big = _inproj(x_big, nw, w_in_p, w_ba, tm=1024, tn=512, row_chunk=128)
    proj_small, ba_small = _inproj(x_small, nw, w_in_p, w_ba, tm=n_small, tn=512, row_chunk=128)

    cw = gdn_conv_w[0]
    small_chunks = proj_small.reshape(n_small // CHUNK, CHUNK, PROJ_COLS)
    small_groups = proj_small.reshape(n_small // SUBLANES, SUBLANES, PROJ_COLS)
    last_chunk = n_small // CHUNK - 1
    gdn_meta, s_meta = _gdn(
        small_chunks, ba_small.reshape(n_small // CHUNK, CHUNK, LANES), lambda s: (last_chunk, 0, 0),
        jnp.zeros((1, SUBLANES, GDN_QKV), f32), lambda s: (0, 0, 0),
        jnp.zeros((1, GDN_HEADS, HEAD_DIM, HEAD_DIM), f32), cw, alog_row, dtb_row, gnw,
        n_steps=1, nb=1, seq=CHUNK, group=CHUNK, carry=True, pad_rows=pad_rows)
    last_group = n_small // SUBLANES - 1
    gdn_big, s_prompt = _gdn(
        proj_big.reshape(1, seq, PROJ_COLS), ba_big.reshape(1, seq, LANES), lambda s: (0, s, 0),
        small_groups, lambda s: (last_group, 0, 0),
        s_meta, cw, alog_row, dtb_row, gnw,
        n_steps=seq // CHUNK, nb=1, seq=CHUNK, group=CHUNK, carry=True, pad_rows=0)
    hist_gdn = jnp.pad(state_gdn_conv[0], ((0, 0), (SUBLANES - (GDN_CONV - 1), 0), (0, 0)))
    nb_gdn = CHUNK // dec_t
    gdn_small, s_sample = _gdn(
        small_groups, ba_small.reshape(n_small // SUBLANES, SUBLANES, LANES), lambda s: (s, 0, 0),
        hist_gdn, lambda s: (s, 0, 0),
        state_gdn, cw, alog_row, dtb_row, gnw,
        n_steps=dec_b // nb_gdn, nb=nb_gdn, seq=dec_t, group=dec_t, carry=False, pad_rows=0)

    qi = np.arange(WINDOW)[:, None]
    kj = np.arange(WINDOW)[None, :]
    mi = np.arange(N_META)[None, :]
    ti = np.arange(dec_t)[:, None]
    small_keys = np.arange(SMALL_KEYS)[None, :]
    new_keys = small_keys - N_META
    id_arrays = [
        _bucket_ids(qi - kj, qi >= kj),
        _bucket_ids(qi - kj + WINDOW, kj > qi),
        _bucket_ids(qi + N_META - mi, np.ones((WINDOW, N_META), bool)),
        _bucket_ids(qi + N_META - mi + WINDOW, np.ones((WINDOW, N_META), bool)),
        _bucket_ids(ti + WINDOW - kj, kj > ti),
        _bucket_ids(np.where(new_keys < 0, PAST_LEN + ti - small_keys, ti - new_keys),
                    (new_keys < 0) | ((new_keys <= ti) & (new_keys < dec_t))),
        _bucket_ids(mi.T - mi, mi.T >= mi),
    ]
    bcur, bprev, bm0, bfar, bwin, bsmall, bmm = _bias_tables(rel_bias_table, id_arrays)
    sink_rows = lambda q: jnp.repeat(swa_sinks[0].reshape(SWA_KV_HEADS, SWA_GROUP), q, axis=1)[..., None]

    sq_blk = COL_SQ // SWA_WIDTH
    kv_blk = COL_KV // KV_WIDTH
    meta_blk = (n_small - N_META) // N_META
    full3 = lambda a: pl.BlockSpec(a.shape, lambda j: (0, 0, 0))
    sink_p = sink_rows(WINDOW)
    swa_big = pl.pallas_call(
        _swa_prompt_kernel,
        grid=(seq // WINDOW,),
        in_specs=[
            pl.BlockSpec((WINDOW, SWA_WIDTH), lambda j: (j, sq_blk)),
            pl.BlockSpec((WINDOW, KV_WIDTH), lambda j: (j, kv_blk)),
            pl.BlockSpec((WINDOW, KV_WIDTH), lambda j: (jnp.maximum(j - 1, 0), kv_blk)),
            pl.BlockSpec((N_META, KV_WIDTH), lambda j: (meta_blk, kv_blk)),
            full3(bcur), full3(bprev), full3(bm0), full3(bfar), full3(sink_p),
        ],
        out_specs=pl.BlockSpec((WINDOW, SWA_WIDTH), lambda j: (j, 0)),
        out_shape=jax.ShapeDtypeStruct((seq, SWA_WIDTH), bf16),
        compiler_params=pltpu.CompilerParams(
            dimension_semantics=("arbitrary",), vmem_limit_bytes=VMEM_LIMIT),
        name="swa_prompt",
    )(proj_big, proj_big, proj_big, proj_small, bcur, bprev, bm0, bfar, sink_p)

    nb_swa = 8
    sink_s = sink_rows(dec_t)
    win = cache_swa_window_kv[0].reshape(dec_b, WINDOW, KV_WIDTH)
    meta_kv = cache_swa_meta_kv[0].reshape(dec_b, N_META, KV_WIDTH)
    swa_small, win_new = pl.pallas_call(
        functools.partial(_swa_sample_kernel, nb=nb_swa, seq=dec_t),
        grid=(dec_b // nb_swa,),
        in_specs=[
            pl.BlockSpec((nb_swa, dec_t, SWA_WIDTH), lambda j: (j, 0, sq_blk)),
            pl.BlockSpec((nb_swa, dec_t, KV_WIDTH), lambda j: (j, 0, kv_blk)),
            pl.BlockSpec((nb_swa, WINDOW, KV_WIDTH), lambda j: (j, 0, 0)),
            pl.BlockSpec((nb_swa, N_META, KV_WIDTH), lambda j: (j, 0, 0)),
            full3(bwin), full3(bsmall), full3(sink_s),
        ],
        out_specs=[pl.BlockSpec((nb_swa * dec_t, SWA_WIDTH), lambda j: (j, 0)),
                   pl.BlockSpec((nb_swa, WINDOW, KV_WIDTH), lambda j: (j, 0, 0))],
        out_shape=[jax.ShapeDtypeStruct((n_dec, SWA_WIDTH), bf16),
                   jax.ShapeDtypeStruct((dec_b, WINDOW, KV_WIDTH), f32)],
        compiler_params=pltpu.CompilerParams(
            dimension_semantics=("arbitrary",), vmem_limit_bytes=VMEM_LIMIT),
        name="swa_sample",
    )(small_groups, small_groups, win, meta_kv, bwin, bsmall, sink_s)

    sink_m = sink_rows(N_META)
    swa_meta = pl.pallas_call(
        _swa_meta_kernel,
        grid=(1,),
        in_specs=[
            pl.BlockSpec((N_META, SWA_WIDTH), lambda j: (meta_blk, sq_blk)),
            pl.BlockSpec((N_META, KV_WIDTH), lambda j: (meta_blk, kv_blk)),
            full3(bmm), full3(sink_m),
        ],
        out_specs=pl.BlockSpec((N_META, SWA_WIDTH), lambda j: (0, 0)),
        out_shape=jax.ShapeDtypeStruct((N_META, SWA_WIDTH), bf16),
        name="swa_meta",
    )(proj_small, proj_small, bmm, sink_m)

    nw_post = norm_mix_post[0].reshape(1, D_MODEL)
    nf_pre = norm_ffn_pre[0].reshape(1, D_MODEL)
    nf_post = norm_ffn_post[0].reshape(1, D_MODEL)
    fcw = ffn_conv_w[0]
    gdn_small_all = jnp.concatenate([gdn_small, gdn_meta], axis=0)
    swa_small_all = jnp.concatenate([swa_small, jnp.zeros((pad_rows, SWA_WIDTH), bf16), swa_meta], axis=0)
    h_small = _outproj(gdn_small_all, swa_small_all, x_small, wo, nw_post, tm=n_small // 2)
    hist_ffn = jnp.pad(state_ffn_conv[0], ((0, CHUNK // SUBLANES), (SUBLANES - (FFN_CONV - 1), 0), (0, 0)))
    y_small, g_small = _ffn(h_small, nf_pre, wg, wu, fcw, wd, nf_post, hist_ffn,
                            batch=True, tm=n_small // 2, tf=512)
    h_big = _outproj(gdn_big, swa_big, x_big, wo, nw_post, tm=512)
    y_big, g_tail = _ffn(h_big, nf_pre, wg, wu, fcw, wd, nf_post, g_small[n_small - SUBLANES:],
                         batch=False, tm=512, tf=512)

    kv_shape = lambda n: (1, n, 2, SWA_KV_HEADS, HEAD_DIM)
    kv_small = proj_small[:, COL_KV:COL_KV + KV_WIDTH]
    y_prompt = y_big.reshape(1, seq, D_MODEL)
    y_sample = y_small[:n_dec].reshape(dec_b, dec_t, D_MODEL)
    p_meta_kv = kv_small[n_small - N_META:].reshape(kv_shape(N_META))[None]
    p_window_kv = proj_big[seq - WINDOW:, COL_KV:COL_KV + KV_WIDTH].reshape(kv_shape(WINDOW))[None]
    p_gdn_conv = proj_big[seq - (GDN_CONV - 1):, :GDN_QKV].reshape(1, 1, GDN_CONV - 1, GDN_QKV)
    p_gdn_state = s_prompt[None]
    p_ffn_conv = g_tail[g_tail.shape[0] - (FFN_CONV - 1):].reshape(1, 1, FFN_CONV - 1, D_FF)
    s_window_kv = win_new.reshape(1, dec_b, WINDOW, 2, SWA_KV_HEADS, HEAD_DIM)
    s_gdn_conv = proj_small[:n_dec, :GDN_QKV].reshape(dec_b, dec_t, GDN_QKV)[:, dec_t - (GDN_CONV - 1):][None]
    s_gdn_state = s_sample
    s_ffn_conv = g_small[:n_dec].reshape(dec_b, dec_t, D_FF)[:, dec_t - (FFN_CONV - 1):][None]
    return (y_prompt, y_sample, p_meta_kv, p_window_kv, p_gdn_conv, p_gdn_state, p_ffn_conv,
            s_window_kv, s_gdn_conv, s_gdn_state, s_ffn_conv)
```

```python
import functools
import math

import numpy as np
import jax
import jax.numpy as jnp
from jax import lax
from jax.experimental import pallas as pl
from jax.experimental.pallas import tpu as pltpu

D_MODEL = 2048
HEAD_DIM = 128
GDN_HEADS = 8
GDN_WIDTH = GDN_HEADS * HEAD_DIM
GDN_QKV = 3 * GDN_WIDTH
SWA_HEADS = 8
SWA_KV_HEADS = 2
SWA_GROUP = SWA_HEADS // SWA_KV_HEADS
SWA_WIDTH = SWA_HEADS * HEAD_DIM
KV_WIDTH = 2 * SWA_KV_HEADS * HEAD_DIM
WINDOW = 128
N_META = 16
N_BUCKETS = 32
MAX_DISTANCE = 128
GDN_CONV = 4
FFN_CONV = 3
D_FF = 5632
EPS = 1e-6
PAST_LEN = 16384

SUBLANES = 8
LANES = 128

CHUNK = 128
INV_BASE = 16
GDN_HEAD_GROUP = 8
FFN_ROW_BLOCKS = 4

COL_Z = GDN_QKV
COL_SQ = COL_Z + GDN_WIDTH
COL_KV = COL_SQ + SWA_WIDTH
PROJ_COLS = COL_KV + KV_WIDTH
GDN_COLS = COL_SQ

SMALL_KEYS = 32
NEG = -1e30
VMEM_LIMIT = 56 * 1024 * 1024

_NT = (((1,), (1,)), ((), ()))


def _dot(a, b):
    return jnp.dot(a.astype(jnp.bfloat16), b.astype(jnp.bfloat16), preferred_element_type=jnp.float32)


def _dot_nt(a, b):
    return lax.dot_general(a.astype(jnp.bfloat16), b.astype(jnp.bfloat16), _NT,
                           preferred_element_type=jnp.float32)


def _split(a):
    hi = a.astype(jnp.bfloat16)
    lo = (a - hi.astype(jnp.float32)).astype(jnp.bfloat16)
    return hi, lo


def _dot3(a, b):
    ah, al = _split(a)
    bh, bl = _split(b)
    d = functools.partial(jnp.dot, preferred_element_type=jnp.float32)
    return d(ah, bh) + (d(ah, bl) + d(al, bh))


_dot_inv = _dot


def _dot_exact(a, b, dims=None):
    if dims is None:
        return jnp.dot(a, b, precision=lax.Precision.HIGHEST, preferred_element_type=jnp.float32)
    return lax.dot_general(a, b, dims, precision=lax.Precision.HIGHEST,
                           preferred_element_type=jnp.float32)


def _rms_scale(x):
    return lax.rsqrt(jnp.mean(x * x, axis=-1, keepdims=True) + EPS)


def _silu(x):
    return x * jax.nn.sigmoid(x)


def _inproj_kernel(x_ref, nw_ref, w_ref, wba_ref, o_ref, ba_ref, xn_ref, *, row_chunk):
    @pl.when(pl.program_id(1) == 0)
    def _():
        def body(c, carry):
            rows = pl.ds(pl.multiple_of(c * row_chunk, row_chunk), row_chunk)
            x = x_ref[rows, :]
            xn_ref[rows, :] = (x * _rms_scale(x) * nw_ref[...]).astype(jnp.bfloat16)
            return carry
        lax.fori_loop(0, x_ref.shape[0] // row_chunk, body, 0)
        ba_ref[...] = jnp.dot(xn_ref[...], wba_ref[...], preferred_element_type=jnp.float32)

    o_ref[...] = jnp.dot(xn_ref[...], w_ref[...], preferred_element_type=jnp.float32)


def _inproj(x, nw, w, wba, *, tm, tn, row_chunk):
    rows = x.shape[0]
    return pl.pallas_call(
        functools.partial(_inproj_kernel, row_chunk=row_chunk),
        grid=(rows // tm, PROJ_COLS // tn),
        in_specs=[
            pl.BlockSpec((tm, D_MODEL), lambda i, j: (i, 0)),
            pl.BlockSpec((1, D_MODEL), lambda i, j: (0, 0)),
            pl.BlockSpec((D_MODEL, tn), lambda i, j: (0, j)),
            pl.BlockSpec((D_MODEL, LANES), lambda i, j: (0, 0)),
        ],
        out_specs=[pl.BlockSpec((tm, tn), lambda i, j: (i, j)),
                   pl.BlockSpec((tm, LANES), lambda i, j: (i, 0))],
        out_shape=[jax.ShapeDtypeStruct((rows, PROJ_COLS), jnp.float32),
                   jax.ShapeDtypeStruct((rows, LANES), jnp.float32)],
        scratch_shapes=[pltpu.VMEM((tm, D_MODEL), jnp.bfloat16)],
        compiler_params=pltpu.CompilerParams(
            dimension_semantics=("arbitrary", "arbitrary"), vmem_limit_bytes=VMEM_LIMIT),
        name="inproj",
    )(x, nw, w, wba)


def _tri_inverse(lms, ri, ci):
    shift = INV_BASE.bit_length() - 1
    eye = (ri == ci).astype(jnp.float32)
    in_block = (ri >> shift) == (ci >> shift)
    ps = [jnp.where(in_block, lm, 0.0) for lm in lms]
    ts = [eye - p for p in ps]
    for _ in range(shift - 1):
        ps = [_dot_inv(p, p) for p in ps]
        ts = [t + _dot_inv(t, p) for t, p in zip(ts, ps)]
    size = INV_BASE
    while size < CHUNK:
        shift += 1
        in_pair = (ri >> shift) == (ci >> shift)
        off_mask = in_pair & jnp.logical_not(in_block)
        tos = [_dot_inv(t, jnp.where(off_mask, lm, 0.0)) for t, lm in zip(ts, lms)]
        ts = [t - _dot_inv(to, t) for t, to in zip(ts, tos)]
        in_block = in_pair
        size *= 2
    return ts


def _gdn_kernel(x_ref, ba_ref, hist_ref, s0_ref, cw_ref, alog_ref, dtb_ref, gnw_ref,
                o_ref, sout_ref, xe_ref, s_ref, *, nb, seq, group, carry, pad_rows):
    head_group = GDN_HEAD_GROUP
    step = pl.program_id(0)
    rows = nb * seq
    n_chunks = rows // CHUNK
    n_groups = CHUNK // group
    gshift = group.bit_length() - 1

    if carry:
        @pl.when(step == 0)
        def _():
            xe_ref[:, 0:SUBLANES, :] = hist_ref[...]
            s_ref[...] = s0_ref[0]
    else:
        xe_ref[:, 0:SUBLANES, :] = hist_ref[...]
    xe_ref[:, SUBLANES:SUBLANES + seq, :] = x_ref[:, :, 0:GDN_QKV]

    seq_rows = min(seq, CHUNK)
    seqs_per_chunk = CHUNK // seq_rows

    def chunk_rows(ref, c, row_off, cols):
        if seq >= CHUNK:
            start = row_off + c * CHUNK
            return ref[0:1, start:start + CHUNK, cols]
        b0 = c * seqs_per_chunk
        return ref[b0:b0 + seqs_per_chunk, row_off:row_off + seq, cols]

    def conv_chunk(col, c):
        cols = slice(col, col + HEAD_DIM)
        acc = None
        for s in range(GDN_CONV):
            term = chunk_rows(xe_ref, c, SUBLANES - s, cols) * cw_ref[GDN_CONV - 1 - s:GDN_CONV - s, cols]
            acc = term if acc is None else acc + term
        return _silu(acc).reshape(CHUNK, HEAD_DIM)

    ri = lax.broadcasted_iota(jnp.int32, (CHUNK, CHUNK), 0)
    ci = lax.broadcasted_iota(jnp.int32, (CHUNK, CHUNK), 1)
    same = (ri >> gshift) == (ci >> gshift)
    m_incl = same & (ri >= ci)
    m_strict = same & (ri > ci)
    f_incl = m_incl.astype(jnp.float32)
    f_same = same.astype(jnp.float32)
    lane = lax.broadcasted_iota(jnp.int32, (CHUNK, LANES), 1)
    row_in_chunk = lax.broadcasted_iota(jnp.int32, (CHUNK, LANES), 0)

    for c in range(n_chunks):
        r0 = c * CHUNK
        bac = chunk_rows(ba_ref, c, 0, slice(0, LANES)).reshape(CHUNK, LANES)
        beta_all = jax.nn.sigmoid(bac)
        sp_in = bac + dtb_ref[...]
        softplus = jnp.maximum(sp_in, 0.0) + jnp.log1p(jnp.exp(-jnp.abs(sp_in)))
        g_all = -jnp.exp(alog_ref[...]) * softplus
        if pad_rows and c == 0:
            valid = row_in_chunk >= pad_rows
            beta_all = jnp.where(valid, beta_all, 0.0)
            g_all = jnp.where(valid, g_all, 0.0)
        g_all = jnp.where((lane >= GDN_HEADS) & (lane < 2 * GDN_HEADS), g_all, 0.0)
        gc_col = _dot_exact(f_incl, g_all)
        gtot_col = _dot_exact(f_same, g_all)
        gc_row = _dot_exact(g_all.T, f_incl, _NT)

        for h0 in range(0, GDN_HEADS, head_group):
            heads = range(h0, h0 + head_group)
            q, k, v, lm, qk, rhs, egc, kd_t, gl = [], [], [], [], [], [], [], [], []
            for h in heads:
                qh = conv_chunk(h * HEAD_DIM, c)
                kh = conv_chunk(GDN_WIDTH + h * HEAD_DIM, c)
                vh = conv_chunk(2 * GDN_WIDTH + h * HEAD_DIM, c)
                qh = qh * lax.rsqrt(jnp.sum(qh * qh, -1, keepdims=True) + EPS) * (HEAD_DIM ** -0.5)
                kh = kh * lax.rsqrt(jnp.sum(kh * kh, -1, keepdims=True) + EPS)
                gcc = gc_col[:, GDN_HEADS + h:GDN_HEADS + h + 1]
                gcr = gc_row[GDN_HEADS + h:GDN_HEADS + h + 1, :]
                gtc = gtot_col[:, GDN_HEADS + h:GDN_HEADS + h + 1]
                beta = beta_all[:, h:h + 1]
                decay = jnp.exp(jnp.where(m_incl, gcc - gcr, NEG))
                kb = kh * beta
                e = jnp.exp(gcc)
                lm.append(jnp.where(m_strict, _dot_nt(kb, kh) * decay, 0.0))
                qk.append(_dot_nt(qh, kh) * decay)
                rhs.append(jnp.concatenate([vh * beta, kb * e], axis=1))
                egc.append(e)
                kd_t.append((kh * jnp.exp(gtc - gcc)).T)
                gl.append(jnp.exp(gtc))
                q.append(qh)
            sol = [_dot_inv(t, r) for t, r in zip(_tri_inverse(lm, ri, ci), rhs)]

            state = lambda h, b: s_ref[h] if carry else s0_ref[c * n_groups + b, h]
            ws, qs = [], []
            for i, h in enumerate(heads):
                w = sol[i][:, HEAD_DIM:]
                qg = q[i] * egc[i]
                ws_parts, qs_parts = [], []
                for b in range(n_groups):
                    g0 = b * group
                    wq = jnp.concatenate([w[g0:g0 + group], qg[g0:g0 + group]], axis=0)
                    res = _dot(wq, state(h, b))
                    ws_parts.append(res[:group])
                    qs_parts.append(res[group:])
                ws.append(ws_parts[0] if n_groups == 1 else jnp.concatenate(ws_parts, axis=0))
                qs.append(qs_parts[0] if n_groups == 1 else jnp.concatenate(qs_parts, axis=0))
            v_new = [s[:, :HEAD_DIM] - w for s, w in zip(sol, ws)]
            o = [a + _dot(b, vn) for a, b, vn in zip(qs, qk, v_new)]
            for i, h in enumerate(heads):
                for b in range(n_groups):
                    g0 = b * group
                    kd_b = kd_t[i] if n_groups == 1 else jnp.where((ci >> gshift) == b, kd_t[i], 0.0)
                    st = state(h, b) * gl[i][g0:g0 + 1, :] + _dot(kd_b, v_new[i])
                    if carry:
                        s_ref[h] = st
                    else:
                        sout_ref[c * n_groups + b, h] = st
            for i, h in enumerate(heads):
                z = chunk_rows(x_ref, c, 0, slice(COL_Z + h * HEAD_DIM, COL_Z + (h + 1) * HEAD_DIM)).reshape(CHUNK, HEAD_DIM)
                y = o[i] * lax.rsqrt(jnp.mean(o[i] * o[i], -1, keepdims=True) + EPS) * gnw_ref[...] * _silu(z)
                o_ref[r0:r0 + CHUNK, h * HEAD_DIM:(h + 1) * HEAD_DIM] = y.astype(o_ref.dtype)

    if carry:
        xe_ref[:, 0:SUBLANES, :] = xe_ref[:, seq:seq + SUBLANES, :]

        @pl.when(step == pl.num_programs(0) - 1)
        def _():
            sout_ref[0] = s_ref[...]


def _gdn(x3, ba3, x_idx, hist, hist_idx, s0, cw, alog_row, dtb_row, gnw, *,
         n_steps, nb, seq, group, carry, pad_rows):
    rows = nb * seq
    if carry:
        state_spec = pl.BlockSpec((1, GDN_HEADS, HEAD_DIM, HEAD_DIM), lambda s: (0, 0, 0, 0))
        state_shape = (1, GDN_HEADS, HEAD_DIM, HEAD_DIM)
    else:
        n_states = rows // group
        state_spec = pl.BlockSpec((None, n_states, GDN_HEADS, HEAD_DIM, HEAD_DIM), lambda s: (0, s, 0, 0, 0))
        state_shape = (1, n_steps * n_states, GDN_HEADS, HEAD_DIM, HEAD_DIM)
    full = lambda shape: pl.BlockSpec(shape, lambda s: (0,) * len(shape))
    return pl.pallas_call(
        functools.partial(_gdn_kernel, nb=nb, seq=seq, group=group, carry=carry, pad_rows=pad_rows),
        grid=(n_steps,),
        in_specs=[
            pl.BlockSpec((nb, seq, GDN_COLS), x_idx),
            pl.BlockSpec((nb, seq, LANES), x_idx),
            pl.BlockSpec((nb, SUBLANES, GDN_QKV), hist_idx),
            state_spec,
            full((GDN_CONV, GDN_QKV)),
            full((1, LANES)),
            full((1, LANES)),
            full((1, HEAD_DIM)),
        ],
        out_specs=[
            pl.BlockSpec((rows, GDN_WIDTH), lambda s: (s, 0)),
            state_spec,
        ],
        out_shape=[
            jax.ShapeDtypeStruct((n_steps * rows, GDN_WIDTH), jnp.bfloat16),
            jax.ShapeDtypeStruct(state_shape, jnp.float32),
        ],
        scratch_shapes=[
            pltpu.VMEM((nb, SUBLANES + seq, GDN_QKV), jnp.float32),
            pltpu.VMEM((GDN_HEADS, HEAD_DIM, HEAD_DIM), jnp.float32),
        ],
        compiler_params=pltpu.CompilerParams(
            dimension_semantics=("arbitrary",), vmem_limit_bytes=VMEM_LIMIT),
        name="gdn_seq" if carry else "gdn_batch",
    )(x3, ba3, hist, s0, cw, alog_row, dtb_row, gnw)


def _t5_bucket_np(dist):
    n = np.maximum(dist, 0)
    exact = N_BUCKETS // 2
    large = exact + (np.log(np.maximum(n, 1).astype(np.float32) / exact)
                     / math.log(MAX_DISTANCE / exact) * (N_BUCKETS - exact)).astype(np.int32)
    return np.where(n < exact, n, np.minimum(large, N_BUCKETS - 1)).astype(np.int32)


def _bucket_ids(dist, valid):
    return np.where(valid, _t5_bucket_np(dist), -1).astype(np.int32)


def _bias_kernel(table_ref, *refs):
    n = len(refs) // 2
    for ids_ref, out_ref in zip(refs[:n], refs[n:]):
        ids = ids_ref[...]
        nq = ids.shape[0]
        for head in range(SWA_HEADS):
            def body(b, acc):
                return jnp.where(ids == b, table_ref[b, head], acc)
            acc = lax.fori_loop(0, N_BUCKETS, body, jnp.full(ids.shape, NEG, jnp.float32))
            kh, g = divmod(head, SWA_GROUP)
            out_ref[kh, g * nq:(g + 1) * nq, :] = acc


def _bias_tables(rel_table, id_arrays):
    out_shapes = [jax.ShapeDtypeStruct((SWA_KV_HEADS, SWA_GROUP * a.shape[0], a.shape[1]), jnp.float32)
                  for a in id_arrays]
    vmem = pl.BlockSpec(memory_space=pltpu.VMEM)
    return pl.pallas_call(
        _bias_kernel,
        in_specs=[pl.BlockSpec(memory_space=pltpu.SMEM)] + [vmem] * len(id_arrays),
        out_specs=[vmem] * len(id_arrays),
        out_shape=out_shapes,
        name="swa_bias",
    )(rel_table, *[jnp.asarray(a) for a in id_arrays])


def _attend(q, keys, values, biases, sink):
    scale = HEAD_DIM ** -0.5
    scores = [_dot_nt(q, k) * scale + b for k, b in zip(keys, biases)]
    m = sink
    for s in scores:
        m = jnp.maximum(m, jnp.max(s, axis=-1, keepdims=True))
    den = jnp.exp(sink - m)
    acc = None
    for s, v in zip(scores, values):
        p = jnp.exp(s - m)
        den = den + jnp.sum(p, axis=-1, keepdims=True)
        pv = _dot(p, v)
        acc = pv if acc is None else acc + pv
    return acc / den


def _swa_prompt_kernel(q_ref, kvc_ref, kvp_ref, kvm_ref, bcur_ref, bprev_ref, bm0_ref, bfar_ref,
                       sink_ref, o_ref):
    first = pl.program_id(0) == 0
    for kh in range(SWA_KV_HEADS):
        ks = slice(kh * HEAD_DIM, (kh + 1) * HEAD_DIM)
        vs = slice((SWA_KV_HEADS + kh) * HEAD_DIM, (SWA_KV_HEADS + kh + 1) * HEAD_DIM)
        q = jnp.concatenate(
            [q_ref[:, (kh * SWA_GROUP + g) * HEAD_DIM:(kh * SWA_GROUP + g + 1) * HEAD_DIM]
             for g in range(SWA_GROUP)], axis=0)
        b_prev = jnp.where(first, NEG, bprev_ref[kh])
        b_meta = jnp.where(first, bm0_ref[kh], bfar_ref[kh])
        o = _attend(q,
                    [kvc_ref[:, ks], kvp_ref[:, ks], kvm_ref[:, ks]],
                    [kvc_ref[:, vs], kvp_ref[:, vs], kvm_ref[:, vs]],
                    [bcur_ref[kh], b_prev, b_meta], sink_ref[kh])
        for g in range(SWA_GROUP):
            head = kh * SWA_GROUP + g
            o_ref[:, head * HEAD_DIM:(head + 1) * HEAD_DIM] = (
                o[g * WINDOW:(g + 1) * WINDOW].astype(o_ref.dtype))


def _swa_meta_kernel(q_ref, kv_ref, bias_ref, sink_ref, o_ref):
    for kh in range(SWA_KV_HEADS):
        ks = slice(kh * HEAD_DIM, (kh + 1) * HEAD_DIM)
        vs = slice((SWA_KV_HEADS + kh) * HEAD_DIM, (SWA_KV_HEADS + kh + 1) * HEAD_DIM)
        q = jnp.concatenate(
            [q_ref[:, (kh * SWA_GROUP + g) * HEAD_DIM:(kh * SWA_GROUP + g + 1) * HEAD_DIM]
             for g in range(SWA_GROUP)], axis=0)
        o = _attend(q, [kv_ref[:, ks]], [kv_ref[:, vs]], [bias_ref[kh]], sink_ref[kh])
        for g in range(SWA_GROUP):
            head = kh * SWA_GROUP + g
            o_ref[:, head * HEAD_DIM:(head + 1) * HEAD_DIM] = (
                o[g * N_META:(g + 1) * N_META].astype(o_ref.dtype))


def _swa_sample_kernel(q_ref, kvn_ref, win_ref, meta_ref, bwin_ref, bsmall_ref, sink_ref,
                       o_ref, wout_ref, *, nb, seq):
    wout_ref[:, 0:WINDOW - seq, :] = win_ref[:, seq:WINDOW, :]
    wout_ref[:, WINDOW - seq:WINDOW, :] = kvn_ref[...]
    pad = jnp.zeros((SMALL_KEYS - N_META - seq, HEAD_DIM), jnp.float32)
    outs = [[] for _ in range(SWA_HEADS)]
    for b in range(nb):
        for kh in range(SWA_KV_HEADS):
            ks = slice(kh * HEAD_DIM, (kh + 1) * HEAD_DIM)
            vs = slice((SWA_KV_HEADS + kh) * HEAD_DIM, (SWA_KV_HEADS + kh + 1) * HEAD_DIM)
            q = jnp.concatenate(
                [q_ref[b, :, (kh * SWA_GROUP + g) * HEAD_DIM:(kh * SWA_GROUP + g + 1) * HEAD_DIM]
                 for g in range(SWA_GROUP)], axis=0)
            k_small = jnp.concatenate([meta_ref[b, :, ks], kvn_ref[b, :, ks], pad], axis=0)
            v_small = jnp.concatenate([meta_ref[b, :, vs], kvn_ref[b, :, vs], pad], axis=0)
            o = _attend(q, [win_ref[b, :, ks], k_small], [win_ref[b, :, vs], v_small],
                        [bwin_ref[kh], bsmall_ref[kh]], sink_ref[kh])
            for g in range(SWA_GROUP):
                outs[kh * SWA_GROUP + g].append(o[g * seq:(g + 1) * seq])
    for head in range(SWA_HEADS):
        o_ref[:, head * HEAD_DIM:(head + 1) * HEAD_DIM] = (
            jnp.concatenate(outs[head], axis=0).astype(o_ref.dtype))


def _outproj_kernel(g_ref, s_ref, h_ref, wo_ref, nw_ref, o_ref):
    mix = (jnp.dot(g_ref[...], wo_ref[0:GDN_WIDTH, :], preferred_element_type=jnp.float32)
           + jnp.dot(s_ref[...], wo_ref[GDN_WIDTH:, :], preferred_element_type=jnp.float32))
    o_ref[...] = h_ref[...] + mix * _rms_scale(mix) * nw_ref[...]


def _outproj(g, s, h, wo, nw, *, tm):
    rows = h.shape[0]
    return pl.pallas_call(
        _outproj_kernel,
        grid=(rows // tm,),
        in_specs=[
            pl.BlockSpec((tm, GDN_WIDTH), lambda i: (i, 0)),
            pl.BlockSpec((tm, SWA_WIDTH), lambda i: (i, 0)),
            pl.BlockSpec((tm, D_MODEL), lambda i: (i, 0)),
            pl.BlockSpec((D_MODEL, D_MODEL), lambda i: (0, 0)),
            pl.BlockSpec((1, D_MODEL), lambda i: (0, 0)),
        ],
        out_specs=pl.BlockSpec((tm, D_MODEL), lambda i: (i, 0)),
        out_shape=jax.ShapeDtypeStruct((rows, D_MODEL), jnp.float32),
        compiler_params=pltpu.CompilerParams(
            dimension_semantics=("arbitrary",), vmem_limit_bytes=VMEM_LIMIT),
        name="outproj",
    )(g, s, h, wo, nw)


def _ffn_kernel(*refs, batch, tm, tf):
    if batch:
        (h_ref, nw_pre_ref, wg_ref, wu_ref, cw_ref, wd_ref, nw_post_ref, hist_ref,
         y_ref, graw_ref, xn_ref, acc_ref, xe_ref) = refs
    else:
        (h_ref, nw_pre_ref, wg_ref, wu_ref, cw_ref, wd_ref, nw_post_ref, hist_ref,
         y_ref, graw_ref, xn_ref, acc_ref, xe_ref, carry_ref) = refs
    i = pl.program_id(0)
    j = pl.program_id(1)
    rb = tm // FFN_ROW_BLOCKS

    @pl.when(j == 0)
    def _():
        h = h_ref[...]
        xn_ref[...] = (h * _rms_scale(h) * nw_pre_ref[...]).astype(jnp.bfloat16)
        acc_ref[...] = jnp.zeros_like(acc_ref)

    if batch:
        xe_ref[:, 0:SUBLANES, :] = hist_ref[...]
    else:
        @pl.when(i == 0)
        def _():
            carry_ref[pl.ds(j, 1)] = hist_ref[...].reshape(1, SUBLANES, tf)
        xe_ref[:, 0:SUBLANES, :] = carry_ref[pl.ds(j, 1)]

    def gate_up(r):
        xn = xn_ref[r * rb:(r + 1) * rb, :]
        return (jnp.dot(xn, wg_ref[...], preferred_element_type=jnp.float32),
                jnp.dot(xn, wu_ref[...], preferred_element_type=jnp.float32))

    nxt = gate_up(0)
    for r in range(FFN_ROW_BLOCKS):
        rows = slice(r * rb, (r + 1) * rb)
        gate, up = nxt
        if r + 1 < FFN_ROW_BLOCKS:
            nxt = gate_up(r + 1)
        if batch:
            seqs = slice(r * rb // SUBLANES, (r + 1) * rb // SUBLANES)
            graw_ref[rows, :] = gate
            xe_ref[seqs, SUBLANES:2 * SUBLANES, :] = gate.reshape(rb // SUBLANES, SUBLANES, tf)
            taps = [xe_ref[seqs, SUBLANES - s:2 * SUBLANES - s, :] for s in range(FFN_CONV)]
        else:
            base = SUBLANES + r * rb
            xe_ref[:, base:base + rb, :] = gate.reshape(1, rb, tf)
            taps = [xe_ref[:, base - s:base - s + rb, :] for s in range(FFN_CONV)]
        conv = None
        for s, tap in enumerate(taps):
            term = tap * cw_ref[FFN_CONV - 1 - s:FFN_CONV - s, :]
            conv = term if conv is None else conv + term
        act = (_silu(conv.reshape(rb, tf)) * up).astype(jnp.bfloat16)
        acc_ref[rows, :] += jnp.dot(act, wd_ref[...], preferred_element_type=jnp.float32)
    if not batch:
        tail = xe_ref[:, tm:tm + SUBLANES, :]
        carry_ref[pl.ds(j, 1)] = tail
        graw_ref[...] = tail.reshape(SUBLANES, tf)

    @pl.when(j == pl.num_programs(1) - 1)
    def _():
        y = acc_ref[...]
        y_ref[...] = h_ref[...] + y * _rms_scale(y) * nw_post_ref[...]


def _ffn(h, nw_pre, wg, wu, cw, wd, nw_post, hist, *, batch, tm, tf):
    rows = h.shape[0]
    nj = D_FF // tf
    in_specs = [
        pl.BlockSpec((tm, D_MODEL), lambda i, j: (i, 0)),
        pl.BlockSpec((1, D_MODEL), lambda i, j: (0, 0)),
        pl.BlockSpec((D_MODEL, tf), lambda i, j: (0, j)),
        pl.BlockSpec((D_MODEL, tf), lambda i, j: (0, j)),
        pl.BlockSpec((FFN_CONV, tf), lambda i, j: (0, j)),
        pl.BlockSpec((tf, D_MODEL), lambda i, j: (j, 0)),
        pl.BlockSpec((1, D_MODEL), lambda i, j: (0, 0)),
    ]
    args = [h, nw_pre, wg, wu, cw, wd, nw_post, hist]
    scratch = [pltpu.VMEM((tm, D_MODEL), jnp.bfloat16), pltpu.VMEM((tm, D_MODEL), jnp.float32)]
    if batch:
        in_specs.append(pl.BlockSpec((tm // SUBLANES, SUBLANES, tf), lambda i, j: (i, 0, j)))
        graw_spec = pl.BlockSpec((tm, tf), lambda i, j: (i, j))
        graw_shape = jax.ShapeDtypeStruct((rows, D_FF), jnp.float32)
        scratch.append(pltpu.VMEM((tm // SUBLANES, 2 * SUBLANES, tf), jnp.float32))
    else:
        in_specs.append(pl.BlockSpec((SUBLANES, tf), lambda i, j: (0, j)))
        graw_spec = pl.BlockSpec((SUBLANES, tf), lambda i, j: (i, j))
        graw_shape = jax.ShapeDtypeStruct((rows // tm * SUBLANES, D_FF), jnp.float32)
        scratch.append(pltpu.VMEM((1, SUBLANES + tm, tf), jnp.float32))
        scratch.append(pltpu.VMEM((nj, SUBLANES, tf), jnp.float32))
    return pl.pallas_call(
        functools.partial(_ffn_kernel, batch=batch, tm=tm, tf=tf),
        grid=(rows // tm, nj),
        in_specs=in_specs,
        out_specs=[pl.BlockSpec((tm, D_MODEL), lambda i, j: (i, 0)), graw_spec],
        out_shape=[jax.ShapeDtypeStruct((rows, D_MODEL), jnp.float32), graw_shape],
        scratch_shapes=scratch,
        compiler_params=pltpu.CompilerParams(
            dimension_semantics=("arbitrary", "arbitrary"), vmem_limit_bytes=VMEM_LIMIT),
        name="ffn_batch" if batch else "ffn_seq",
    )(*args)


def kernel(x_prompt, x_sample, cache_swa_meta_kv, cache_swa_window_kv, state_gdn_conv, state_gdn, state_ffn_conv, meta_tokens, rel_bias_table, w_in, gdn_conv_w, gdn_a_log, gdn_dt_bias, gdn_norm_w, swa_sinks, w_out, norm_mix_pre, norm_mix_post, norm_ffn_pre, norm_ffn_post, ffn_w_gate, ffn_w_up, ffn_conv_w, ffn_w_down):
    f32, bf16 = jnp.float32, jnp.bfloat16
    seq = x_prompt.shape[1]
    dec_b, dec_t = x_sample.shape[0], x_sample.shape[1]
    n_dec = dec_b * dec_t
    assert x_prompt.shape[0] == 1 and seq % CHUNK == 0 and dec_t == SUBLANES and n_dec % CHUNK == 0

    wi = w_in[0]
    n_ba = 2 * GDN_HEADS
    w_in_p = jnp.concatenate([wi[:, :COL_SQ].astype(bf16), wi[:, COL_SQ + n_ba:].astype(bf16)], axis=1)
    w_ba = jnp.pad(wi[:, COL_SQ:COL_SQ + n_ba], ((0, 0), (0, LANES - n_ba))).astype(bf16)
    wo = w_out[0].astype(bf16)
    wg = ffn_w_gate[0].astype(bf16)
    wu = ffn_w_up[0].astype(bf16)
    wd = ffn_w_down[0].astype(bf16)
    lane_pad = lambda v: jnp.pad(v.reshape(1, GDN_HEADS), ((0, 0), (GDN_HEADS, LANES - 2 * GDN_HEADS)))
    alog_row = lane_pad(gdn_a_log[0])
    dtb_row = lane_pad(gdn_dt_bias[0])
    gnw = gdn_norm_w[0].reshape(1, HEAD_DIM)

    pad_rows = CHUNK - N_META
    n_small = n_dec + CHUNK
    x_big = x_prompt.reshape(seq, D_MODEL)
    x_small = jnp.concatenate(
        [x_sample.reshape(n_dec, D_MODEL), jnp.zeros((pad_rows, D_MODEL), f32), meta_tokens.astype(f32)], axis=0)
    nw = norm_mix_pre[0].reshape(1, D_MODEL)
    proj_big, ba_big = _inproj(x_big, nw, w_in_p, w_ba, tm=1024, tn=512, row_chunk=128)
    proj_small, ba_small = _inproj(x_small, nw, w_in_p, w_ba, tm=n_small, tn=512, row_chunk=128)

    cw = gdn_conv_w[0]
    small_chunks = proj_small.reshape(n_small // CHUNK, CHUNK, PROJ_COLS)
    small_groups = proj_small.reshape(n_small // SUBLANES, SUBLANES, PROJ_COLS)
    last_chunk = n_small // CHUNK - 1
    gdn_meta, s_meta = _gdn(
        small_chunks, ba_small.reshape(n_small // CHUNK, CHUNK, LANES), lambda s: (last_chunk, 0, 0),
        jnp.zeros((1, SUBLANES, GDN_QKV), f32), lambda s: (0, 0, 0),
        jnp.zeros((1, GDN_HEADS, HEAD_DIM, HEAD_DIM), f32), cw, alog_row, dtb_row, gnw,
        n_steps=1, nb=1, seq=CHUNK, group=CHUNK, carry=True, pad_rows=pad_rows)
    last_group = n_small // SUBLANES - 1
    gdn_big, s_prompt = _gdn(
        proj_big.reshape(1, seq, PROJ_COLS), ba_big.reshape(1, seq, LANES), lambda s: (0, s, 0),
        small_groups, lambda s: (last_group, 0, 0),
        s_meta, cw, alog_row, dtb_row, gnw,
        n_steps=seq // CHUNK, nb=1, seq=CHUNK, group=CHUNK, carry=True, pad_rows=0)
    hist_gdn = jnp.pad(state_gdn_conv[0], ((0, 0), (SUBLANES - (GDN_CONV - 1), 0), (0, 0)))
    nb_gdn = CHUNK // dec_t
    gdn_small, s_sample = _gdn(
        small_groups, ba_small.reshape(n_small // SUBLANES, SUBLANES, LANES), lambda s: (s, 0, 0),
        hist_gdn, lambda s: (s, 0, 0),
        state_gdn, cw, alog_row, dtb_row, gnw,
        n_steps=dec_b // nb_gdn, nb=nb_gdn, seq=dec_t, group=dec_t, carry=False, pad_rows=0)

    qi = np.arange(WINDOW)[:, None]
    kj = np.arange(WINDOW)[None, :]
    mi = np.arange(N_META)[None, :]
    ti = np.arange(dec_t)[:, None]
    small_keys = np.arange(SMALL_KEYS)[None, :]
    new_keys = small_keys - N_META
    id_arrays = [
        _bucket_ids(qi - kj, qi >= kj),
        _bucket_ids(qi - kj + WINDOW, kj > qi),
        _bucket_ids(qi + N_META - mi, np.ones((WINDOW, N_META), bool)),
        _bucket_ids(qi + N_META - mi + WINDOW, np.ones((WINDOW, N_META), bool)),
        _bucket_ids(ti + WINDOW - kj, kj > ti),
        _bucket_ids(np.where(new_keys < 0, PAST_LEN + ti - small_keys, ti - new_keys),
                    (new_keys < 0) | ((new_keys <= ti) & (new_keys < dec_t))),
        _bucket_ids(mi.T - mi, mi.T >= mi),
    ]
    bcur, bprev, bm0, bfar, bwin, bsmall, bmm = _bias_tables(rel_bias_table, id_arrays)
    sink_rows = lambda q: jnp.repeat(swa_sinks[0].reshape(SWA_KV_HEADS, SWA_GROUP), q, axis=1)[..., None]

    sq_blk = COL_SQ // SWA_WIDTH
    kv_blk = COL_KV // KV_WIDTH
    meta_blk = (n_small - N_META) // N_META
    full3 = lambda a: pl.BlockSpec(a.shape, lambda j: (0, 0, 0))
    sink_p = sink_rows(WINDOW)
    swa_big = pl.pallas_call(
        _swa_prompt_kernel,
        grid=(seq // WINDOW,),
        in_specs=[
            pl.BlockSpec((WINDOW, SWA_WIDTH), lambda j: (j, sq_blk)),
            pl.BlockSpec((WINDOW, KV_WIDTH), lambda j: (j, kv_blk)),
            pl.BlockSpec((WINDOW, KV_WIDTH), lambda j: (jnp.maximum(j - 1, 0), kv_blk)),
            pl.BlockSpec((N_META, KV_WIDTH), lambda j: (meta_blk, kv_blk)),
            full3(bcur), full3(bprev), full3(bm0), full3(bfar), full3(sink_p),
        ],
        out_specs=pl.BlockSpec((WINDOW, SWA_WIDTH), lambda j: (j, 0)),
        out_shape=jax.ShapeDtypeStruct((seq, SWA_WIDTH), bf16),
        compiler_params=pltpu.CompilerParams(
            dimension_semantics=("arbitrary",), vmem_limit_bytes=VMEM_LIMIT),
        name="swa_prompt",
    )(proj_big, proj_big, proj_big, proj_small, bcur, bprev, bm0, bfar, sink_p)

    nb_swa = 8
    sink_s = sink_rows(dec_t)
    win = cache_swa_window_kv[0].reshape(dec_b, WINDOW, KV_WIDTH)
    meta_kv = cache_swa_meta_kv[0].reshape(dec_b, N_META, KV_WIDTH)
    swa_small, win_new = pl.pallas_call(
        functools.partial(_swa_sample_kernel, nb=nb_swa, seq=dec_t),
        grid=(dec_b // nb_swa,),
        in_specs=[
            pl.BlockSpec((nb_swa, dec_t, SWA_WIDTH), lambda j: (j, 0, sq_blk)),
            pl.BlockSpec((nb_swa, dec_t, KV_WIDTH), lambda j: (j, 0, kv_blk)),
            pl.BlockSpec((nb_swa, WINDOW, KV_WIDTH), lambda j: (j, 0, 0)),
            pl.BlockSpec((nb_swa, N_META, KV_WIDTH), lambda j: (j, 0, 0)),
            full3(bwin), full3(bsmall), full3(sink_s),
        ],
        out_specs=[pl.BlockSpec((nb_swa * dec_t, SWA_WIDTH), lambda j: (j, 0)),
                   pl.BlockSpec((nb_swa, WINDOW, KV_WIDTH), lambda j: (j, 0, 0))],
        out_shape=[jax.ShapeDtypeStruct((n_dec, SWA_WIDTH), bf16),
                   jax.ShapeDtypeStruct((dec_b, WINDOW, KV_WIDTH), f32)],
        compiler_params=pltpu.CompilerParams(
            dimension_semantics=("arbitrary",), vmem_limit_bytes=VMEM_LIMIT),
        name="swa_sample",
    )(small_groups, small_groups, win, meta_kv, bwin, bsmall, sink_s)

    sink_m = sink_rows(N_META)
    swa_meta = pl.pallas_call(
        _swa_meta_kernel,
        grid=(1,),
        in_specs=[
            pl.BlockSpec((N_META, SWA_WIDTH), lambda j: (meta_blk, sq_blk)),
            pl.BlockSpec((N_META, KV_WIDTH), lambda j: (meta_blk, kv_blk)),
            full3(bmm), full3(sink_m),
        ],
        out_specs=pl.BlockSpec((N_META, SWA_WIDTH), lambda j: (0, 0)),
        out_shape=jax.ShapeDtypeStruct((N_META, SWA_WIDTH), bf16),
        name="swa_meta",
    )(proj_small, proj_small, bmm, sink_m)

    nw_post = norm_mix_post[0].reshape(1, D_MODEL)
    nf_pre = norm_ffn_pre[0].reshape(1, D_MODEL)
    nf_post = norm_ffn_post[0].reshape(1, D_MODEL)
    fcw = ffn_conv_w[0]
    gdn_small_all = jnp.concatenate([gdn_small, gdn_meta], axis=0)
    swa_small_all = jnp.concatenate([swa_small, jnp.zeros((pad_rows, SWA_WIDTH), bf16), swa_meta], axis=0)
    h_small = _outproj(gdn_small_all, swa_small_all, x_small, wo, nw_post, tm=n_small // 2)
    hist_ffn = jnp.pad(state_ffn_conv[0], ((0, CHUNK // SUBLANES), (SUBLANES - (FFN_CONV - 1), 0), (0, 0)))
    y_small, g_small = _ffn(h_small, nf_pre, wg, wu, fcw, wd, nf_post, hist_ffn,
                            batch=True, tm=n_small // 2, tf=512)
    h_big = _outproj(gdn_big, swa_big, x_big, wo, nw_post, tm=512)
    y_big, g_tail = _ffn(h_big, nf_pre, wg, wu, fcw, wd, nf_post, g_small[n_small - SUBLANES:],
                         batch=False, tm=512, tf=512)

    kv_shape = lambda n: (1, n, 2, SWA_KV_HEADS, HEAD_DIM)
    kv_small = proj_small[:, COL_KV:COL_KV + KV_WIDTH]
    y_prompt = y_big.reshape(1, seq, D_MODEL)
    y_sample = y_small[:n_dec].reshape(dec_b, dec_t, D_MODEL)
    p_meta_kv = kv_small[n_small - N_META:].reshape(kv_shape(N_META))[None]
    p_window_kv = proj_big[seq - WINDOW:, COL_KV:COL_KV + KV_WIDTH].reshape(kv_shape(WINDOW))[None]
    p_gdn_conv = proj_big[seq - (GDN_CONV - 1):, :GDN_QKV].reshape(1, 1, GDN_CONV - 1, GDN_QKV)
    p_gdn_state = s_prompt[None]
    p_ffn_conv = g_tail[g_tail.shape[0] - (FFN_CONV - 1):].reshape(1, 1, FFN_CONV - 1, D_FF)
    s_window_kv = win_new.reshape(1, dec_b, WINDOW, 2, SWA_KV_HEADS, HEAD_DIM)
    s_gdn_conv = proj_small[:n_dec, :GDN_QKV].reshape(dec_b, dec_t, GDN_QKV)[:, dec_t - (GDN_CONV - 1):][None]
    s_gdn_state = s_sample
    s_ffn_conv = g_small[:n_dec].reshape(dec_b, dec_t, D_FF)[:, dec_t - (FFN_CONV - 1):][None]
    return (y_prompt, y_sample, p_meta_kv, p_window_kv, p_gdn_conv, p_gdn_state, p_ffn_conv,
            s_window_kv, s_gdn_conv, s_gdn_state, s_ffn_conv)
```

```python
import functools
import math

import numpy as np
import jax
import jax.numpy as jnp
from jax import lax
from jax.experimental import pallas as pl
from jax.experimental.pallas import tpu as pltpu

D_MODEL = 2048
HEAD_DIM = 128
GDN_HEADS = 8
GDN_WIDTH = GDN_HEADS * HEAD_DIM
GDN_QKV = 3 * GDN_WIDTH
SWA_HEADS = 8
SWA_KV_HEADS = 2
SWA_GROUP = SWA_HEADS // SWA_KV_HEADS
SWA_WIDTH = SWA_HEADS * HEAD_DIM
KV_WIDTH = 2 * SWA_KV_HEADS * HEAD_DIM
WINDOW = 128
N_META = 16
N_BUCKETS = 32
MAX_DISTANCE = 128
GDN_CONV = 4
FFN_CONV = 3
D_FF = 5632
EPS = 1e-6
PAST_LEN = 16384

SUBLANES = 8
LANES = 128

CHUNK = 128
assert CHUNK == HEAD_DIM == LANES
INV_BASE = 16
GDN_SEQ_CHUNKS = 2
INPROJ_ROW_BLOCKS = 4
FFN_ROW_BLOCKS = 4

COL_Z = GDN_QKV
COL_SQ = COL_Z + GDN_WIDTH
COL_KV = COL_SQ + SWA_WIDTH
PROJ_COLS = COL_KV + KV_WIDTH
GDN_COLS = COL_SQ

SMALL_KEYS = 32
NEG = -1e30
VMEM_LIMIT = 56 * 1024 * 1024

_NT = (((1,), (1,)), ((), ()))


def _dot(a, b):
    return jnp.dot(a.astype(jnp.bfloat16), b.astype(jnp.bfloat16), preferred_element_type=jnp.float32)


def _dot_nt(a, b):
    return lax.dot_general(a.astype(jnp.bfloat16), b.astype(jnp.bfloat16), _NT,
                           preferred_element_type=jnp.float32)


def _split(a):
    hi = a.astype(jnp.bfloat16)
    lo = (a - hi.astype(jnp.float32)).astype(jnp.bfloat16)
    return hi, lo


def _dot3(a, b):
    ah, al = _split(a)
    bh, bl = _split(b)
    d = functools.partial(jnp.dot, preferred_element_type=jnp.float32)
    return d(ah, bh) + (d(ah, bl) + d(al, bh))


_dot_inv = _dot


def _dot_exact(a, b, dims=None):
    if dims is None:
        return jnp.dot(a, b, precision=lax.Precision.HIGHEST, preferred_element_type=jnp.float32)
    return lax.dot_general(a, b, dims, precision=lax.Precision.HIGHEST,
                           preferred_element_type=jnp.float32)


def _rms_scale(x):
    return lax.rsqrt(jnp.mean(x * x, axis=-1, keepdims=True) + EPS)


def _silu(x):
    return x * jax.nn.sigmoid(x)


def _inproj_kernel(*refs, row_chunk, conv_tiles):
    if conv_tiles:
        (x_ref, nw_ref, w_ref, wba_ref, cw_ref, hist_ref,
         o_ref, ba_ref, tail_ref, xn_ref, xe_ref, carry_ref) = refs
    else:
        x_ref, nw_ref, w_ref, wba_ref, o_ref, ba_ref, xn_ref = refs
    i = pl.program_id(0)
    j = pl.program_id(1)
    tm, tn = o_ref.shape

    @pl.when(j == 0)
    def _():
        def body(c, carry):
            rows = pl.ds(pl.multiple_of(c * row_chunk, row_chunk), row_chunk)
            x = x_ref[rows, :]
            xn_ref[rows, :] = (x * _rms_scale(x) * nw_ref[...]).astype(jnp.bfloat16)
            return carry
        lax.fori_loop(0, tm // row_chunk, body, 0)
        ba_ref[...] = jnp.dot(xn_ref[...], wba_ref[...], preferred_element_type=jnp.float32)

    def plain():
        o_ref[...] = jnp.dot(xn_ref[...], w_ref[...], preferred_element_type=jnp.float32)

    if not conv_tiles:
        plain()
        return
    pl.when(j >= conv_tiles)(plain)

    @pl.when(j < conv_tiles)
    def _():
        @pl.when(i == 0)
        def _():
            carry_ref[pl.ds(j, 1)] = hist_ref[...].reshape(1, SUBLANES, tn)
        xe_ref[0:SUBLANES, :] = carry_ref[pl.ds(j, 1)].reshape(SUBLANES, tn)
        rb = tm // INPROJ_ROW_BLOCKS
        raw_block = lambda r: jnp.dot(xn_ref[r * rb:(r + 1) * rb, :], w_ref[...],
                                      preferred_element_type=jnp.float32)
        nxt = raw_block(0)
        for r in range(INPROJ_ROW_BLOCKS):
            raw = nxt
            if r + 1 < INPROJ_ROW_BLOCKS:
                nxt = raw_block(r + 1)
            base = SUBLANES + r * rb
            xe_ref[base:base + rb, :] = raw
            conv = None
            for s in range(GDN_CONV):
                term = xe_ref[base - s:base - s + rb, :] * cw_ref[GDN_CONV - 1 - s:GDN_CONV - s, :]
                conv = term if conv is None else conv + term
            o_ref[r * rb:(r + 1) * rb, :] = _silu(conv)
        tail = xe_ref[tm:tm + SUBLANES, :]
        carry_ref[pl.ds(j, 1)] = tail.reshape(1, SUBLANES, tn)
        tail_ref[...] = tail


def _inproj(x, nw, w, wba, conv=None, *, tm, tn, row_chunk):
    rows = x.shape[0]
    conv_tiles = GDN_QKV // tn if conv else 0
    in_specs = [
        pl.BlockSpec((tm, D_MODEL), lambda i, j: (i, 0)),
        pl.BlockSpec((1, D_MODEL), lambda i, j: (0, 0)),
        pl.BlockSpec((D_MODEL, tn), lambda i, j: (0, j)),
        pl.BlockSpec((D_MODEL, LANES), lambda i, j: (0, 0)),
    ]
    out_specs = [pl.BlockSpec((tm, tn), lambda i, j: (i, j)),
                 pl.BlockSpec((tm, LANES), lambda i, j: (i, 0))]
    out_shape = [jax.ShapeDtypeStruct((rows, PROJ_COLS), jnp.float32),
                 jax.ShapeDtypeStruct((rows, LANES), jnp.float32)]
    scratch = [pltpu.VMEM((tm, D_MODEL), jnp.bfloat16)]
    args = [x, nw, w, wba]
    if conv:
        conv_col = lambda i, j: (0, jnp.minimum(j, conv_tiles - 1))
        in_specs += [pl.BlockSpec((GDN_CONV, tn), conv_col), pl.BlockSpec((SUBLANES, tn), conv_col)]
        out_specs.append(pl.BlockSpec((SUBLANES, tn), lambda i, j: (i, jnp.minimum(j, conv_tiles - 1))))
        out_shape.append(jax.ShapeDtypeStruct((rows // tm * SUBLANES, GDN_QKV), jnp.float32))
        scratch += [pltpu.VMEM((SUBLANES + tm, tn), jnp.float32),
                    pltpu.VMEM((conv_tiles, SUBLANES, tn), jnp.float32)]
        args += list(conv)
    return pl.pallas_call(
        functools.partial(_inproj_kernel, row_chunk=row_chunk, conv_tiles=conv_tiles),
        grid=(rows // tm, PROJ_COLS // tn),
        in_specs=in_specs,
        out_specs=out_specs,
        out_shape=out_shape,
        scratch_shapes=scratch,
        compiler_params=pltpu.CompilerParams(
            dimension_semantics=("arbitrary", "arbitrary"), vmem_limit_bytes=VMEM_LIMIT),
        name="inproj_conv" if conv else "inproj",
    )(*args)


def _tri_inverse(lms, ri, ci):
    shift = INV_BASE.bit_length() - 1
    eye = (ri == ci).astype(jnp.float32)
    in_block = (ri >> shift) == (ci >> shift)
    ps = [jnp.where(in_block, lm, 0.0) for lm in lms]
    ts = [eye - p for p in ps]
    for _ in range(shift - 1):
        ps = [_dot_inv(p, p) for p in ps]
        ts = [t + _dot_inv(t, p) for t, p in zip(ts, ps)]
    size = INV_BASE
    while size < CHUNK:
        shift += 1
        in_pair = (ri >> shift) == (ci >> shift)
        off_mask = in_pair & jnp.logical_not(in_block)
        tos = [_dot_inv(t, jnp.where(off_mask, lm, 0.0)) for t, lm in zip(ts, lms)]
        ts = [t - _dot_inv(to, t) for t, to in zip(ts, tos)]
        in_block = in_pair
        size *= 2
    return ts


def _gdn_kernel(x_ref, ba_ref, hist_ref, s0_ref, cw_ref, alog_ref, dtb_ref, gnw_ref,
                o_ref, sout_ref, xe_ref, s_ref, *, nb, seq, group, carry, pad_rows, preconv):
    step = pl.program_id(0)
    rows = nb * seq
    n_chunks = rows // CHUNK
    n_groups = CHUNK // group
    gshift = group.bit_length() - 1

    if carry:
        @pl.when(step == 0)
        def _():
            s_ref[...] = s0_ref[0]
    if not preconv:
        if carry:
            @pl.when(step == 0)
            def _():
                xe_ref[:, 0:SUBLANES, :] = hist_ref[...]
        else:
            xe_ref[:, 0:SUBLANES, :] = hist_ref[...]
        xe_ref[:, SUBLANES:SUBLANES + seq, :] = x_ref[:, :, 0:GDN_QKV]

    seq_rows = min(seq, CHUNK)
    seqs_per_chunk = CHUNK // seq_rows

    def chunk_rows(ref, c, row_off, cols):
        if seq >= CHUNK:
            start = row_off + c * CHUNK
            return ref[0:1, start:start + CHUNK, cols]
        b0 = c * seqs_per_chunk
        return ref[b0:b0 + seqs_per_chunk, row_off:row_off + seq, cols]

    def conv_chunk(col, c):
        cols = slice(col, col + HEAD_DIM)
        if preconv:
            return chunk_rows(x_ref, c, 0, cols).reshape(CHUNK, HEAD_DIM)
        acc = None
        for s in range(GDN_CONV):
            term = chunk_rows(xe_ref, c, SUBLANES - s, cols) * cw_ref[GDN_CONV - 1 - s:GDN_CONV - s, cols]
            acc = term if acc is None else acc + term
        return _silu(acc).reshape(CHUNK, HEAD_DIM)

    ri = lax.broadcasted_iota(jnp.int32, (CHUNK, CHUNK), 0)
    ci = lax.broadcasted_iota(jnp.int32, (CHUNK, CHUNK), 1)
    same = (ri >> gshift) == (ci >> gshift)
    m_incl = same & (ri >= ci)
    m_strict = same & (ri > ci)
    f_incl = m_incl.astype(jnp.float32)
    f_same = same.astype(jnp.float32)
    lane = lax.broadcasted_iota(jnp.int32, (CHUNK, LANES), 1)
    row_in_chunk = lax.broadcasted_iota(jnp.int32, (CHUNK, LANES), 0)

    pre = []
    for c in range(n_chunks):
        bac = chunk_rows(ba_ref, c, 0, slice(0, LANES)).reshape(CHUNK, LANES)
        beta_all = jax.nn.sigmoid(bac)
        sp_in = bac + dtb_ref[...]
        softplus = jnp.maximum(sp_in, 0.0) + jnp.log1p(jnp.exp(-jnp.abs(sp_in)))
        g_all = -jnp.exp(alog_ref[...]) * softplus
        if pad_rows and c == 0:
            valid = row_in_chunk >= pad_rows
            beta_all = jnp.where(valid, beta_all, 0.0)
            g_all = jnp.where(valid, g_all, 0.0)
        g_all = jnp.where((lane >= GDN_HEADS) & (lane < 2 * GDN_HEADS), g_all, 0.0)
        gc_col = _dot_exact(f_incl, g_all)
        gtot_col = _dot_exact(f_same, g_all)
        gc_row = _dot_exact(g_all.T, f_incl, _NT)

        for h in range(GDN_HEADS):
            qh = conv_chunk(h * HEAD_DIM, c)
            kh = conv_chunk(GDN_WIDTH + h * HEAD_DIM, c)
            vh = conv_chunk(2 * GDN_WIDTH + h * HEAD_DIM, c)
            qh = qh * lax.rsqrt(jnp.sum(qh * qh, -1, keepdims=True) + EPS) * (HEAD_DIM ** -0.5)
            kh = kh * lax.rsqrt(jnp.sum(kh * kh, -1, keepdims=True) + EPS)
            bcast = lambda col: jnp.broadcast_to(col, (CHUNK, HEAD_DIM))
            gcc = bcast(gc_col[:, GDN_HEADS + h:GDN_HEADS + h + 1])
            gtc = bcast(gtot_col[:, GDN_HEADS + h:GDN_HEADS + h + 1])
            beta = bcast(beta_all[:, h:h + 1])
            gcr = gc_row[GDN_HEADS + h:GDN_HEADS + h + 1, :]
            decay = jnp.exp(jnp.where(m_incl, gcc - gcr, NEG))
            kb = kh * beta
            egc = jnp.exp(gcc)
            pre.append(dict(
                c=c, h=h,
                lm=jnp.where(m_strict, _dot_nt(kb, kh) * decay, 0.0),
                qk=_dot_nt(qh, kh) * decay,
                rhs=jnp.concatenate([vh * beta, kb * egc], axis=1),
                qg=qh * egc,
                kd_t=(kh * jnp.exp(gtc - gcc)).T,
                gl=jnp.exp(gtc)))

    inverses = _tri_inverse([p["lm"] for p in pre], ri, ci)
    sols = [_dot_inv(t, p["rhs"]) for t, p in zip(inverses, pre)]

    for c in range(n_chunks):
        r0 = c * CHUNK
        items = [(p, sol) for p, sol in zip(pre, sols) if p["c"] == c]
        state = lambda h, b: s_ref[h] if carry else s0_ref[c * n_groups + b, h]
        ws, qs = [], []
        for p, sol in items:
            w = sol[:, HEAD_DIM:]
            ws_parts, qs_parts = [], []
            for b in range(n_groups):
                g0 = b * group
                wq = jnp.concatenate([w[g0:g0 + group], p["qg"][g0:g0 + group]], axis=0)
                res = _dot(wq, state(p["h"], b))
                ws_parts.append(res[:group])
                qs_parts.append(res[group:])
            ws.append(ws_parts[0] if n_groups == 1 else jnp.concatenate(ws_parts, axis=0))
            qs.append(qs_parts[0] if n_groups == 1 else jnp.concatenate(qs_parts, axis=0))
        v_new = [sol[:, :HEAD_DIM] - w for (p, sol), w in zip(items, ws)]
        o = [a + _dot(p["qk"], vn) for a, (p, sol), vn in zip(qs, items, v_new)]
        for (p, sol), vn in zip(items, v_new):
            h = p["h"]
            for b in range(n_groups):
                g0 = b * group
                kd_b = p["kd_t"] if n_groups == 1 else jnp.where((ci >> gshift) == b, p["kd_t"], 0.0)
                st = state(h, b) * p["gl"][g0:g0 + 1, :] + _dot(kd_b, vn)
                if carry:
                    s_ref[h] = st
                else:
                    sout_ref[c * n_groups + b, h] = st
        for (p, sol), oh in zip(items, o):
            h = p["h"]
            z = chunk_rows(x_ref, c, 0, slice(COL_Z + h * HEAD_DIM, COL_Z + (h + 1) * HEAD_DIM)).reshape(CHUNK, HEAD_DIM)
            y = oh * lax.rsqrt(jnp.mean(oh * oh, -1, keepdims=True) + EPS) * gnw_ref[...] * _silu(z)
            o_ref[r0:r0 + CHUNK, h * HEAD_DIM:(h + 1) * HEAD_DIM] = y.astype(o_ref.dtype)

    if carry:
        if not preconv:
            xe_ref[:, 0:SUBLANES, :] = xe_ref[:, seq:seq + SUBLANES, :]

        @pl.when(step == pl.num_programs(0) - 1)
        def _():
            sout_ref[0] = s_ref[...]


def _gdn(x3, ba3, x_idx, hist, hist_idx, s0, cw, alog_row, dtb_row, gnw, *,
         n_steps, nb, seq, group, carry, pad_rows, preconv):
    rows = nb * seq
    if carry:
        state_spec = pl.BlockSpec((1, GDN_HEADS, HEAD_DIM, HEAD_DIM), lambda s: (0, 0, 0, 0))
        state_shape = (1, GDN_HEADS, HEAD_DIM, HEAD_DIM)
    else:
        n_states = rows // group
        state_spec = pl.BlockSpec((None, n_states, GDN_HEADS, HEAD_DIM, HEAD_DIM), lambda s: (0, s, 0, 0, 0))
        state_shape = (1, n_steps * n_states, GDN_HEADS, HEAD_DIM, HEAD_DIM)
    full = lambda shape: pl.BlockSpec(shape, lambda s: (0,) * len(shape))
    return pl.pallas_call(
        functools.partial(_gdn_kernel, nb=nb, seq=seq, group=group, carry=carry, pad_rows=pad_rows,
                          preconv=preconv),
        grid=(n_steps,),
        in_specs=[
            pl.BlockSpec((nb, seq, GDN_COLS), x_idx),
            pl.BlockSpec((nb, seq, LANES), x_idx),
            pl.BlockSpec((nb, SUBLANES, GDN_QKV), hist_idx),
            state_spec,
            full((GDN_CONV, GDN_QKV)),
            full((1, LANES)),
            full((1, LANES)),
            full((1, HEAD_DIM)),
        ],
        out_specs=[
            pl.BlockSpec((rows, GDN_WIDTH), lambda s: (s, 0)),
            state_spec,
        ],
        out_shape=[
            jax.ShapeDtypeStruct((n_steps * rows, GDN_WIDTH), jnp.bfloat16),
            jax.ShapeDtypeStruct(state_shape, jnp.float32),
        ],
        scratch_shapes=[
            pltpu.VMEM((nb, SUBLANES + (SUBLANES if preconv else seq), GDN_QKV), jnp.float32),
            pltpu.VMEM((GDN_HEADS, HEAD_DIM, HEAD_DIM), jnp.float32),
        ],
        compiler_params=pltpu.CompilerParams(
            dimension_semantics=("arbitrary",), vmem_limit_bytes=VMEM_LIMIT),
        name="gdn_seq" if carry else "gdn_batch",
    )(x3, ba3, hist, s0, cw, alog_row, dtb_row, gnw)


def _t5_bucket_np(dist):
    n = np.maximum(dist, 0)
    exact = N_BUCKETS // 2
    large = exact + (np.log(np.maximum(n, 1).astype(np.float32) / exact)
                     / math.log(MAX_DISTANCE / exact) * (N_BUCKETS - exact)).astype(np.int32)
    return np.where(n < exact, n, np.minimum(large, N_BUCKETS - 1)).astype(np.int32)


def _bucket_ids(dist, valid):
    return np.where(valid, _t5_bucket_np(dist), -1).astype(np.int32)


def _bias_kernel(table_ref, *refs):
    n = len(refs) // 2
    for ids_ref, out_ref in zip(refs[:n], refs[n:]):
        ids = ids_ref[...]
        nq = ids.shape[0]
        for head in range(SWA_HEADS):
            def body(b, acc):
                return jnp.where(ids == b, table_ref[b, head], acc)
            acc = lax.fori_loop(0, N_BUCKETS, body, jnp.full(ids.shape, NEG, jnp.float32))
            kh, g = divmod(head, SWA_GROUP)
            out_ref[kh, g * nq:(g + 1) * nq, :] = acc


def _bias_tables(rel_table, id_arrays):
    out_shapes = [jax.ShapeDtypeStruct((SWA_KV_HEADS, SWA_GROUP * a.shape[0], a.shape[1]), jnp.float32)
                  for a in id_arrays]
    vmem = pl.BlockSpec(memory_space=pltpu.VMEM)
    return pl.pallas_call(
        _bias_kernel,
        in_specs=[pl.BlockSpec(memory_space=pltpu.SMEM)] + [vmem] * len(id_arrays),
        out_specs=[vmem] * len(id_arrays),
        out_shape=out_shapes,
        name="swa_bias",
    )(rel_table, *[jnp.asarray(a) for a in id_arrays])


def _attend(q, keys, values, biases, sink):
    scale = HEAD_DIM ** -0.5
    scores = [_dot_nt(q, k) * scale + b for k, b in zip(keys, biases)]
    m = sink
    for s in scores:
        m = jnp.maximum(m, jnp.max(s, axis=-1, keepdims=True))
    den = jnp.exp(sink - m)
    acc = None
    for s, v in zip(scores, values):
        p = jnp.exp(s - m)
        den = den + jnp.sum(p, axis=-1, keepdims=True)
        pv = _dot(p, v)
        acc = pv if acc is None else acc + pv
    return acc / den


def _swa_prompt_kernel(q_ref, kvc_ref, kvp_ref, kvm_ref, bcur_ref, bprev_ref, bm0_ref, bfar_ref,
                       sink_ref, o_ref):
    first = pl.program_id(0) == 0
    for kh in range(SWA_KV_HEADS):
        ks = slice(kh * HEAD_DIM, (kh + 1) * HEAD_DIM)
        vs = slice((SWA_KV_HEADS + kh) * HEAD_DIM, (SWA_KV_HEADS + kh + 1) * HEAD_DIM)
        q = jnp.concatenate(
            [q_ref[:, (kh * SWA_GROUP + g) * HEAD_DIM:(kh * SWA_GROUP + g + 1) * HEAD_DIM]
             for g in range(SWA_GROUP)], axis=0)
        b_prev = jnp.where(first, NEG, bprev_ref[kh])
        b_meta = jnp.where(first, bm0_ref[kh], bfar_ref[kh])
        o = _attend(q,
                    [kvc_ref[:, ks], kvp_ref[:, ks], kvm_ref[:, ks]],
                    [kvc_ref[:, vs], kvp_ref[:, vs], kvm_ref[:, vs]],
                    [bcur_ref[kh], b_prev, b_meta], sink_ref[kh])
        for g in range(SWA_GROUP):
            head = kh * SWA_GROUP + g
            o_ref[:, head * HEAD_DIM:(head + 1) * HEAD_DIM] = (
                o[g * WINDOW:(g + 1) * WINDOW].astype(o_ref.dtype))


def _swa_meta_kernel(q_ref, kv_ref, bias_ref, sink_ref, o_ref):
    for kh in range(SWA_KV_HEADS):
        ks = slice(kh * HEAD_DIM, (kh + 1) * HEAD_DIM)
        vs = slice((SWA_KV_HEADS + kh) * HEAD_DIM, (SWA_KV_HEADS + kh + 1) * HEAD_DIM)
        q = jnp.concatenate(
            [q_ref[:, (kh * SWA_GROUP + g) * HEAD_DIM:(kh * SWA_GROUP + g + 1) * HEAD_DIM]
             for g in range(SWA_GROUP)], axis=0)
        o = _attend(q, [kv_ref[:, ks]], [kv_ref[:, vs]], [bias_ref[kh]], sink_ref[kh])
        for g in range(SWA_GROUP):
            head = kh * SWA_GROUP + g
            o_ref[:, head * HEAD_DIM:(head + 1) * HEAD_DIM] = (
                o[g * N_META:(g + 1) * N_META].astype(o_ref.dtype))


def _swa_sample_kernel(q_ref, kvn_ref, win_ref, meta_ref, bwin_ref, bsmall_ref, sink_ref,
                       o_ref, wout_ref, *, nb, seq):
    wout_ref[:, 0:WINDOW - seq, :] = win_ref[:, seq:WINDOW, :]
    wout_ref[:, WINDOW - seq:WINDOW, :] = kvn_ref[...]
    pad = jnp.zeros((SMALL_KEYS - N_META - seq, HEAD_DIM), jnp.float32)
    outs = [[] for _ in range(SWA_HEADS)]
    for b in range(nb):
        for kh in range(SWA_KV_HEADS):
            ks = slice(kh * HEAD_DIM, (kh + 1) * HEAD_DIM)
            vs = slice((SWA_KV_HEADS + kh) * HEAD_DIM, (SWA_KV_HEADS + kh + 1) * HEAD_DIM)
            q = jnp.concatenate(
                [q_ref[b, :, (kh * SWA_GROUP + g) * HEAD_DIM:(kh * SWA_GROUP + g + 1) * HEAD_DIM]
                 for g in range(SWA_GROUP)], axis=0)
            k_small = jnp.concatenate([meta_ref[b, :, ks], kvn_ref[b, :, ks], pad], axis=0)
            v_small = jnp.concatenate([meta_ref[b, :, vs], kvn_ref[b, :, vs], pad], axis=0)
            o = _attend(q, [win_ref[b, :, ks], k_small], [win_ref[b, :, vs], v_small],
                        [bwin_ref[kh], bsmall_ref[kh]], sink_ref[kh])
            for g in range(SWA_GROUP):
                outs[kh * SWA_GROUP + g].append(o[g * seq:(g + 1) * seq])
    for head in range(SWA_HEADS):
        o_ref[:, head * HEAD_DIM:(head + 1) * HEAD_DIM] = (
            jnp.concatenate(outs[head], axis=0).astype(o_ref.dtype))


def _outproj_kernel(g_ref, s_ref, h_ref, wo_ref, nw_ref, o_ref):
    mix = (jnp.dot(g_ref[...], wo_ref[0:GDN_WIDTH, :], preferred_element_type=jnp.float32)
           + jnp.dot(s_ref[...], wo_ref[GDN_WIDTH:, :], preferred_element_type=jnp.float32))
    o_ref[...] = h_ref[...] + mix * _rms_scale(mix) * nw_ref[...]


def _outproj(g, s, h, wo, nw, *, tm):
    rows = h.shape[0]
    return pl.pallas_call(
        _outproj_kernel,
        grid=(rows // tm,),
        in_specs=[
            pl.BlockSpec((tm, GDN_WIDTH), lambda i: (i, 0)),
            pl.BlockSpec((tm, SWA_WIDTH), lambda i: (i, 0)),
            pl.BlockSpec((tm, D_MODEL), lambda i: (i, 0)),
            pl.BlockSpec((D_MODEL, D_MODEL), lambda i: (0, 0)),
            pl.BlockSpec((1, D_MODEL), lambda i: (0, 0)),
        ],
        out_specs=pl.BlockSpec((tm, D_MODEL), lambda i: (i, 0)),
        out_shape=jax.ShapeDtypeStruct((rows, D_MODEL), jnp.float32),
        compiler_params=pltpu.CompilerParams(
            dimension_semantics=("arbitrary",), vmem_limit_bytes=VMEM_LIMIT),
        name="outproj",
    )(g, s, h, wo, nw)


def _ffn_kernel(*refs, batch, tm, tf):
    if batch:
        (h_ref, nw_pre_ref, wg_ref, wu_ref, cw_ref, wd_ref, nw_post_ref, hist_ref,
         y_ref, graw_ref, xn_ref, acc_ref, xe_ref) = refs
    else:
        (h_ref, nw_pre_ref, wg_ref, wu_ref, cw_ref, wd_ref, nw_post_ref, hist_ref,
         y_ref, graw_ref, xn_ref, acc_ref, xe_ref, carry_ref) = refs
    i = pl.program_id(0)
    j = pl.program_id(1)
    rb = tm // FFN_ROW_BLOCKS

    @pl.when(j == 0)
    def _():
        h = h_ref[...]
        xn_ref[...] = (h * _rms_scale(h) * nw_pre_ref[...]).astype(jnp.bfloat16)
        acc_ref[...] = jnp.zeros_like(acc_ref)

    if batch:
        xe_ref[:, 0:SUBLANES, :] = hist_ref[...]
    else:
        @pl.when(i == 0)
        def _():
            carry_ref[pl.ds(j, 1)] = hist_ref[...].reshape(1, SUBLANES, tf)
        xe_ref[:, 0:SUBLANES, :] = carry_ref[pl.ds(j, 1)]

    def gate_up(r):
        xn = xn_ref[r * rb:(r + 1) * rb, :]
        return (jnp.dot(xn, wg_ref[...], preferred_element_type=jnp.float32),
                jnp.dot(xn, wu_ref[...], preferred_element_type=jnp.float32))

    nxt = gate_up(0)
    for r in range(FFN_ROW_BLOCKS):
        rows = slice(r * rb, (r + 1) * rb)
        gate, up = nxt
        if r + 1 < FFN_ROW_BLOCKS:
            nxt = gate_up(r + 1)
        if batch:
            seqs = slice(r * rb // SUBLANES, (r + 1) * rb // SUBLANES)
            graw_ref[rows, :] = gate
            xe_ref[seqs, SUBLANES:2 * SUBLANES, :] = gate.reshape(rb // SUBLANES, SUBLANES, tf)
            taps = [xe_ref[seqs, SUBLANES - s:2 * SUBLANES - s, :] for s in range(FFN_CONV)]
        else:
            base = SUBLANES + r * rb
            xe_ref[:, base:base + rb, :] = gate.reshape(1, rb, tf)
            taps = [xe_ref[:, base - s:base - s + rb, :] for s in range(FFN_CONV)]
        conv = None
        for s, tap in enumerate(taps):
            term = tap * cw_ref[FFN_CONV - 1 - s:FFN_CONV - s, :]
            conv = term if conv is None else conv + term
        act = (_silu(conv.reshape(rb, tf)) * up).astype(jnp.bfloat16)
        acc_ref[rows, :] += jnp.dot(act, wd_ref[...], preferred_element_type=jnp.float32)
    if not batch:
        tail = xe_ref[:, tm:tm + SUBLANES, :]
        carry_ref[pl.ds(j, 1)] = tail
        graw_ref[...] = tail.reshape(SUBLANES, tf)

    @pl.when(j == pl.num_programs(1) - 1)
    def _():
        y = acc_ref[...]
        y_ref[...] = h_ref[...] + y * _rms_scale(y) * nw_post_ref[...]


def _ffn(h, nw_pre, wg, wu, cw, wd, nw_post, hist, *, batch, tm, tf):
    rows = h.shape[0]
    nj = D_FF // tf
    in_specs = [
        pl.BlockSpec((tm, D_MODEL), lambda i, j: (i, 0)),
        pl.BlockSpec((1, D_MODEL), lambda i, j: (0, 0)),
        pl.BlockSpec((D_MODEL, tf), lambda i, j: (0, j)),
        pl.BlockSpec((D_MODEL, tf), lambda i, j: (0, j)),
        pl.BlockSpec((FFN_CONV, tf), lambda i, j: (0, j)),
        pl.BlockSpec((tf, D_MODEL), lambda i, j: (j, 0)),
        pl.BlockSpec((1, D_MODEL), lambda i, j: (0, 0)),
    ]
    args = [h, nw_pre, wg, wu, cw, wd, nw_post, hist]
    scratch = [pltpu.VMEM((tm, D_MODEL), jnp.bfloat16), pltpu.VMEM((tm, D_MODEL), jnp.float32)]
    if batch:
        in_specs.append(pl.BlockSpec((tm // SUBLANES, SUBLANES, tf), lambda i, j: (i, 0, j)))
        graw_spec = pl.BlockSpec((tm, tf), lambda i, j: (i, j))
        graw_shape = jax.ShapeDtypeStruct((rows, D_FF), jnp.float32)
        scratch.append(pltpu.VMEM((tm // SUBLANES, 2 * SUBLANES, tf), jnp.float32))
    else:
        in_specs.append(pl.BlockSpec((SUBLANES, tf), lambda i, j: (0, j)))
        graw_spec = pl.BlockSpec((SUBLANES, tf), lambda i, j: (i, j))
        graw_shape = jax.ShapeDtypeStruct((rows // tm * SUBLANES, D_FF), jnp.float32)
        scratch.append(pltpu.VMEM((1, SUBLANES + tm, tf), jnp.float32))
        scratch.append(pltpu.VMEM((nj, SUBLANES, tf), jnp.float32))
    return pl.pallas_call(
        functools.partial(_ffn_kernel, batch=batch, tm=tm, tf=tf),
        grid=(rows // tm, nj),
        in_specs=in_specs,
        out_specs=[pl.BlockSpec((tm, D_MODEL), lambda i, j: (i, 0)), graw_spec],
        out_shape=[jax.ShapeDtypeStruct((rows, D_MODEL), jnp.float32), graw_shape],
        scratch_shapes=scratch,
        compiler_params=pltpu.CompilerParams(
            dimension_semantics=("arbitrary", "arbitrary"), vmem_limit_bytes=VMEM_LIMIT),
        name="ffn_batch" if batch else "ffn_seq",
    )(*args)


def kernel(x_prompt, x_sample, cache_swa_meta_kv, cache_swa_window_kv, state_gdn_conv, state_gdn, state_ffn_conv, meta_tokens, rel_bias_table, w_in, gdn_conv_w, gdn_a_log, gdn_dt_bias, gdn_norm_w, swa_sinks, w_out, norm_mix_pre, norm_mix_post, norm_ffn_pre, norm_ffn_post, ffn_w_gate, ffn_w_up, ffn_conv_w, ffn_w_down):
    f32, bf16 = jnp.float32, jnp.bfloat16
    seq = x_prompt.shape[1]
    dec_b, dec_t = x_sample.shape[0], x_sample.shape[1]
    n_dec = dec_b * dec_t
    assert x_prompt.shape[0] == 1 and seq % CHUNK == 0 and dec_t == SUBLANES and n_dec % CHUNK == 0

    wi = w_in[0]
    n_ba = 2 * GDN_HEADS
    w_in_p = jnp.concatenate([wi[:, :COL_SQ].astype(bf16), wi[:, COL_SQ + n_ba:].astype(bf16)], axis=1)
    w_ba = jnp.pad(wi[:, COL_SQ:COL_SQ + n_ba], ((0, 0), (0, LANES - n_ba))).astype(bf16)
    wo = w_out[0].astype(bf16)
    wg = ffn_w_gate[0].astype(bf16)
    wu = ffn_w_up[0].astype(bf16)
    wd = ffn_w_down[0].astype(bf16)
    lane_pad = lambda v: jnp.pad(v.reshape(1, GDN_HEADS), ((0, 0), (GDN_HEADS, LANES - 2 * GDN_HEADS)))
    alog_row = lane_pad(gdn_a_log[0])
    dtb_row = lane_pad(gdn_dt_bias[0])
    gnw = gdn_norm_w[0].reshape(1, HEAD_DIM)

    pad_rows = CHUNK - N_META
    n_small = n_dec + CHUNK
    x_big = x_prompt.reshape(seq, D_MODEL)
    x_small = jnp.concatenate(
        [x_sample.reshape(n_dec, D_MODEL), jnp.zeros((pad_rows, D_MODEL), f32), meta_tokens.astype(f32)], axis=0)
    nw = norm_mix_pre[0].reshape(1, D_MODEL)
    cw = gdn_conv_w[0]
    proj_small, ba_small = _inproj(x_small, nw, w_in_p, w_ba, tm=n_small, tn=512, row_chunk=128)
    proj_big, ba_big, qkv_tail = _inproj(
        x_big, nw, w_in_p, w_ba, (cw, proj_small[n_small - SUBLANES:, :GDN_QKV]),
        tm=1024, tn=512, row_chunk=128)

    small_chunks = proj_small.reshape(n_small // CHUNK, CHUNK, PROJ_COLS)
    small_groups = proj_small.reshape(n_small // SUBLANES, SUBLANES, PROJ_COLS)
    last_chunk = n_small // CHUNK - 1
    gdn_meta, s_meta = _gdn(
        small_chunks, ba_small.reshape(n_small // CHUNK, CHUNK, LANES), lambda s: (last_chunk, 0, 0),
        jnp.zeros((1, SUBLANES, GDN_QKV), f32), lambda s: (0, 0, 0),
        jnp.zeros((1, GDN_HEADS, HEAD_DIM, HEAD_DIM), f32), cw, alog_row, dtb_row, gnw,
        n_steps=1, nb=1, seq=CHUNK, group=CHUNK, carry=True, pad_rows=pad_rows, preconv=False)
    gdn_big, s_prompt = _gdn(
        proj_big.reshape(1, seq, PROJ_COLS), ba_big.reshape(1, seq, LANES), lambda s: (0, s, 0),
        jnp.zeros((1, SUBLANES, GDN_QKV), f32), lambda s: (0, 0, 0),
        s_meta, cw, alog_row, dtb_row, gnw,
        n_steps=seq // (GDN_SEQ_CHUNKS * CHUNK), nb=1, seq=GDN_SEQ_CHUNKS * CHUNK, group=CHUNK,
        carry=True, pad_rows=0, preconv=True)
    hist_gdn = jnp.pad(state_gdn_conv[0], ((0, 0), (SUBLANES - (GDN_CONV - 1), 0), (0, 0)))
    nb_gdn = CHUNK // dec_t
    gdn_small, s_sample = _gdn(
        small_groups, ba_small.reshape(n_small // SUBLANES, SUBLANES, LANES), lambda s: (s, 0, 0),
        hist_gdn, lambda s: (s, 0, 0),
        state_gdn, cw, alog_row, dtb_row, gnw,
        n_steps=dec_b // nb_gdn, nb=nb_gdn, seq=dec_t, group=dec_t, carry=False, pad_rows=0, preconv=False)

    qi = np.arange(WINDOW)[:, None]
    kj = np.arange(WINDOW)[None, :]
    mi = np.arange(N_META)[None, :]
    ti = np.arange(dec_t)[:, None]
    small_keys = np.arange(SMALL_KEYS)[None, :]
    new_keys = small_keys - N_META
    id_arrays = [
        _bucket_ids(qi - kj, qi >= kj),
        _bucket_ids(qi - kj + WINDOW, kj > qi),
        _bucket_ids(qi + N_META - mi, np.ones((WINDOW, N_META), bool)),
        _bucket_ids(qi + N_META - mi + WINDOW, np.ones((WINDOW, N_META), bool)),
        _bucket_ids(ti + WINDOW - kj, kj > ti),
        _bucket_ids(np.where(new_keys < 0, PAST_LEN + ti - small_keys, ti - new_keys),
                    (new_keys < 0) | ((new_keys <= ti) & (new_keys < dec_t))),
        _bucket_ids(mi.T - mi, mi.T >= mi),
    ]
    bcur, bprev, bm0, bfar, bwin, bsmall, bmm = _bias_tables(rel_bias_table, id_arrays)
    sink_rows = lambda q: jnp.repeat(swa_sinks[0].reshape(SWA_KV_HEADS, SWA_GROUP), q, axis=1)[..., None]

    sq_blk = COL_SQ // SWA_WIDTH
    kv_blk = COL_KV // KV_WIDTH
    meta_blk = (n_small - N_META) // N_META
    full3 = lambda a: pl.BlockSpec(a.shape, lambda j: (0, 0, 0))
    sink_p = sink_rows(WINDOW)
    swa_big = pl.pallas_call(
        _swa_prompt_kernel,
        grid=(seq // WINDOW,),
        in_specs=[
            pl.BlockSpec((WINDOW, SWA_WIDTH), lambda j: (j, sq_blk)),
            pl.BlockSpec((WINDOW, KV_WIDTH), lambda j: (j, kv_blk)),
            pl.BlockSpec((WINDOW, KV_WIDTH), lambda j: (jnp.maximum(j - 1, 0), kv_blk)),
            pl.BlockSpec((N_META, KV_WIDTH), lambda j: (meta_blk, kv_blk)),
            full3(bcur), full3(bprev), full3(bm0), full3(bfar), full3(sink_p),
        ],
        out_specs=pl.BlockSpec((WINDOW, SWA_WIDTH), lambda j: (j, 0)),
        out_shape=jax.ShapeDtypeStruct((seq, SWA_WIDTH), bf16),
        compiler_params=pltpu.CompilerParams(
            dimension_semantics=("arbitrary",), vmem_limit_bytes=VMEM_LIMIT),
        name="swa_prompt",
    )(proj_big, proj_big, proj_big, proj_small, bcur, bprev, bm0, bfar, sink_p)

    nb_swa = 8
    sink_s = sink_rows(dec_t)
    win = cache_swa_window_kv[0].reshape(dec_b, WINDOW, KV_WIDTH)
    meta_kv = cache_swa_meta_kv[0].reshape(dec_b, N_META, KV_WIDTH)
    swa_small, win_new = pl.pallas_call(
        functools.partial(_swa_sample_kernel, nb=nb_swa, seq=dec_t),
        grid=(dec_b // nb_swa,),
        in_specs=[
            pl.BlockSpec((nb_swa, dec_t, SWA_WIDTH), lambda j: (j, 0, sq_blk)),
            pl.BlockSpec((nb_swa, dec_t, KV_WIDTH), lambda j: (j, 0, kv_blk)),
            pl.BlockSpec((nb_swa, WINDOW, KV_WIDTH), lambda j: (j, 0, 0)),
            pl.BlockSpec((nb_swa, N_META, KV_WIDTH), lambda j: (j, 0, 0)),
            full3(bwin), full3(bsmall), full3(sink_s),
        ],
        out_specs=[pl.BlockSpec((nb_swa * dec_t, SWA_WIDTH), lambda j: (j, 0)),
                   pl.BlockSpec((nb_swa, WINDOW, KV_WIDTH), lambda j: (j, 0, 0))],
        out_shape=[jax.ShapeDtypeStruct((n_dec, SWA_WIDTH), bf16),
                   jax.ShapeDtypeStruct((dec_b, WINDOW, KV_WIDTH), f32)],
        compiler_params=pltpu.CompilerParams(
            dimension_semantics=("arbitrary",), vmem_limit_bytes=VMEM_LIMIT),
        name="swa_sample",
    )(small_groups, small_groups, win, meta_kv, bwin, bsmall, sink_s)

    sink_m = sink_rows(N_META)
    swa_meta = pl.pallas_call(
        _swa_meta_kernel,
        grid=(1,),
        in_specs=[
            pl.BlockSpec((N_META, SWA_WIDTH), lambda j: (meta_blk, sq_blk)),
            pl.BlockSpec((N_META, KV_WIDTH), lambda j: (meta_blk, kv_blk)),
            full3(bmm), full3(sink_m),
        ],
        out_specs=pl.BlockSpec((N_META, SWA_WIDTH), lambda j: (0, 0)),
        out_shape=jax.ShapeDtypeStruct((N_META, SWA_WIDTH), bf16),
        name="swa_meta",
    )(proj_small, proj_small, bmm, sink_m)

    nw_post = norm_mix_post[0].reshape(1, D_MODEL)
    nf_pre = norm_ffn_pre[0].reshape(1, D_MODEL)
    nf_post = norm_ffn_post[0].reshape(1, D_MODEL)
    fcw = ffn_conv_w[0]
    gdn_small_all = jnp.concatenate([gdn_small, gdn_meta], axis=0)
    swa_small_all = jnp.concatenate([swa_small, jnp.zeros((pad_rows, SWA_WIDTH), bf16), swa_meta], axis=0)
    h_small = _outproj(gdn_small_all, swa_small_all, x_small, wo, nw_post, tm=n_small // 2)
    hist_ffn = jnp.pad(state_ffn_conv[0], ((0, CHUNK // SUBLANES), (SUBLANES - (FFN_CONV - 1), 0), (0, 0)))
    y_small, g_small = _ffn(h_small, nf_pre, wg, wu, fcw, wd, nf_post, hist_ffn,
                            batch=True, tm=n_small // 2, tf=512)
    h_big = _outproj(gdn_big, swa_big, x_big, wo, nw_post, tm=512)
    y_big, g_tail = _ffn(h_big, nf_pre, wg, wu, fcw, wd, nf_post, g_small[n_small - SUBLANES:],
                         batch=False, tm=512, tf=512)

    kv_shape = lambda n: (1, n, 2, SWA_KV_HEADS, HEAD_DIM)
    kv_small = proj_small[:, COL_KV:COL_KV + KV_WIDTH]
    y_prompt = y_big.reshape(1, seq, D_MODEL)
    y_sample = y_small[:n_dec].reshape(dec_b, dec_t, D_MODEL)
    p_meta_kv = kv_small[n_small - N_META:].reshape(kv_shape(N_META))[None]
    p_window_kv = proj_big[seq - WINDOW:, COL_KV:COL_KV + KV_WIDTH].reshape(kv_shape(WINDOW))[None]
    p_gdn_conv = qkv_tail[qkv_tail.shape[0] - (GDN_CONV - 1):].reshape(1, 1, GDN_CONV - 1, GDN_QKV)
    p_gdn_state = s_prompt[None]
    p_ffn_conv = g_tail[g_tail.shape[0] - (FFN_CONV - 1):].reshape(1, 1, FFN_CONV - 1, D_FF)
    s_window_kv = win_new.reshape(1, dec_b, WINDOW, 2, SWA_KV_HEADS, HEAD_DIM)
    s_gdn_conv = proj_small[:n_dec, :GDN_QKV].reshape(dec_b, dec_t, GDN_QKV)[:, dec_t - (GDN_CONV - 1):][None]
    s_gdn_state = s_sample
    s_ffn_conv = g_small[:n_dec].reshape(dec_b, dec_t, D_FF)[:, dec_t - (FFN_CONV - 1):][None]
    return (y_prompt, y_sample, p_meta_kv, p_window_kv, p_gdn_conv, p_gdn_state, p_ffn_conv,
            s_window_kv, s_gdn_conv, s_gdn_state, s_ffn_conv)
```

```python
import functools
import math

import numpy as np
import jax
import jax.numpy as jnp
from jax import lax
from jax.experimental import pallas as pl
from jax.experimental.pallas import tpu as pltpu

D_MODEL = 2048
HEAD_DIM = 128
GDN_HEADS = 8
GDN_WIDTH = GDN_HEADS * HEAD_DIM
GDN_QKV = 3 * GDN_WIDTH
SWA_HEADS = 8
SWA_KV_HEADS = 2
SWA_GROUP = SWA_HEADS // SWA_KV_HEADS
SWA_WIDTH = SWA_HEADS * HEAD_DIM
KV_SLOTS = 2 * SWA_KV_HEADS
KV_WIDTH = KV_SLOTS * HEAD_DIM
WINDOW = 128
N_META = 16
N_BUCKETS = 32
MAX_DISTANCE = 128
GDN_CONV = 4
FFN_CONV = 3
D_FF = 5632
EPS = 1e-6
PAST_LEN = 16384

SUBLANES = 8
LANES = 128

CHUNK = 128
assert CHUNK == HEAD_DIM == LANES
INV_BASE = 16
GDN_SEQ_CHUNKS = 2
INPROJ_ROW_BLOCKS = 4
FFN_ROW_BLOCKS = 4

COL_Z = GDN_QKV
COL_SQ = COL_Z + GDN_WIDTH
COL_KV = COL_SQ + SWA_WIDTH
PROJ_COLS = COL_KV + KV_WIDTH
GDN_COLS = COL_SQ

SWA_Q_BLOCKS = 4
NEG = -1e30
VMEM_LIMIT = 56 * 1024 * 1024

_NT = (((1,), (1,)), ((), ()))


def _dot(a, b):
    return jnp.dot(a.astype(jnp.bfloat16), b.astype(jnp.bfloat16), preferred_element_type=jnp.float32)


def _dot_nt(a, b):
    return lax.dot_general(a.astype(jnp.bfloat16), b.astype(jnp.bfloat16), _NT,
                           preferred_element_type=jnp.float32)


def _split(a):
    hi = a.astype(jnp.bfloat16)
    lo = (a - hi.astype(jnp.float32)).astype(jnp.bfloat16)
    return hi, lo


def _dot3(a, b):
    ah, al = _split(a)
    bh, bl = _split(b)
    d = functools.partial(jnp.dot, preferred_element_type=jnp.float32)
    return d(ah, bh) + (d(ah, bl) + d(al, bh))


_dot_inv = _dot


def _dot_exact(a, b, dims=None):
    if dims is None:
        return jnp.dot(a, b, precision=lax.Precision.HIGHEST, preferred_element_type=jnp.float32)
    return lax.dot_general(a, b, dims, precision=lax.Precision.HIGHEST,
                           preferred_element_type=jnp.float32)


def _pack_w_in_kernel(a_ref, b_ref, o_ref, *, first_shifted, shift):
    j = pl.program_id(0)

    @pl.when(j < first_shifted)
    def _():
        o_ref[...] = a_ref[...].astype(o_ref.dtype)

    @pl.when(j >= first_shifted)
    def _():
        o_ref[...] = jnp.concatenate([a_ref[:, shift:], b_ref[:, :shift]], axis=1).astype(o_ref.dtype)


def _pack_w_in(wi, n_ba, *, tn):
    n_blocks = PROJ_COLS // tn
    assert COL_SQ % tn == 0 and wi.shape[1] == PROJ_COLS + n_ba
    return pl.pallas_call(
        functools.partial(_pack_w_in_kernel, first_shifted=COL_SQ // tn, shift=n_ba),
        grid=(n_blocks,),
        in_specs=[pl.BlockSpec((D_MODEL, tn), lambda j: (0, j)),
                  pl.BlockSpec((D_MODEL, tn), lambda j: (0, jnp.maximum(j, COL_SQ // tn) + 1))],
        out_specs=pl.BlockSpec((D_MODEL, tn), lambda j: (0, j)),
        out_shape=jax.ShapeDtypeStruct((D_MODEL, PROJ_COLS), jnp.bfloat16),
        compiler_params=pltpu.CompilerParams(
            dimension_semantics=("arbitrary",), vmem_limit_bytes=VMEM_LIMIT),
        name="pack_w_in",
    )(wi, wi)


def _rms_scale(x):
    return lax.rsqrt(jnp.mean(x * x, axis=-1, keepdims=True) + EPS)


def _silu(x):
    return x * jax.nn.sigmoid(x)


def _inproj_kernel(*refs, row_chunk, conv_tiles):
    if conv_tiles:
        (x_ref, nw_ref, w_ref, wba_ref, cw_ref, hist_ref,
         o_ref, ba_ref, tail_ref, xn_ref, xe_ref, carry_ref) = refs
    else:
        x_ref, nw_ref, w_ref, wba_ref, o_ref, ba_ref, xn_ref = refs
    i = pl.program_id(0)
    j = pl.program_id(1)
    tm, tn = o_ref.shape

    @pl.when(j == 0)
    def _():
        def body(c, carry):
            rows = pl.ds(pl.multiple_of(c * row_chunk, row_chunk), row_chunk)
            x = x_ref[rows, :]
            xn_ref[rows, :] = (x * _rms_scale(x) * nw_ref[...]).astype(jnp.bfloat16)
            return carry
        lax.fori_loop(0, tm // row_chunk, body, 0)
        ba_ref[...] = jnp.dot(xn_ref[...], wba_ref[...], preferred_element_type=jnp.float32)

    def plain():
        o_ref[...] = jnp.dot(xn_ref[...], w_ref[...], preferred_element_type=jnp.float32)

    if not conv_tiles:
        plain()
        return
    pl.when(j >= conv_tiles)(plain)

    @pl.when(j < conv_tiles)
    def _():
        @pl.when(i == 0)
        def _():
            carry_ref[pl.ds(j, 1)] = hist_ref[...].reshape(1, SUBLANES, tn)
        xe_ref[0:SUBLANES, :] = carry_ref[pl.ds(j, 1)].reshape(SUBLANES, tn)
        rb = tm // INPROJ_ROW_BLOCKS
        raw_block = lambda r: jnp.dot(xn_ref[r * rb:(r + 1) * rb, :], w_ref[...],
                                      preferred_element_type=jnp.float32)
        nxt = raw_block(0)
        for r in range(INPROJ_ROW_BLOCKS):
            raw = nxt
            if r + 1 < INPROJ_ROW_BLOCKS:
                nxt = raw_block(r + 1)
            base = SUBLANES + r * rb
            xe_ref[base:base + rb, :] = raw
            conv = None
            for s in range(GDN_CONV):
                term = xe_ref[base - s:base - s + rb, :] * cw_ref[GDN_CONV - 1 - s:GDN_CONV - s, :]
                conv = term if conv is None else conv + term
            o_ref[r * rb:(r + 1) * rb, :] = _silu(conv)
        tail = xe_ref[tm:tm + SUBLANES, :]
        carry_ref[pl.ds(j, 1)] = tail.reshape(1, SUBLANES, tn)
        tail_ref[...] = tail


def _inproj(x, nw, w, wba, conv=None, *, tm, tn, row_chunk):
    rows = x.shape[0]
    conv_tiles = GDN_QKV // tn if conv else 0
    in_specs = [
        pl.BlockSpec((tm, D_MODEL), lambda i, j: (i, 0)),
        pl.BlockSpec((1, D_MODEL), lambda i, j: (0, 0)),
        pl.BlockSpec((D_MODEL, tn), lambda i, j: (0, j)),
        pl.BlockSpec((D_MODEL, LANES), lambda i, j: (0, 0)),
    ]
    out_specs = [pl.BlockSpec((tm, tn), lambda i, j: (i, j)),
                 pl.BlockSpec((tm, LANES), lambda i, j: (i, 0))]
    out_shape = [jax.ShapeDtypeStruct((rows, PROJ_COLS), jnp.float32),
                 jax.ShapeDtypeStruct((rows, LANES), jnp.float32)]
    scratch = [pltpu.VMEM((tm, D_MODEL), jnp.bfloat16)]
    args = [x, nw, w, wba]
    if conv:
        conv_col = lambda i, j: (0, jnp.minimum(j, conv_tiles - 1))
        in_specs += [pl.BlockSpec((GDN_CONV, tn), conv_col), pl.BlockSpec((SUBLANES, tn), conv_col)]
        out_specs.append(pl.BlockSpec((SUBLANES, tn), lambda i, j: (i, jnp.minimum(j, conv_tiles - 1))))
        out_shape.append(jax.ShapeDtypeStruct((rows // tm * SUBLANES, GDN_QKV), jnp.float32))
        scratch += [pltpu.VMEM((SUBLANES + tm, tn), jnp.float32),
                    pltpu.VMEM((conv_tiles, SUBLANES, tn), jnp.float32)]
        args += list(conv)
    return pl.pallas_call(
        functools.partial(_inproj_kernel, row_chunk=row_chunk, conv_tiles=conv_tiles),
        grid=(rows // tm, PROJ_COLS // tn),
        in_specs=in_specs,
        out_specs=out_specs,
        out_shape=out_shape,
        scratch_shapes=scratch,
        compiler_params=pltpu.CompilerParams(
            dimension_semantics=("arbitrary", "arbitrary"), vmem_limit_bytes=VMEM_LIMIT),
        name="inproj_conv" if conv else "inproj",
    )(*args)


def _tri_inverse(lms, ri, ci):
    shift = INV_BASE.bit_length() - 1
    eye = (ri == ci).astype(jnp.float32)
    in_block = (ri >> shift) == (ci >> shift)
    ps = [jnp.where(in_block, lm, 0.0) for lm in lms]
    ts = [eye - p for p in ps]
    for _ in range(shift - 1):
        ps = [_dot_inv(p, p) for p in ps]
        ts = [t + _dot_inv(t, p) for t, p in zip(ts, ps)]
    size = INV_BASE
    while size < CHUNK:
        shift += 1
        in_pair = (ri >> shift) == (ci >> shift)
        off_mask = in_pair & jnp.logical_not(in_block)
        tos = [_dot_inv(t, jnp.where(off_mask, lm, 0.0)) for t, lm in zip(ts, lms)]
        ts = [t - _dot_inv(to, t) for t, to in zip(ts, tos)]
        in_block = in_pair
        size *= 2
    return ts


def _gdn_kernel(x_ref, ba_ref, hist_ref, s0_ref, cw_ref, alog_ref, dtb_ref, gnw_ref,
                o_ref, sout_ref, xe_ref, s_ref, *, nb, seq, group, carry, pad_rows, preconv):
    step = pl.program_id(0)
    rows = nb * seq
    n_chunks = rows // CHUNK
    n_groups = CHUNK // group
    gshift = group.bit_length() - 1

    if carry:
        @pl.when(step == 0)
        def _():
            s_ref[...] = s0_ref[0]
    if not preconv:
        if carry:
            @pl.when(step == 0)
            def _():
                xe_ref[:, 0:SUBLANES, :] = hist_ref[...]
        else:
            xe_ref[:, 0:SUBLANES, :] = hist_ref[...]
        xe_ref[:, SUBLANES:SUBLANES + seq, :] = x_ref[:, :, 0:GDN_QKV]

    seq_rows = min(seq, CHUNK)
    seqs_per_chunk = CHUNK // seq_rows

    def chunk_rows(ref, c, row_off, cols):
        if seq >= CHUNK:
            start = row_off + c * CHUNK
            return ref[0:1, start:start + CHUNK, cols]
        b0 = c * seqs_per_chunk
        return ref[b0:b0 + seqs_per_chunk, row_off:row_off + seq, cols]

    def conv_chunk(col, c):
        cols = slice(col, col + HEAD_DIM)
        if preconv:
            return chunk_rows(x_ref, c, 0, cols).reshape(CHUNK, HEAD_DIM)
        acc = None
        for s in range(GDN_CONV):
            term = chunk_rows(xe_ref, c, SUBLANES - s, cols) * cw_ref[GDN_CONV - 1 - s:GDN_CONV - s, cols]
            acc = term if acc is None else acc + term
        return _silu(acc).reshape(CHUNK, HEAD_DIM)

    ri = lax.broadcasted_iota(jnp.int32, (CHUNK, CHUNK), 0)
    ci = lax.broadcasted_iota(jnp.int32, (CHUNK, CHUNK), 1)
    same = (ri >> gshift) == (ci >> gshift)
    m_incl = same & (ri >= ci)
    m_strict = same & (ri > ci)
    f_incl = m_incl.astype(jnp.float32)
    f_same = same.astype(jnp.float32)
    lane = lax.broadcasted_iota(jnp.int32, (CHUNK, LANES), 1)
    row_in_chunk = lax.broadcasted_iota(jnp.int32, (CHUNK, LANES), 0)

    pre = []
    for c in range(n_chunks):
        bac = chunk_rows(ba_ref, c, 0, slice(0, LANES)).reshape(CHUNK, LANES)
        beta_all = jax.nn.sigmoid(bac)
        sp_in = bac + dtb_ref[...]
        softplus = jnp.maximum(sp_in, 0.0) + jnp.log1p(jnp.exp(-jnp.abs(sp_in)))
        g_all = -jnp.exp(alog_ref[...]) * softplus
        if pad_rows and c == 0:
            valid = row_in_chunk >= pad_rows
            beta_all = jnp.where(valid, beta_all, 0.0)
            g_all = jnp.where(valid, g_all, 0.0)
        g_all = jnp.where((lane >= GDN_HEADS) & (lane < 2 * GDN_HEADS), g_all, 0.0)
        gc_col = _dot_exact(f_incl, g_all)
        gtot_col = _dot_exact(f_same, g_all)
        gc_row = _dot_exact(g_all.T, f_incl, _NT)

        for h in range(GDN_HEADS):
            qh = conv_chunk(h * HEAD_DIM, c)
            kh = conv_chunk(GDN_WIDTH + h * HEAD_DIM, c)
            vh = conv_chunk(2 * GDN_WIDTH + h * HEAD_DIM, c)
            qh = qh * lax.rsqrt(jnp.sum(qh * qh, -1, keepdims=True) + EPS) * (HEAD_DIM ** -0.5)
            kh = kh * lax.rsqrt(jnp.sum(kh * kh, -1, keepdims=True) + EPS)
            bcast = lambda col: jnp.broadcast_to(col, (CHUNK, HEAD_DIM))
            gcc = bcast(gc_col[:, GDN_HEADS + h:GDN_HEADS + h + 1])
            gtc = bcast(gtot_col[:, GDN_HEADS + h:GDN_HEADS + h + 1])
            beta = bcast(beta_all[:, h:h + 1])
            gcr = gc_row[GDN_HEADS + h:GDN_HEADS + h + 1, :]
            decay = jnp.exp(jnp.where(m_incl, gcc - gcr, NEG))
            kb = kh * beta
            egc = jnp.exp(gcc)
            pre.append(dict(
                c=c, h=h,
                lm=jnp.where(m_strict, _dot_nt(kb, kh) * decay, 0.0),
                qk=_dot_nt(qh, kh) * decay,
                rhs=jnp.concatenate([vh * beta, kb * egc], axis=1),
                qg=qh * egc,
                kd_t=(kh * jnp.exp(gtc - gcc)).T,
                gl=jnp.exp(gtc)))

    inverses = _tri_inverse([p["lm"] for p in pre], ri, ci)
    sols = [_dot_inv(t, p["rhs"]) for t, p in zip(inverses, pre)]

    for c in range(n_chunks):
        r0 = c * CHUNK
        items = [(p, sol) for p, sol in zip(pre, sols) if p["c"] == c]
        state = lambda h, b: s_ref[h] if carry else s0_ref[c * n_groups + b, h]
        ws, qs = [], []
        for p, sol in items:
            w = sol[:, HEAD_DIM:]
            ws_parts, qs_parts = [], []
            for b in range(n_groups):
                g0 = b * group
                wq = jnp.concatenate([w[g0:g0 + group], p["qg"][g0:g0 + group]], axis=0)
                res = _dot(wq, state(p["h"], b))
                ws_parts.append(res[:group])
                qs_parts.append(res[group:])
            ws.append(ws_parts[0] if n_groups == 1 else jnp.concatenate(ws_parts, axis=0))
            qs.append(qs_parts[0] if n_groups == 1 else jnp.concatenate(qs_parts, axis=0))
        v_new = [sol[:, :HEAD_DIM] - w for (p, sol), w in zip(items, ws)]
        o = [a + _dot(p["qk"], vn) for a, (p, sol), vn in zip(qs, items, v_new)]
        for (p, sol), vn in zip(items, v_new):
            h = p["h"]
            for b in range(n_groups):
                g0 = b * group
                kd_b = p["kd_t"] if n_groups == 1 else jnp.where((ci >> gshift) == b, p["kd_t"], 0.0)
                st = state(h, b) * p["gl"][g0:g0 + 1, :] + _dot(kd_b, vn)
                if carry:
                    s_ref[h] = st
                else:
                    sout_ref[c * n_groups + b, h] = st
        for (p, sol), oh in zip(items, o):
            h = p["h"]
            z = chunk_rows(x_ref, c, 0, slice(COL_Z + h * HEAD_DIM, COL_Z + (h + 1) * HEAD_DIM)).reshape(CHUNK, HEAD_DIM)
            y = oh * lax.rsqrt(jnp.mean(oh * oh, -1, keepdims=True) + EPS) * gnw_ref[...] * _silu(z)
            o_ref[r0:r0 + CHUNK, h * HEAD_DIM:(h + 1) * HEAD_DIM] = y.astype(o_ref.dtype)

    if carry:
        if not preconv:
            xe_ref[:, 0:SUBLANES, :] = xe_ref[:, seq:seq + SUBLANES, :]

        @pl.when(step == pl.num_programs(0) - 1)
        def _():
            sout_ref[0] = s_ref[...]


def _gdn(x3, ba3, x_idx, hist, hist_idx, s0, cw, alog_row, dtb_row, gnw, *,
         n_steps, nb, seq, group, carry, pad_rows, preconv):
    rows = nb * seq
    if carry:
        state_spec = pl.BlockSpec((1, GDN_HEADS, HEAD_DIM, HEAD_DIM), lambda s: (0, 0, 0, 0))
        state_shape = (1, GDN_HEADS, HEAD_DIM, HEAD_DIM)
    else:
        n_states = rows // group
        state_spec = pl.BlockSpec((None, n_states, GDN_HEADS, HEAD_DIM, HEAD_DIM), lambda s: (0, s, 0, 0, 0))
        state_shape = (1, n_steps * n_states, GDN_HEADS, HEAD_DIM, HEAD_DIM)
    full = lambda shape: pl.BlockSpec(shape, lambda s: (0,) * len(shape))
    return pl.pallas_call(
        functools.partial(_gdn_kernel, nb=nb, seq=seq, group=group, carry=carry, pad_rows=pad_rows,
                          preconv=preconv),
        grid=(n_steps,),
        in_specs=[
            pl.BlockSpec((nb, seq, GDN_COLS), x_idx),
            pl.BlockSpec((nb, seq, LANES), x_idx),
            pl.BlockSpec((nb, SUBLANES, GDN_QKV), hist_idx),
            state_spec,
            full((GDN_CONV, GDN_QKV)),
            full((1, LANES)),
            full((1, LANES)),
            full((1, HEAD_DIM)),
        ],
        out_specs=[
            pl.BlockSpec((rows, GDN_WIDTH), lambda s: (s, 0)),
            state_spec,
        ],
        out_shape=[
            jax.ShapeDtypeStruct((n_steps * rows, GDN_WIDTH), jnp.bfloat16),
            jax.ShapeDtypeStruct(state_shape, jnp.float32),
        ],
        scratch_shapes=[
            pltpu.VMEM((nb, SUBLANES + (SUBLANES if preconv else seq), GDN_QKV), jnp.float32),
            pltpu.VMEM((GDN_HEADS, HEAD_DIM, HEAD_DIM), jnp.float32),
        ],
        compiler_params=pltpu.CompilerParams(
            dimension_semantics=("arbitrary",), vmem_limit_bytes=VMEM_LIMIT),
        name="gdn_seq" if carry else "gdn_batch",
    )(x3, ba3, hist, s0, cw, alog_row, dtb_row, gnw)


def _t5_bucket_np(dist):
    n = np.maximum(dist, 0)
    exact = N_BUCKETS // 2
    large = exact + (np.log(np.maximum(n, 1).astype(np.float32) / exact)
                     / math.log(MAX_DISTANCE / exact) * (N_BUCKETS - exact)).astype(np.int32)
    return np.where(n < exact, n, np.minimum(large, N_BUCKETS - 1)).astype(np.int32)


def _bucket_ids(dist, valid):
    return np.where(valid, _t5_bucket_np(dist), -1).astype(np.int32)


def _bias_kernel(table_ref, *refs):
    n = len(refs) // 2
    for ids_ref, out_ref in zip(refs[:n], refs[n:]):
        ids = ids_ref[...]
        nq = ids.shape[0]
        for head in range(SWA_HEADS):
            def body(b, acc):
                return jnp.where(ids == b, table_ref[b, head], acc)
            acc = lax.fori_loop(0, N_BUCKETS, body, jnp.full(ids.shape, NEG, jnp.float32))
            kh, g = divmod(head, SWA_GROUP)
            out_ref[kh, g * nq:(g + 1) * nq, :] = acc


def _bias_tables(rel_table, id_arrays):
    out_shapes = [jax.ShapeDtypeStruct((SWA_KV_HEADS, SWA_GROUP * a.shape[0], a.shape[1]), jnp.float32)
                  for a in id_arrays]
    vmem = pl.BlockSpec(memory_space=pltpu.VMEM)
    return pl.pallas_call(
        _bias_kernel,
        in_specs=[pl.BlockSpec(memory_space=pltpu.SMEM)] + [vmem] * len(id_arrays),
        out_specs=[vmem] * len(id_arrays),
        out_shape=out_shapes,
        name="swa_bias",
    )(rel_table, *[jnp.asarray(a) for a in id_arrays])


def _attend(problems):
    scale = HEAD_DIM ** -0.5
    scores = [[_dot_nt(q, k) * scale + b for k, b in zip(keys, biases)]
              for q, keys, _, biases, _ in problems]
    maxes = []
    for (_, _, _, _, sink), segs in zip(problems, scores):
        m = sink
        for s in segs:
            m = jnp.maximum(m, jnp.max(s, axis=-1, keepdims=True))
        maxes.append(m)
    probs = [[jnp.exp(s - m) for s in segs] for segs, m in zip(scores, maxes)]
    outs = []
    for (_, _, values, _, sink), ps, m in zip(problems, probs, maxes):
        acc = None
        for p, v in zip(ps, values):
            v_ones = jnp.concatenate([v, jnp.ones((v.shape[0], HEAD_DIM), v.dtype)], axis=1)
            pv = _dot(p, v_ones)
            acc = pv if acc is None else acc + pv
        den = acc[:, HEAD_DIM:] + jnp.exp(sink - m)
        outs.append(acc[:, :HEAD_DIM] / den)
    return outs


def _group_queries(q_rows, kh):
    return jnp.concatenate(
        [q_rows((kh * SWA_GROUP + g) * HEAD_DIM, (kh * SWA_GROUP + g + 1) * HEAD_DIM)
         for g in range(SWA_GROUP)], axis=0)


def _pad_keys(rows):
    return jnp.concatenate([rows, jnp.zeros((WINDOW - rows.shape[0], rows.shape[1]), rows.dtype)], axis=0)


def _k_cols(kh):
    return slice(kh * HEAD_DIM, (kh + 1) * HEAD_DIM)


def _v_cols(kh):
    return slice((SWA_KV_HEADS + kh) * HEAD_DIM, (SWA_KV_HEADS + kh + 1) * HEAD_DIM)


def _swa_prompt_kernel(q_ref, kvc_ref, kvp_ref, kvm_ref, bcur_ref, bprev_ref, bm0_ref, bfar_ref,
                       sink_ref, o_ref):
    first = pl.program_id(0) == 0
    problems = []
    for blk in range(SWA_Q_BLOCKS):
        rows = slice(blk * WINDOW, (blk + 1) * WINDOW)
        prev_ref, prev_rows = (kvp_ref, slice(0, WINDOW)) if blk == 0 else (
            kvc_ref, slice((blk - 1) * WINDOW, blk * WINDOW))
        for kh in range(SWA_KV_HEADS):
            ks, vs = _k_cols(kh), _v_cols(kh)
            first_block = first if blk == 0 else False
            b_prev = jnp.where(first_block, NEG, bprev_ref[kh])
            b_meta = jnp.where(first_block, bm0_ref[kh], bfar_ref[kh])
            problems.append((_group_queries(lambda a, b, rows=rows: q_ref[rows, a:b], kh),
                             [kvc_ref[rows, ks], prev_ref[prev_rows, ks], _pad_keys(kvm_ref[:, ks])],
                             [kvc_ref[rows, vs], prev_ref[prev_rows, vs], _pad_keys(kvm_ref[:, vs])],
                             [bcur_ref[kh], b_prev, b_meta], sink_ref[kh]))
    for i, o in enumerate(_attend(problems)):
        blk, kh = divmod(i, SWA_KV_HEADS)
        for g in range(SWA_GROUP):
            head = kh * SWA_GROUP + g
            o_ref[blk * WINDOW:(blk + 1) * WINDOW, head * HEAD_DIM:(head + 1) * HEAD_DIM] = (
                o[g * WINDOW:(g + 1) * WINDOW].astype(o_ref.dtype))


def _swa_meta_kernel(q_ref, kv_ref, bias_ref, sink_ref, o_ref):
    problems = [(_group_queries(lambda a, b: q_ref[:, a:b], kh), [kv_ref[:, _k_cols(kh)]],
                 [kv_ref[:, _v_cols(kh)]], [bias_ref[kh]], sink_ref[kh])
                for kh in range(SWA_KV_HEADS)]
    for kh, o in enumerate(_attend(problems)):
        for g in range(SWA_GROUP):
            head = kh * SWA_GROUP + g
            o_ref[:, head * HEAD_DIM:(head + 1) * HEAD_DIM] = (
                o[g * N_META:(g + 1) * N_META].astype(o_ref.dtype))


def _swa_sample_kernel(q_ref, kvn_ref, win_ref, meta_ref, bwin_ref, bsmall_ref, sink_ref,
                       o_ref, wout_ref, *, nb, seq):
    cached = lambda ref, b, slot, n: ref[b, pl.ds(slot, n, stride=KV_SLOTS), :]
    keep = (WINDOW - seq) * KV_SLOTS
    wout_ref[:, 0:keep, :] = win_ref[:, seq * KV_SLOTS:WINDOW * KV_SLOTS, :]
    for slot in range(KV_SLOTS):
        wout_ref[:, pl.ds(keep + slot, seq, stride=KV_SLOTS), :] = (
            kvn_ref[:, :, slot * HEAD_DIM:(slot + 1) * HEAD_DIM])
    problems = []
    for b in range(nb):
        for kh in range(SWA_KV_HEADS):
            ks, vs = _k_cols(kh), _v_cols(kh)
            k_small = _pad_keys(jnp.concatenate([cached(meta_ref, b, kh, N_META), kvn_ref[b, :, ks]], axis=0))
            v_small = _pad_keys(jnp.concatenate(
                [cached(meta_ref, b, SWA_KV_HEADS + kh, N_META), kvn_ref[b, :, vs]], axis=0))
            problems.append((_group_queries(lambda a, c, b=b: q_ref[b, :, a:c], kh),
                             [cached(win_ref, b, kh, WINDOW), k_small],
                             [cached(win_ref, b, SWA_KV_HEADS + kh, WINDOW), v_small],
                             [bwin_ref[kh], bsmall_ref[kh]], sink_ref[kh]))
    outs = _attend(problems)
    for head in range(SWA_HEADS):
        kh, g = divmod(head, SWA_GROUP)
        rows = [outs[b * SWA_KV_HEADS + kh][g * seq:(g + 1) * seq] for b in range(nb)]
        o_ref[:, head * HEAD_DIM:(head + 1) * HEAD_DIM] = jnp.concatenate(rows, axis=0).astype(o_ref.dtype)


def _outproj_kernel(g_ref, s_ref, h_ref, wo_ref, nw_ref, o_ref):
    mix = (jnp.dot(g_ref[...], wo_ref[0:GDN_WIDTH, :], preferred_element_type=jnp.float32)
           + jnp.dot(s_ref[...], wo_ref[GDN_WIDTH:, :], preferred_element_type=jnp.float32))
    o_ref[...] = h_ref[...] + mix * _rms_scale(mix) * nw_ref[...]


def _outproj(g, s, h, wo, nw, *, tm):
    rows = h.shape[0]
    return pl.pallas_call(
        _outproj_kernel,
        grid=(rows // tm,),
        in_specs=[
            pl.BlockSpec((tm, GDN_WIDTH), lambda i: (i, 0)),
            pl.BlockSpec((tm, SWA_WIDTH), lambda i: (i, 0)),
            pl.BlockSpec((tm, D_MODEL), lambda i: (i, 0)),
            pl.BlockSpec((D_MODEL, D_MODEL), lambda i: (0, 0)),
            pl.BlockSpec((1, D_MODEL), lambda i: (0, 0)),
        ],
        out_specs=pl.BlockSpec((tm, D_MODEL), lambda i: (i, 0)),
        out_shape=jax.ShapeDtypeStruct((rows, D_MODEL), jnp.float32),
        compiler_params=pltpu.CompilerParams(
            dimension_semantics=("arbitrary",), vmem_limit_bytes=VMEM_LIMIT),
        name="outproj",
    )(g, s, h, wo, nw)


def _ffn_kernel(*refs, batch, tm, tf):
    if batch:
        (h_ref, nw_pre_ref, wg_ref, wu_ref, cw_ref, wd_ref, nw_post_ref, hist_ref,
         y_ref, graw_ref, xn_ref, acc_ref, xe_ref) = refs
    else:
        (h_ref, nw_pre_ref, wg_ref, wu_ref, cw_ref, wd_ref, nw_post_ref, hist_ref,
         y_ref, graw_ref, xn_ref, acc_ref, xe_ref, carry_ref) = refs
    i = pl.program_id(0)
    j = pl.program_id(1)
    rb = tm // FFN_ROW_BLOCKS

    @pl.when(j == 0)
    def _():
        h = h_ref[...]
        xn_ref[...] = (h * _rms_scale(h) * nw_pre_ref[...]).astype(jnp.bfloat16)
        acc_ref[...] = jnp.zeros_like(acc_ref)

    if batch:
        xe_ref[:, 0:SUBLANES, :] = hist_ref[...]
    else:
        @pl.when(i == 0)
        def _():
            carry_ref[pl.ds(j, 1)] = hist_ref[...].reshape(1, SUBLANES, tf)
        xe_ref[:, 0:SUBLANES, :] = carry_ref[pl.ds(j, 1)]

    def gate_up(r):
        xn = xn_ref[r * rb:(r + 1) * rb, :]
        return (jnp.dot(xn, wg_ref[...], preferred_element_type=jnp.float32),
                jnp.dot(xn, wu_ref[...], preferred_element_type=jnp.float32))

    nxt = gate_up(0)
    for r in range(FFN_ROW_BLOCKS):
        rows = slice(r * rb, (r + 1) * rb)
        gate, up = nxt
        if r + 1 < FFN_ROW_BLOCKS:
            nxt = gate_up(r + 1)
        if batch:
            seqs = slice(r * rb // SUBLANES, (r + 1) * rb // SUBLANES)
            graw_ref[rows, :] = gate
            xe_ref[seqs, SUBLANES:2 * SUBLANES, :] = gate.reshape(rb // SUBLANES, SUBLANES, tf)
            taps = [xe_ref[seqs, SUBLANES - s:2 * SUBLANES - s, :] for s in range(FFN_CONV)]
        else:
            base = SUBLANES + r * rb
            xe_ref[:, base:base + rb, :] = gate.reshape(1, rb, tf)
            taps = [xe_ref[:, base - s:base - s + rb, :] for s in range(FFN_CONV)]
        conv = None
        for s, tap in enumerate(taps):
            term = tap * cw_ref[FFN_CONV - 1 - s:FFN_CONV - s, :]
            conv = term if conv is None else conv + term
        act = (_silu(conv.reshape(rb, tf)) * up).astype(jnp.bfloat16)
        acc_ref[rows, :] += jnp.dot(act, wd_ref[...], preferred_element_type=jnp.float32)
    if not batch:
        tail = xe_ref[:, tm:tm + SUBLANES, :]
        carry_ref[pl.ds(j, 1)] = tail
        graw_ref[...] = tail.reshape(SUBLANES, tf)

    @pl.when(j == pl.num_programs(1) - 1)
    def _():
        y = acc_ref[...]
        y_ref[...] = h_ref[...] + y * _rms_scale(y) * nw_post_ref[...]


def _ffn(h, nw_pre, wg, wu, cw, wd, nw_post, hist, *, batch, tm, tf):
    rows = h.shape[0]
    nj = D_FF // tf
    in_specs = [
        pl.BlockSpec((tm, D_MODEL), lambda i, j: (i, 0)),
        pl.BlockSpec((1, D_MODEL), lambda i, j: (0, 0)),
        pl.BlockSpec((D_MODEL, tf), lambda i, j: (0, j)),
        pl.BlockSpec((D_MODEL, tf), lambda i, j: (0, j)),
        pl.BlockSpec((FFN_CONV, tf), lambda i, j: (0, j)),
        pl.BlockSpec((tf, D_MODEL), lambda i, j: (j, 0)),
        pl.BlockSpec((1, D_MODEL), lambda i, j: (0, 0)),
    ]
    args = [h, nw_pre, wg, wu, cw, wd, nw_post, hist]
    scratch = [pltpu.VMEM((tm, D_MODEL), jnp.bfloat16), pltpu.VMEM((tm, D_MODEL), jnp.float32)]
    if batch:
        in_specs.append(pl.BlockSpec((tm // SUBLANES, SUBLANES, tf), lambda i, j: (i, 0, j)))
        graw_spec = pl.BlockSpec((tm, tf), lambda i, j: (i, j))
        graw_shape = jax.ShapeDtypeStruct((rows, D_FF), jnp.float32)
        scratch.append(pltpu.VMEM((tm // SUBLANES, 2 * SUBLANES, tf), jnp.float32))
    else:
        in_specs.append(pl.BlockSpec((SUBLANES, tf), lambda i, j: (0, j)))
        graw_spec = pl.BlockSpec((SUBLANES, tf), lambda i, j: (i, j))
        graw_shape = jax.ShapeDtypeStruct((rows // tm * SUBLANES, D_FF), jnp.float32)
        scratch.append(pltpu.VMEM((1, SUBLANES + tm, tf), jnp.float32))
        scratch.append(pltpu.VMEM((nj, SUBLANES, tf), jnp.float32))
    return pl.pallas_call(
        functools.partial(_ffn_kernel, batch=batch, tm=tm, tf=tf),
        grid=(rows // tm, nj),
        in_specs=in_specs,
        out_specs=[pl.BlockSpec((tm, D_MODEL), lambda i, j: (i, 0)), graw_spec],
        out_shape=[jax.ShapeDtypeStruct((rows, D_MODEL), jnp.float32), graw_shape],
        scratch_shapes=scratch,
        compiler_params=pltpu.CompilerParams(
            dimension_semantics=("arbitrary", "arbitrary"), vmem_limit_bytes=VMEM_LIMIT),
        name="ffn_batch" if batch else "ffn_seq",
    )(*args)


def kernel(x_prompt, x_sample, cache_swa_meta_kv, cache_swa_window_kv, state_gdn_conv, state_gdn, state_ffn_conv, meta_tokens, rel_bias_table, w_in, gdn_conv_w, gdn_a_log, gdn_dt_bias, gdn_norm_w, swa_sinks, w_out, norm_mix_pre, norm_mix_post, norm_ffn_pre, norm_ffn_post, ffn_w_gate, ffn_w_up, ffn_conv_w, ffn_w_down):
    f32, bf16 = jnp.float32, jnp.bfloat16
    seq = x_prompt.shape[1]
    dec_b, dec_t = x_sample.shape[0], x_sample.shape[1]
    n_dec = dec_b * dec_t
    assert x_prompt.shape[0] == 1 and seq % CHUNK == 0 and dec_t == SUBLANES and n_dec % CHUNK == 0

    wi = w_in[0]
    n_ba = 2 * GDN_HEADS
    w_in_p = _pack_w_in(wi, n_ba, tn=512)
    w_ba = jnp.pad(wi[:, COL_SQ:COL_SQ + n_ba], ((0, 0), (0, LANES - n_ba))).astype(bf16)
    wo = w_out[0].astype(bf16)
    wg = ffn_w_gate[0].astype(bf16)
    wu = ffn_w_up[0].astype(bf16)
    wd = ffn_w_down[0].astype(bf16)
    lane_pad = lambda v: jnp.pad(v.reshape(1, GDN_HEADS), ((0, 0), (GDN_HEADS, LANES - 2 * GDN_HEADS)))
    alog_row = lane_pad(gdn_a_log[0])
    dtb_row = lane_pad(gdn_dt_bias[0])
    gnw = gdn_norm_w[0].reshape(1, HEAD_DIM)

    pad_rows = CHUNK - N_META
    n_small = n_dec + CHUNK
    x_big = x_prompt.reshape(seq, D_MODEL)
    x_small = jnp.concatenate(
        [x_sample.reshape(n_dec, D_MODEL), jnp.zeros((pad_rows, D_MODEL), f32), meta_tokens.astype(f32)], axis=0)
    nw = norm_mix_pre[0].reshape(1, D_MODEL)
    cw = gdn_conv_w[0]
    proj_small, ba_small = _inproj(x_small, nw, w_in_p, w_ba, tm=n_small, tn=512, row_chunk=128)
    proj_big, ba_big, qkv_tail = _inproj(
        x_big, nw, w_in_p, w_ba, (cw, proj_small[n_small - SUBLANES:, :GDN_QKV]),
        tm=1024, tn=512, row_chunk=128)

    small_chunks = proj_small.reshape(n_small // CHUNK, CHUNK, PROJ_COLS)
    small_groups = proj_small.reshape(n_small // SUBLANES, SUBLANES, PROJ_COLS)
    last_chunk = n_small // CHUNK - 1
    gdn_meta, s_meta = _gdn(
        small_chunks, ba_small.reshape(n_small // CHUNK, CHUNK, LANES), lambda s: (last_chunk, 0, 0),
        jnp.zeros((1, SUBLANES, GDN_QKV), f32), lambda s: (0, 0, 0),
        jnp.zeros((1, GDN_HEADS, HEAD_DIM, HEAD_DIM), f32), cw, alog_row, dtb_row, gnw,
        n_steps=1, nb=1, seq=CHUNK, group=CHUNK, carry=True, pad_rows=pad_rows, preconv=False)
    gdn_big, s_prompt = _gdn(
        proj_big.reshape(1, seq, PROJ_COLS), ba_big.reshape(1, seq, LANES), lambda s: (0, s, 0),
        jnp.zeros((1, SUBLANES, GDN_QKV), f32), lambda s: (0, 0, 0),
        s_meta, cw, alog_row, dtb_row, gnw,
        n_steps=seq // (GDN_SEQ_CHUNKS * CHUNK), nb=1, seq=GDN_SEQ_CHUNKS * CHUNK, group=CHUNK,
        carry=True, pad_rows=0, preconv=True)
    hist_gdn = jnp.pad(state_gdn_conv[0], ((0, 0), (SUBLANES - (GDN_CONV - 1), 0), (0, 0)))
    nb_gdn = CHUNK // dec_t
    gdn_small, s_sample = _gdn(
        small_groups, ba_small.reshape(n_small // SUBLANES, SUBLANES, LANES), lambda s: (s, 0, 0),
        hist_gdn, lambda s: (s, 0, 0),
        state_gdn, cw, alog_row, dtb_row, gnw,
        n_steps=dec_b // nb_gdn, nb=nb_gdn, seq=dec_t, group=dec_t, carry=False, pad_rows=0, preconv=False)

    qi = np.arange(WINDOW)[:, None]
    kj = np.arange(WINDOW)[None, :]
    mi = np.arange(N_META)[None, :]
    ti = np.arange(dec_t)[:, None]
    new_keys = kj - N_META
    id_arrays = [
        _bucket_ids(qi - kj, qi >= kj),
        _bucket_ids(qi - kj + WINDOW, kj > qi),
        _bucket_ids(qi + N_META - kj, kj < N_META),
        _bucket_ids(qi + N_META - kj + WINDOW, kj < N_META),
        _bucket_ids(ti + WINDOW - kj, kj > ti),
        _bucket_ids(np.where(new_keys < 0, PAST_LEN + ti - kj, ti - new_keys),
                    (new_keys < 0) | ((new_keys <= ti) & (new_keys < dec_t))),
        _bucket_ids(mi.T - mi, mi.T >= mi),
    ]
    bcur, bprev, bm0, bfar, bwin, bsmall, bmm = _bias_tables(rel_bias_table, id_arrays)
    sink_rows = lambda q: jnp.repeat(swa_sinks[0].reshape(SWA_KV_HEADS, SWA_GROUP), q, axis=1)[..., None]

    sq_blk = COL_SQ // SWA_WIDTH
    kv_blk = COL_KV // KV_WIDTH
    meta_blk = (n_small - N_META) // N_META
    full3 = lambda a: pl.BlockSpec(a.shape, lambda j: (0, 0, 0))
    sink_p = sink_rows(WINDOW)
    swa_rows = SWA_Q_BLOCKS * WINDOW
    swa_big = pl.pallas_call(
        _swa_prompt_kernel,
        grid=(seq // swa_rows,),
        in_specs=[
            pl.BlockSpec((swa_rows, SWA_WIDTH), lambda j: (j, sq_blk)),
            pl.BlockSpec((swa_rows, KV_WIDTH), lambda j: (j, kv_blk)),
            pl.BlockSpec((WINDOW, KV_WIDTH), lambda j: (jnp.maximum(j * SWA_Q_BLOCKS - 1, 0), kv_blk)),
            pl.BlockSpec((N_META, KV_WIDTH), lambda j: (meta_blk, kv_blk)),
            full3(bcur), full3(bprev), full3(bm0), full3(bfar), full3(sink_p),
        ],
        out_specs=pl.BlockSpec((swa_rows, SWA_WIDTH), lambda j: (j, 0)),
        out_shape=jax.ShapeDtypeStruct((seq, SWA_WIDTH), bf16),
        compiler_params=pltpu.CompilerParams(
            dimension_semantics=("arbitrary",), vmem_limit_bytes=VMEM_LIMIT),
        name="swa_prompt",
    )(proj_big, proj_big, proj_big, proj_small, bcur, bprev, bm0, bfar, sink_p)

    nb_swa = 8
    sink_s = sink_rows(dec_t)
    win = cache_swa_window_kv.reshape(dec_b, WINDOW * KV_SLOTS, HEAD_DIM)
    meta_kv = cache_swa_meta_kv.reshape(dec_b, N_META * KV_SLOTS, HEAD_DIM)
    swa_small, win_new = pl.pallas_call(
        functools.partial(_swa_sample_kernel, nb=nb_swa, seq=dec_t),
        grid=(dec_b // nb_swa,),
        in_specs=[
            pl.BlockSpec((nb_swa, dec_t, SWA_WIDTH), lambda j: (j, 0, sq_blk)),
            pl.BlockSpec((nb_swa, dec_t, KV_WIDTH), lambda j: (j, 0, kv_blk)),
            pl.BlockSpec((nb_swa, WINDOW * KV_SLOTS, HEAD_DIM), lambda j: (j, 0, 0)),
            pl.BlockSpec((nb_swa, N_META * KV_SLOTS, HEAD_DIM), lambda j: (j, 0, 0)),
            full3(bwin), full3(bsmall), full3(sink_s),
        ],
        out_specs=[pl.BlockSpec((nb_swa * dec_t, SWA_WIDTH), lambda j: (j, 0)),
                   pl.BlockSpec((nb_swa, WINDOW * KV_SLOTS, HEAD_DIM), lambda j: (j, 0, 0))],
        out_shape=[jax.ShapeDtypeStruct((n_dec, SWA_WIDTH), bf16),
                   jax.ShapeDtypeStruct((dec_b, WINDOW * KV_SLOTS, HEAD_DIM), f32)],
        compiler_params=pltpu.CompilerParams(
            dimension_semantics=("arbitrary",), vmem_limit_bytes=VMEM_LIMIT),
        name="swa_sample",
    )(small_groups, small_groups, win, meta_kv, bwin, bsmall, sink_s)

    sink_m = sink_rows(N_META)
    swa_meta = pl.pallas_call(
        _swa_meta_kernel,
        grid=(1,),
        in_specs=[
            pl.BlockSpec((N_META, SWA_WIDTH), lambda j: (meta_blk, sq_blk)),
            pl.BlockSpec((N_META, KV_WIDTH), lambda j: (meta_blk, kv_blk)),
            full3(bmm), full3(sink_m),
        ],
        out_specs=pl.BlockSpec((N_META, SWA_WIDTH), lambda j: (0, 0)),
        out_shape=jax.ShapeDtypeStruct((N_META, SWA_WIDTH), bf16),
        name="swa_meta",
    )(proj_small, proj_small, bmm, sink_m)

    nw_post = norm_mix_post[0].reshape(1, D_MODEL)
    nf_pre = norm_ffn_pre[0].reshape(1, D_MODEL)
    nf_post = norm_ffn_post[0].reshape(1, D_MODEL)
    fcw = ffn_conv_w[0]
    gdn_small_all = jnp.concatenate([gdn_small, gdn_meta], axis=0)
    swa_small_all = jnp.concatenate([swa_small, jnp.zeros((pad_rows, SWA_WIDTH), bf16), swa_meta], axis=0)
    h_small = _outproj(gdn_small_all, swa_small_all, x_small, wo, nw_post, tm=n_small // 2)
    hist_ffn = jnp.pad(state_ffn_conv[0], ((0, CHUNK // SUBLANES), (SUBLANES - (FFN_CONV - 1), 0), (0, 0)))
    y_small, g_small = _ffn(h_small, nf_pre, wg, wu, fcw, wd, nf_post, hist_ffn,
                            batch=True, tm=n_small // 2, tf=512)
    h_big = _outproj(gdn_big, swa_big, x_big, wo, nw_post, tm=512)
    y_big, g_tail = _ffn(h_big, nf_pre, wg, wu, fcw, wd, nf_post, g_small[n_small - SUBLANES:],
                         batch=False, tm=512, tf=512)

    kv_shape = lambda n: (1, n, 2, SWA_KV_HEADS, HEAD_DIM)
    kv_small = proj_small[:, COL_KV:COL_KV + KV_WIDTH]
    y_prompt = y_big.reshape(1, seq, D_MODEL)
    y_sample = y_small[:n_dec].reshape(dec_b, dec_t, D_MODEL)
    p_meta_kv = kv_small[n_small - N_META:].reshape(kv_shape(N_META))[None]
    p_window_kv = proj_big[seq - WINDOW:, COL_KV:COL_KV + KV_WIDTH].reshape(kv_shape(WINDOW))[None]
    p_gdn_conv = qkv_tail[qkv_tail.shape[0] - (GDN_CONV - 1):].reshape(1, 1, GDN_CONV - 1, GDN_QKV)
    p_gdn_state = s_prompt[None]
    p_ffn_conv = g_tail[g_tail.shape[0] - (FFN_CONV - 1):].reshape(1, 1, FFN_CONV - 1, D_FF)
    s_window_kv = win_new.reshape(1, dec_b, WINDOW, 2, SWA_KV_HEADS, HEAD_DIM)
    s_gdn_conv = proj_small[:n_dec, :GDN_QKV].reshape(dec_b, dec_t, GDN_QKV)[:, dec_t - (GDN_CONV - 1):][None]
    s_gdn_state = s_sample
    s_ffn_conv = g_small[:n_dec].reshape(dec_b, dec_t, D_FF)[:, dec_t - (FFN_CONV - 1):][None]
    return (y_prompt, y_sample, p_meta_kv, p_window_kv, p_gdn_conv, p_gdn_state, p_ffn_conv,
            s_window_kv, s_gdn_conv, s_gdn_state, s_ffn_conv)
```

```python
import functools
import math

import numpy as np
import jax
import jax.numpy as jnp
from jax import lax
from jax.experimental import pallas as pl
from jax.experimental.pallas import tpu as pltpu

D_MODEL = 2048
HEAD_DIM = 128
GDN_HEADS = 8
GDN_WIDTH = GDN_HEADS * HEAD_DIM
GDN_QKV = 3 * GDN_WIDTH
SWA_HEADS = 8
SWA_KV_HEADS = 2
SWA_GROUP = SWA_HEADS // SWA_KV_HEADS
SWA_WIDTH = SWA_HEADS * HEAD_DIM
KV_SLOTS = 2 * SWA_KV_HEADS
KV_WIDTH = KV_SLOTS * HEAD_DIM
WINDOW = 128
N_META = 16
N_BUCKETS = 32
MAX_DISTANCE = 128
GDN_CONV = 4
FFN_CONV = 3
D_FF = 5632
EPS = 1e-6
PAST_LEN = 16384

SUBLANES = 8
LANES = 128

CHUNK = 128
assert CHUNK == HEAD_DIM == LANES
INV_BASE = 16
GDN_SEQ_CHUNKS = 2
INPROJ_ROW_BLOCKS = 4
FFN_ROW_BLOCKS = 4

COL_Z = GDN_QKV
COL_SQ = COL_Z + GDN_WIDTH
COL_KV = COL_SQ + SWA_WIDTH
PROJ_COLS = COL_KV + KV_WIDTH
GDN_COLS = COL_SQ

SWA_Q_BLOCKS = 4
NEG = -1e30
VMEM_LIMIT = 56 * 1024 * 1024

_NT = (((1,), (1,)), ((), ()))


def _dot(a, b):
    return jnp.dot(a.astype(jnp.bfloat16), b.astype(jnp.bfloat16), preferred_element_type=jnp.float32)


def _dot_nt(a, b):
    return lax.dot_general(a.astype(jnp.bfloat16), b.astype(jnp.bfloat16), _NT,
                           preferred_element_type=jnp.float32)


def _split(a):
    hi = a.astype(jnp.bfloat16)
    lo = (a - hi.astype(jnp.float32)).astype(jnp.bfloat16)
    return hi, lo


def _dot3(a, b):
    ah, al = _split(a)
    bh, bl = _split(b)
    d = functools.partial(jnp.dot, preferred_element_type=jnp.float32)
    return d(ah, bh) + (d(ah, bl) + d(al, bh))


_dot_inv = _dot


def _dot_exact(a, b, dims=None):
    if dims is None:
        return jnp.dot(a, b, precision=lax.Precision.HIGHEST, preferred_element_type=jnp.float32)
    return lax.dot_general(a, b, dims, precision=lax.Precision.HIGHEST,
                           preferred_element_type=jnp.float32)


def _pack_w_in_kernel(a_ref, b_ref, o_ref, *, first_shifted, shift):
    j = pl.program_id(0)

    @pl.when(j < first_shifted)
    def _():
        o_ref[...] = a_ref[...].astype(o_ref.dtype)

    @pl.when(j >= first_shifted)
    def _():
        o_ref[...] = jnp.concatenate([a_ref[:, shift:], b_ref[:, :shift]], axis=1).astype(o_ref.dtype)


def _pack_w_in(w_in, n_ba, *, tn):
    n_blocks = PROJ_COLS // tn
    first_shifted = COL_SQ // tn
    assert COL_SQ % tn == 0 and w_in.shape[2] == PROJ_COLS + n_ba
    tails = jnp.stack([jnp.pad(w_in[0, :, (j + 1) * tn:(j + 1) * tn + n_ba], ((0, 0), (0, LANES - n_ba)))
                       for j in range(first_shifted, n_blocks)])
    return pl.pallas_call(
        functools.partial(_pack_w_in_kernel, first_shifted=first_shifted, shift=n_ba),
        grid=(n_blocks,),
        in_specs=[pl.BlockSpec((None, D_MODEL, tn), lambda j: (0, 0, j)),
                  pl.BlockSpec((None, D_MODEL, LANES), lambda j: (jnp.maximum(j - first_shifted, 0), 0, 0))],
        out_specs=pl.BlockSpec((D_MODEL, tn), lambda j: (0, j)),
        out_shape=jax.ShapeDtypeStruct((D_MODEL, PROJ_COLS), jnp.bfloat16),
        compiler_params=pltpu.CompilerParams(
            dimension_semantics=("arbitrary",), vmem_limit_bytes=VMEM_LIMIT),
        name="pack_w_in",
    )(w_in, tails)


def _rms_scale(x):
    return lax.rsqrt(jnp.mean(x * x, axis=-1, keepdims=True) + EPS)


def _silu(x):
    return x * jax.nn.sigmoid(x)


def _inproj_kernel(*refs, row_chunk, conv_tiles):
    if conv_tiles:
        (x_ref, nw_ref, w_ref, wba_ref, cw_ref, hist_ref,
         o_ref, ba_ref, tail_ref, xn_ref, xe_ref, carry_ref) = refs
    else:
        x_ref, nw_ref, w_ref, wba_ref, o_ref, ba_ref, xn_ref = refs
    i = pl.program_id(0)
    j = pl.program_id(1)
    tm, tn = o_ref.shape

    def normalize(rows):
        x = x_ref[rows, :]
        xn_ref[rows, :] = (x * _rms_scale(x) * nw_ref[...]).astype(jnp.bfloat16)

    def plain():
        o_ref[...] = jnp.dot(xn_ref[...], w_ref[...], preferred_element_type=jnp.float32)

    if not conv_tiles:
        @pl.when(j == 0)
        def _():
            def body(c, carry):
                normalize(pl.ds(pl.multiple_of(c * row_chunk, row_chunk), row_chunk))
                return carry
            lax.fori_loop(0, tm // row_chunk, body, 0)
            ba_ref[...] = jnp.dot(xn_ref[...], wba_ref[...], preferred_element_type=jnp.float32)
        plain()
        return
    pl.when(j >= conv_tiles)(plain)

    def conv_tile(first):
        @pl.when(i == 0)
        def _():
            carry_ref[pl.ds(j, 1)] = hist_ref[...].reshape(1, SUBLANES, tn)
        xe_ref[0:SUBLANES, :] = carry_ref[pl.ds(j, 1)].reshape(SUBLANES, tn)
        rb = tm // INPROJ_ROW_BLOCKS

        def raw_block(r):
            rows = slice(r * rb, (r + 1) * rb)
            if first:
                normalize(rows)
                ba_ref[rows, :] = jnp.dot(xn_ref[rows, :], wba_ref[...], preferred_element_type=jnp.float32)
            return jnp.dot(xn_ref[rows, :], w_ref[...], preferred_element_type=jnp.float32)

        nxt = raw_block(0)
        for r in range(INPROJ_ROW_BLOCKS):
            raw = nxt
            if r + 1 < INPROJ_ROW_BLOCKS:
                nxt = raw_block(r + 1)
            base = SUBLANES + r * rb
            xe_ref[base:base + rb, :] = raw
            conv = None
            for s in range(GDN_CONV):
                term = xe_ref[base - s:base - s + rb, :] * cw_ref[GDN_CONV - 1 - s:GDN_CONV - s, :]
                conv = term if conv is None else conv + term
            o_ref[r * rb:(r + 1) * rb, :] = _silu(conv)
        tail = xe_ref[tm:tm + SUBLANES, :]
        carry_ref[pl.ds(j, 1)] = tail.reshape(1, SUBLANES, tn)
        tail_ref[...] = tail

    pl.when(j == 0)(functools.partial(conv_tile, True))
    pl.when((j > 0) & (j < conv_tiles))(functools.partial(conv_tile, False))


def _inproj(x, nw, w, wba, conv=None, *, tm, tn, row_chunk):
    rows = x.shape[0]
    conv_tiles = GDN_QKV // tn if conv else 0
    in_specs = [
        pl.BlockSpec((tm, D_MODEL), lambda i, j: (i, 0)),
        pl.BlockSpec((1, D_MODEL), lambda i, j: (0, 0)),
        pl.BlockSpec((D_MODEL, tn), lambda i, j: (0, j)),
        pl.BlockSpec((D_MODEL, LANES), lambda i, j: (0, 0)),
    ]
    out_specs = [pl.BlockSpec((tm, tn), lambda i, j: (i, j)),
                 pl.BlockSpec((tm, LANES), lambda i, j: (i, 0))]
    out_shape = [jax.ShapeDtypeStruct((rows, PROJ_COLS), jnp.float32),
                 jax.ShapeDtypeStruct((rows, LANES), jnp.float32)]
    scratch = [pltpu.VMEM((tm, D_MODEL), jnp.bfloat16)]
    args = [x, nw, w, wba]
    if conv:
        conv_col = lambda i, j: (0, jnp.minimum(j, conv_tiles - 1))
        in_specs += [pl.BlockSpec((GDN_CONV, tn), conv_col), pl.BlockSpec((SUBLANES, tn), conv_col)]
        out_specs.append(pl.BlockSpec((SUBLANES, tn), lambda i, j: (i, jnp.minimum(j, conv_tiles - 1))))
        out_shape.append(jax.ShapeDtypeStruct((rows // tm * SUBLANES, GDN_QKV), jnp.float32))
        scratch += [pltpu.VMEM((SUBLANES + tm, tn), jnp.float32),
                    pltpu.VMEM((conv_tiles, SUBLANES, tn), jnp.float32)]
        args += list(conv)
    return pl.pallas_call(
        functools.partial(_inproj_kernel, row_chunk=row_chunk, conv_tiles=conv_tiles),
        grid=(rows // tm, PROJ_COLS // tn),
        in_specs=in_specs,
        out_specs=out_specs,
        out_shape=out_shape,
        scratch_shapes=scratch,
        compiler_params=pltpu.CompilerParams(
            dimension_semantics=("arbitrary", "arbitrary"), vmem_limit_bytes=VMEM_LIMIT),
        name="inproj_conv" if conv else "inproj",
    )(*args)


def _tri_inverse(lms, ri, ci):
    shift = INV_BASE.bit_length() - 1
    eye = (ri == ci).astype(jnp.float32)
    in_block = (ri >> shift) == (ci >> shift)
    ps = [jnp.where(in_block, lm, 0.0) for lm in lms]
    ts = [eye - p for p in ps]
    for _ in range(shift - 1):
        ps = [_dot_inv(p, p) for p in ps]
        ts = [t + _dot_inv(t, p) for t, p in zip(ts, ps)]
    size = INV_BASE
    while size < CHUNK:
        shift += 1
        in_pair = (ri >> shift) == (ci >> shift)
        off_mask = in_pair & jnp.logical_not(in_block)
        tos = [_dot_inv(t, jnp.where(off_mask, lm, 0.0)) for t, lm in zip(ts, lms)]
        ts = [t - _dot_inv(to, t) for t, to in zip(ts, tos)]
        in_block = in_pair
        size *= 2
    return ts


def _gdn_kernel(x_ref, ba_ref, hist_ref, s0_ref, cw_ref, alog_ref, dtb_ref, gnw_ref,
                o_ref, sout_ref, xe_ref, s_ref, *, nb, seq, group, carry, pad_rows, preconv):
    step = pl.program_id(0)
    rows = nb * seq
    n_chunks = rows // CHUNK
    n_groups = CHUNK // group
    gshift = group.bit_length() - 1

    if carry:
        @pl.when(step == 0)
        def _():
            s_ref[...] = s0_ref[0]
    if not preconv:
        if carry:
            @pl.when(step == 0)
            def _():
                xe_ref[:, 0:SUBLANES, :] = hist_ref[...]
        else:
            xe_ref[:, 0:SUBLANES, :] = hist_ref[...]
        xe_ref[:, SUBLANES:SUBLANES + seq, :] = x_ref[:, :, 0:GDN_QKV]

    seq_rows = min(seq, CHUNK)
    seqs_per_chunk = CHUNK // seq_rows

    def chunk_rows(ref, c, row_off, cols):
        if seq >= CHUNK:
            start = row_off + c * CHUNK
            return ref[0:1, start:start + CHUNK, cols]
        b0 = c * seqs_per_chunk
        return ref[b0:b0 + seqs_per_chunk, row_off:row_off + seq, cols]

    def conv_chunk(col, c):
        cols = slice(col, col + HEAD_DIM)
        if preconv:
            return chunk_rows(x_ref, c, 0, cols).reshape(CHUNK, HEAD_DIM)
        acc = None
        for s in range(GDN_CONV):
            term = chunk_rows(xe_ref, c, SUBLANES - s, cols) * cw_ref[GDN_CONV - 1 - s:GDN_CONV - s, cols]
            acc = term if acc is None else acc + term
        return _silu(acc).reshape(CHUNK, HEAD_DIM)

    ri = lax.broadcasted_iota(jnp.int32, (CHUNK, CHUNK), 0)
    ci = lax.broadcasted_iota(jnp.int32, (CHUNK, CHUNK), 1)
    same = (ri >> gshift) == (ci >> gshift)
    m_incl = same & (ri >= ci)
    m_strict = same & (ri > ci)
    f_incl = m_incl.astype(jnp.float32)
    f_same = same.astype(jnp.float32)
    lane = lax.broadcasted_iota(jnp.int32, (CHUNK, LANES), 1)
    row_in_chunk = lax.broadcasted_iota(jnp.int32, (CHUNK, LANES), 0)

    pre = []
    for c in range(n_chunks):
        bac = chunk_rows(ba_ref, c, 0, slice(0, LANES)).reshape(CHUNK, LANES)
        beta_all = jax.nn.sigmoid(bac)
        sp_in = bac + dtb_ref[...]
        softplus = jnp.maximum(sp_in, 0.0) + jnp.log1p(jnp.exp(-jnp.abs(sp_in)))
        g_all = -jnp.exp(alog_ref[...]) * softplus
        if pad_rows and c == 0:
            valid = row_in_chunk >= pad_rows
            beta_all = jnp.where(valid, beta_all, 0.0)
            g_all = jnp.where(valid, g_all, 0.0)
        g_all = jnp.where((lane >= GDN_HEADS) & (lane < 2 * GDN_HEADS), g_all, 0.0)
        gc_col = _dot_exact(f_incl, g_all)
        gtot_col = _dot_exact(f_same, g_all)
        gc_row = _dot_exact(g_all.T, f_incl, _NT)

        for h in range(GDN_HEADS):
            qh = conv_chunk(h * HEAD_DIM, c)
            kh = conv_chunk(GDN_WIDTH + h * HEAD_DIM, c)
            vh = conv_chunk(2 * GDN_WIDTH + h * HEAD_DIM, c)
            qh = qh * lax.rsqrt(jnp.sum(qh * qh, -1, keepdims=True) + EPS) * (HEAD_DIM ** -0.5)
            kh = kh * lax.rsqrt(jnp.sum(kh * kh, -1, keepdims=True) + EPS)
            bcast = lambda col: jnp.broadcast_to(col, (CHUNK, HEAD_DIM))
            gcc = bcast(gc_col[:, GDN_HEADS + h:GDN_HEADS + h + 1])
            gtc = bcast(gtot_col[:, GDN_HEADS + h:GDN_HEADS + h + 1])
            beta = bcast(beta_all[:, h:h + 1])
            gcr = gc_row[GDN_HEADS + h:GDN_HEADS + h + 1, :]
            decay = jnp.exp(jnp.where(m_incl, gcc - gcr, NEG))
            kb = kh * beta
            egc = jnp.exp(gcc)
            pre.append(dict(
                c=c, h=h,
                lm=jnp.where(m_strict, _dot_nt(kb, kh) * decay, 0.0),
                qk=_dot_nt(qh, kh) * decay,
                rhs=jnp.concatenate([vh * beta, kb * egc], axis=1),
                qg=qh * egc,
                kd_t=(kh * jnp.exp(gtc - gcc)).T,
                gl=jnp.exp(gtc)))

    inverses = _tri_inverse([p["lm"] for p in pre], ri, ci)
    sols = [_dot_inv(t, p["rhs"]) for t, p in zip(inverses, pre)]

    for c in range(n_chunks):
        r0 = c * CHUNK
        items = [(p, sol) for p, sol in zip(pre, sols) if p["c"] == c]
        state = lambda h, b: s_ref[h] if carry else s0_ref[c * n_groups + b, h]
        ws, qs = [], []
        for p, sol in items:
            w = sol[:, HEAD_DIM:]
            ws_parts, qs_parts = [], []
            for b in range(n_groups):
                g0 = b * group
                wq = jnp.concatenate([w[g0:g0 + group], p["qg"][g0:g0 + group]], axis=0)
                res = _dot(wq, state(p["h"], b))
                ws_parts.append(res[:group])
                qs_parts.append(res[group:])
            ws.append(ws_parts[0] if n_groups == 1 else jnp.concatenate(ws_parts, axis=0))
            qs.append(qs_parts[0] if n_groups == 1 else jnp.concatenate(qs_parts, axis=0))
        v_new = [sol[:, :HEAD_DIM] - w for (p, sol), w in zip(items, ws)]
        o = [a + _dot(p["qk"], vn) for a, (p, sol), vn in zip(qs, items, v_new)]
        for (p, sol), vn in zip(items, v_new):
            h = p["h"]
            for b in range(n_groups):
                g0 = b * group
                kd_b = p["kd_t"] if n_groups == 1 else jnp.where((ci >> gshift) == b, p["kd_t"], 0.0)
                st = state(h, b) * p["gl"][g0:g0 + 1, :] + _dot(kd_b, vn)
                if carry:
                    s_ref[h] = st
                else:
                    sout_ref[c * n_groups + b, h] = st
        for (p, sol), oh in zip(items, o):
            h = p["h"]
            z = chunk_rows(x_ref, c, 0, slice(COL_Z + h * HEAD_DIM, COL_Z + (h + 1) * HEAD_DIM)).reshape(CHUNK, HEAD_DIM)
            y = oh * lax.rsqrt(jnp.mean(oh * oh, -1, keepdims=True) + EPS) * gnw_ref[...] * _silu(z)
            o_ref[r0:r0 + CHUNK, h * HEAD_DIM:(h + 1) * HEAD_DIM] = y.astype(o_ref.dtype)

    if carry:
        if not preconv:
            xe_ref[:, 0:SUBLANES, :] = xe_ref[:, seq:seq + SUBLANES, :]

        @pl.when(step == pl.num_programs(0) - 1)
        def _():
            sout_ref[0] = s_ref[...]


def _gdn(x3, ba3, x_idx, hist, hist_idx, s0, cw, alog_row, dtb_row, gnw, *,
         n_steps, nb, seq, group, carry, pad_rows, preconv):
    rows = nb * seq
    if carry:
        state_spec = pl.BlockSpec((1, GDN_HEADS, HEAD_DIM, HEAD_DIM), lambda s: (0, 0, 0, 0))
        state_shape = (1, GDN_HEADS, HEAD_DIM, HEAD_DIM)
    else:
        n_states = rows // group
        state_spec = pl.BlockSpec((None, n_states, GDN_HEADS, HEAD_DIM, HEAD_DIM), lambda s: (0, s, 0, 0, 0))
        state_shape = (1, n_steps * n_states, GDN_HEADS, HEAD_DIM, HEAD_DIM)
    full = lambda shape: pl.BlockSpec(shape, lambda s: (0,) * len(shape))
    return pl.pallas_call(
        functools.partial(_gdn_kernel, nb=nb, seq=seq, group=group, carry=carry, pad_rows=pad_rows,
                          preconv=preconv),
        grid=(n_steps,),
        in_specs=[
            pl.BlockSpec((nb, seq, GDN_COLS), x_idx),
            pl.BlockSpec((nb, seq, LANES), x_idx),
            pl.BlockSpec((nb, SUBLANES, GDN_QKV), hist_idx),
            state_spec,
            full((GDN_CONV, GDN_QKV)),
            full((1, LANES)),
            full((1, LANES)),
            full((1, HEAD_DIM)),
        ],
        out_specs=[
            pl.BlockSpec((rows, GDN_WIDTH), lambda s: (s, 0)),
            state_spec,
        ],
        out_shape=[
            jax.ShapeDtypeStruct((n_steps * rows, GDN_WIDTH), jnp.bfloat16),
            jax.ShapeDtypeStruct(state_shape, jnp.float32),
        ],
        scratch_shapes=[
            pltpu.VMEM((nb, SUBLANES + (SUBLANES if preconv else seq), GDN_QKV), jnp.float32),
            pltpu.VMEM((GDN_HEADS, HEAD_DIM, HEAD_DIM), jnp.float32),
        ],
        compiler_params=pltpu.CompilerParams(
            dimension_semantics=("arbitrary",), vmem_limit_bytes=VMEM_LIMIT),
        name="gdn_seq" if carry else "gdn_batch",
    )(x3, ba3, hist, s0, cw, alog_row, dtb_row, gnw)


def _t5_bucket_np(dist):
    n = np.maximum(dist, 0)
    exact = N_BUCKETS // 2
    large = exact + (np.log(np.maximum(n, 1).astype(np.float32) / exact)
                     / math.log(MAX_DISTANCE / exact) * (N_BUCKETS - exact)).astype(np.int32)
    return np.where(n < exact, n, np.minimum(large, N_BUCKETS - 1)).astype(np.int32)


def _bucket_ids(dist, valid):
    return np.where(valid, _t5_bucket_np(dist), -1).astype(np.int32)


def _bias_kernel(table_ref, *refs):
    n = len(refs) // 2
    for ids_ref, out_ref in zip(refs[:n], refs[n:]):
        ids = ids_ref[...]
        nq = ids.shape[0]
        for head in range(SWA_HEADS):
            def body(b, acc):
                return jnp.where(ids == b, table_ref[b, head], acc)
            acc = lax.fori_loop(0, N_BUCKETS, body, jnp.full(ids.shape, NEG, jnp.float32))
            kh, g = divmod(head, SWA_GROUP)
            out_ref[kh, g * nq:(g + 1) * nq, :] = acc


def _bias_tables(rel_table, id_arrays):
    out_shapes = [jax.ShapeDtypeStruct((SWA_KV_HEADS, SWA_GROUP * a.shape[0], a.shape[1]), jnp.float32)
                  for a in id_arrays]
    vmem = pl.BlockSpec(memory_space=pltpu.VMEM)
    return pl.pallas_call(
        _bias_kernel,
        in_specs=[pl.BlockSpec(memory_space=pltpu.SMEM)] + [vmem] * len(id_arrays),
        out_specs=[vmem] * len(id_arrays),
        out_shape=out_shapes,
        name="swa_bias",
    )(rel_table, *[jnp.asarray(a) for a in id_arrays])


def _attend(problems):
    scale = HEAD_DIM ** -0.5
    scores = [[_dot_nt(q, k) * scale + b for k, b in zip(keys, biases)]
              for q, keys, _, biases, _ in problems]
    maxes = []
    for (_, _, _, _, sink), segs in zip(problems, scores):
        m = sink
        for s in segs:
            m = jnp.maximum(m, jnp.max(s, axis=-1, keepdims=True))
        maxes.append(m)
    probs = [[jnp.exp(s - m) for s in segs] for segs, m in zip(scores, maxes)]
    outs = []
    for (_, _, values, _, sink), ps, m in zip(problems, probs, maxes):
        acc = None
        for p, v in zip(ps, values):
            v_ones = jnp.concatenate([v, jnp.ones((v.shape[0], HEAD_DIM), v.dtype)], axis=1)
            pv = _dot(p, v_ones)
            acc = pv if acc is None else acc + pv
        den = acc[:, HEAD_DIM:] + jnp.exp(sink - m)
        outs.append(acc[:, :HEAD_DIM] / den)
    return outs


def _group_queries(q_rows, kh):
    return jnp.concatenate(
        [q_rows((kh * SWA_GROUP + g) * HEAD_DIM, (kh * SWA_GROUP + g + 1) * HEAD_DIM)
         for g in range(SWA_GROUP)], axis=0)


def _pad_keys(rows):
    return jnp.concatenate([rows, jnp.zeros((WINDOW - rows.shape[0], rows.shape[1]), rows.dtype)], axis=0)


def _k_cols(kh):
    return slice(kh * HEAD_DIM, (kh + 1) * HEAD_DIM)


def _v_cols(kh):
    return slice((SWA_KV_HEADS + kh) * HEAD_DIM, (SWA_KV_HEADS + kh + 1) * HEAD_DIM)


def _swa_prompt_kernel(q_ref, kvc_ref, kvp_ref, kvm_ref, bcur_ref, bprev_ref, bm0_ref, bfar_ref,
                       sink_ref, o_ref):
    first = pl.program_id(0) == 0
    problems = []
    for blk in range(SWA_Q_BLOCKS):
        rows = slice(blk * WINDOW, (blk + 1) * WINDOW)
        prev_ref, prev_rows = (kvp_ref, slice(0, WINDOW)) if blk == 0 else (
            kvc_ref, slice((blk - 1) * WINDOW, blk * WINDOW))
        for kh in range(SWA_KV_HEADS):
            ks, vs = _k_cols(kh), _v_cols(kh)
            first_block = first if blk == 0 else False
            b_prev = jnp.where(first_block, NEG, bprev_ref[kh])
            b_meta = jnp.where(first_block, bm0_ref[kh], bfar_ref[kh])
            problems.append((_group_queries(lambda a, b, rows=rows: q_ref[rows, a:b], kh),
                             [kvc_ref[rows, ks], prev_ref[prev_rows, ks], _pad_keys(kvm_ref[:, ks])],
                             [kvc_ref[rows, vs], prev_ref[prev_rows, vs], _pad_keys(kvm_ref[:, vs])],
                             [bcur_ref[kh], b_prev, b_meta], sink_ref[kh]))
    for i, o in enumerate(_attend(problems)):
        blk, kh = divmod(i, SWA_KV_HEADS)
        for g in range(SWA_GROUP):
            head = kh * SWA_GROUP + g
            o_ref[blk * WINDOW:(blk + 1) * WINDOW, head * HEAD_DIM:(head + 1) * HEAD_DIM] = (
                o[g * WINDOW:(g + 1) * WINDOW].astype(o_ref.dtype))


def _swa_meta_kernel(q_ref, kv_ref, bias_ref, sink_ref, o_ref):
    problems = [(_group_queries(lambda a, b: q_ref[:, a:b], kh), [kv_ref[:, _k_cols(kh)]],
                 [kv_ref[:, _v_cols(kh)]], [bias_ref[kh]], sink_ref[kh])
                for kh in range(SWA_KV_HEADS)]
    for kh, o in enumerate(_attend(problems)):
        for g in range(SWA_GROUP):
            head = kh * SWA_GROUP + g
            o_ref[:, head * HEAD_DIM:(head + 1) * HEAD_DIM] = (
                o[g * N_META:(g + 1) * N_META].astype(o_ref.dtype))


def _swa_sample_kernel(q_ref, kvn_ref, win_ref, meta_ref, bwin_ref, bsmall_ref, sink_ref,
                       o_ref, wout_ref, *, nb, seq):
    cached = lambda ref, b, slot, n: ref[b, pl.ds(slot, n, stride=KV_SLOTS), :]
    keep = (WINDOW - seq) * KV_SLOTS
    wout_ref[:, 0:keep, :] = win_ref[:, seq * KV_SLOTS:WINDOW * KV_SLOTS, :]
    for slot in range(KV_SLOTS):
        wout_ref[:, pl.ds(keep + slot, seq, stride=KV_SLOTS), :] = (
            kvn_ref[:, :, slot * HEAD_DIM:(slot + 1) * HEAD_DIM])
    problems = []
    for b in range(nb):
        for kh in range(SWA_KV_HEADS):
            ks, vs = _k_cols(kh), _v_cols(kh)
            k_small = _pad_keys(jnp.concatenate([cached(meta_ref, b, kh, N_META), kvn_ref[b, :, ks]], axis=0))
            v_small = _pad_keys(jnp.concatenate(
                [cached(meta_ref, b, SWA_KV_HEADS + kh, N_META), kvn_ref[b, :, vs]], axis=0))
            problems.append((_group_queries(lambda a, c, b=b: q_ref[b, :, a:c], kh),
                             [cached(win_ref, b, kh, WINDOW), k_small],
                             [cached(win_ref, b, SWA_KV_HEADS + kh, WINDOW), v_small],
                             [bwin_ref[kh], bsmall_ref[kh]], sink_ref[kh]))
    outs = _attend(problems)
    for head in range(SWA_HEADS):
        kh, g = divmod(head, SWA_GROUP)
        rows = [outs[b * SWA_KV_HEADS + kh][g * seq:(g + 1) * seq] for b in range(nb)]
        o_ref[:, head * HEAD_DIM:(head + 1) * HEAD_DIM] = jnp.concatenate(rows, axis=0).astype(o_ref.dtype)


def _outproj_kernel(g_ref, s_ref, h_ref, wo_ref, nw_ref, o_ref):
    mix = (jnp.dot(g_ref[...], wo_ref[0:GDN_WIDTH, :], preferred_element_type=jnp.float32)
           + jnp.dot(s_ref[...], wo_ref[GDN_WIDTH:, :], preferred_element_type=jnp.float32))
    o_ref[...] = h_ref[...] + mix * _rms_scale(mix) * nw_ref[...]


def _outproj(g, s, h, wo, nw, *, tm):
    rows = h.shape[0]
    return pl.pallas_call(
        _outproj_kernel,
        grid=(rows // tm,),
        in_specs=[
            pl.BlockSpec((tm, GDN_WIDTH), lambda i: (i, 0)),
            pl.BlockSpec((tm, SWA_WIDTH), lambda i: (i, 0)),
            pl.BlockSpec((tm, D_MODEL), lambda i: (i, 0)),
            pl.BlockSpec((D_MODEL, D_MODEL), lambda i: (0, 0)),
            pl.BlockSpec((1, D_MODEL), lambda i: (0, 0)),
        ],
        out_specs=pl.BlockSpec((tm, D_MODEL), lambda i: (i, 0)),
        out_shape=jax.ShapeDtypeStruct((rows, D_MODEL), jnp.float32),
        compiler_params=pltpu.CompilerParams(
            dimension_semantics=("arbitrary",), vmem_limit_bytes=VMEM_LIMIT),
        name="outproj",
    )(g, s, h, wo, nw)


def _ffn_kernel(*refs, batch, tm, tf):
    if batch:
        (h_ref, nw_pre_ref, wg_ref, wu_ref, cw_ref, wd_ref, nw_post_ref, hist_ref,
         y_ref, graw_ref, xn_ref, acc_ref, xe_ref) = refs
    else:
        (h_ref, nw_pre_ref, wg_ref, wu_ref, cw_ref, wd_ref, nw_post_ref, hist_ref,
         y_ref, graw_ref, xn_ref, acc_ref, xe_ref, carry_ref) = refs
    i = pl.program_id(0)
    j = pl.program_id(1)
    last_j = pl.num_programs(1) - 1
    rb = tm // FFN_ROW_BLOCKS

    if batch:
        xe_ref[:, 0:SUBLANES, :] = hist_ref[...]
    else:
        @pl.when(i == 0)
        def _():
            carry_ref[pl.ds(j, 1)] = hist_ref[...].reshape(1, SUBLANES, tf)
        xe_ref[:, 0:SUBLANES, :] = carry_ref[pl.ds(j, 1)]

    def step(first, last):
        def gate_up(r):
            rows = slice(r * rb, (r + 1) * rb)
            if first:
                h = h_ref[rows, :]
                xn_ref[rows, :] = (h * _rms_scale(h) * nw_pre_ref[...]).astype(jnp.bfloat16)
            xn = xn_ref[rows, :]
            return (jnp.dot(xn, wg_ref[...], preferred_element_type=jnp.float32),
                    jnp.dot(xn, wu_ref[...], preferred_element_type=jnp.float32))

        nxt = gate_up(0)
        for r in range(FFN_ROW_BLOCKS):
            rows = slice(r * rb, (r + 1) * rb)
            gate, up = nxt
            if r + 1 < FFN_ROW_BLOCKS:
                nxt = gate_up(r + 1)
            if batch:
                seqs = slice(r * rb // SUBLANES, (r + 1) * rb // SUBLANES)
                graw_ref[rows, :] = gate
                xe_ref[seqs, SUBLANES:2 * SUBLANES, :] = gate.reshape(rb // SUBLANES, SUBLANES, tf)
                taps = [xe_ref[seqs, SUBLANES - s:2 * SUBLANES - s, :] for s in range(FFN_CONV)]
            else:
                base = SUBLANES + r * rb
                xe_ref[:, base:base + rb, :] = gate.reshape(1, rb, tf)
                taps = [xe_ref[:, base - s:base - s + rb, :] for s in range(FFN_CONV)]
            conv = None
            for s, tap in enumerate(taps):
                term = tap * cw_ref[FFN_CONV - 1 - s:FFN_CONV - s, :]
                conv = term if conv is None else conv + term
            act = (_silu(conv.reshape(rb, tf)) * up).astype(jnp.bfloat16)
            down = jnp.dot(act, wd_ref[...], preferred_element_type=jnp.float32)
            if first:
                acc_ref[rows, :] = down
            elif last:
                y = acc_ref[rows, :] + down
                y_ref[rows, :] = h_ref[rows, :] + y * _rms_scale(y) * nw_post_ref[...]
            else:
                acc_ref[rows, :] += down
        if not batch:
            tail = xe_ref[:, tm:tm + SUBLANES, :]
            carry_ref[pl.ds(j, 1)] = tail
            graw_ref[...] = tail.reshape(SUBLANES, tf)

    pl.when(j == 0)(functools.partial(step, True, False))
    pl.when((j > 0) & (j < last_j))(functools.partial(step, False, False))
    pl.when(j == last_j)(functools.partial(step, False, True))


def _ffn(h, nw_pre, wg, wu, cw, wd, nw_post, hist, *, batch, tm, tf):
    rows = h.shape[0]
    nj = D_FF // tf
    in_specs = [
        pl.BlockSpec((tm, D_MODEL), lambda i, j: (i, 0)),
        pl.BlockSpec((1, D_MODEL), lambda i, j: (0, 0)),
        pl.BlockSpec((D_MODEL, tf), lambda i, j: (0, j)),
        pl.BlockSpec((D_MODEL, tf), lambda i, j: (0, j)),
        pl.BlockSpec((FFN_CONV, tf), lambda i, j: (0, j)),
        pl.BlockSpec((tf, D_MODEL), lambda i, j: (j, 0)),
        pl.BlockSpec((1, D_MODEL), lambda i, j: (0, 0)),
    ]
    args = [h, nw_pre, wg, wu, cw, wd, nw_post, hist]
    scratch = [pltpu.VMEM((tm, D_MODEL), jnp.bfloat16), pltpu.VMEM((tm, D_MODEL), jnp.float32)]
    if batch:
        in_specs.append(pl.BlockSpec((tm // SUBLANES, SUBLANES, tf), lambda i, j: (i, 0, j)))
        graw_spec = pl.BlockSpec((tm, tf), lambda i, j: (i, j))
        graw_shape = jax.ShapeDtypeStruct((rows, D_FF), jnp.float32)
        scratch.append(pltpu.VMEM((tm // SUBLANES, 2 * SUBLANES, tf), jnp.float32))
    else:
        in_specs.append(pl.BlockSpec((SUBLANES, tf), lambda i, j: (0, j)))
        graw_spec = pl.BlockSpec((SUBLANES, tf), lambda i, j: (i, j))
        graw_shape = jax.ShapeDtypeStruct((rows // tm * SUBLANES, D_FF), jnp.float32)
        scratch.append(pltpu.VMEM((1, SUBLANES + tm, tf), jnp.float32))
        scratch.append(pltpu.VMEM((nj, SUBLANES, tf), jnp.float32))
    return pl.pallas_call(
        functools.partial(_ffn_kernel, batch=batch, tm=tm, tf=tf),
        grid=(rows // tm, nj),
        in_specs=in_specs,
        out_specs=[pl.BlockSpec((tm, D_MODEL), lambda i, j: (i, 0)), graw_spec],
        out_shape=[jax.ShapeDtypeStruct((rows, D_MODEL), jnp.float32), graw_shape],
        scratch_shapes=scratch,
        compiler_params=pltpu.CompilerParams(
            dimension_semantics=("arbitrary", "arbitrary"), vmem_limit_bytes=VMEM_LIMIT),
        name="ffn_batch" if batch else "ffn_seq",
    )(*args)


def kernel(x_prompt, x_sample, cache_swa_meta_kv, cache_swa_window_kv, state_gdn_conv, state_gdn, state_ffn_conv, meta_tokens, rel_bias_table, w_in, gdn_conv_w, gdn_a_log, gdn_dt_bias, gdn_norm_w, swa_sinks, w_out, norm_mix_pre, norm_mix_post, norm_ffn_pre, norm_ffn_post, ffn_w_gate, ffn_w_up, ffn_conv_w, ffn_w_down):
    f32, bf16 = jnp.float32, jnp.bfloat16
    seq = x_prompt.shape[1]
    dec_b, dec_t = x_sample.shape[0], x_sample.shape[1]
    n_dec = dec_b * dec_t
    assert x_prompt.shape[0] == 1 and seq % CHUNK == 0 and dec_t == SUBLANES and n_dec % CHUNK == 0

    wi = w_in[0]
    n_ba = 2 * GDN_HEADS
    w_in_p = _pack_w_in(w_in, n_ba, tn=512)
    w_ba = jnp.pad(wi[:, COL_SQ:COL_SQ + n_ba], ((0, 0), (0, LANES - n_ba))).astype(bf16)
    wo = w_out[0].astype(bf16)
    wg = ffn_w_gate[0].astype(bf16)
    wu = ffn_w_up[0].astype(bf16)
    wd = ffn_w_down[0].astype(bf16)
    lane_pad = lambda v: jnp.pad(v.reshape(1, GDN_HEADS), ((0, 0), (GDN_HEADS, LANES - 2 * GDN_HEADS)))
    alog_row = lane_pad(gdn_a_log[0])
    dtb_row = lane_pad(gdn_dt_bias[0])
    gnw = gdn_norm_w[0].reshape(1, HEAD_DIM)

    pad_rows = CHUNK - N_META
    n_small = n_dec + CHUNK
    x_big = x_prompt.reshape(seq, D_MODEL)
    x_small = jnp.concatenate(
        [x_sample.reshape(n_dec, D_MODEL), jnp.zeros((pad_rows, D_MODEL), f32), meta_tokens.astype(f32)], axis=0)
    nw = norm_mix_pre[0].reshape(1, D_MODEL)
    cw = gdn_conv_w[0]
    proj_small, ba_small = _inproj(x_small, nw, w_in_p, w_ba, tm=n_small, tn=512, row_chunk=128)
    proj_big, ba_big, qkv_tail = _inproj(
        x_big, nw, w_in_p, w_ba, (cw, proj_small[n_small - SUBLANES:, :GDN_QKV]),
        tm=1024, tn=512, row_chunk=128)

    small_chunks = proj_small.reshape(n_small // CHUNK, CHUNK, PROJ_COLS)
    small_groups = proj_small.reshape(n_small // SUBLANES, SUBLANES, PROJ_COLS)
    last_chunk = n_small // CHUNK - 1
    gdn_meta, s_meta = _gdn(
        small_chunks, ba_small.reshape(n_small // CHUNK, CHUNK, LANES), lambda s: (last_chunk, 0, 0),
        jnp.zeros((1, SUBLANES, GDN_QKV), f32), lambda s: (0, 0, 0),
        jnp.zeros((1, GDN_HEADS, HEAD_DIM, HEAD_DIM), f32), cw, alog_row, dtb_row, gnw,
        n_steps=1, nb=1, seq=CHUNK, group=CHUNK, carry=True, pad_rows=pad_rows, preconv=False)
    gdn_big, s_prompt = _gdn(
        proj_big.reshape(1, seq, PROJ_COLS), ba_big.reshape(1, seq, LANES), lambda s: (0, s, 0),
        jnp.zeros((1, SUBLANES, GDN_QKV), f32), lambda s: (0, 0, 0),
        s_meta, cw, alog_row, dtb_row, gnw,
        n_steps=seq // (GDN_SEQ_CHUNKS * CHUNK), nb=1, seq=GDN_SEQ_CHUNKS * CHUNK, group=CHUNK,
        carry=True, pad_rows=0, preconv=True)
    hist_gdn = jnp.pad(state_gdn_conv[0], ((0, 0), (SUBLANES - (GDN_CONV - 1), 0), (0, 0)))
    nb_gdn = CHUNK // dec_t
    gdn_small, s_sample = _gdn(
        small_groups, ba_small.reshape(n_small // SUBLANES, SUBLANES, LANES), lambda s: (s, 0, 0),
        hist_gdn, lambda s: (s, 0, 0),
        state_gdn, cw, alog_row, dtb_row, gnw,
        n_steps=dec_b // nb_gdn, nb=nb_gdn, seq=dec_t, group=dec_t, carry=False, pad_rows=0, preconv=False)

    qi = np.arange(WINDOW)[:, None]
    kj = np.arange(WINDOW)[None, :]
    mi = np.arange(N_META)[None, :]
    ti = np.arange(dec_t)[:, None]
    new_keys = kj - N_META
    id_arrays = [
        _bucket_ids(qi - kj, qi >= kj),
        _bucket_ids(qi - kj + WINDOW, kj > qi),
        _bucket_ids(qi + N_META - kj, kj < N_META),
        _bucket_ids(qi + N_META - kj + WINDOW, kj < N_META),
        _bucket_ids(ti + WINDOW - kj, kj > ti),
        _bucket_ids(np.where(new_keys < 0, PAST_LEN + ti - kj, ti - new_keys),
                    (new_keys < 0) | ((new_keys <= ti) & (new_keys < dec_t))),
        _bucket_ids(mi.T - mi, mi.T >= mi),
    ]
    bcur, bprev, bm0, bfar, bwin, bsmall, bmm = _bias_tables(rel_bias_table, id_arrays)
    sink_rows = lambda q: jnp.repeat(swa_sinks[0].reshape(SWA_KV_HEADS, SWA_GROUP), q, axis=1)[..., None]

    sq_blk = COL_SQ // SWA_WIDTH
    kv_blk = COL_KV // KV_WIDTH
    meta_blk = (n_small - N_META) // N_META
    full3 = lambda a: pl.BlockSpec(a.shape, lambda j: (0, 0, 0))
    sink_p = sink_rows(WINDOW)
    swa_rows = SWA_Q_BLOCKS * WINDOW
    swa_big = pl.pallas_call(
        _swa_prompt_kernel,
        grid=(seq // swa_rows,),
        in_specs=[
            pl.BlockSpec((swa_rows, SWA_WIDTH), lambda j: (j, sq_blk)),
            pl.BlockSpec((swa_rows, KV_WIDTH), lambda j: (j, kv_blk)),
            pl.BlockSpec((WINDOW, KV_WIDTH), lambda j: (jnp.maximum(j * SWA_Q_BLOCKS - 1, 0), kv_blk)),
            pl.BlockSpec((N_META, KV_WIDTH), lambda j: (meta_blk, kv_blk)),
            full3(bcur), full3(bprev), full3(bm0), full3(bfar), full3(sink_p),
        ],
        out_specs=pl.BlockSpec((swa_rows, SWA_WIDTH), lambda j: (j, 0)),
        out_shape=jax.ShapeDtypeStruct((seq, SWA_WIDTH), bf16),
        compiler_params=pltpu.CompilerParams(
            dimension_semantics=("arbitrary",), vmem_limit_bytes=VMEM_LIMIT),
        name="swa_prompt",
    )(proj_big, proj_big, proj_big, proj_small, bcur, bprev, bm0, bfar, sink_p)

    nb_swa = 8
    sink_s = sink_rows(dec_t)
    win = cache_swa_window_kv.reshape(dec_b, WINDOW * KV_SLOTS, HEAD_DIM)
    meta_kv = cache_swa_meta_kv.reshape(dec_b, N_META * KV_SLOTS, HEAD_DIM)
    swa_small, win_new = pl.pallas_call(
        functools.partial(_swa_sample_kernel, nb=nb_swa, seq=dec_t),
        grid=(dec_b // nb_swa,),
        in_specs=[
            pl.BlockSpec((nb_swa, dec_t, SWA_WIDTH), lambda j: (j, 0, sq_blk)),
            pl.BlockSpec((nb_swa, dec_t, KV_WIDTH), lambda j: (j, 0, kv_blk)),
            pl.BlockSpec((nb_swa, WINDOW * KV_SLOTS, HEAD_DIM), lambda j: (j, 0, 0)),
            pl.BlockSpec((nb_swa, N_META * KV_SLOTS, HEAD_DIM), lambda j: (j, 0, 0)),
            full3(bwin), full3(bsmall), full3(sink_s),
        ],
        out_specs=[pl.BlockSpec((nb_swa * dec_t, SWA_WIDTH), lambda j: (j, 0)),
                   pl.BlockSpec((nb_swa, WINDOW * KV_SLOTS, HEAD_DIM), lambda j: (j, 0, 0))],
        out_shape=[jax.ShapeDtypeStruct((n_dec, SWA_WIDTH), bf16),
                   jax.ShapeDtypeStruct((dec_b, WINDOW * KV_SLOTS, HEAD_DIM), f32)],
        compiler_params=pltpu.CompilerParams(
            dimension_semantics=("arbitrary",), vmem_limit_bytes=VMEM_LIMIT),
        name="swa_sample",
    )(small_groups, small_groups, win, meta_kv, bwin, bsmall, sink_s)

    sink_m = sink_rows(N_META)
    swa_meta = pl.pallas_call(
        _swa_meta_kernel,
        grid=(1,),
        in_specs=[
            pl.BlockSpec((N_META, SWA_WIDTH), lambda j: (meta_blk, sq_blk)),
            pl.BlockSpec((N_META, KV_WIDTH), lambda j: (meta_blk, kv_blk)),
            full3(bmm), full3(sink_m),
        ],
        out_specs=pl.BlockSpec((N_META, SWA_WIDTH), lambda j: (0, 0)),
        out_shape=jax.ShapeDtypeStruct((N_META, SWA_WIDTH), bf16),
        name="swa_meta",
    )(proj_small, proj_small, bmm, sink_m)

    nw_post = norm_mix_post[0].reshape(1, D_MODEL)
    nf_pre = norm_ffn_pre[0].reshape(1, D_MODEL)
    nf_post = norm_ffn_post[0].reshape(1, D_MODEL)
    fcw = ffn_conv_w[0]
    gdn_small_all = jnp.concatenate([gdn_small, gdn_meta], axis=0)
    swa_small_all = jnp.concatenate([swa_small, jnp.zeros((pad_rows, SWA_WIDTH), bf16), swa_meta], axis=0)
    h_small = _outproj(gdn_small_all, swa_small_all, x_small, wo, nw_post, tm=n_small // 2)
    hist_ffn = jnp.pad(state_ffn_conv[0], ((0, CHUNK // SUBLANES), (SUBLANES - (FFN_CONV - 1), 0), (0, 0)))
    y_small, g_small = _ffn(h_small, nf_pre, wg, wu, fcw, wd, nf_post, hist_ffn,
                            batch=True, tm=n_small // 2, tf=512)
    h_big = _outproj(gdn_big, swa_big, x_big, wo, nw_post, tm=512)
    y_big, g_tail = _ffn(h_big, nf_pre, wg, wu, fcw, wd, nf_post, g_small[n_small - SUBLANES:],
                         batch=False, tm=512, tf=512)

    kv_shape = lambda n: (1, n, 2, SWA_KV_HEADS, HEAD_DIM)
    kv_small = proj_small[:, COL_KV:COL_KV + KV_WIDTH]
    y_prompt = y_big.reshape(1, seq, D_MODEL)
    y_sample = y_small[:n_dec].reshape(dec_b, dec_t, D_MODEL)
    p_meta_kv = kv_small[n_small - N_META:].reshape(kv_shape(N_META))[None]
    p_window_kv = proj_big[seq - WINDOW:, COL_KV:COL_KV + KV_WIDTH].reshape(kv_shape(WINDOW))[None]
    p_gdn_conv = qkv_tail[qkv_tail.shape[0] - (GDN_CONV - 1):].reshape(1, 1, GDN_CONV - 1, GDN_QKV)
    p_gdn_state = s_prompt[None]
    p_ffn_conv = g_tail[g_tail.shape[0] - (FFN_CONV - 1):].reshape(1, 1, FFN_CONV - 1, D_FF)
    s_window_kv = win_new.reshape(1, dec_b, WINDOW, 2, SWA_KV_HEADS, HEAD_DIM)
    s_gdn_conv = proj_small[:n_dec, :GDN_QKV].reshape(dec_b, dec_t, GDN_QKV)[:, dec_t - (GDN_CONV - 1):][None]
    s_gdn_state = s_sample
    s_ffn_conv = g_small[:n_dec].reshape(dec_b, dec_t, D_FF)[:, dec_t - (FFN_CONV - 1):][None]
    return (y_prompt, y_sample, p_meta_kv, p_window_kv, p_gdn_conv, p_gdn_state, p_ffn_conv,
            s_window_kv, s_gdn_conv, s_gdn_state, s_ffn_conv)
```

```python
import functools
import math

import numpy as np
import jax
import jax.numpy as jnp
from jax import lax
from jax.experimental import pallas as pl
from jax.experimental.pallas import tpu as pltpu

D_MODEL = 2048
HEAD_DIM = 128
GDN_HEADS = 8
GDN_WIDTH = GDN_HEADS * HEAD_DIM
GDN_QKV = 3 * GDN_WIDTH
SWA_HEADS = 8
SWA_KV_HEADS = 2
SWA_GROUP = SWA_HEADS // SWA_KV_HEADS
SWA_WIDTH = SWA_HEADS * HEAD_DIM
KV_SLOTS = 2 * SWA_KV_HEADS
KV_WIDTH = KV_SLOTS * HEAD_DIM
WINDOW = 128
N_META = 16
N_BUCKETS = 32
MAX_DISTANCE = 128
GDN_CONV = 4
FFN_CONV = 3
D_FF = 5632
EPS = 1e-6
PAST_LEN = 16384

SUBLANES = 8
LANES = 128

CHUNK = 128
assert CHUNK == HEAD_DIM == LANES
INV_BASE = 16
GDN_SEQ_CHUNKS = 2
INPROJ_ROW_BLOCKS = 4
FFN_ROW_BLOCKS = 4

COL_Z = GDN_QKV
COL_SQ = COL_Z + GDN_WIDTH
COL_KV = COL_SQ + SWA_WIDTH
PROJ_COLS = COL_KV + KV_WIDTH
GDN_COLS = COL_SQ

SWA_Q_BLOCKS = 4
NEG = -1e30
VMEM_LIMIT = 56 * 1024 * 1024

_NT = (((1,), (1,)), ((), ()))


def _dot(a, b):
    return jnp.dot(a.astype(jnp.bfloat16), b.astype(jnp.bfloat16), preferred_element_type=jnp.float32)


def _dot_nt(a, b):
    return lax.dot_general(a.astype(jnp.bfloat16), b.astype(jnp.bfloat16), _NT,
                           preferred_element_type=jnp.float32)


def _split(a):
    hi = a.astype(jnp.bfloat16)
    lo = (a - hi.astype(jnp.float32)).astype(jnp.bfloat16)
    return hi, lo


def _dot3(a, b):
    ah, al = _split(a)
    bh, bl = _split(b)
    d = functools.partial(jnp.dot, preferred_element_type=jnp.float32)
    return d(ah, bh) + (d(ah, bl) + d(al, bh))


_dot_inv = _dot


def _dot_exact(a, b, dims=None):
    if dims is None:
        return jnp.dot(a, b, precision=lax.Precision.HIGHEST, preferred_element_type=jnp.float32)
    return lax.dot_general(a, b, dims, precision=lax.Precision.HIGHEST,
                           preferred_element_type=jnp.float32)


def _pack_w_in_kernel(a_ref, b_ref, o_ref, ba_ref, *, first_shifted, shift):
    j = pl.program_id(0)

    @pl.when(j < first_shifted)
    def _():
        o_ref[...] = a_ref[...].T.astype(o_ref.dtype)

    @pl.when(j >= first_shifted)
    def _():
        rows = jnp.concatenate([a_ref[shift:, :], b_ref[:shift, :]], axis=0)
        o_ref[...] = rows.T.astype(o_ref.dtype)

    @pl.when(j == first_shifted)
    def _():
        head = a_ref[:LANES, :]
        row = lax.broadcasted_iota(jnp.int32, head.shape, 0)
        ba_ref[...] = jnp.where(row < shift, head, 0.0).T.astype(ba_ref.dtype)


def _pack_w_in(w_in_t, n_ba, *, tn):
    n_blocks = PROJ_COLS // tn
    first_shifted = COL_SQ // tn
    assert COL_SQ % tn == 0 and w_in_t.shape[0] == PROJ_COLS + n_ba and n_ba % SUBLANES == 0
    return pl.pallas_call(
        functools.partial(_pack_w_in_kernel, first_shifted=first_shifted, shift=n_ba),
        grid=(n_blocks,),
        in_specs=[pl.BlockSpec((tn, D_MODEL), lambda j: (j, 0)),
                  pl.BlockSpec((tn, D_MODEL), lambda j: (jnp.maximum(j, first_shifted) + 1, 0))],
        out_specs=[pl.BlockSpec((D_MODEL, tn), lambda j: (0, j)),
                   pl.BlockSpec((D_MODEL, LANES), lambda j: (0, 0))],
        out_shape=[jax.ShapeDtypeStruct((D_MODEL, PROJ_COLS), jnp.bfloat16),
                   jax.ShapeDtypeStruct((D_MODEL, LANES), jnp.bfloat16)],
        compiler_params=pltpu.CompilerParams(
            dimension_semantics=("arbitrary",), vmem_limit_bytes=VMEM_LIMIT),
        name="pack_w_in",
    )(w_in_t, w_in_t)


def _rms_scale(x):
    return lax.rsqrt(jnp.mean(x * x, axis=-1, keepdims=True) + EPS)


def _silu(x):
    return x * jax.nn.sigmoid(x)


def _inproj_kernel(*refs, row_chunk, conv_tiles):
    if conv_tiles:
        (x_ref, nw_ref, w_ref, wba_ref, cw_ref, hist_ref,
         o_ref, ba_ref, tail_ref, xn_ref, xe_ref, carry_ref) = refs
    else:
        x_ref, nw_ref, w_ref, wba_ref, o_ref, ba_ref, xn_ref = refs
    i = pl.program_id(0)
    j = pl.program_id(1)
    tm, tn = o_ref.shape

    def normalize(rows):
        x = x_ref[rows, :]
        xn_ref[rows, :] = (x * _rms_scale(x) * nw_ref[...]).astype(jnp.bfloat16)

    def plain():
        o_ref[...] = jnp.dot(xn_ref[...], w_ref[...], preferred_element_type=jnp.float32)

    if not conv_tiles:
        @pl.when(j == 0)
        def _():
            def body(c, carry):
                normalize(pl.ds(pl.multiple_of(c * row_chunk, row_chunk), row_chunk))
                return carry
            lax.fori_loop(0, tm // row_chunk, body, 0)
            ba_ref[...] = jnp.dot(xn_ref[...], wba_ref[...], preferred_element_type=jnp.float32)
        plain()
        return
    pl.when(j >= conv_tiles)(plain)

    def conv_tile(first):
        @pl.when(i == 0)
        def _():
            carry_ref[pl.ds(j, 1)] = hist_ref[...].reshape(1, SUBLANES, tn)
        xe_ref[0:SUBLANES, :] = carry_ref[pl.ds(j, 1)].reshape(SUBLANES, tn)
        rb = tm // INPROJ_ROW_BLOCKS

        def raw_block(r):
            rows = slice(r * rb, (r + 1) * rb)
            if first:
                normalize(rows)
                ba_ref[rows, :] = jnp.dot(xn_ref[rows, :], wba_ref[...], preferred_element_type=jnp.float32)
            return jnp.dot(xn_ref[rows, :], w_ref[...], preferred_element_type=jnp.float32)

        nxt = raw_block(0)
        for r in range(INPROJ_ROW_BLOCKS):
            raw = nxt
            if r + 1 < INPROJ_ROW_BLOCKS:
                nxt = raw_block(r + 1)
            base = SUBLANES + r * rb
            xe_ref[base:base + rb, :] = raw
            conv = None
            for s in range(GDN_CONV):
                term = xe_ref[base - s:base - s + rb, :] * cw_ref[GDN_CONV - 1 - s:GDN_CONV - s, :]
                conv = term if conv is None else conv + term
            o_ref[r * rb:(r + 1) * rb, :] = _silu(conv)
        tail = xe_ref[tm:tm + SUBLANES, :]
        carry_ref[pl.ds(j, 1)] = tail.reshape(1, SUBLANES, tn)
        tail_ref[...] = tail

    pl.when(j == 0)(functools.partial(conv_tile, True))
    pl.when((j > 0) & (j < conv_tiles))(functools.partial(conv_tile, False))


def _inproj(x, nw, w, wba, conv=None, *, tm, tn, row_chunk):
    rows = x.shape[0]
    conv_tiles = GDN_QKV // tn if conv else 0
    in_specs = [
        pl.BlockSpec((tm, D_MODEL), lambda i, j: (i, 0)),
        pl.BlockSpec((1, D_MODEL), lambda i, j: (0, 0)),
        pl.BlockSpec((D_MODEL, tn), lambda i, j: (0, j)),
        pl.BlockSpec((D_MODEL, LANES), lambda i, j: (0, 0)),
    ]
    out_specs = [pl.BlockSpec((tm, tn), lambda i, j: (i, j)),
                 pl.BlockSpec((tm, LANES), lambda i, j: (i, 0))]
    out_shape = [jax.ShapeDtypeStruct((rows, PROJ_COLS), jnp.float32),
                 jax.ShapeDtypeStruct((rows, LANES), jnp.float32)]
    scratch = [pltpu.VMEM((tm, D_MODEL), jnp.bfloat16)]
    args = [x, nw, w, wba]
    if conv:
        conv_col = lambda i, j: (0, jnp.minimum(j, conv_tiles - 1))
        in_specs += [pl.BlockSpec((GDN_CONV, tn), conv_col), pl.BlockSpec((SUBLANES, tn), conv_col)]
        out_specs.append(pl.BlockSpec((SUBLANES, tn), lambda i, j: (i, jnp.minimum(j, conv_tiles - 1))))
        out_shape.append(jax.ShapeDtypeStruct((rows // tm * SUBLANES, GDN_QKV), jnp.float32))
        scratch += [pltpu.VMEM((SUBLANES + tm, tn), jnp.float32),
                    pltpu.VMEM((conv_tiles, SUBLANES, tn), jnp.float32)]
        args += list(conv)
    return pl.pallas_call(
        functools.partial(_inproj_kernel, row_chunk=row_chunk, conv_tiles=conv_tiles),
        grid=(rows // tm, PROJ_COLS // tn),
        in_specs=in_specs,
        out_specs=out_specs,
        out_shape=out_shape,
        scratch_shapes=scratch,
        compiler_params=pltpu.CompilerParams(
            dimension_semantics=("arbitrary", "arbitrary"), vmem_limit_bytes=VMEM_LIMIT),
        name="inproj_conv" if conv else "inproj",
    )(*args)


def _tri_inverse(lms, ri, ci):
    shift = INV_BASE.bit_length() - 1
    eye = (ri == ci).astype(jnp.float32)
    in_block = (ri >> shift) == (ci >> shift)
    ps = [jnp.where(in_block, lm, 0.0) for lm in lms]
    ts = [eye - p for p in ps]
    for _ in range(shift - 1):
        ps = [_dot_inv(p, p) for p in ps]
        ts = [t + _dot_inv(t, p) for t, p in zip(ts, ps)]
    size = INV_BASE
    while size < CHUNK:
        shift += 1
        in_pair = (ri >> shift) == (ci >> shift)
        off_mask = in_pair & jnp.logical_not(in_block)
        tos = [_dot_inv(t, jnp.where(off_mask, lm, 0.0)) for t, lm in zip(ts, lms)]
        ts = [t - _dot_inv(to, t) for t, to in zip(ts, tos)]
        in_block = in_pair
        size *= 2
    return ts


def _gdn_kernel(x_ref, ba_ref, hist_ref, s0_ref, cw_ref, alog_ref, dtb_ref, gnw_ref,
                o_ref, sout_ref, xe_ref, s_ref, *, nb, seq, group, carry, pad_rows, preconv):
    step = pl.program_id(0)
    rows = nb * seq
    n_chunks = rows // CHUNK
    n_groups = CHUNK // group
    gshift = group.bit_length() - 1

    if carry:
        @pl.when(step == 0)
        def _():
            s_ref[...] = s0_ref[0]
    if not preconv:
        if carry:
            @pl.when(step == 0)
            def _():
                xe_ref[:, 0:SUBLANES, :] = hist_ref[...]
        else:
            xe_ref[:, 0:SUBLANES, :] = hist_ref[...]
        xe_ref[:, SUBLANES:SUBLANES + seq, :] = x_ref[:, :, 0:GDN_QKV]

    seq_rows = min(seq, CHUNK)
    seqs_per_chunk = CHUNK // seq_rows

    def chunk_rows(ref, c, row_off, cols):
        if seq >= CHUNK:
            start = row_off + c * CHUNK
            return ref[0:1, start:start + CHUNK, cols]
        b0 = c * seqs_per_chunk
        return ref[b0:b0 + seqs_per_chunk, row_off:row_off + seq, cols]

    def conv_chunk(col, c):
        cols = slice(col, col + HEAD_DIM)
        if preconv:
            return chunk_rows(x_ref, c, 0, cols).reshape(CHUNK, HEAD_DIM)
        acc = None
        for s in range(GDN_CONV):
            term = chunk_rows(xe_ref, c, SUBLANES - s, cols) * cw_ref[GDN_CONV - 1 - s:GDN_CONV - s, cols]
            acc = term if acc is None else acc + term
        return _silu(acc).reshape(CHUNK, HEAD_DIM)

    ri = lax.broadcasted_iota(jnp.int32, (CHUNK, CHUNK), 0)
    ci = lax.broadcasted_iota(jnp.int32, (CHUNK, CHUNK), 1)
    same = (ri >> gshift) == (ci >> gshift)
    m_incl = same & (ri >= ci)
    m_strict = same & (ri > ci)
    f_incl = m_incl.astype(jnp.float32)
    f_same = same.astype(jnp.float32)
    lane = lax.broadcasted_iota(jnp.int32, (CHUNK, LANES), 1)
    row_in_chunk = lax.broadcasted_iota(jnp.int32, (CHUNK, LANES), 0)

    pre = []
    for c in range(n_chunks):
        bac = chunk_rows(ba_ref, c, 0, slice(0, LANES)).reshape(CHUNK, LANES)
        beta_all = jax.nn.sigmoid(bac)
        sp_in = bac + dtb_ref[...]
        softplus = jnp.maximum(sp_in, 0.0) + jnp.log1p(jnp.exp(-jnp.abs(sp_in)))
        g_all = -jnp.exp(alog_ref[...]) * softplus
        if pad_rows and c == 0:
            valid = row_in_chunk >= pad_rows
            beta_all = jnp.where(valid, beta_all, 0.0)
            g_all = jnp.where(valid, g_all, 0.0)
        g_all = jnp.where((lane >= GDN_HEADS) & (lane < 2 * GDN_HEADS), g_all, 0.0)
        gc_col = _dot_exact(f_incl, g_all)
        gtot_col = _dot_exact(f_same, g_all)
        gc_row = _dot_exact(g_all.T, f_incl, _NT)

        for h in range(GDN_HEADS):
            qh = conv_chunk(h * HEAD_DIM, c)
            kh = conv_chunk(GDN_WIDTH + h * HEAD_DIM, c)
            vh = conv_chunk(2 * GDN_WIDTH + h * HEAD_DIM, c)
            qh = qh * lax.rsqrt(jnp.sum(qh * qh, -1, keepdims=True) + EPS) * (HEAD_DIM ** -0.5)
            kh = kh * lax.rsqrt(jnp.sum(kh * kh, -1, keepdims=True) + EPS)
            bcast = lambda col: jnp.broadcast_to(col, (CHUNK, HEAD_DIM))
            gcc = bcast(gc_col[:, GDN_HEADS + h:GDN_HEADS + h + 1])
            gtc = bcast(gtot_col[:, GDN_HEADS + h:GDN_HEADS + h + 1])
            beta = bcast(beta_all[:, h:h + 1])
            gcr = gc_row[GDN_HEADS + h:GDN_HEADS + h + 1, :]
            decay = jnp.exp(jnp.where(m_incl, gcc - gcr, NEG))
            kb = kh * beta
            egc = jnp.exp(gcc)
            pre.append(dict(
                c=c, h=h,
                lm=jnp.where(m_strict, _dot_nt(kb, kh) * decay, 0.0),
                qk=_dot_nt(qh, kh) * decay,
                rhs=jnp.concatenate([vh * beta, kb * egc], axis=1),
                qg=qh * egc,
                kd_t=(kh * jnp.exp(gtc - gcc)).T,
                gl=jnp.exp(gtc)))

    inverses = _tri_inverse([p["lm"] for p in pre], ri, ci)
    sols = [_dot_inv(t, p["rhs"]) for t, p in zip(inverses, pre)]

    for c in range(n_chunks):
        r0 = c * CHUNK
        items = [(p, sol) for p, sol in zip(pre, sols) if p["c"] == c]
        state = lambda h, b: s_ref[h] if carry else s0_ref[c * n_groups + b, h]
        ws, qs = [], []
        for p, sol in items:
            w = sol[:, HEAD_DIM:]
            ws_parts, qs_parts = [], []
            for b in range(n_groups):
                g0 = b * group
                wq = jnp.concatenate([w[g0:g0 + group], p["qg"][g0:g0 + group]], axis=0)
                res = _dot(wq, state(p["h"], b))
                ws_parts.append(res[:group])
                qs_parts.append(res[group:])
            ws.append(ws_parts[0] if n_groups == 1 else jnp.concatenate(ws_parts, axis=0))
            qs.append(qs_parts[0] if n_groups == 1 else jnp.concatenate(qs_parts, axis=0))
        v_new = [sol[:, :HEAD_DIM] - w for (p, sol), w in zip(items, ws)]
        o = [a + _dot(p["qk"], vn) for a, (p, sol), vn in zip(qs, items, v_new)]
        for (p, sol), vn in zip(items, v_new):
            h = p["h"]
            for b in range(n_groups):
                g0 = b * group
                kd_b = p["kd_t"] if n_groups == 1 else jnp.where((ci >> gshift) == b, p["kd_t"], 0.0)
                st = state(h, b) * p["gl"][g0:g0 + 1, :] + _dot(kd_b, vn)
                if carry:
                    s_ref[h] = st
                else:
                    sout_ref[c * n_groups + b, h] = st
        for (p, sol), oh in zip(items, o):
            h = p["h"]
            z = chunk_rows(x_ref, c, 0, slice(COL_Z + h * HEAD_DIM, COL_Z + (h + 1) * HEAD_DIM)).reshape(CHUNK, HEAD_DIM)
            y = oh * lax.rsqrt(jnp.mean(oh * oh, -1, keepdims=True) + EPS) * gnw_ref[...] * _silu(z)
            o_ref[r0:r0 + CHUNK, h * HEAD_DIM:(h + 1) * HEAD_DIM] = y.astype(o_ref.dtype)

    if carry:
        if not preconv:
            xe_ref[:, 0:SUBLANES, :] = xe_ref[:, seq:seq + SUBLANES, :]

        @pl.when(step == pl.num_programs(0) - 1)
        def _():
            sout_ref[0] = s_ref[...]


def _gdn(x3, ba3, x_idx, hist, hist_idx, s0, cw, alog_row, dtb_row, gnw, *,
         n_steps, nb, seq, group, carry, pad_rows, preconv):
    rows = nb * seq
    if carry:
        state_spec = pl.BlockSpec((1, GDN_HEADS, HEAD_DIM, HEAD_DIM), lambda s: (0, 0, 0, 0))
        state_shape = (1, GDN_HEADS, HEAD_DIM, HEAD_DIM)
    else:
        n_states = rows // group
        state_spec = pl.BlockSpec((None, n_states, GDN_HEADS, HEAD_DIM, HEAD_DIM), lambda s: (0, s, 0, 0, 0))
        state_shape = (1, n_steps * n_states, GDN_HEADS, HEAD_DIM, HEAD_DIM)
    full = lambda shape: pl.BlockSpec(shape, lambda s: (0,) * len(shape))
    return pl.pallas_call(
        functools.partial(_gdn_kernel, nb=nb, seq=seq, group=group, carry=carry, pad_rows=pad_rows,
                          preconv=preconv),
        grid=(n_steps,),
        in_specs=[
            pl.BlockSpec((nb, seq, GDN_COLS), x_idx),
            pl.BlockSpec((nb, seq, LANES), x_idx),
            pl.BlockSpec((nb, SUBLANES, GDN_QKV), hist_idx),
            state_spec,
            full((GDN_CONV, GDN_QKV)),
            full((1, LANES)),
            full((1, LANES)),
            full((1, HEAD_DIM)),
        ],
        out_specs=[
            pl.BlockSpec((rows, GDN_WIDTH), lambda s: (s, 0)),
            state_spec,
        ],
        out_shape=[
            jax.ShapeDtypeStruct((n_steps * rows, GDN_WIDTH), jnp.bfloat16),
            jax.ShapeDtypeStruct(state_shape, jnp.float32),
        ],
        scratch_shapes=[
            pltpu.VMEM((nb, SUBLANES + (SUBLANES if preconv else seq), GDN_QKV), jnp.float32),
            pltpu.VMEM((GDN_HEADS, HEAD_DIM, HEAD_DIM), jnp.float32),
        ],
        compiler_params=pltpu.CompilerParams(
            dimension_semantics=("arbitrary",), vmem_limit_bytes=VMEM_LIMIT),
        name="gdn_seq" if carry else "gdn_batch",
    )(x3, ba3, hist, s0, cw, alog_row, dtb_row, gnw)


def _t5_bucket_np(dist):
    n = np.maximum(dist, 0)
    exact = N_BUCKETS // 2
    large = exact + (np.log(np.maximum(n, 1).astype(np.float32) / exact)
                     / math.log(MAX_DISTANCE / exact) * (N_BUCKETS - exact)).astype(np.int32)
    return np.where(n < exact, n, np.minimum(large, N_BUCKETS - 1)).astype(np.int32)


def _bucket_ids(dist, valid):
    return np.where(valid, _t5_bucket_np(dist), -1).astype(np.int32)


def _bias_kernel(table_ref, *refs):
    n = len(refs) // 2
    for ids_ref, out_ref in zip(refs[:n], refs[n:]):
        ids = ids_ref[...]
        nq = ids.shape[0]
        for head in range(SWA_HEADS):
            def body(b, acc):
                return jnp.where(ids == b, table_ref[b, head], acc)
            acc = lax.fori_loop(0, N_BUCKETS, body, jnp.full(ids.shape, NEG, jnp.float32))
            kh, g = divmod(head, SWA_GROUP)
            out_ref[kh, g * nq:(g + 1) * nq, :] = acc


def _bias_tables(rel_table, id_arrays):
    out_shapes = [jax.ShapeDtypeStruct((SWA_KV_HEADS, SWA_GROUP * a.shape[0], a.shape[1]), jnp.float32)
                  for a in id_arrays]
    vmem = pl.BlockSpec(memory_space=pltpu.VMEM)
    return pl.pallas_call(
        _bias_kernel,
        in_specs=[pl.BlockSpec(memory_space=pltpu.SMEM)] + [vmem] * len(id_arrays),
        out_specs=[vmem] * len(id_arrays),
        out_shape=out_shapes,
        name="swa_bias",
    )(rel_table, *[jnp.asarray(a) for a in id_arrays])


def _attend(problems):
    scale = HEAD_DIM ** -0.5
    scores = [[_dot_nt(q, k) * scale + b for k, b in zip(keys, biases)]
              for q, keys, _, biases, _ in problems]
    maxes = []
    for (_, _, _, _, sink), segs in zip(problems, scores):
        m = sink
        for s in segs:
            m = jnp.maximum(m, jnp.max(s, axis=-1, keepdims=True))
        maxes.append(m)
    probs = [[jnp.exp(s - m) for s in segs] for segs, m in zip(scores, maxes)]
    outs = []
    for (_, _, values, _, sink), ps, m in zip(problems, probs, maxes):
        acc = None
        for p, v in zip(ps, values):
            v_ones = jnp.concatenate([v, jnp.ones((v.shape[0], HEAD_DIM), v.dtype)], axis=1)
            pv = _dot(p, v_ones)
            acc = pv if acc is None else acc + pv
        den = acc[:, HEAD_DIM:] + jnp.exp(sink - m)
        outs.append(acc[:, :HEAD_DIM] / den)
    return outs


def _group_queries(q_rows, kh):
    return jnp.concatenate(
        [q_rows((kh * SWA_GROUP + g) * HEAD_DIM, (kh * SWA_GROUP + g + 1) * HEAD_DIM)
         for g in range(SWA_GROUP)], axis=0)


def _pad_keys(rows):
    return jnp.concatenate([rows, jnp.zeros((WINDOW - rows.shape[0], rows.shape[1]), rows.dtype)], axis=0)


def _k_cols(kh):
    return slice(kh * HEAD_DIM, (kh + 1) * HEAD_DIM)


def _v_cols(kh):
    return slice((SWA_KV_HEADS + kh) * HEAD_DIM, (SWA_KV_HEADS + kh + 1) * HEAD_DIM)


def _swa_prompt_kernel(q_ref, kvc_ref, kvp_ref, kvm_ref, bcur_ref, bprev_ref, bm0_ref, bfar_ref,
                       sink_ref, o_ref):
    first = pl.program_id(0) == 0
    problems = []
    for blk in range(SWA_Q_BLOCKS):
        rows = slice(blk * WINDOW, (blk + 1) * WINDOW)
        prev_ref, prev_rows = (kvp_ref, slice(0, WINDOW)) if blk == 0 else (
            kvc_ref, slice((blk - 1) * WINDOW, blk * WINDOW))
        for kh in range(SWA_KV_HEADS):
            ks, vs = _k_cols(kh), _v_cols(kh)
            first_block = first if blk == 0 else False
            b_prev = jnp.where(first_block, NEG, bprev_ref[kh])
            b_meta = jnp.where(first_block, bm0_ref[kh], bfar_ref[kh])
            problems.append((_group_queries(lambda a, b, rows=rows: q_ref[rows, a:b], kh),
                             [kvc_ref[rows, ks], prev_ref[prev_rows, ks], _pad_keys(kvm_ref[:, ks])],
                             [kvc_ref[rows, vs], prev_ref[prev_rows, vs], _pad_keys(kvm_ref[:, vs])],
                             [bcur_ref[kh], b_prev, b_meta], sink_ref[kh]))
    for i, o in enumerate(_attend(problems)):
        blk, kh = divmod(i, SWA_KV_HEADS)
        for g in range(SWA_GROUP):
            head = kh * SWA_GROUP + g
            o_ref[blk * WINDOW:(blk + 1) * WINDOW, head * HEAD_DIM:(head + 1) * HEAD_DIM] = (
                o[g * WINDOW:(g + 1) * WINDOW].astype(o_ref.dtype))


def _swa_meta_kernel(q_ref, kv_ref, bias_ref, sink_ref, o_ref):
    problems = [(_group_queries(lambda a, b: q_ref[:, a:b], kh), [kv_ref[:, _k_cols(kh)]],
                 [kv_ref[:, _v_cols(kh)]], [bias_ref[kh]], sink_ref[kh])
                for kh in range(SWA_KV_HEADS)]
    for kh, o in enumerate(_attend(problems)):
        for g in range(SWA_GROUP):
            head = kh * SWA_GROUP + g
            o_ref[:, head * HEAD_DIM:(head + 1) * HEAD_DIM] = (
                o[g * N_META:(g + 1) * N_META].astype(o_ref.dtype))


def _swa_sample_kernel(q_ref, kvn_ref, win_ref, meta_ref, bwin_ref, bsmall_ref, sink_ref,
                       o_ref, wout_ref, *, nb, seq):
    cached = lambda ref, b, slot, n: ref[b, pl.ds(slot, n, stride=KV_SLOTS), :]
    keep = (WINDOW - seq) * KV_SLOTS
    wout_ref[:, 0:keep, :] = win_ref[:, seq * KV_SLOTS:WINDOW * KV_SLOTS, :]
    for slot in range(KV_SLOTS):
        wout_ref[:, pl.ds(keep + slot, seq, stride=KV_SLOTS), :] = (
            kvn_ref[:, :, slot * HEAD_DIM:(slot + 1) * HEAD_DIM])
    problems = []
    for b in range(nb):
        for kh in range(SWA_KV_HEADS):
            ks, vs = _k_cols(kh), _v_cols(kh)
            k_small = _pad_keys(jnp.concatenate([cached(meta_ref, b, kh, N_META), kvn_ref[b, :, ks]], axis=0))
            v_small = _pad_keys(jnp.concatenate(
                [cached(meta_ref, b, SWA_KV_HEADS + kh, N_META), kvn_ref[b, :, vs]], axis=0))
            problems.append((_group_queries(lambda a, c, b=b: q_ref[b, :, a:c], kh),
                             [cached(win_ref, b, kh, WINDOW), k_small],
                             [cached(win_ref, b, SWA_KV_HEADS + kh, WINDOW), v_small],
                             [bwin_ref[kh], bsmall_ref[kh]], sink_ref[kh]))
    outs = _attend(problems)
    for head in range(SWA_HEADS):
        kh, g = divmod(head, SWA_GROUP)
        rows = [outs[b * SWA_KV_HEADS + kh][g * seq:(g + 1) * seq] for b in range(nb)]
        o_ref[:, head * HEAD_DIM:(head + 1) * HEAD_DIM] = jnp.concatenate(rows, axis=0).astype(o_ref.dtype)


def _outproj_kernel(g_ref, s_ref, h_ref, wo_ref, nw_ref, o_ref):
    mix = (jnp.dot(g_ref[...], wo_ref[0:GDN_WIDTH, :], preferred_element_type=jnp.float32)
           + jnp.dot(s_ref[...], wo_ref[GDN_WIDTH:, :], preferred_element_type=jnp.float32))
    o_ref[...] = h_ref[...] + mix * _rms_scale(mix) * nw_ref[...]


def _outproj(g, s, h, wo, nw, *, tm):
    rows = h.shape[0]
    return pl.pallas_call(
        _outproj_kernel,
        grid=(rows // tm,),
        in_specs=[
            pl.BlockSpec((tm, GDN_WIDTH), lambda i: (i, 0)),
            pl.BlockSpec((tm, SWA_WIDTH), lambda i: (i, 0)),
            pl.BlockSpec((tm, D_MODEL), lambda i: (i, 0)),
            pl.BlockSpec((D_MODEL, D_MODEL), lambda i: (0, 0)),
            pl.BlockSpec((1, D_MODEL), lambda i: (0, 0)),
        ],
        out_specs=pl.BlockSpec((tm, D_MODEL), lambda i: (i, 0)),
        out_shape=jax.ShapeDtypeStruct((rows, D_MODEL), jnp.float32),
        compiler_params=pltpu.CompilerParams(
            dimension_semantics=("arbitrary",), vmem_limit_bytes=VMEM_LIMIT),
        name="outproj",
    )(g, s, h, wo, nw)


def _ffn_kernel(*refs, batch, tm, tf):
    if batch:
        (h_ref, nw_pre_ref, wg_ref, wu_ref, cw_ref, wd_ref, nw_post_ref, hist_ref,
         y_ref, graw_ref, xn_ref, xe_ref) = refs
    else:
        (h_ref, nw_pre_ref, wg_ref, wu_ref, cw_ref, wd_ref, nw_post_ref, hist_ref,
         y_ref, graw_ref, xn_ref, xe_ref, carry_ref) = refs
    i = pl.program_id(0)
    j = pl.program_id(1)
    last_j = pl.num_programs(1) - 1
    rb = tm // FFN_ROW_BLOCKS

    if batch:
        xe_ref[:, 0:SUBLANES, :] = hist_ref[...]
    else:
        @pl.when(i == 0)
        def _():
            carry_ref[pl.ds(j, 1)] = hist_ref[...].reshape(1, SUBLANES, tf)
        xe_ref[:, 0:SUBLANES, :] = carry_ref[pl.ds(j, 1)]

    def step(first, last):
        def gate_up(r):
            rows = slice(r * rb, (r + 1) * rb)
            if first:
                h = h_ref[rows, :]
                xn_ref[rows, :] = (h * _rms_scale(h) * nw_pre_ref[...]).astype(jnp.bfloat16)
            xn = xn_ref[rows, :]
            return (jnp.dot(xn, wg_ref[...], preferred_element_type=jnp.float32),
                    jnp.dot(xn, wu_ref[...], preferred_element_type=jnp.float32))

        nxt = gate_up(0)
        for r in range(FFN_ROW_BLOCKS):
            rows = slice(r * rb, (r + 1) * rb)
            gate, up = nxt
            if r + 1 < FFN_ROW_BLOCKS:
                nxt = gate_up(r + 1)
            if batch:
                seqs = slice(r * rb // SUBLANES, (r + 1) * rb // SUBLANES)
                graw_ref[rows, :] = gate
                xe_ref[seqs, SUBLANES:2 * SUBLANES, :] = gate.reshape(rb // SUBLANES, SUBLANES, tf)
                taps = [xe_ref[seqs, SUBLANES - s:2 * SUBLANES - s, :] for s in range(FFN_CONV)]
            else:
                base = SUBLANES + r * rb
                xe_ref[:, base:base + rb, :] = gate.reshape(1, rb, tf)
                taps = [xe_ref[:, base - s:base - s + rb, :] for s in range(FFN_CONV)]
            conv = None
            for s, tap in enumerate(taps):
                term = tap * cw_ref[FFN_CONV - 1 - s:FFN_CONV - s, :]
                conv = term if conv is None else conv + term
            act = (_silu(conv.reshape(rb, tf)) * up).astype(jnp.bfloat16)
            down = jnp.dot(act, wd_ref[...], preferred_element_type=jnp.float32)
            if first:
                y_ref[rows, :] = down
            elif last:
                y = y_ref[rows, :] + down
                y_ref[rows, :] = h_ref[rows, :] + y * _rms_scale(y) * nw_post_ref[...]
            else:
                y_ref[rows, :] += down
        if not batch:
            tail = xe_ref[:, tm:tm + SUBLANES, :]
            carry_ref[pl.ds(j, 1)] = tail
            graw_ref[...] = tail.reshape(SUBLANES, tf)

    pl.when(j == 0)(functools.partial(step, True, False))
    pl.when((j > 0) & (j < last_j))(functools.partial(step, False, False))
    pl.when(j == last_j)(functools.partial(step, False, True))


def _ffn(h, nw_pre, wg, wu, cw, wd, nw_post, hist, *, batch, tm, tf):
    rows = h.shape[0]
    nj = D_FF // tf
    in_specs = [
        pl.BlockSpec((tm, D_MODEL), lambda i, j: (i, 0)),
        pl.BlockSpec((1, D_MODEL), lambda i, j: (0, 0)),
        pl.BlockSpec((D_MODEL, tf), lambda i, j: (0, j)),
        pl.BlockSpec((D_MODEL, tf), lambda i, j: (0, j)),
        pl.BlockSpec((FFN_CONV, tf), lambda i, j: (0, j)),
        pl.BlockSpec((tf, D_MODEL), lambda i, j: (j, 0)),
        pl.BlockSpec((1, D_MODEL), lambda i, j: (0, 0)),
    ]
    args = [h, nw_pre, wg, wu, cw, wd, nw_post, hist]
    scratch = [pltpu.VMEM((tm, D_MODEL), jnp.bfloat16)]
    if batch:
        in_specs.append(pl.BlockSpec((tm // SUBLANES, SUBLANES, tf), lambda i, j: (i, 0, j)))
        graw_spec = pl.BlockSpec((tm, tf), lambda i, j: (i, j))
        graw_shape = jax.ShapeDtypeStruct((rows, D_FF), jnp.float32)
        scratch.append(pltpu.VMEM((tm // SUBLANES, 2 * SUBLANES, tf), jnp.float32))
    else:
        in_specs.append(pl.BlockSpec((SUBLANES, tf), lambda i, j: (0, j)))
        graw_spec = pl.BlockSpec((SUBLANES, tf), lambda i, j: (i, j))
        graw_shape = jax.ShapeDtypeStruct((rows // tm * SUBLANES, D_FF), jnp.float32)
        scratch.append(pltpu.VMEM((1, SUBLANES + tm, tf), jnp.float32))
        scratch.append(pltpu.VMEM((nj, SUBLANES, tf), jnp.float32))
    return pl.pallas_call(
        functools.partial(_ffn_kernel, batch=batch, tm=tm, tf=tf),
        grid=(rows // tm, nj),
        in_specs=in_specs,
        out_specs=[pl.BlockSpec((tm, D_MODEL), lambda i, j: (i, 0)), graw_spec],
        out_shape=[jax.ShapeDtypeStruct((rows, D_MODEL), jnp.float32), graw_shape],
        scratch_shapes=scratch,
        compiler_params=pltpu.CompilerParams(
            dimension_semantics=("arbitrary", "arbitrary"), vmem_limit_bytes=VMEM_LIMIT),
        name="ffn_batch" if batch else "ffn_seq",
    )(*args)


def kernel(x_prompt, x_sample, cache_swa_meta_kv, cache_swa_window_kv, state_gdn_conv, state_gdn, state_ffn_conv, meta_tokens, rel_bias_table, w_in, gdn_conv_w, gdn_a_log, gdn_dt_bias, gdn_norm_w, swa_sinks, w_out, norm_mix_pre, norm_mix_post, norm_ffn_pre, norm_ffn_post, ffn_w_gate, ffn_w_up, ffn_conv_w, ffn_w_down):
    f32, bf16 = jnp.float32, jnp.bfloat16
    seq = x_prompt.shape[1]
    dec_b, dec_t = x_sample.shape[0], x_sample.shape[1]
    n_dec = dec_b * dec_t
    assert x_prompt.shape[0] == 1 and seq % CHUNK == 0 and dec_t == SUBLANES and n_dec % CHUNK == 0

    w_in_p, w_ba = _pack_w_in(jnp.transpose(w_in[0]), 2 * GDN_HEADS, tn=512)
    wo = w_out[0].astype(bf16)
    wg = ffn_w_gate[0].astype(bf16)
    wu = ffn_w_up[0].astype(bf16)
    wd = ffn_w_down[0].astype(bf16)
    lane_pad = lambda v: jnp.pad(v.reshape(1, GDN_HEADS), ((0, 0), (GDN_HEADS, LANES - 2 * GDN_HEADS)))
    alog_row = lane_pad(gdn_a_log[0])
    dtb_row = lane_pad(gdn_dt_bias[0])
    gnw = gdn_norm_w[0].reshape(1, HEAD_DIM)

    pad_rows = CHUNK - N_META
    n_small = n_dec + CHUNK
    x_big = x_prompt.reshape(seq, D_MODEL)
    x_small = jnp.concatenate(
        [x_sample.reshape(n_dec, D_MODEL), jnp.zeros((pad_rows, D_MODEL), f32), meta_tokens.astype(f32)], axis=0)
    nw = norm_mix_pre[0].reshape(1, D_MODEL)
    cw = gdn_conv_w[0]
    proj_small, ba_small = _inproj(x_small, nw, w_in_p, w_ba, tm=n_small, tn=512, row_chunk=128)
    proj_big, ba_big, qkv_tail = _inproj(
        x_big, nw, w_in_p, w_ba, (cw, proj_small[n_small - SUBLANES:, :GDN_QKV]),
        tm=1024, tn=512, row_chunk=128)

    small_chunks = proj_small.reshape(n_small // CHUNK, CHUNK, PROJ_COLS)
    small_groups = proj_small.reshape(n_small // SUBLANES, SUBLANES, PROJ_COLS)
    last_chunk = n_small // CHUNK - 1
    gdn_meta, s_meta = _gdn(
        small_chunks, ba_small.reshape(n_small // CHUNK, CHUNK, LANES), lambda s: (last_chunk, 0, 0),
        jnp.zeros((1, SUBLANES, GDN_QKV), f32), lambda s: (0, 0, 0),
        jnp.zeros((1, GDN_HEADS, HEAD_DIM, HEAD_DIM), f32), cw, alog_row, dtb_row, gnw,
        n_steps=1, nb=1, seq=CHUNK, group=CHUNK, carry=True, pad_rows=pad_rows, preconv=False)
    gdn_big, s_prompt = _gdn(
        proj_big.reshape(1, seq, PROJ_COLS), ba_big.reshape(1, seq, LANES), lambda s: (0, s, 0),
        jnp.zeros((1, SUBLANES, GDN_QKV), f32), lambda s: (0, 0, 0),
        s_meta, cw, alog_row, dtb_row, gnw,
        n_steps=seq // (GDN_SEQ_CHUNKS * CHUNK), nb=1, seq=GDN_SEQ_CHUNKS * CHUNK, group=CHUNK,
        carry=True, pad_rows=0, preconv=True)
    hist_gdn = jnp.pad(state_gdn_conv[0], ((0, 0), (SUBLANES - (GDN_CONV - 1), 0), (0, 0)))
    nb_gdn = CHUNK // dec_t
    gdn_small, s_sample = _gdn(
        small_groups, ba_small.reshape(n_small // SUBLANES, SUBLANES, LANES), lambda s: (s, 0, 0),
        hist_gdn, lambda s: (s, 0, 0),
        state_gdn, cw, alog_row, dtb_row, gnw,
        n_steps=dec_b // nb_gdn, nb=nb_gdn, seq=dec_t, group=dec_t, carry=False, pad_rows=0, preconv=False)

    qi = np.arange(WINDOW)[:, None]
    kj = np.arange(WINDOW)[None, :]
    mi = np.arange(N_META)[None, :]
    ti = np.arange(dec_t)[:, None]
    new_keys = kj - N_META
    id_arrays = [
        _bucket_ids(qi - kj, qi >= kj),
        _bucket_ids(qi - kj + WINDOW, kj > qi),
        _bucket_ids(qi + N_META - kj, kj < N_META),
        _bucket_ids(qi + N_META - kj + WINDOW, kj < N_META),
        _bucket_ids(ti + WINDOW - kj, kj > ti),
        _bucket_ids(np.where(new_keys < 0, PAST_LEN + ti - kj, ti - new_keys),
                    (new_keys < 0) | ((new_keys <= ti) & (new_keys < dec_t))),
        _bucket_ids(mi.T - mi, mi.T >= mi),
    ]
    bcur, bprev, bm0, bfar, bwin, bsmall, bmm = _bias_tables(rel_bias_table, id_arrays)
    sink_rows = lambda q: jnp.repeat(swa_sinks[0].reshape(SWA_KV_HEADS, SWA_GROUP), q, axis=1)[..., None]

    sq_blk = COL_SQ // SWA_WIDTH
    kv_blk = COL_KV // KV_WIDTH
    meta_blk = (n_small - N_META) // N_META
    full3 = lambda a: pl.BlockSpec(a.shape, lambda j: (0, 0, 0))
    sink_p = sink_rows(WINDOW)
    swa_rows = SWA_Q_BLOCKS * WINDOW
    swa_big = pl.pallas_call(
        _swa_prompt_kernel,
        grid=(seq // swa_rows,),
        in_specs=[
            pl.BlockSpec((swa_rows, SWA_WIDTH), lambda j: (j, sq_blk)),
            pl.BlockSpec((swa_rows, KV_WIDTH), lambda j: (j, kv_blk)),
            pl.BlockSpec((WINDOW, KV_WIDTH), lambda j: (jnp.maximum(j * SWA_Q_BLOCKS - 1, 0), kv_blk)),
            pl.BlockSpec((N_META, KV_WIDTH), lambda j: (meta_blk, kv_blk)),
            full3(bcur), full3(bprev), full3(bm0), full3(bfar), full3(sink_p),
        ],
        out_specs=pl.BlockSpec((swa_rows, SWA_WIDTH), lambda j: (j, 0)),
        out_shape=jax.ShapeDtypeStruct((seq, SWA_WIDTH), bf16),
        compiler_params=pltpu.CompilerParams(
            dimension_semantics=("arbitrary",), vmem_limit_bytes=VMEM_LIMIT),
        name="swa_prompt",
    )(proj_big, proj_big, proj_big, proj_small, bcur, bprev, bm0, bfar, sink_p)

    nb_swa = 8
    sink_s = sink_rows(dec_t)
    win = cache_swa_window_kv.reshape(dec_b, WINDOW * KV_SLOTS, HEAD_DIM)
    meta_kv = cache_swa_meta_kv.reshape(dec_b, N_META * KV_SLOTS, HEAD_DIM)
    swa_small, win_new = pl.pallas_call(
        functools.partial(_swa_sample_kernel, nb=nb_swa, seq=dec_t),
        grid=(dec_b // nb_swa,),
        in_specs=[
            pl.BlockSpec((nb_swa, dec_t, SWA_WIDTH), lambda j: (j, 0, sq_blk)),
            pl.BlockSpec((nb_swa, dec_t, KV_WIDTH), lambda j: (j, 0, kv_blk)),
            pl.BlockSpec((nb_swa, WINDOW * KV_SLOTS, HEAD_DIM), lambda j: (j, 0, 0)),
            pl.BlockSpec((nb_swa, N_META * KV_SLOTS, HEAD_DIM), lambda j: (j, 0, 0)),
            full3(bwin), full3(bsmall), full3(sink_s),
        ],
        out_specs=[pl.BlockSpec((nb_swa * dec_t, SWA_WIDTH), lambda j: (j, 0)),
                   pl.BlockSpec((nb_swa, WINDOW * KV_SLOTS, HEAD_DIM), lambda j: (j, 0, 0))],
        out_shape=[jax.ShapeDtypeStruct((n_dec, SWA_WIDTH), bf16),
                   jax.ShapeDtypeStruct((dec_b, WINDOW * KV_SLOTS, HEAD_DIM), f32)],
        compiler_params=pltpu.CompilerParams(
            dimension_semantics=("arbitrary",), vmem_limit_bytes=VMEM_LIMIT),
        name="swa_sample",
    )(small_groups, small_groups, win, meta_kv, bwin, bsmall, sink_s)

    sink_m = sink_rows(N_META)
    swa_meta = pl.pallas_call(
        _swa_meta_kernel,
        grid=(1,),
        in_specs=[
            pl.BlockSpec((N_META, SWA_WIDTH), lambda j: (meta_blk, sq_blk)),
            pl.BlockSpec((N_META, KV_WIDTH), lambda j: (meta_blk, kv_blk)),
            full3(bmm), full3(sink_m),
        ],
        out_specs=pl.BlockSpec((N_META, SWA_WIDTH), lambda j: (0, 0)),
        out_shape=jax.ShapeDtypeStruct((N_META, SWA_WIDTH), bf16),
        name="swa_meta",
    )(proj_small, proj_small, bmm, sink_m)

    nw_post = norm_mix_post[0].reshape(1, D_MODEL)
    nf_pre = norm_ffn_pre[0].reshape(1, D_MODEL)
    nf_post = norm_ffn_post[0].reshape(1, D_MODEL)
    fcw = ffn_conv_w[0]
    gdn_small_all = jnp.concatenate([gdn_small, gdn_meta], axis=0)
    swa_small_all = jnp.concatenate([swa_small, jnp.zeros((pad_rows, SWA_WIDTH), bf16), swa_meta], axis=0)
    h_small = _outproj(gdn_small_all, swa_small_all, x_small, wo, nw_post, tm=n_small // 2)
    hist_ffn = jnp.pad(state_ffn_conv[0], ((0, CHUNK // SUBLANES), (SUBLANES - (FFN_CONV - 1), 0), (0, 0)))
    y_small, g_small = _ffn(h_small, nf_pre, wg, wu, fcw, wd, nf_post, hist_ffn,
                            batch=True, tm=n_small // 2, tf=512)
    h_big = _outproj(gdn_big, swa_big, x_big, wo, nw_post, tm=512)
    y_big, g_tail = _ffn(h_big, nf_pre, wg, wu, fcw, wd, nf_post, g_small[n_small - SUBLANES:],
                         batch=False, tm=1024, tf=512)

    kv_shape = lambda n: (1, n, 2, SWA_KV_HEADS, HEAD_DIM)
    kv_small = proj_small[:, COL_KV:COL_KV + KV_WIDTH]
    y_prompt = y_big.reshape(1, seq, D_MODEL)
    y_sample = y_small[:n_dec].reshape(dec_b, dec_t, D_MODEL)
    p_meta_kv = kv_small[n_small - N_META:].reshape(kv_shape(N_META))[None]
    p_window_kv = proj_big[seq - WINDOW:, COL_KV:COL_KV + KV_WIDTH].reshape(kv_shape(WINDOW))[None]
    p_gdn_conv = qkv_tail[qkv_tail.shape[0] - (GDN_CONV - 1):].reshape(1, 1, GDN_CONV - 1, GDN_QKV)
    p_gdn_state = s_prompt[None]
    p_ffn_conv = g_tail[g_tail.shape[0] - (FFN_CONV - 1):].reshape(1, 1, FFN_CONV - 1, D_FF)
    s_window_kv = win_new.reshape(1, dec_b, WINDOW, 2, SWA_KV_HEADS, HEAD_DIM)
    s_gdn_conv = proj_small[:n_dec, :GDN_QKV].reshape(dec_b, dec_t, GDN_QKV)[:, dec_t - (GDN_CONV - 1):][None]
    s_gdn_state = s_sample
    s_ffn_conv = g_small[:n_dec].reshape(dec_b, dec_t, D_FF)[:, dec_t - (FFN_CONV - 1):][None]
    return (y_prompt, y_sample, p_meta_kv, p_window_kv, p_gdn_conv, p_gdn_state, p_ffn_conv,
            s_window_kv, s_gdn_conv, s_gdn_state, s_ffn_conv)
```

```python
import functools
import math

import numpy as np
import jax
import jax.numpy as jnp
from jax import lax
from jax.experimental import pallas as pl
from jax.experimental.pallas import tpu as pltpu

D_MODEL = 2048
HEAD_DIM = 128
GDN_HEADS = 8
GDN_WIDTH = GDN_HEADS * HEAD_DIM
GDN_QKV = 3 * GDN_WIDTH
SWA_HEADS = 8
SWA_KV_HEADS = 2
SWA_GROUP = SWA_HEADS // SWA_KV_HEADS
SWA_WIDTH = SWA_HEADS * HEAD_DIM
KV_SLOTS = 2 * SWA_KV_HEADS
KV_WIDTH = KV_SLOTS * HEAD_DIM
WINDOW = 128
N_META = 16
N_BUCKETS = 32
MAX_DISTANCE = 128
GDN_CONV = 4
FFN_CONV = 3
D_FF = 5632
EPS = 1e-6
PAST_LEN = 16384

SUBLANES = 8
LANES = 128

CHUNK = 128
assert CHUNK == HEAD_DIM == LANES
INV_BASE = 16
GDN_SEQ_CHUNKS = 2
INPROJ_ROW_BLOCKS = 8
FFN_ROW_BLOCKS = 4
COL_Z = GDN_QKV
COL_SQ = COL_Z + GDN_WIDTH
COL_KV = COL_SQ + SWA_WIDTH
PROJ_COLS = COL_KV + KV_WIDTH
GDN_COLS = COL_SQ

SWA_Q_BLOCKS = 4
NEG = -1e30
VMEM_LIMIT = 56 * 1024 * 1024

_NT = (((1,), (1,)), ((), ()))


def _dot(a, b):
    return jnp.dot(a.astype(jnp.bfloat16), b.astype(jnp.bfloat16), preferred_element_type=jnp.float32)


def _dot_nt(a, b):
    return lax.dot_general(a.astype(jnp.bfloat16), b.astype(jnp.bfloat16), _NT,
                           preferred_element_type=jnp.float32)


_dot_inv = _dot


def _dot_exact(a, b, dims=None):
    if dims is None:
        return jnp.dot(a, b, precision=lax.Precision.HIGHEST, preferred_element_type=jnp.float32)
    return lax.dot_general(a, b, dims, precision=lax.Precision.HIGHEST,
                           preferred_element_type=jnp.float32)


def _pack_w_in_kernel(a_ref, b_ref, o_ref, ba_ref, *, first_shifted, shift):
    j = pl.program_id(0)

    @pl.when(j < first_shifted)
    def _():
        o_ref[...] = a_ref[...].T.astype(o_ref.dtype)

    @pl.when(j >= first_shifted)
    def _():
        rows = jnp.concatenate([a_ref[shift:, :], b_ref[:shift, :]], axis=0)
        o_ref[...] = rows.T.astype(o_ref.dtype)

    @pl.when(j == first_shifted)
    def _():
        head = a_ref[:LANES, :]
        row = lax.broadcasted_iota(jnp.int32, head.shape, 0)
        ba_ref[...] = jnp.where(row < shift, head, 0.0).T.astype(ba_ref.dtype)


def _pack_w_in(w_in_t, n_ba, *, tn):
    n_blocks = PROJ_COLS // tn
    first_shifted = COL_SQ // tn
    assert COL_SQ % tn == 0 and w_in_t.shape[0] == PROJ_COLS + n_ba and n_ba % SUBLANES == 0
    return pl.pallas_call(
        functools.partial(_pack_w_in_kernel, first_shifted=first_shifted, shift=n_ba),
        grid=(n_blocks,),
        in_specs=[pl.BlockSpec((tn, D_MODEL), lambda j: (j, 0)),
                  pl.BlockSpec((tn, D_MODEL), lambda j: (jnp.maximum(j, first_shifted) + 1, 0))],
        out_specs=[pl.BlockSpec((D_MODEL, tn), lambda j: (0, j)),
                   pl.BlockSpec((D_MODEL, LANES), lambda j: (0, 0))],
        out_shape=[jax.ShapeDtypeStruct((D_MODEL, PROJ_COLS), jnp.bfloat16),
                   jax.ShapeDtypeStruct((D_MODEL, LANES), jnp.bfloat16)],
        compiler_params=pltpu.CompilerParams(
            dimension_semantics=("arbitrary",), vmem_limit_bytes=VMEM_LIMIT),
        name="pack_w_in",
    )(w_in_t, w_in_t)


def _rms_scale(x):
    return lax.rsqrt(jnp.mean(x * x, axis=-1, keepdims=True) + EPS)


def _silu(x):
    return x * jax.nn.sigmoid(x)


def _inproj_kernel(*refs, row_chunk, conv_tiles):
    if conv_tiles:
        (x_ref, nw_ref, w_ref, wba_ref, cw_ref, hist_ref,
         o_ref, ba_ref, tail_ref, xn_ref, xe_ref, carry_ref) = refs
    else:
        x_ref, nw_ref, w_ref, wba_ref, o_ref, ba_ref, xn_ref = refs
    i = pl.program_id(0)
    j = pl.program_id(1)
    tm, tn = o_ref.shape

    def normalize(rows):
        x = x_ref[rows, :]
        xn_ref[rows, :] = (x * _rms_scale(x) * nw_ref[...]).astype(jnp.bfloat16)

    def plain():
        o_ref[...] = jnp.dot(xn_ref[...], w_ref[...], preferred_element_type=jnp.float32)

    if not conv_tiles:
        @pl.when(j == 0)
        def _():
            def body(c, carry):
                normalize(pl.ds(pl.multiple_of(c * row_chunk, row_chunk), row_chunk))
                return carry
            lax.fori_loop(0, tm // row_chunk, body, 0)
            ba_ref[...] = jnp.dot(xn_ref[...], wba_ref[...], preferred_element_type=jnp.float32)
        plain()
        return
    pl.when(j >= conv_tiles)(plain)

    def conv_tile(first):
        @pl.when(i == 0)
        def _():
            carry_ref[pl.ds(j, 1)] = hist_ref[...].reshape(1, SUBLANES, tn)
        xe_ref[0:SUBLANES, :] = carry_ref[pl.ds(j, 1)].reshape(SUBLANES, tn)
        rb = tm // INPROJ_ROW_BLOCKS

        def raw_block(r):
            rows = slice(r * rb, (r + 1) * rb)
            if first:
                normalize(rows)
                ba_ref[rows, :] = jnp.dot(xn_ref[rows, :], wba_ref[...], preferred_element_type=jnp.float32)
            return jnp.dot(xn_ref[rows, :], w_ref[...], preferred_element_type=jnp.float32)

        nxt = raw_block(0)
        for r in range(INPROJ_ROW_BLOCKS):
            raw = nxt
            if r + 1 < INPROJ_ROW_BLOCKS:
                nxt = raw_block(r + 1)
            base = SUBLANES + r * rb
            xe_ref[base:base + rb, :] = raw
            conv = None
            for s in range(GDN_CONV):
                term = xe_ref[base - s:base - s + rb, :] * cw_ref[GDN_CONV - 1 - s:GDN_CONV - s, :]
                conv = term if conv is None else conv + term
            o_ref[r * rb:(r + 1) * rb, :] = _silu(conv)
        tail = xe_ref[tm:tm + SUBLANES, :]
        carry_ref[pl.ds(j, 1)] = tail.reshape(1, SUBLANES, tn)
        tail_ref[...] = tail

    pl.when(j == 0)(functools.partial(conv_tile, True))
    pl.when((j > 0) & (j < conv_tiles))(functools.partial(conv_tile, False))


def _inproj(x, nw, w, wba, conv=None, *, tm, tn, row_chunk):
    rows = x.shape[0]
    conv_tiles = GDN_QKV // tn if conv else 0
    in_specs = [
        pl.BlockSpec((tm, D_MODEL), lambda i, j: (i, 0)),
        pl.BlockSpec((1, D_MODEL), lambda i, j: (0, 0)),
        pl.BlockSpec((D_MODEL, tn), lambda i, j: (0, j)),
        pl.BlockSpec((D_MODEL, LANES), lambda i, j: (0, 0)),
    ]
    out_specs = [pl.BlockSpec((tm, tn), lambda i, j: (i, j)),
                 pl.BlockSpec((tm, LANES), lambda i, j: (i, 0))]
    out_shape = [jax.ShapeDtypeStruct((rows, PROJ_COLS), jnp.float32),
                 jax.ShapeDtypeStruct((rows, LANES), jnp.float32)]
    scratch = [pltpu.VMEM((tm, D_MODEL), jnp.bfloat16)]
    args = [x, nw, w, wba]
    if conv:
        conv_col = lambda i, j: (0, jnp.minimum(j, conv_tiles - 1))
        in_specs += [pl.BlockSpec((GDN_CONV, tn), conv_col), pl.BlockSpec((SUBLANES, tn), conv_col)]
        out_specs.append(pl.BlockSpec((SUBLANES, tn), lambda i, j: (i, jnp.minimum(j, conv_tiles - 1))))
        out_shape.append(jax.ShapeDtypeStruct((rows // tm * SUBLANES, GDN_QKV), jnp.float32))
        scratch += [pltpu.VMEM((SUBLANES + tm, tn), jnp.float32),
                    pltpu.VMEM((conv_tiles, SUBLANES, tn), jnp.float32)]
        args += list(conv)
    return pl.pallas_call(
        functools.partial(_inproj_kernel, row_chunk=row_chunk, conv_tiles=conv_tiles),
        grid=(rows // tm, PROJ_COLS // tn),
        in_specs=in_specs,
        out_specs=out_specs,
        out_shape=out_shape,
        scratch_shapes=scratch,
        compiler_params=pltpu.CompilerParams(
            dimension_semantics=("arbitrary", "arbitrary"), vmem_limit_bytes=VMEM_LIMIT),
        name="inproj_conv" if conv else "inproj",
    )(*args)


def _tri_inverse(lms, ri, ci):
    shift = INV_BASE.bit_length() - 1
    eye = (ri == ci).astype(jnp.float32)
    in_block = (ri >> shift) == (ci >> shift)
    ps = [jnp.where(in_block, lm, 0.0) for lm in lms]
    ts = [eye - p for p in ps]
    for _ in range(shift - 1):
        ps = [_dot_inv(p, p) for p in ps]
        ts = [t + _dot_inv(t, p) for t, p in zip(ts, ps)]
    size = INV_BASE
    while size < CHUNK:
        shift += 1
        in_pair = (ri >> shift) == (ci >> shift)
        off_mask = in_pair & jnp.logical_not(in_block)
        tos = [_dot_inv(t, jnp.where(off_mask, lm, 0.0)) for t, lm in zip(ts, lms)]
        ts = [t - _dot_inv(to, t) for t, to in zip(ts, tos)]
        in_block = in_pair
        size *= 2
    return ts


def _gdn_kernel(x_ref, ba_ref, hist_ref, s0_ref, cw_ref, alog_ref, dtb_ref, gnw_ref,
                o_ref, sout_ref, xe_ref, s_ref, *, nb, seq, group, carry, pad_rows, preconv):
    step = pl.program_id(0)
    rows = nb * seq
    n_chunks = rows // CHUNK
    n_groups = CHUNK // group
    gshift = group.bit_length() - 1

    if carry:
        @pl.when(step == 0)
        def _():
            s_ref[...] = s0_ref[0]
    if not preconv:
        if carry:
            @pl.when(step == 0)
            def _():
                xe_ref[:, 0:SUBLANES, :] = hist_ref[...]
        else:
            xe_ref[:, 0:SUBLANES, :] = hist_ref[...]
        xe_ref[:, SUBLANES:SUBLANES + seq, :] = x_ref[:, :, 0:GDN_QKV]

    seq_rows = min(seq, CHUNK)
    seqs_per_chunk = CHUNK // seq_rows

    def chunk_rows(ref, c, row_off, cols):
        if seq >= CHUNK:
            start = row_off + c * CHUNK
            return ref[0:1, start:start + CHUNK, cols]
        b0 = c * seqs_per_chunk
        return ref[b0:b0 + seqs_per_chunk, row_off:row_off + seq, cols]

    def conv_chunk(col, c):
        cols = slice(col, col + HEAD_DIM)
        if preconv:
            return chunk_rows(x_ref, c, 0, cols).reshape(CHUNK, HEAD_DIM)
        acc = None
        for s in range(GDN_CONV):
            term = chunk_rows(xe_ref, c, SUBLANES - s, cols) * cw_ref[GDN_CONV - 1 - s:GDN_CONV - s, cols]
            acc = term if acc is None else acc + term
        return _silu(acc).reshape(CHUNK, HEAD_DIM)

    ri = lax.broadcasted_iota(jnp.int32, (CHUNK, CHUNK), 0)
    ci = lax.broadcasted_iota(jnp.int32, (CHUNK, CHUNK), 1)
    same = (ri >> gshift) == (ci >> gshift)
    m_incl = same & (ri >= ci)
    m_strict = same & (ri > ci)
    f_incl = m_incl.astype(jnp.float32)
    f_same = same.astype(jnp.float32)
    lane = lax.broadcasted_iota(jnp.int32, (CHUNK, LANES), 1)
    row_in_chunk = lax.broadcasted_iota(jnp.int32, (CHUNK, LANES), 0)

    pre = []
    for c in range(n_chunks):
        bac = chunk_rows(ba_ref, c, 0, slice(0, LANES)).reshape(CHUNK, LANES)
        beta_all = jax.nn.sigmoid(bac)
        sp_in = bac + dtb_ref[...]
        softplus = jnp.maximum(sp_in, 0.0) + jnp.log1p(jnp.exp(-jnp.abs(sp_in)))
        g_all = -jnp.exp(alog_ref[...]) * softplus
        if pad_rows and c == 0:
            valid = row_in_chunk >= pad_rows
            beta_all = jnp.where(valid, beta_all, 0.0)
            g_all = jnp.where(valid, g_all, 0.0)
        g_all = jnp.where((lane >= GDN_HEADS) & (lane < 2 * GDN_HEADS), g_all, 0.0)
        gc_col = _dot_exact(f_incl, g_all)
        if n_groups == 1:
            gtot_col = jnp.broadcast_to(gc_col[CHUNK - 1:CHUNK, :], (CHUNK, LANES))
        else:
            gtot_col = _dot_exact(f_same, g_all)
        gc_row = gc_col.T

        for h in range(GDN_HEADS):
            qh = conv_chunk(h * HEAD_DIM, c)
            kh = conv_chunk(GDN_WIDTH + h * HEAD_DIM, c)
            vh = conv_chunk(2 * GDN_WIDTH + h * HEAD_DIM, c)
            qh = qh * lax.rsqrt(jnp.sum(qh * qh, -1, keepdims=True) + EPS) * (HEAD_DIM ** -0.5)
            kh = kh * lax.rsqrt(jnp.sum(kh * kh, -1, keepdims=True) + EPS)
            bcast = lambda col: jnp.broadcast_to(col, (CHUNK, HEAD_DIM))
            gcc = bcast(gc_col[:, GDN_HEADS + h:GDN_HEADS + h + 1])
            gtc = bcast(gtot_col[:, GDN_HEADS + h:GDN_HEADS + h + 1])
            beta = bcast(beta_all[:, h:h + 1])
            gcr = gc_row[GDN_HEADS + h:GDN_HEADS + h + 1, :]
            decay = jnp.exp(jnp.where(m_incl, gcc - gcr, NEG))
            kb = kh * beta
            egc = jnp.exp(gcc)
            pre.append(dict(
                c=c, h=h,
                lm=jnp.where(m_strict, _dot_nt(kb, kh) * decay, 0.0),
                qk=_dot_nt(qh, kh) * decay,
                rhs=jnp.concatenate([vh * beta, kb * egc], axis=1),
                qg=qh * egc,
                kd_t=(kh * jnp.exp(gtc - gcc)).T,
                gl=jnp.exp(gtc)))

    inverses = _tri_inverse([p["lm"] for p in pre], ri, ci)
    sols = [_dot_inv(t, p["rhs"]) for t, p in zip(inverses, pre)]

    for c in range(n_chunks):
        r0 = c * CHUNK
        items = [(p, sol) for p, sol in zip(pre, sols) if p["c"] == c]
        state = lambda h, b: s_ref[h] if carry else s0_ref[c * n_groups + b, h]
        ws, qs = [], []
        for p, sol in items:
            w = sol[:, HEAD_DIM:]
            ws_parts, qs_parts = [], []
            for b in range(n_groups):
                g0 = b * group
                wq = jnp.concatenate([w[g0:g0 + group], p["qg"][g0:g0 + group]], axis=0)
                res = _dot(wq, state(p["h"], b))
                ws_parts.append(res[:group])
                qs_parts.append(res[group:])
            ws.append(ws_parts[0] if n_groups == 1 else jnp.concatenate(ws_parts, axis=0))
            qs.append(qs_parts[0] if n_groups == 1 else jnp.concatenate(qs_parts, axis=0))
        v_new = [sol[:, :HEAD_DIM] - w for (p, sol), w in zip(items, ws)]
        o = [a + _dot(p["qk"], vn) for a, (p, sol), vn in zip(qs, items, v_new)]
        for (p, sol), vn in zip(items, v_new):
            h = p["h"]
            for b in range(n_groups):
                g0 = b * group
                kd_b = p["kd_t"] if n_groups == 1 else jnp.where((ci >> gshift) == b, p["kd_t"], 0.0)
                st = state(h, b) * p["gl"][g0:g0 + 1, :] + _dot(kd_b, vn)
                if carry:
                    s_ref[h] = st
                else:
                    sout_ref[c * n_groups + b, h] = st
        for (p, sol), oh in zip(items, o):
            h = p["h"]
            z = chunk_rows(x_ref, c, 0, slice(COL_Z + h * HEAD_DIM, COL_Z + (h + 1) * HEAD_DIM)).reshape(CHUNK, HEAD_DIM)
            y = oh * lax.rsqrt(jnp.mean(oh * oh, -1, keepdims=True) + EPS) * gnw_ref[...] * _silu(z)
            o_ref[r0:r0 + CHUNK, h * HEAD_DIM:(h + 1) * HEAD_DIM] = y.astype(o_ref.dtype)

    if carry:
        if not preconv:
            xe_ref[:, 0:SUBLANES, :] = xe_ref[:, seq:seq + SUBLANES, :]

        @pl.when(step == pl.num_programs(0) - 1)
        def _():
            sout_ref[0] = s_ref[...]


def _gdn(x3, ba3, x_idx, hist, hist_idx, s0, cw, alog_row, dtb_row, gnw, *,
         n_steps, nb, seq, group, carry, pad_rows, preconv):
    rows = nb * seq
    if carry:
        state_spec = pl.BlockSpec((1, GDN_HEADS, HEAD_DIM, HEAD_DIM), lambda s: (0, 0, 0, 0))
        state_shape = (1, GDN_HEADS, HEAD_DIM, HEAD_DIM)
    else:
        n_states = rows // group
        state_spec = pl.BlockSpec((None, n_states, GDN_HEADS, HEAD_DIM, HEAD_DIM), lambda s: (0, s, 0, 0, 0))
        state_shape = (1, n_steps * n_states, GDN_HEADS, HEAD_DIM, HEAD_DIM)
    full = lambda shape: pl.BlockSpec(shape, lambda s: (0,) * len(shape))
    return pl.pallas_call(
        functools.partial(_gdn_kernel, nb=nb, seq=seq, group=group, carry=carry, pad_rows=pad_rows,
                          preconv=preconv),
        grid=(n_steps,),
        in_specs=[
            pl.BlockSpec((nb, seq, GDN_COLS), x_idx),
            pl.BlockSpec((nb, seq, LANES), x_idx),
            pl.BlockSpec((nb, SUBLANES, GDN_QKV), hist_idx),
            state_spec,
            full((GDN_CONV, GDN_QKV)),
            full((1, LANES)),
            full((1, LANES)),
            full((1, HEAD_DIM)),
        ],
        out_specs=[
            pl.BlockSpec((rows, GDN_WIDTH), lambda s: (s, 0)),
            state_spec,
        ],
        out_shape=[
            jax.ShapeDtypeStruct((n_steps * rows, GDN_WIDTH), jnp.bfloat16),
            jax.ShapeDtypeStruct(state_shape, jnp.float32),
        ],
        scratch_shapes=[
            pltpu.VMEM((nb, SUBLANES + (SUBLANES if preconv else seq), GDN_QKV), jnp.float32),
            pltpu.VMEM((GDN_HEADS, HEAD_DIM, HEAD_DIM), jnp.float32),
        ],
        compiler_params=pltpu.CompilerParams(
            dimension_semantics=("arbitrary",), vmem_limit_bytes=VMEM_LIMIT),
        name="gdn_seq" if carry else "gdn_batch",
    )(x3, ba3, hist, s0, cw, alog_row, dtb_row, gnw)


def _t5_bucket_np(dist):
    n = np.maximum(dist, 0)
    exact = N_BUCKETS // 2
    large = exact + (np.log(np.maximum(n, 1).astype(np.float32) / exact)
                     / math.log(MAX_DISTANCE / exact) * (N_BUCKETS - exact)).astype(np.int32)
    return np.where(n < exact, n, np.minimum(large, N_BUCKETS - 1)).astype(np.int32)


def _bucket_ids(dist, valid):
    return np.where(valid, _t5_bucket_np(dist), -1).astype(np.int32)


def _bias_kernel(table_ref, *refs):
    n = len(refs) // 2
    for ids_ref, out_ref in zip(refs[:n], refs[n:]):
        ids = ids_ref[...]
        nq = ids.shape[0]
        for head in range(SWA_HEADS):
            def body(b, acc):
                return jnp.where(ids == b, table_ref[b, head], acc)
            acc = lax.fori_loop(0, N_BUCKETS, body, jnp.full(ids.shape, NEG, jnp.float32))
            kh, g = divmod(head, SWA_GROUP)
            out_ref[kh, g * nq:(g + 1) * nq, :] = acc


def _bias_tables(rel_table, id_arrays):
    out_shapes = [jax.ShapeDtypeStruct((SWA_KV_HEADS, SWA_GROUP * a.shape[0], a.shape[1]), jnp.float32)
                  for a in id_arrays]
    vmem = pl.BlockSpec(memory_space=pltpu.VMEM)
    return pl.pallas_call(
        _bias_kernel,
        in_specs=[pl.BlockSpec(memory_space=pltpu.SMEM)] + [vmem] * len(id_arrays),
        out_specs=[vmem] * len(id_arrays),
        out_shape=out_shapes,
        name="swa_bias",
    )(rel_table, *[jnp.asarray(a) for a in id_arrays])


def _attend(problems):
    scale = HEAD_DIM ** -0.5
    scores = [[_dot_nt(q, k) * scale + b for k, b in zip(keys, biases)]
              for q, keys, _, biases, _ in problems]
    maxes = []
    for (_, _, _, _, sink), segs in zip(problems, scores):
        m = sink
        for s in segs:
            m = jnp.maximum(m, jnp.max(s, axis=-1, keepdims=True))
        maxes.append(m)
    probs = [[jnp.exp(s - m) for s in segs] for segs, m in zip(scores, maxes)]
    outs = []
    for (_, _, values, _, sink), ps, m in zip(problems, probs, maxes):
        acc = None
        for p, v in zip(ps, values):
            v_ones = jnp.concatenate([v, jnp.ones((v.shape[0], HEAD_DIM), v.dtype)], axis=1)
            pv = _dot(p, v_ones)
            acc = pv if acc is None else acc + pv
        den = acc[:, HEAD_DIM:] + jnp.exp(sink - m)
        outs.append(acc[:, :HEAD_DIM] / den)
    return outs


def _group_queries(q_rows, kh):
    return jnp.concatenate(
        [q_rows((kh * SWA_GROUP + g) * HEAD_DIM, (kh * SWA_GROUP + g + 1) * HEAD_DIM)
         for g in range(SWA_GROUP)], axis=0)


def _pad_keys(rows):
    return jnp.concatenate([rows, jnp.zeros((WINDOW - rows.shape[0], rows.shape[1]), rows.dtype)], axis=0)


def _k_cols(kh):
    return slice(kh * HEAD_DIM, (kh + 1) * HEAD_DIM)


def _v_cols(kh):
    return slice((SWA_KV_HEADS + kh) * HEAD_DIM, (SWA_KV_HEADS + kh + 1) * HEAD_DIM)


def _swa_prompt_kernel(q_ref, kvc_ref, kvp_ref, kvm_ref, bcur_ref, bprev_ref, bm0_ref, bfar_ref,
                       sink_ref, o_ref):
    first = pl.program_id(0) == 0
    problems = []
    for blk in range(SWA_Q_BLOCKS):
        rows = slice(blk * WINDOW, (blk + 1) * WINDOW)
        prev_ref, prev_rows = (kvp_ref, slice(0, WINDOW)) if blk == 0 else (
            kvc_ref, slice((blk - 1) * WINDOW, blk * WINDOW))
        for kh in range(SWA_KV_HEADS):
            ks, vs = _k_cols(kh), _v_cols(kh)
            first_block = first if blk == 0 else False
            b_prev = jnp.where(first_block, NEG, bprev_ref[kh])
            b_meta = jnp.where(first_block, bm0_ref[kh], bfar_ref[kh])
            problems.append((_group_queries(lambda a, b, rows=rows: q_ref[rows, a:b], kh),
                             [kvc_ref[rows, ks], prev_ref[prev_rows, ks], _pad_keys(kvm_ref[:, ks])],
                             [kvc_ref[rows, vs], prev_ref[prev_rows, vs], _pad_keys(kvm_ref[:, vs])],
                             [bcur_ref[kh], b_prev, b_meta], sink_ref[kh]))
    for i, o in enumerate(_attend(problems)):
        blk, kh = divmod(i, SWA_KV_HEADS)
        for g in range(SWA_GROUP):
            head = kh * SWA_GROUP + g
            o_ref[blk * WINDOW:(blk + 1) * WINDOW, head * HEAD_DIM:(head + 1) * HEAD_DIM] = (
                o[g * WINDOW:(g + 1) * WINDOW].astype(o_ref.dtype))


def _swa_meta_kernel(q_ref, kv_ref, bias_ref, sink_ref, o_ref):
    problems = [(_group_queries(lambda a, b: q_ref[:, a:b], kh), [kv_ref[:, _k_cols(kh)]],
                 [kv_ref[:, _v_cols(kh)]], [bias_ref[kh]], sink_ref[kh])
                for kh in range(SWA_KV_HEADS)]
    for kh, o in enumerate(_attend(problems)):
        for g in range(SWA_GROUP):
            head = kh * SWA_GROUP + g
            o_ref[:, head * HEAD_DIM:(head + 1) * HEAD_DIM] = (
                o[g * N_META:(g + 1) * N_META].astype(o_ref.dtype))


def _swa_sample_kernel(q_ref, kvn_ref, win_ref, meta_ref, bwin_ref, bsmall_ref, sink_ref,
                       o_ref, wout_ref, *, nb, seq):
    cached = lambda ref, b, slot, n: ref[b, pl.ds(slot, n, stride=KV_SLOTS), :]
    keep = (WINDOW - seq) * KV_SLOTS
    wout_ref[:, 0:keep, :] = win_ref[:, seq * KV_SLOTS:WINDOW * KV_SLOTS, :]
    for slot in range(KV_SLOTS):
        wout_ref[:, pl.ds(keep + slot, seq, stride=KV_SLOTS), :] = (
            kvn_ref[:, :, slot * HEAD_DIM:(slot + 1) * HEAD_DIM])
    problems = []
    for b in range(nb):
        for kh in range(SWA_KV_HEADS):
            ks, vs = _k_cols(kh), _v_cols(kh)
            k_small = _pad_keys(jnp.concatenate([cached(meta_ref, b, kh, N_META), kvn_ref[b, :, ks]], axis=0))
            v_small = _pad_keys(jnp.concatenate(
                [cached(meta_ref, b, SWA_KV_HEADS + kh, N_META), kvn_ref[b, :, vs]], axis=0))
            problems.append((_group_queries(lambda a, c, b=b: q_ref[b, :, a:c], kh),
                             [cached(win_ref, b, kh, WINDOW), k_small],
                             [cached(win_ref, b, SWA_KV_HEADS + kh, WINDOW), v_small],
                             [bwin_ref[kh], bsmall_ref[kh]], sink_ref[kh]))
    outs = _attend(problems)
    for head in range(SWA_HEADS):
        kh, g = divmod(head, SWA_GROUP)
        rows = [outs[b * SWA_KV_HEADS + kh][g * seq:(g + 1) * seq] for b in range(nb)]
        o_ref[:, head * HEAD_DIM:(head + 1) * HEAD_DIM] = jnp.concatenate(rows, axis=0).astype(o_ref.dtype)


def _outproj_kernel(g_ref, s_ref, h_ref, wo_ref, nw_ref, o_ref):
    mix = (jnp.dot(g_ref[...], wo_ref[0:GDN_WIDTH, :], preferred_element_type=jnp.float32)
           + jnp.dot(s_ref[...], wo_ref[GDN_WIDTH:, :], preferred_element_type=jnp.float32))
    o_ref[...] = h_ref[...] + mix * _rms_scale(mix) * nw_ref[...]


def _outproj(g, s, h, wo, nw, *, tm):
    rows = h.shape[0]
    return pl.pallas_call(
        _outproj_kernel,
        grid=(rows // tm,),
        in_specs=[
            pl.BlockSpec((tm, GDN_WIDTH), lambda i: (i, 0)),
            pl.BlockSpec((tm, SWA_WIDTH), lambda i: (i, 0)),
            pl.BlockSpec((tm, D_MODEL), lambda i: (i, 0)),
            pl.BlockSpec((D_MODEL, D_MODEL), lambda i: (0, 0)),
            pl.BlockSpec((1, D_MODEL), lambda i: (0, 0)),
        ],
        out_specs=pl.BlockSpec((tm, D_MODEL), lambda i: (i, 0)),
        out_shape=jax.ShapeDtypeStruct((rows, D_MODEL), jnp.float32),
        compiler_params=pltpu.CompilerParams(
            dimension_semantics=("arbitrary",), vmem_limit_bytes=VMEM_LIMIT),
        name="outproj",
    )(g, s, h, wo, nw)


def _ffn_kernel(*refs, batch, tm, tf):
    if batch:
        (h_ref, nw_pre_ref, wg_ref, wu_ref, cw_ref, wd_ref, nw_post_ref, hist_ref,
         y_ref, graw_ref, xn_ref, xe_ref) = refs
    else:
        (h_ref, nw_pre_ref, wg_ref, wu_ref, cw_ref, wd_ref, nw_post_ref, hist_ref,
         y_ref, graw_ref, xn_ref, xe_ref, carry_ref) = refs
    i = pl.program_id(0)
    j = pl.program_id(1)
    last_j = pl.num_programs(1) - 1
    rb = tm // FFN_ROW_BLOCKS

    if batch:
        xe_ref[:, 0:SUBLANES, :] = hist_ref[...]
    else:
        @pl.when(i == 0)
        def _():
            carry_ref[pl.ds(j, 1)] = hist_ref[...].reshape(1, SUBLANES, tf)
        xe_ref[:, 0:SUBLANES, :] = carry_ref[pl.ds(j, 1)]

    def step(first, last):
        def gate_up(r):
            rows = slice(r * rb, (r + 1) * rb)
            if first:
                h = h_ref[rows, :]
                xn_ref[rows, :] = (h * _rms_scale(h) * nw_pre_ref[...]).astype(jnp.bfloat16)
            xn = xn_ref[rows, :]
            return (jnp.dot(xn, wg_ref[...], preferred_element_type=jnp.float32),
                    jnp.dot(xn, wu_ref[...], preferred_element_type=jnp.float32))

        nxt = gate_up(0)
        for r in range(FFN_ROW_BLOCKS):
            rows = slice(r * rb, (r + 1) * rb)
            gate, up = nxt
            if r + 1 < FFN_ROW_BLOCKS:
                nxt = gate_up(r + 1)
            if batch:
                seqs = slice(r * rb // SUBLANES, (r + 1) * rb // SUBLANES)
                graw_ref[rows, :] = gate
                xe_ref[seqs, SUBLANES:2 * SUBLANES, :] = gate.reshape(rb // SUBLANES, SUBLANES, tf)
                taps = [xe_ref[seqs, SUBLANES - s:2 * SUBLANES - s, :] for s in range(FFN_CONV)]
            else:
                base = SUBLANES + r * rb
                xe_ref[:, base:base + rb, :] = gate.reshape(1, rb, tf)
                taps = [xe_ref[:, base - s:base - s + rb, :] for s in range(FFN_CONV)]
            conv = None
            for s, tap in enumerate(taps):
                term = tap * cw_ref[FFN_CONV - 1 - s:FFN_CONV - s, :]
                conv = term if conv is None else conv + term
            act = (_silu(conv.reshape(rb, tf)) * up).astype(jnp.bfloat16)
            down = jnp.dot(act, wd_ref[...], preferred_element_type=jnp.float32)
            if first:
                y_ref[rows, :] = down
            elif last:
                y = y_ref[rows, :] + down
                y_ref[rows, :] = h_ref[rows, :] + y * _rms_scale(y) * nw_post_ref[...]
            else:
                y_ref[rows, :] += down
        if not batch:
            tail = xe_ref[:, tm:tm + SUBLANES, :]
            carry_ref[pl.ds(j, 1)] = tail
            graw_ref[...] = tail.reshape(SUBLANES, tf)

    pl.when(j == 0)(functools.partial(step, True, False))
    pl.when((j > 0) & (j < last_j))(functools.partial(step, False, False))
    pl.when(j == last_j)(functools.partial(step, False, True))


def _ffn(h, nw_pre, wg, wu, cw, wd, nw_post, hist, *, batch, tm, tf):
    rows = h.shape[0]
    nj = D_FF // tf
    in_specs = [
        pl.BlockSpec((tm, D_MODEL), lambda i, j: (i, 0)),
        pl.BlockSpec((1, D_MODEL), lambda i, j: (0, 0)),
        pl.BlockSpec((D_MODEL, tf), lambda i, j: (0, j)),
        pl.BlockSpec((D_MODEL, tf), lambda i, j: (0, j)),
        pl.BlockSpec((FFN_CONV, tf), lambda i, j: (0, j)),
        pl.BlockSpec((tf, D_MODEL), lambda i, j: (j, 0)),
        pl.BlockSpec((1, D_MODEL), lambda i, j: (0, 0)),
    ]
    args = [h, nw_pre, wg, wu, cw, wd, nw_post, hist]
    scratch = [pltpu.VMEM((tm, D_MODEL), jnp.bfloat16)]
    if batch:
        in_specs.append(pl.BlockSpec((tm // SUBLANES, SUBLANES, tf), lambda i, j: (i, 0, j)))
        graw_spec = pl.BlockSpec((tm, tf), lambda i, j: (i, j))
        graw_shape = jax.ShapeDtypeStruct((rows, D_FF), jnp.float32)
        scratch.append(pltpu.VMEM((tm // SUBLANES, 2 * SUBLANES, tf), jnp.float32))
    else:
        in_specs.append(pl.BlockSpec((SUBLANES, tf), lambda i, j: (0, j)))
        graw_spec = pl.BlockSpec((SUBLANES, tf), lambda i, j: (i, j))
        graw_shape = jax.ShapeDtypeStruct((rows // tm * SUBLANES, D_FF), jnp.float32)
        scratch.append(pltpu.VMEM((1, SUBLANES + tm, tf), jnp.float32))
        scratch.append(pltpu.VMEM((nj, SUBLANES, tf), jnp.float32))
    return pl.pallas_call(
        functools.partial(_ffn_kernel, batch=batch, tm=tm, tf=tf),
        grid=(rows // tm, nj),
        in_specs=in_specs,
        out_specs=[pl.BlockSpec((tm, D_MODEL), lambda i, j: (i, 0)), graw_spec],
        out_shape=[jax.ShapeDtypeStruct((rows, D_MODEL), jnp.float32), graw_shape],
        scratch_shapes=scratch,
        compiler_params=pltpu.CompilerParams(
            dimension_semantics=("arbitrary", "arbitrary"), vmem_limit_bytes=VMEM_LIMIT),
        name="ffn_batch" if batch else "ffn_seq",
    )(*args)


def kernel(x_prompt, x_sample, cache_swa_meta_kv, cache_swa_window_kv, state_gdn_conv, state_gdn, state_ffn_conv, meta_tokens, rel_bias_table, w_in, gdn_conv_w, gdn_a_log, gdn_dt_bias, gdn_norm_w, swa_sinks, w_out, norm_mix_pre, norm_mix_post, norm_ffn_pre, norm_ffn_post, ffn_w_gate, ffn_w_up, ffn_conv_w, ffn_w_down):
    f32, bf16 = jnp.float32, jnp.bfloat16
    seq = x_prompt.shape[1]
    dec_b, dec_t = x_sample.shape[0], x_sample.shape[1]
    n_dec = dec_b * dec_t
    assert x_prompt.shape[0] == 1 and seq % CHUNK == 0 and dec_t == SUBLANES and n_dec % CHUNK == 0

    w_in_p, w_ba = _pack_w_in(jnp.transpose(w_in[0]), 2 * GDN_HEADS, tn=512)
    wo = w_out[0].astype(bf16)
    wg = ffn_w_gate[0].astype(bf16)
    wu = ffn_w_up[0].astype(bf16)
    wd = ffn_w_down[0].astype(bf16)
    lane_pad = lambda v: jnp.pad(v.reshape(1, GDN_HEADS), ((0, 0), (GDN_HEADS, LANES - 2 * GDN_HEADS)))
    alog_row = lane_pad(gdn_a_log[0])
    dtb_row = lane_pad(gdn_dt_bias[0])
    gnw = gdn_norm_w[0].reshape(1, HEAD_DIM)

    pad_rows = CHUNK - N_META
    n_small = n_dec + CHUNK
    x_big = x_prompt.reshape(seq, D_MODEL)
    x_small = jnp.concatenate(
        [x_sample.reshape(n_dec, D_MODEL), jnp.zeros((pad_rows, D_MODEL), f32), meta_tokens.astype(f32)], axis=0)
    nw = norm_mix_pre[0].reshape(1, D_MODEL)
    cw = gdn_conv_w[0]
    proj_small, ba_small = _inproj(x_small, nw, w_in_p, w_ba, tm=n_small, tn=512, row_chunk=128)
    proj_big, ba_big, qkv_tail = _inproj(
        x_big, nw, w_in_p, w_ba, (cw, proj_small[n_small - SUBLANES:, :GDN_QKV]),
        tm=1024, tn=512, row_chunk=128)

    small_chunks = proj_small.reshape(n_small // CHUNK, CHUNK, PROJ_COLS)
    small_groups = proj_small.reshape(n_small // SUBLANES, SUBLANES, PROJ_COLS)
    last_chunk = n_small // CHUNK - 1
    gdn_meta, s_meta = _gdn(
        small_chunks, ba_small.reshape(n_small // CHUNK, CHUNK, LANES), lambda s: (last_chunk, 0, 0),
        jnp.zeros((1, SUBLANES, GDN_QKV), f32), lambda s: (0, 0, 0),
        jnp.zeros((1, GDN_HEADS, HEAD_DIM, HEAD_DIM), f32), cw, alog_row, dtb_row, gnw,
        n_steps=1, nb=1, seq=CHUNK, group=CHUNK, carry=True, pad_rows=pad_rows, preconv=False)
    gdn_big, s_prompt = _gdn(
        proj_big.reshape(1, seq, PROJ_COLS), ba_big.reshape(1, seq, LANES), lambda s: (0, s, 0),
        jnp.zeros((1, SUBLANES, GDN_QKV), f32), lambda s: (0, 0, 0),
        s_meta, cw, alog_row, dtb_row, gnw,
        n_steps=seq // (GDN_SEQ_CHUNKS * CHUNK), nb=1, seq=GDN_SEQ_CHUNKS * CHUNK, group=CHUNK,
        carry=True, pad_rows=0, preconv=True)
    hist_gdn = jnp.pad(state_gdn_conv[0], ((0, 0), (SUBLANES - (GDN_CONV - 1), 0), (0, 0)))
    nb_gdn = CHUNK // dec_t
    gdn_small, s_sample = _gdn(
        small_groups, ba_small.reshape(n_small // SUBLANES, SUBLANES, LANES), lambda s: (s, 0, 0),
        hist_gdn, lambda s: (s, 0, 0),
        state_gdn, cw, alog_row, dtb_row, gnw,
        n_steps=dec_b // nb_gdn, nb=nb_gdn, seq=dec_t, group=dec_t, carry=False, pad_rows=0, preconv=False)

    qi = np.arange(WINDOW)[:, None]
    kj = np.arange(WINDOW)[None, :]
    mi = np.arange(N_META)[None, :]
    ti = np.arange(dec_t)[:, None]
    new_keys = kj - N_META
    id_arrays = [
        _bucket_ids(qi - kj, qi >= kj),
        _bucket_ids(qi - kj + WINDOW, kj > qi),
        _bucket_ids(qi + N_META - kj, kj < N_META),
        _bucket_ids(qi + N_META - kj + WINDOW, kj < N_META),
        _bucket_ids(ti + WINDOW - kj, kj > ti),
        _bucket_ids(np.where(new_keys < 0, PAST_LEN + ti - kj, ti - new_keys),
                    (new_keys < 0) | ((new_keys <= ti) & (new_keys < dec_t))),
        _bucket_ids(mi.T - mi, mi.T >= mi),
    ]
    bcur, bprev, bm0, bfar, bwin, bsmall, bmm = _bias_tables(rel_bias_table, id_arrays)
    sink_rows = lambda q: jnp.repeat(swa_sinks[0].reshape(SWA_KV_HEADS, SWA_GROUP), q, axis=1)[..., None]

    sq_blk = COL_SQ // SWA_WIDTH
    kv_blk = COL_KV // KV_WIDTH
    meta_blk = (n_small - N_META) // N_META
    full3 = lambda a: pl.BlockSpec(a.shape, lambda j: (0, 0, 0))
    sink_p = sink_rows(WINDOW)
    swa_rows = SWA_Q_BLOCKS * WINDOW
    swa_big = pl.pallas_call(
        _swa_prompt_kernel,
        grid=(seq // swa_rows,),
        in_specs=[
            pl.BlockSpec((swa_rows, SWA_WIDTH), lambda j: (j, sq_blk)),
            pl.BlockSpec((swa_rows, KV_WIDTH), lambda j: (j, kv_blk)),
            pl.BlockSpec((WINDOW, KV_WIDTH), lambda j: (jnp.maximum(j * SWA_Q_BLOCKS - 1, 0), kv_blk)),
            pl.BlockSpec((N_META, KV_WIDTH), lambda j: (meta_blk, kv_blk)),
            full3(bcur), full3(bprev), full3(bm0), full3(bfar), full3(sink_p),
        ],
        out_specs=pl.BlockSpec((swa_rows, SWA_WIDTH), lambda j: (j, 0)),
        out_shape=jax.ShapeDtypeStruct((seq, SWA_WIDTH), bf16),
        compiler_params=pltpu.CompilerParams(
            dimension_semantics=("arbitrary",), vmem_limit_bytes=VMEM_LIMIT),
        name="swa_prompt",
    )(proj_big, proj_big, proj_big, proj_small, bcur, bprev, bm0, bfar, sink_p)

    nb_swa = 8
    sink_s = sink_rows(dec_t)
    win = cache_swa_window_kv.reshape(dec_b, WINDOW * KV_SLOTS, HEAD_DIM)
    meta_kv = cache_swa_meta_kv.reshape(dec_b, N_META * KV_SLOTS, HEAD_DIM)
    swa_small, win_new = pl.pallas_call(
        functools.partial(_swa_sample_kernel, nb=nb_swa, seq=dec_t),
        grid=(dec_b // nb_swa,),
        in_specs=[
            pl.BlockSpec((nb_swa, dec_t, SWA_WIDTH), lambda j: (j, 0, sq_blk)),
            pl.BlockSpec((nb_swa, dec_t, KV_WIDTH), lambda j: (j, 0, kv_blk)),
            pl.BlockSpec((nb_swa, WINDOW * KV_SLOTS, HEAD_DIM), lambda j: (j, 0, 0)),
            pl.BlockSpec((nb_swa, N_META * KV_SLOTS, HEAD_DIM), lambda j: (j, 0, 0)),
            full3(bwin), full3(bsmall), full3(sink_s),
        ],
        out_specs=[pl.BlockSpec((nb_swa * dec_t, SWA_WIDTH), lambda j: (j, 0)),
                   pl.BlockSpec((nb_swa, WINDOW * KV_SLOTS, HEAD_DIM), lambda j: (j, 0, 0))],
        out_shape=[jax.ShapeDtypeStruct((n_dec, SWA_WIDTH), bf16),
                   jax.ShapeDtypeStruct((dec_b, WINDOW * KV_SLOTS, HEAD_DIM), f32)],
        compiler_params=pltpu.CompilerParams(
            dimension_semantics=("arbitrary",), vmem_limit_bytes=VMEM_LIMIT),
        name="swa_sample",
    )(small_groups, small_groups, win, meta_kv, bwin, bsmall, sink_s)

    sink_m = sink_rows(N_META)
    swa_meta = pl.pallas_call(
        _swa_meta_kernel,
        grid=(1,),
        in_specs=[
            pl.BlockSpec((N_META, SWA_WIDTH), lambda j: (meta_blk, sq_blk)),
            pl.BlockSpec((N_META, KV_WIDTH), lambda j: (meta_blk, kv_blk)),
            full3(bmm), full3(sink_m),
        ],
        out_specs=pl.BlockSpec((N_META, SWA_WIDTH), lambda j: (0, 0)),
        out_shape=jax.ShapeDtypeStruct((N_META, SWA_WIDTH), bf16),
        name="swa_meta",
    )(proj_small, proj_small, bmm, sink_m)

    nw_post = norm_mix_post[0].reshape(1, D_MODEL)
    nf_pre = norm_ffn_pre[0].reshape(1, D_MODEL)
    nf_post = norm_ffn_post[0].reshape(1, D_MODEL)
    fcw = ffn_conv_w[0]
    gdn_small_all = jnp.concatenate([gdn_small, gdn_meta], axis=0)
    swa_small_all = jnp.concatenate([swa_small, jnp.zeros((pad_rows, SWA_WIDTH), bf16), swa_meta], axis=0)
    h_small = _outproj(gdn_small_all, swa_small_all, x_small, wo, nw_post, tm=n_small // 2)
    hist_ffn = jnp.pad(state_ffn_conv[0], ((0, CHUNK // SUBLANES), (SUBLANES - (FFN_CONV - 1), 0), (0, 0)))
    y_small, g_small = _ffn(h_small, nf_pre, wg, wu, fcw, wd, nf_post, hist_ffn,
                            batch=True, tm=n_small // 2, tf=512)
    h_big = _outproj(gdn_big, swa_big, x_big, wo, nw_post, tm=512)
    y_big, g_tail = _ffn(h_big, nf_pre, wg, wu, fcw, wd, nf_post, g_small[n_small - SUBLANES:],
                         batch=False, tm=1024, tf=512)

    kv_shape = lambda n: (1, n, 2, SWA_KV_HEADS, HEAD_DIM)
    kv_small = proj_small[:, COL_KV:COL_KV + KV_WIDTH]
    y_prompt = y_big.reshape(1, seq, D_MODEL)
    y_sample = y_small[:n_dec].reshape(dec_b, dec_t, D_MODEL)
    p_meta_kv = kv_small[n_small - N_META:].reshape(kv_shape(N_META))[None]
    p_window_kv = proj_big[seq - WINDOW:, COL_KV:COL_KV + KV_WIDTH].reshape(kv_shape(WINDOW))[None]
    p_gdn_conv = qkv_tail[qkv_tail.shape[0] - (GDN_CONV - 1):].reshape(1, 1, GDN_CONV - 1, GDN_QKV)
    p_gdn_state = s_prompt[None]
    p_ffn_conv = g_tail[g_tail.shape[0] - (FFN_CONV - 1):].reshape(1, 1, FFN_CONV - 1, D_FF)
    s_window_kv = win_new.reshape(1, dec_b, WINDOW, 2, SWA_KV_HEADS, HEAD_DIM)
    s_gdn_conv = proj_small[:n_dec, :GDN_QKV].reshape(dec_b, dec_t, GDN_QKV)[:, dec_t - (GDN_CONV - 1):][None]
    s_gdn_state = s_sample
    s_ffn_conv = g_small[:n_dec].reshape(dec_b, dec_t, D_FF)[:, dec_t - (FFN_CONV - 1):][None]
    return (y_prompt, y_sample, p_meta_kv, p_window_kv, p_gdn_conv, p_gdn_state, p_ffn_conv,
            s_window_kv, s_gdn_conv, s_gdn_state, s_ffn_conv)
```

```python
import functools
import math

import numpy as np
import jax
import jax.numpy as jnp
from jax import lax
from jax.experimental import pallas as pl
from jax.experimental.pallas import tpu as pltpu

D_MODEL = 2048
HEAD_DIM = 128
GDN_HEADS = 8
GDN_WIDTH = GDN_HEADS * HEAD_DIM
GDN_QKV = 3 * GDN_WIDTH
SWA_HEADS = 8
SWA_KV_HEADS = 2
SWA_GROUP = SWA_HEADS // SWA_KV_HEADS
SWA_WIDTH = SWA_HEADS * HEAD_DIM
KV_SLOTS = 2 * SWA_KV_HEADS
KV_WIDTH = KV_SLOTS * HEAD_DIM
WINDOW = 128
N_META = 16
N_BUCKETS = 32
MAX_DISTANCE = 128
GDN_CONV = 4
FFN_CONV = 3
D_FF = 5632
EPS = 1e-6
PAST_LEN = 16384

SUBLANES = 8
LANES = 128

CHUNK = 128
assert CHUNK == HEAD_DIM == LANES
INV_BASE = 16
GDN_SEQ_CHUNKS = 2
INPROJ_ROW_BLOCKS = 2
FFN_ROW_BLOCKS = 2
COL_Z = GDN_QKV
COL_SQ = COL_Z + GDN_WIDTH
COL_KV = COL_SQ + SWA_WIDTH
PROJ_COLS = COL_KV + KV_WIDTH
GDN_COLS = COL_SQ

SWA_Q_BLOCKS = 4
NEG = -1e30
VMEM_LIMIT = 56 * 1024 * 1024

_NT = (((1,), (1,)), ((), ()))


def _dot(a, b):
    return jnp.dot(a.astype(jnp.bfloat16), b.astype(jnp.bfloat16), preferred_element_type=jnp.float32)


def _dot_nt(a, b):
    return lax.dot_general(a.astype(jnp.bfloat16), b.astype(jnp.bfloat16), _NT,
                           preferred_element_type=jnp.float32)


_dot_inv = _dot


def _dot_exact(a, b, dims=None):
    if dims is None:
        return jnp.dot(a, b, precision=lax.Precision.HIGHEST, preferred_element_type=jnp.float32)
    return lax.dot_general(a, b, dims, precision=lax.Precision.HIGHEST,
                           preferred_element_type=jnp.float32)


def _pack_w_in_kernel(a_ref, b_ref, o_ref, ba_ref, *, first_shifted, shift):
    j = pl.program_id(0)

    @pl.when(j < first_shifted)
    def _():
        o_ref[...] = a_ref[...].T.astype(o_ref.dtype)

    @pl.when(j >= first_shifted)
    def _():
        rows = jnp.concatenate([a_ref[shift:, :], b_ref[:shift, :]], axis=0)
        o_ref[...] = rows.T.astype(o_ref.dtype)

    @pl.when(j == first_shifted)
    def _():
        head = a_ref[:LANES, :]
        row = lax.broadcasted_iota(jnp.int32, head.shape, 0)
        ba_ref[...] = jnp.where(row < shift, head, 0.0).T.astype(ba_ref.dtype)


def _pack_w_in(w_in_t, n_ba, *, tn):
    n_blocks = PROJ_COLS // tn
    first_shifted = COL_SQ // tn
    assert COL_SQ % tn == 0 and w_in_t.shape[0] == PROJ_COLS + n_ba and n_ba % SUBLANES == 0
    return pl.pallas_call(
        functools.partial(_pack_w_in_kernel, first_shifted=first_shifted, shift=n_ba),
        grid=(n_blocks,),
        in_specs=[pl.BlockSpec((tn, D_MODEL), lambda j: (j, 0)),
                  pl.BlockSpec((tn, D_MODEL), lambda j: (jnp.maximum(j, first_shifted) + 1, 0))],
        out_specs=[pl.BlockSpec((D_MODEL, tn), lambda j: (0, j)),
                   pl.BlockSpec((D_MODEL, LANES), lambda j: (0, 0))],
        out_shape=[jax.ShapeDtypeStruct((D_MODEL, PROJ_COLS), jnp.bfloat16),
                   jax.ShapeDtypeStruct((D_MODEL, LANES), jnp.bfloat16)],
        compiler_params=pltpu.CompilerParams(
            dimension_semantics=("arbitrary",), vmem_limit_bytes=VMEM_LIMIT),
        name="pack_w_in",
    )(w_in_t, w_in_t)


def _rms_scale(x):
    return lax.rsqrt(jnp.mean(x * x, axis=-1, keepdims=True) + EPS)


def _silu(x):
    return x * jax.nn.sigmoid(x)


def _inproj_kernel(*refs, row_chunk, conv_tiles):
    if conv_tiles:
        (x_ref, nw_ref, w_ref, wba_ref, cw_ref, hist_ref,
         o_ref, ba_ref, tail_ref, xn_ref, xe_ref, carry_ref) = refs
    else:
        x_ref, nw_ref, w_ref, wba_ref, o_ref, ba_ref, xn_ref = refs
    i = pl.program_id(0)
    j = pl.program_id(1)
    tm, tn = o_ref.shape

    def normalize(rows):
        x = x_ref[rows, :]
        xn_ref[rows, :] = (x * _rms_scale(x) * nw_ref[...]).astype(jnp.bfloat16)

    def plain():
        o_ref[...] = jnp.dot(xn_ref[...], w_ref[...], preferred_element_type=jnp.float32)

    if not conv_tiles:
        @pl.when(j == 0)
        def _():
            def body(c, carry):
                normalize(pl.ds(pl.multiple_of(c * row_chunk, row_chunk), row_chunk))
                return carry
            lax.fori_loop(0, tm // row_chunk, body, 0)
            ba_ref[...] = jnp.dot(xn_ref[...], wba_ref[...], preferred_element_type=jnp.float32)
        plain()
        return
    pl.when(j >= conv_tiles)(plain)

    def conv_tile(first):
        @pl.when(i == 0)
        def _():
            carry_ref[pl.ds(j, 1)] = hist_ref[...].reshape(1, SUBLANES, tn)
        xe_ref[0:SUBLANES, :] = carry_ref[pl.ds(j, 1)].reshape(SUBLANES, tn)
        rb = tm // INPROJ_ROW_BLOCKS

        def raw_block(r):
            rows = slice(r * rb, (r + 1) * rb)
            if first:
                normalize(rows)
                ba_ref[rows, :] = jnp.dot(xn_ref[rows, :], wba_ref[...], preferred_element_type=jnp.float32)
            return jnp.dot(xn_ref[rows, :], w_ref[...], preferred_element_type=jnp.float32)

        nxt = raw_block(0)
        for r in range(INPROJ_ROW_BLOCKS):
            raw = nxt
            if r + 1 < INPROJ_ROW_BLOCKS:
                nxt = raw_block(r + 1)
            base = SUBLANES + r * rb
            xe_ref[base:base + rb, :] = raw
            conv = None
            for s in range(GDN_CONV):
                term = xe_ref[base - s:base - s + rb, :] * cw_ref[GDN_CONV - 1 - s:GDN_CONV - s, :]
                conv = term if conv is None else conv + term
            o_ref[r * rb:(r + 1) * rb, :] = _silu(conv)
        tail = xe_ref[tm:tm + SUBLANES, :]
        carry_ref[pl.ds(j, 1)] = tail.reshape(1, SUBLANES, tn)
        tail_ref[...] = tail

    pl.when(j == 0)(functools.partial(conv_tile, True))
    pl.when((j > 0) & (j < conv_tiles))(functools.partial(conv_tile, False))


def _inproj(x, nw, w, wba, conv=None, *, tm, tn, row_chunk):
    rows = x.shape[0]
    conv_tiles = GDN_QKV // tn if conv else 0
    in_specs = [
        pl.BlockSpec((tm, D_MODEL), lambda i, j: (i, 0)),
        pl.BlockSpec((1, D_MODEL), lambda i, j: (0, 0)),
        pl.BlockSpec((D_MODEL, tn), lambda i, j: (0, j)),
        pl.BlockSpec((D_MODEL, LANES), lambda i, j: (0, 0)),
    ]
    out_specs = [pl.BlockSpec((tm, tn), lambda i, j: (i, j)),
                 pl.BlockSpec((tm, LANES), lambda i, j: (i, 0))]
    out_shape = [jax.ShapeDtypeStruct((rows, PROJ_COLS), jnp.float32),
                 jax.ShapeDtypeStruct((rows, LANES), jnp.float32)]
    scratch = [pltpu.VMEM((tm, D_MODEL), jnp.bfloat16)]
    args = [x, nw, w, wba]
    if conv:
        conv_col = lambda i, j: (0, jnp.minimum(j, conv_tiles - 1))
        in_specs += [pl.BlockSpec((GDN_CONV, tn), conv_col), pl.BlockSpec((SUBLANES, tn), conv_col)]
        out_specs.append(pl.BlockSpec((SUBLANES, tn), lambda i, j: (i, jnp.minimum(j, conv_tiles - 1))))
        out_shape.append(jax.ShapeDtypeStruct((rows // tm * SUBLANES, GDN_QKV), jnp.float32))
        scratch += [pltpu.VMEM((SUBLANES + tm, tn), jnp.float32),
                    pltpu.VMEM((conv_tiles, SUBLANES, tn), jnp.float32)]
        args += list(conv)
    return pl.pallas_call(
        functools.partial(_inproj_kernel, row_chunk=row_chunk, conv_tiles=conv_tiles),
        grid=(rows // tm, PROJ_COLS // tn),
        in_specs=in_specs,
        out_specs=out_specs,
        out_shape=out_shape,
        scratch_shapes=scratch,
        compiler_params=pltpu.CompilerParams(
            dimension_semantics=("arbitrary", "arbitrary"), vmem_limit_bytes=VMEM_LIMIT),
        name="inproj_conv" if conv else "inproj",
    )(*args)


def _tri_inverse(lms, ri, ci):
    shift = INV_BASE.bit_length() - 1
    eye = (ri == ci).astype(jnp.float32)
    in_block = (ri >> shift) == (ci >> shift)
    ps = [jnp.where(in_block, lm, 0.0) for lm in lms]
    ts = [eye - p for p in ps]
    for _ in range(shift - 1):
        ps = [_dot_inv(p, p) for p in ps]
        ts = [t + _dot_inv(t, p) for t, p in zip(ts, ps)]
    size = INV_BASE
    while size < CHUNK:
        shift += 1
        in_pair = (ri >> shift) == (ci >> shift)
        off_mask = in_pair & jnp.logical_not(in_block)
        tos = [_dot_inv(t, jnp.where(off_mask, lm, 0.0)) for t, lm in zip(ts, lms)]
        ts = [t - _dot_inv(to, t) for t, to in zip(ts, tos)]
        in_block = in_pair
        size *= 2
    return ts


def _gdn_kernel(x_ref, ba_ref, hist_ref, s0_ref, cw_ref, alog_ref, dtb_ref, gnw_ref,
                o_ref, sout_ref, xe_ref, s_ref, *, nb, seq, group, carry, pad_rows, preconv):
    step = pl.program_id(0)
    rows = nb * seq
    n_chunks = rows // CHUNK
    n_groups = CHUNK // group
    gshift = group.bit_length() - 1

    if carry:
        @pl.when(step == 0)
        def _():
            s_ref[...] = s0_ref[0]
    if not preconv:
        if carry:
            @pl.when(step == 0)
            def _():
                xe_ref[:, 0:SUBLANES, :] = hist_ref[...]
        else:
            xe_ref[:, 0:SUBLANES, :] = hist_ref[...]
        xe_ref[:, SUBLANES:SUBLANES + seq, :] = x_ref[:, :, 0:GDN_QKV]

    seq_rows = min(seq, CHUNK)
    seqs_per_chunk = CHUNK // seq_rows

    def chunk_rows(ref, c, row_off, cols):
        if seq >= CHUNK:
            start = row_off + c * CHUNK
            return ref[0:1, start:start + CHUNK, cols]
        b0 = c * seqs_per_chunk
        return ref[b0:b0 + seqs_per_chunk, row_off:row_off + seq, cols]

    def conv_chunk(col, c):
        cols = slice(col, col + HEAD_DIM)
        if preconv:
            return chunk_rows(x_ref, c, 0, cols).reshape(CHUNK, HEAD_DIM)
        acc = None
        for s in range(GDN_CONV):
            term = chunk_rows(xe_ref, c, SUBLANES - s, cols) * cw_ref[GDN_CONV - 1 - s:GDN_CONV - s, cols]
            acc = term if acc is None else acc + term
        return _silu(acc).reshape(CHUNK, HEAD_DIM)

    ri = lax.broadcasted_iota(jnp.int32, (CHUNK, CHUNK), 0)
    ci = lax.broadcasted_iota(jnp.int32, (CHUNK, CHUNK), 1)
    same = (ri >> gshift) == (ci >> gshift)
    m_incl = same & (ri >= ci)
    m_strict = same & (ri > ci)
    f_incl = m_incl.astype(jnp.float32)
    f_same = same.astype(jnp.float32)
    lane = lax.broadcasted_iota(jnp.int32, (CHUNK, LANES), 1)
    row_in_chunk = lax.broadcasted_iota(jnp.int32, (CHUNK, LANES), 0)

    pre = []
    for c in range(n_chunks):
        bac = chunk_rows(ba_ref, c, 0, slice(0, LANES)).reshape(CHUNK, LANES)
        beta_all = jax.nn.sigmoid(bac)
        sp_in = bac + dtb_ref[...]
        softplus = jnp.maximum(sp_in, 0.0) + jnp.log1p(jnp.exp(-jnp.abs(sp_in)))
        g_all = -jnp.exp(alog_ref[...]) * softplus
        if pad_rows and c == 0:
            valid = row_in_chunk >= pad_rows
            beta_all = jnp.where(valid, beta_all, 0.0)
            g_all = jnp.where(valid, g_all, 0.0)
        g_all = jnp.where((lane >= GDN_HEADS) & (lane < 2 * GDN_HEADS), g_all, 0.0)
        gc_col = _dot_exact(f_incl, g_all)
        if n_groups == 1:
            gtot_col = jnp.broadcast_to(gc_col[CHUNK - 1:CHUNK, :], (CHUNK, LANES))
        else:
            gtot_col = _dot_exact(f_same, g_all)
        gc_row = gc_col.T

        for h in range(GDN_HEADS):
            qh = conv_chunk(h * HEAD_DIM, c)
            kh = conv_chunk(GDN_WIDTH + h * HEAD_DIM, c)
            vh = conv_chunk(2 * GDN_WIDTH + h * HEAD_DIM, c)
            qh = qh * lax.rsqrt(jnp.sum(qh * qh, -1, keepdims=True) + EPS) * (HEAD_DIM ** -0.5)
            kh = kh * lax.rsqrt(jnp.sum(kh * kh, -1, keepdims=True) + EPS)
            bcast = lambda col: jnp.broadcast_to(col, (CHUNK, HEAD_DIM))
            gcc = bcast(gc_col[:, GDN_HEADS + h:GDN_HEADS + h + 1])
            gtc = bcast(gtot_col[:, GDN_HEADS + h:GDN_HEADS + h + 1])
            beta = bcast(beta_all[:, h:h + 1])
            gcr = gc_row[GDN_HEADS + h:GDN_HEADS + h + 1, :]
            decay = jnp.exp(jnp.where(m_incl, gcc - gcr, NEG))
            kb = kh * beta
            egc = jnp.exp(gcc)
            pre.append(dict(
                c=c, h=h,
                lm=jnp.where(m_strict, _dot_nt(kb, kh) * decay, 0.0),
                qk=_dot_nt(qh, kh) * decay,
                rhs=jnp.concatenate([vh * beta, kb * egc], axis=1),
                qg=qh * egc,
                kd_t=(kh * jnp.exp(gtc - gcc)).T,
                gl=jnp.exp(gtc)))

    inverses = _tri_inverse([p["lm"] for p in pre], ri, ci)
    sols = [_dot_inv(t, p["rhs"]) for t, p in zip(inverses, pre)]

    for c in range(n_chunks):
        r0 = c * CHUNK
        items = [(p, sol) for p, sol in zip(pre, sols) if p["c"] == c]
        state = lambda h, b: s_ref[h] if carry else s0_ref[c * n_groups + b, h]
        ws, qs = [], []
        for p, sol in items:
            w = sol[:, HEAD_DIM:]
            ws_parts, qs_parts = [], []
            for b in range(n_groups):
                g0 = b * group
                wq = jnp.concatenate([w[g0:g0 + group], p["qg"][g0:g0 + group]], axis=0)
                res = _dot(wq, state(p["h"], b))
                ws_parts.append(res[:group])
                qs_parts.append(res[group:])
            ws.append(ws_parts[0] if n_groups == 1 else jnp.concatenate(ws_parts, axis=0))
            qs.append(qs_parts[0] if n_groups == 1 else jnp.concatenate(qs_parts, axis=0))
        v_new = [sol[:, :HEAD_DIM] - w for (p, sol), w in zip(items, ws)]
        o = [a + _dot(p["qk"], vn) for a, (p, sol), vn in zip(qs, items, v_new)]
        for (p, sol), vn in zip(items, v_new):
            h = p["h"]
            for b in range(n_groups):
                g0 = b * group
                kd_b = p["kd_t"] if n_groups == 1 else jnp.where((ci >> gshift) == b, p["kd_t"], 0.0)
                st = state(h, b) * p["gl"][g0:g0 + 1, :] + _dot(kd_b, vn)
                if carry:
                    s_ref[h] = st
                else:
                    sout_ref[c * n_groups + b, h] = st
        for (p, sol), oh in zip(items, o):
            h = p["h"]
            z = chunk_rows(x_ref, c, 0, slice(COL_Z + h * HEAD_DIM, COL_Z + (h + 1) * HEAD_DIM)).reshape(CHUNK, HEAD_DIM)
            y = oh * lax.rsqrt(jnp.mean(oh * oh, -1, keepdims=True) + EPS) * gnw_ref[...] * _silu(z)
            o_ref[r0:r0 + CHUNK, h * HEAD_DIM:(h + 1) * HEAD_DIM] = y.astype(o_ref.dtype)

    if carry:
        if not preconv:
            xe_ref[:, 0:SUBLANES, :] = xe_ref[:, seq:seq + SUBLANES, :]

        @pl.when(step == pl.num_programs(0) - 1)
        def _():
            sout_ref[0] = s_ref[...]


def _gdn(x3, ba3, x_idx, hist, hist_idx, s0, cw, alog_row, dtb_row, gnw, *,
         n_steps, nb, seq, group, carry, pad_rows, preconv):
    rows = nb * seq
    if carry:
        state_spec = pl.BlockSpec((1, GDN_HEADS, HEAD_DIM, HEAD_DIM), lambda s: (0, 0, 0, 0))
        state_shape = (1, GDN_HEADS, HEAD_DIM, HEAD_DIM)
    else:
        n_states = rows // group
        state_spec = pl.BlockSpec((None, n_states, GDN_HEADS, HEAD_DIM, HEAD_DIM), lambda s: (0, s, 0, 0, 0))
        state_shape = (1, n_steps * n_states, GDN_HEADS, HEAD_DIM, HEAD_DIM)
    full = lambda shape: pl.BlockSpec(shape, lambda s: (0,) * len(shape))
    return pl.pallas_call(
        functools.partial(_gdn_kernel, nb=nb, seq=seq, group=group, carry=carry, pad_rows=pad_rows,
                          preconv=preconv),
        grid=(n_steps,),
        in_specs=[
            pl.BlockSpec((nb, seq, GDN_COLS), x_idx),
            pl.BlockSpec((nb, seq, LANES), x_idx),
            pl.BlockSpec((nb, SUBLANES, GDN_QKV), hist_idx),
            state_spec,
            full((GDN_CONV, GDN_QKV)),
            full((1, LANES)),
            full((1, LANES)),
            full((1, HEAD_DIM)),
        ],
        out_specs=[
            pl.BlockSpec((rows, GDN_WIDTH), lambda s: (s, 0)),
            state_spec,
        ],
        out_shape=[
            jax.ShapeDtypeStruct((n_steps * rows, GDN_WIDTH), jnp.bfloat16),
            jax.ShapeDtypeStruct(state_shape, jnp.float32),
        ],
        scratch_shapes=[
            pltpu.VMEM((nb, SUBLANES + (SUBLANES if preconv else seq), GDN_QKV), jnp.float32),
            pltpu.VMEM((GDN_HEADS, HEAD_DIM, HEAD_DIM), jnp.float32),
        ],
        compiler_params=pltpu.CompilerParams(
            dimension_semantics=("arbitrary",), vmem_limit_bytes=VMEM_LIMIT),
        name="gdn_seq" if carry else "gdn_batch",
    )(x3, ba3, hist, s0, cw, alog_row, dtb_row, gnw)


def _t5_bucket_np(dist):
    n = np.maximum(dist, 0)
    exact = N_BUCKETS // 2
    large = exact + (np.log(np.maximum(n, 1).astype(np.float32) / exact)
                     / math.log(MAX_DISTANCE / exact) * (N_BUCKETS - exact)).astype(np.int32)
    return np.where(n < exact, n, np.minimum(large, N_BUCKETS - 1)).astype(np.int32)


def _bucket_ids(dist, valid):
    return np.where(valid, _t5_bucket_np(dist), -1).astype(np.int32)


def _bias_kernel(table_ref, *refs):
    n = len(refs) // 2
    for ids_ref, out_ref in zip(refs[:n], refs[n:]):
        ids = ids_ref[...]
        nq = ids.shape[0]
        for head in range(SWA_HEADS):
            def body(b, acc):
                return jnp.where(ids == b, table_ref[b, head], acc)
            acc = lax.fori_loop(0, N_BUCKETS, body, jnp.full(ids.shape, NEG, jnp.float32))
            kh, g = divmod(head, SWA_GROUP)
            out_ref[kh, g * nq:(g + 1) * nq, :] = acc


def _bias_tables(rel_table, id_arrays):
    out_shapes = [jax.ShapeDtypeStruct((SWA_KV_HEADS, SWA_GROUP * a.shape[0], a.shape[1]), jnp.float32)
                  for a in id_arrays]
    vmem = pl.BlockSpec(memory_space=pltpu.VMEM)
    return pl.pallas_call(
        _bias_kernel,
        in_specs=[pl.BlockSpec(memory_space=pltpu.SMEM)] + [vmem] * len(id_arrays),
        out_specs=[vmem] * len(id_arrays),
        out_shape=out_shapes,
        name="swa_bias",
    )(rel_table, *[jnp.asarray(a) for a in id_arrays])


def _attend(problems):
    scale = HEAD_DIM ** -0.5
    scores = [[_dot_nt(q, k) * scale + b for k, b in zip(keys, biases)]
              for q, keys, _, biases, _ in problems]
    maxes = []
    for (_, _, _, _, sink), segs in zip(problems, scores):
        m = sink
        for s in segs:
            m = jnp.maximum(m, jnp.max(s, axis=-1, keepdims=True))
        maxes.append(m)
    probs = [[jnp.exp(s - m) for s in segs] for segs, m in zip(scores, maxes)]
    outs = []
    for (_, _, values, _, sink), ps, m in zip(problems, probs, maxes):
        acc = None
        for p, v in zip(ps, values):
            v_ones = jnp.concatenate([v, jnp.ones((v.shape[0], HEAD_DIM), v.dtype)], axis=1)
            pv = _dot(p, v_ones)
            acc = pv if acc is None else acc + pv
        den = acc[:, HEAD_DIM:] + jnp.exp(sink - m)
        outs.append(acc[:, :HEAD_DIM] / den)
    return outs


def _group_queries(q_rows, kh):
    return jnp.concatenate(
        [q_rows((kh * SWA_GROUP + g) * HEAD_DIM, (kh * SWA_GROUP + g + 1) * HEAD_DIM)
         for g in range(SWA_GROUP)], axis=0)


def _pad_keys(rows):
    return jnp.concatenate([rows, jnp.zeros((WINDOW - rows.shape[0], rows.shape[1]), rows.dtype)], axis=0)


def _k_cols(kh):
    return slice(kh * HEAD_DIM, (kh + 1) * HEAD_DIM)


def _v_cols(kh):
    return slice((SWA_KV_HEADS + kh) * HEAD_DIM, (SWA_KV_HEADS + kh + 1) * HEAD_DIM)


def _swa_prompt_kernel(q_ref, kvc_ref, kvp_ref, kvm_ref, bcur_ref, bprev_ref, bm0_ref, bfar_ref,
                       sink_ref, o_ref):
    first = pl.program_id(0) == 0
    problems = []
    for blk in range(SWA_Q_BLOCKS):
        rows = slice(blk * WINDOW, (blk + 1) * WINDOW)
        prev_ref, prev_rows = (kvp_ref, slice(0, WINDOW)) if blk == 0 else (
            kvc_ref, slice((blk - 1) * WINDOW, blk * WINDOW))
        for kh in range(SWA_KV_HEADS):
            ks, vs = _k_cols(kh), _v_cols(kh)
            first_block = first if blk == 0 else False
            b_prev = jnp.where(first_block, NEG, bprev_ref[kh])
            b_meta = jnp.where(first_block, bm0_ref[kh], bfar_ref[kh])
            problems.append((_group_queries(lambda a, b, rows=rows: q_ref[rows, a:b], kh),
                             [kvc_ref[rows, ks], prev_ref[prev_rows, ks], _pad_keys(kvm_ref[:, ks])],
                             [kvc_ref[rows, vs], prev_ref[prev_rows, vs], _pad_keys(kvm_ref[:, vs])],
                             [bcur_ref[kh], b_prev, b_meta], sink_ref[kh]))
    for i, o in enumerate(_attend(problems)):
        blk, kh = divmod(i, SWA_KV_HEADS)
        for g in range(SWA_GROUP):
            head = kh * SWA_GROUP + g
            o_ref[blk * WINDOW:(blk + 1) * WINDOW, head * HEAD_DIM:(head + 1) * HEAD_DIM] = (
                o[g * WINDOW:(g + 1) * WINDOW].astype(o_ref.dtype))


def _swa_meta_kernel(q_ref, kv_ref, bias_ref, sink_ref, o_ref):
    problems = [(_group_queries(lambda a, b: q_ref[:, a:b], kh), [kv_ref[:, _k_cols(kh)]],
                 [kv_ref[:, _v_cols(kh)]], [bias_ref[kh]], sink_ref[kh])
                for kh in range(SWA_KV_HEADS)]
    for kh, o in enumerate(_attend(problems)):
        for g in range(SWA_GROUP):
            head = kh * SWA_GROUP + g
            o_ref[:, head * HEAD_DIM:(head + 1) * HEAD_DIM] = (
                o[g * N_META:(g + 1) * N_META].astype(o_ref.dtype))


def _swa_sample_kernel(q_ref, kvn_ref, win_ref, meta_ref, bwin_ref, bsmall_ref, sink_ref,
                       o_ref, wout_ref, *, nb, seq):
    cached = lambda ref, b, slot, n: ref[b, pl.ds(slot, n, stride=KV_SLOTS), :]
    keep = (WINDOW - seq) * KV_SLOTS
    wout_ref[:, 0:keep, :] = win_ref[:, seq * KV_SLOTS:WINDOW * KV_SLOTS, :]
    for slot in range(KV_SLOTS):
        wout_ref[:, pl.ds(keep + slot, seq, stride=KV_SLOTS), :] = (
            kvn_ref[:, :, slot * HEAD_DIM:(slot + 1) * HEAD_DIM])
    problems = []
    for b in range(nb):
        for kh in range(SWA_KV_HEADS):
            ks, vs = _k_cols(kh), _v_cols(kh)
            k_small = _pad_keys(jnp.concatenate([cached(meta_ref, b, kh, N_META), kvn_ref[b, :, ks]], axis=0))
            v_small = _pad_keys(jnp.concatenate(
                [cached(meta_ref, b, SWA_KV_HEADS + kh, N_META), kvn_ref[b, :, vs]], axis=0))
            problems.append((_group_queries(lambda a, c, b=b: q_ref[b, :, a:c], kh),
                             [cached(win_ref, b, kh, WINDOW), k_small],
                             [cached(win_ref, b, SWA_KV_HEADS + kh, WINDOW), v_small],
                             [bwin_ref[kh], bsmall_ref[kh]], sink_ref[kh]))
    outs = _attend(problems)
    for head in range(SWA_HEADS):
        kh, g = divmod(head, SWA_GROUP)
        rows = [outs[b * SWA_KV_HEADS + kh][g * seq:(g + 1) * seq] for b in range(nb)]
        o_ref[:, head * HEAD_DIM:(head + 1) * HEAD_DIM] = jnp.concatenate(rows, axis=0).astype(o_ref.dtype)


def _outproj_kernel(g_ref, s_ref, h_ref, wo_ref, nw_ref, o_ref):
    mix = (jnp.dot(g_ref[...], wo_ref[0:GDN_WIDTH, :], preferred_element_type=jnp.float32)
           + jnp.dot(s_ref[...], wo_ref[GDN_WIDTH:, :], preferred_element_type=jnp.float32))
    o_ref[...] = h_ref[...] + mix * _rms_scale(mix) * nw_ref[...]


def _outproj(g, s, h, wo, nw, *, tm):
    rows = h.shape[0]
    return pl.pallas_call(
        _outproj_kernel,
        grid=(rows // tm,),
        in_specs=[
            pl.BlockSpec((tm, GDN_WIDTH), lambda i: (i, 0)),
            pl.BlockSpec((tm, SWA_WIDTH), lambda i: (i, 0)),
            pl.BlockSpec((tm, D_MODEL), lambda i: (i, 0)),
            pl.BlockSpec((D_MODEL, D_MODEL), lambda i: (0, 0)),
            pl.BlockSpec((1, D_MODEL), lambda i: (0, 0)),
        ],
        out_specs=pl.BlockSpec((tm, D_MODEL), lambda i: (i, 0)),
        out_shape=jax.ShapeDtypeStruct((rows, D_MODEL), jnp.float32),
        compiler_params=pltpu.CompilerParams(
            dimension_semantics=("arbitrary",), vmem_limit_bytes=VMEM_LIMIT),
        name="outproj",
    )(g, s, h, wo, nw)


def _ffn_kernel(*refs, batch, tm, tf):
    if batch:
        (h_ref, nw_pre_ref, wg_ref, wu_ref, cw_ref, wd_ref, nw_post_ref, hist_ref,
         y_ref, graw_ref, xn_ref, xe_ref) = refs
    else:
        (h_ref, nw_pre_ref, wg_ref, wu_ref, cw_ref, wd_ref, nw_post_ref, hist_ref,
         y_ref, graw_ref, xn_ref, xe_ref, carry_ref) = refs
    i = pl.program_id(0)
    j = pl.program_id(1)
    last_j = pl.num_programs(1) - 1
    rb = tm // FFN_ROW_BLOCKS

    if batch:
        xe_ref[:, 0:SUBLANES, :] = hist_ref[...]
    else:
        @pl.when(i == 0)
        def _():
            carry_ref[pl.ds(j, 1)] = hist_ref[...].reshape(1, SUBLANES, tf)
        xe_ref[:, 0:SUBLANES, :] = carry_ref[pl.ds(j, 1)]

    def step(first, last):
        def gate_up(r):
            rows = slice(r * rb, (r + 1) * rb)
            if first:
                h = h_ref[rows, :]
                xn_ref[rows, :] = (h * _rms_scale(h) * nw_pre_ref[...]).astype(jnp.bfloat16)
            xn = xn_ref[rows, :]
            return (jnp.dot(xn, wg_ref[...], preferred_element_type=jnp.float32),
                    jnp.dot(xn, wu_ref[...], preferred_element_type=jnp.float32))

        nxt = gate_up(0)
        for r in range(FFN_ROW_BLOCKS):
            rows = slice(r * rb, (r + 1) * rb)
            gate, up = nxt
            if r + 1 < FFN_ROW_BLOCKS:
                nxt = gate_up(r + 1)
            if batch:
                seqs = slice(r * rb // SUBLANES, (r + 1) * rb // SUBLANES)
                graw_ref[rows, :] = gate
                xe_ref[seqs, SUBLANES:2 * SUBLANES, :] = gate.reshape(rb // SUBLANES, SUBLANES, tf)
                taps = [xe_ref[seqs, SUBLANES - s:2 * SUBLANES - s, :] for s in range(FFN_CONV)]
            else:
                base = SUBLANES + r * rb
                xe_ref[:, base:base + rb, :] = gate.reshape(1, rb, tf)
                taps = [xe_ref[:, base - s:base - s + rb, :] for s in range(FFN_CONV)]
            conv = None
            for s, tap in enumerate(taps):
                term = tap * cw_ref[FFN_CONV - 1 - s:FFN_CONV - s, :]
                conv = term if conv is None else conv + term
            act = (_silu(conv.reshape(rb, tf)) * up).astype(jnp.bfloat16)
            down = jnp.dot(act, wd_ref[...], preferred_element_type=jnp.float32)
            if first:
                y_ref[rows, :] = down
            elif last:
                y = y_ref[rows, :] + down
                y_ref[rows, :] = h_ref[rows, :] + y * _rms_scale(y) * nw_post_ref[...]
            else:
                y_ref[rows, :] += down
        if not batch:
            tail = xe_ref[:, tm:tm + SUBLANES, :]
            carry_ref[pl.ds(j, 1)] = tail
            graw_ref[...] = tail.reshape(SUBLANES, tf)

    pl.when(j == 0)(functools.partial(step, True, False))
    pl.when((j > 0) & (j < last_j))(functools.partial(step, False, False))
    pl.when(j == last_j)(functools.partial(step, False, True))


def _ffn(h, nw_pre, wg, wu, cw, wd, nw_post, hist, *, batch, tm, tf):
    rows = h.shape[0]
    nj = D_FF // tf
    in_specs = [
        pl.BlockSpec((tm, D_MODEL), lambda i, j: (i, 0)),
        pl.BlockSpec((1, D_MODEL), lambda i, j: (0, 0)),
        pl.BlockSpec((D_MODEL, tf), lambda i, j: (0, j)),
        pl.BlockSpec((D_MODEL, tf), lambda i, j: (0, j)),
        pl.BlockSpec((FFN_CONV, tf), lambda i, j: (0, j)),
        pl.BlockSpec((tf, D_MODEL), lambda i, j: (j, 0)),
        pl.BlockSpec((1, D_MODEL), lambda i, j: (0, 0)),
    ]
    args = [h, nw_pre, wg, wu, cw, wd, nw_post, hist]
    scratch = [pltpu.VMEM((tm, D_MODEL), jnp.bfloat16)]
    if batch:
        in_specs.append(pl.BlockSpec((tm // SUBLANES, SUBLANES, tf), lambda i, j: (i, 0, j)))
        graw_spec = pl.BlockSpec((tm, tf), lambda i, j: (i, j))
        graw_shape = jax.ShapeDtypeStruct((rows, D_FF), jnp.float32)
        scratch.append(pltpu.VMEM((tm // SUBLANES, 2 * SUBLANES, tf), jnp.float32))
    else:
        in_specs.append(pl.BlockSpec((SUBLANES, tf), lambda i, j: (0, j)))
        graw_spec = pl.BlockSpec((SUBLANES, tf), lambda i, j: (i, j))
        graw_shape = jax.ShapeDtypeStruct((rows // tm * SUBLANES, D_FF), jnp.float32)
        scratch.append(pltpu.VMEM((1, SUBLANES + tm, tf), jnp.float32))
        scratch.append(pltpu.VMEM((nj, SUBLANES, tf), jnp.float32))
    return pl.pallas_call(
        functools.partial(_ffn_kernel, batch=batch, tm=tm, tf=tf),
        grid=(rows // tm, nj),
        in_specs=in_specs,
        out_specs=[pl.BlockSpec((tm, D_MODEL), lambda i, j: (i, 0)), graw_spec],
        out_shape=[jax.ShapeDtypeStruct((rows, D_MODEL), jnp.float32), graw_shape],
        scratch_shapes=scratch,
        compiler_params=pltpu.CompilerParams(
            dimension_semantics=("arbitrary", "arbitrary"), vmem_limit_bytes=VMEM_LIMIT),
        name="ffn_batch" if batch else "ffn_seq",
    )(*args)


def kernel(x_prompt, x_sample, cache_swa_meta_kv, cache_swa_window_kv, state_gdn_conv, state_gdn, state_ffn_conv, meta_tokens, rel_bias_table, w_in, gdn_conv_w, gdn_a_log, gdn_dt_bias, gdn_norm_w, swa_sinks, w_out, norm_mix_pre, norm_mix_post, norm_ffn_pre, norm_ffn_post, ffn_w_gate, ffn_w_up, ffn_conv_w, ffn_w_down):
    f32, bf16 = jnp.float32, jnp.bfloat16
    seq = x_prompt.shape[1]
    dec_b, dec_t = x_sample.shape[0], x_sample.shape[1]
    n_dec = dec_b * dec_t
    assert x_prompt.shape[0] == 1 and seq % CHUNK == 0 and dec_t == SUBLANES and n_dec % CHUNK == 0

    w_in_p, w_ba = _pack_w_in(jnp.transpose(w_in[0]), 2 * GDN_HEADS, tn=512)
    wo = w_out[0].astype(bf16)
    wg = ffn_w_gate[0].astype(bf16)
    wu = ffn_w_up[0].astype(bf16)
    wd = ffn_w_down[0].astype(bf16)
    lane_pad = lambda v: jnp.pad(v.reshape(1, GDN_HEADS), ((0, 0), (GDN_HEADS, LANES - 2 * GDN_HEADS)))
    alog_row = lane_pad(gdn_a_log[0])
    dtb_row = lane_pad(gdn_dt_bias[0])
    gnw = gdn_norm_w[0].reshape(1, HEAD_DIM)

    pad_rows = CHUNK - N_META
    n_small = n_dec + CHUNK
    x_big = x_prompt.reshape(seq, D_MODEL)
    x_small = jnp.concatenate(
        [x_sample.reshape(n_dec, D_MODEL), jnp.zeros((pad_rows, D_MODEL), f32), meta_tokens.astype(f32)], axis=0)
    nw = norm_mix_pre[0].reshape(1, D_MODEL)
    cw = gdn_conv_w[0]
    proj_small, ba_small = _inproj(x_small, nw, w_in_p, w_ba, tm=n_small, tn=512, row_chunk=128)
    proj_big, ba_big, qkv_tail = _inproj(
        x_big, nw, w_in_p, w_ba, (cw, proj_small[n_small - SUBLANES:, :GDN_QKV]),
        tm=1024, tn=512, row_chunk=128)

    small_chunks = proj_small.reshape(n_small // CHUNK, CHUNK, PROJ_COLS)
    small_groups = proj_small.reshape(n_small // SUBLANES, SUBLANES, PROJ_COLS)
    last_chunk = n_small // CHUNK - 1
    gdn_meta, s_meta = _gdn(
        small_chunks, ba_small.reshape(n_small // CHUNK, CHUNK, LANES), lambda s: (last_chunk, 0, 0),
        jnp.zeros((1, SUBLANES, GDN_QKV), f32), lambda s: (0, 0, 0),
        jnp.zeros((1, GDN_HEADS, HEAD_DIM, HEAD_DIM), f32), cw, alog_row, dtb_row, gnw,
        n_steps=1, nb=1, seq=CHUNK, group=CHUNK, carry=True, pad_rows=pad_rows, preconv=False)
    gdn_big, s_prompt = _gdn(
        proj_big.reshape(1, seq, PROJ_COLS), ba_big.reshape(1, seq, LANES), lambda s: (0, s, 0),
        jnp.zeros((1, SUBLANES, GDN_QKV), f32), lambda s: (0, 0, 0),
        s_meta, cw, alog_row, dtb_row, gnw,
        n_steps=seq // (GDN_SEQ_CHUNKS * CHUNK), nb=1, seq=GDN_SEQ_CHUNKS * CHUNK, group=CHUNK,
        carry=True, pad_rows=0, preconv=True)
    hist_gdn = jnp.pad(state_gdn_conv[0], ((0, 0), (SUBLANES - (GDN_CONV - 1), 0), (0, 0)))
    nb_gdn = CHUNK // dec_t
    gdn_small, s_sample = _gdn(
        small_groups, ba_small.reshape(n_small // SUBLANES, SUBLANES, LANES), lambda s: (s, 0, 0),
        hist_gdn, lambda s: (s, 0, 0),
        state_gdn, cw, alog_row, dtb_row, gnw,
        n_steps=dec_b // nb_gdn, nb=nb_gdn, seq=dec_t, group=dec_t, carry=False, pad_rows=0, preconv=False)

    qi = np.arange(WINDOW)[:, None]
    kj = np.arange(WINDOW)[None, :]
    mi = np.arange(N_META)[None, :]
    ti = np.arange(dec_t)[:, None]
    new_keys = kj - N_META
    id_arrays = [
        _bucket_ids(qi - kj, qi >= kj),
        _bucket_ids(qi - kj + WINDOW, kj > qi),
        _bucket_ids(qi + N_META - kj, kj < N_META),
        _bucket_ids(qi + N_META - kj + WINDOW, kj < N_META),
        _bucket_ids(ti + WINDOW - kj, kj > ti),
        _bucket_ids(np.where(new_keys < 0, PAST_LEN + ti - kj, ti - new_keys),
                    (new_keys < 0) | ((new_keys <= ti) & (new_keys < dec_t))),
        _bucket_ids(mi.T - mi, mi.T >= mi),
    ]
    bcur, bprev, bm0, bfar, bwin, bsmall, bmm = _bias_tables(rel_bias_table, id_arrays)
    sink_rows = lambda q: jnp.repeat(swa_sinks[0].reshape(SWA_KV_HEADS, SWA_GROUP), q, axis=1)[..., None]

    sq_blk = COL_SQ // SWA_WIDTH
    kv_blk = COL_KV // KV_WIDTH
    meta_blk = (n_small - N_META) // N_META
    full3 = lambda a: pl.BlockSpec(a.shape, lambda j: (0, 0, 0))
    sink_p = sink_rows(WINDOW)
    swa_rows = SWA_Q_BLOCKS * WINDOW
    swa_big = pl.pallas_call(
        _swa_prompt_kernel,
        grid=(seq // swa_rows,),
        in_specs=[
            pl.BlockSpec((swa_rows, SWA_WIDTH), lambda j: (j, sq_blk)),
            pl.BlockSpec((swa_rows, KV_WIDTH), lambda j: (j, kv_blk)),
            pl.BlockSpec((WINDOW, KV_WIDTH), lambda j: (jnp.maximum(j * SWA_Q_BLOCKS - 1, 0), kv_blk)),
            pl.BlockSpec((N_META, KV_WIDTH), lambda j: (meta_blk, kv_blk)),
            full3(bcur), full3(bprev), full3(bm0), full3(bfar), full3(sink_p),
        ],
        out_specs=pl.BlockSpec((swa_rows, SWA_WIDTH), lambda j: (j, 0)),
        out_shape=jax.ShapeDtypeStruct((seq, SWA_WIDTH), bf16),
        compiler_params=pltpu.CompilerParams(
            dimension_semantics=("arbitrary",), vmem_limit_bytes=VMEM_LIMIT),
        name="swa_prompt",
    )(proj_big, proj_big, proj_big, proj_small, bcur, bprev, bm0, bfar, sink_p)

    nb_swa = 8
    sink_s = sink_rows(dec_t)
    win = cache_swa_window_kv.reshape(dec_b, WINDOW * KV_SLOTS, HEAD_DIM)
    meta_kv = cache_swa_meta_kv.reshape(dec_b, N_META * KV_SLOTS, HEAD_DIM)
    swa_small, win_new = pl.pallas_call(
        functools.partial(_swa_sample_kernel, nb=nb_swa, seq=dec_t),
        grid=(dec_b // nb_swa,),
        in_specs=[
            pl.BlockSpec((nb_swa, dec_t, SWA_WIDTH), lambda j: (j, 0, sq_blk)),
            pl.BlockSpec((nb_swa, dec_t, KV_WIDTH), lambda j: (j, 0, kv_blk)),
            pl.BlockSpec((nb_swa, WINDOW * KV_SLOTS, HEAD_DIM), lambda j: (j, 0, 0)),
            pl.BlockSpec((nb_swa, N_META * KV_SLOTS, HEAD_DIM), lambda j: (j, 0, 0)),
            full3(bwin), full3(bsmall), full3(sink_s),
        ],
        out_specs=[pl.BlockSpec((nb_swa * dec_t, SWA_WIDTH), lambda j: (j, 0)),
                   pl.BlockSpec((nb_swa, WINDOW * KV_SLOTS, HEAD_DIM), lambda j: (j, 0, 0))],
        out_shape=[jax.ShapeDtypeStruct((n_dec, SWA_WIDTH), bf16),
                   jax.ShapeDtypeStruct((dec_b, WINDOW * KV_SLOTS, HEAD_DIM), f32)],
        compiler_params=pltpu.CompilerParams(
            dimension_semantics=("arbitrary",), vmem_limit_bytes=VMEM_LIMIT),
        name="swa_sample",
    )(small_groups, small_groups, win, meta_kv, bwin, bsmall, sink_s)

    sink_m = sink_rows(N_META)
    swa_meta = pl.pallas_call(
        _swa_meta_kernel,
        grid=(1,),
        in_specs=[
            pl.BlockSpec((N_META, SWA_WIDTH), lambda j: (meta_blk, sq_blk)),
            pl.BlockSpec((N_META, KV_WIDTH), lambda j: (meta_blk, kv_blk)),
            full3(bmm), full3(sink_m),
        ],
        out_specs=pl.BlockSpec((N_META, SWA_WIDTH), lambda j: (0, 0)),
        out_shape=jax.ShapeDtypeStruct((N_META, SWA_WIDTH), bf16),
        name="swa_meta",
    )(proj_small, proj_small, bmm, sink_m)

    nw_post = norm_mix_post[0].reshape(1, D_MODEL)
    nf_pre = norm_ffn_pre[0].reshape(1, D_MODEL)
    nf_post = norm_ffn_post[0].reshape(1, D_MODEL)
    fcw = ffn_conv_w[0]
    gdn_small_all = jnp.concatenate([gdn_small, gdn_meta], axis=0)
    swa_small_all = jnp.concatenate([swa_small, jnp.zeros((pad_rows, SWA_WIDTH), bf16), swa_meta], axis=0)
    h_small = _outproj(gdn_small_all, swa_small_all, x_small, wo, nw_post, tm=n_small // 2)
    hist_ffn = jnp.pad(state_ffn_conv[0], ((0, CHUNK // SUBLANES), (SUBLANES - (FFN_CONV - 1), 0), (0, 0)))
    y_small, g_small = _ffn(h_small, nf_pre, wg, wu, fcw, wd, nf_post, hist_ffn,
                            batch=True, tm=n_small // 2, tf=512)
    h_big = _outproj(gdn_big, swa_big, x_big, wo, nw_post, tm=512)
    y_big, g_tail = _ffn(h_big, nf_pre, wg, wu, fcw, wd, nf_post, g_small[n_small - SUBLANES:],
                         batch=False, tm=1024, tf=512)

    kv_shape = lambda n: (1, n, 2, SWA_KV_HEADS, HEAD_DIM)
    kv_small = proj_small[:, COL_KV:COL_KV + KV_WIDTH]
    y_prompt = y_big.reshape(1, seq, D_MODEL)
    y_sample = y_small[:n_dec].reshape(dec_b, dec_t, D_MODEL)
    p_meta_kv = kv_small[n_small - N_META:].reshape(kv_shape(N_META))[None]
    p_window_kv = proj_big[seq - WINDOW:, COL_KV:COL_KV + KV_WIDTH].reshape(kv_shape(WINDOW))[None]
    p_gdn_conv = qkv_tail[qkv_tail.shape[0] - (GDN_CONV - 1):].reshape(1, 1, GDN_CONV - 1, GDN_QKV)
    p_gdn_state = s_prompt[None]
    p_ffn_conv = g_tail[g_tail.shape[0] - (FFN_CONV - 1):].reshape(1, 1, FFN_CONV - 1, D_FF)
    s_window_kv = win_new.reshape(1, dec_b, WINDOW, 2, SWA_KV_HEADS, HEAD_DIM)
    s_gdn_conv = proj_small[:n_dec, :GDN_QKV].reshape(dec_b, dec_t, GDN_QKV)[:, dec_t - (GDN_CONV - 1):][None]
    s_gdn_state = s_sample
    s_ffn_conv = g_small[:n_dec].reshape(dec_b, dec_t, D_FF)[:, dec_t - (FFN_CONV - 1):][None]
    return (y_prompt, y_sample, p_meta_kv, p_window_kv, p_gdn_conv, p_gdn_state, p_ffn_conv,
            s_window_kv, s_gdn_conv, s_gdn_state, s_ffn_conv)
```

```python
import functools
import math

import numpy as np
import jax
import jax.numpy as jnp
from jax import lax
from jax.experimental import pallas as pl
from jax.experimental.pallas import tpu as pltpu

D_MODEL = 2048
HEAD_DIM = 128
GDN_HEADS = 8
GDN_WIDTH = GDN_HEADS * HEAD_DIM
GDN_QKV = 3 * GDN_WIDTH
SWA_HEADS = 8
SWA_KV_HEADS = 2
SWA_GROUP = SWA_HEADS // SWA_KV_HEADS
SWA_WIDTH = SWA_HEADS * HEAD_DIM
KV_SLOTS = 2 * SWA_KV_HEADS
KV_WIDTH = KV_SLOTS * HEAD_DIM
WINDOW = 128
N_META = 16
N_BUCKETS = 32
MAX_DISTANCE = 128
GDN_CONV = 4
FFN_CONV = 3
D_FF = 5632
EPS = 1e-6
PAST_LEN = 16384

SUBLANES = 8
LANES = 128

CHUNK = 128
assert CHUNK == HEAD_DIM == LANES
INV_BASE = 16
GDN_SEQ_CHUNKS = 2
INPROJ_ROW_BLOCKS = 1
FFN_ROW_BLOCKS = 2

COL_Z = GDN_QKV
COL_SQ = COL_Z + GDN_WIDTH
COL_KV = COL_SQ + SWA_WIDTH
PROJ_COLS = COL_KV + KV_WIDTH
GDN_COLS = COL_SQ

SWA_Q_BLOCKS = 4
NEG = -1e30
VMEM_LIMIT = 56 * 1024 * 1024

_NT = (((1,), (1,)), ((), ()))


def _dot(a, b):
    return jnp.dot(a.astype(jnp.bfloat16), b.astype(jnp.bfloat16), preferred_element_type=jnp.float32)


def _dot_nt(a, b):
    return lax.dot_general(a.astype(jnp.bfloat16), b.astype(jnp.bfloat16), _NT,
                           preferred_element_type=jnp.float32)


_dot_inv = _dot


def _dot_exact(a, b, dims=None):
    if dims is None:
        return jnp.dot(a, b, precision=lax.Precision.HIGHEST, preferred_element_type=jnp.float32)
    return lax.dot_general(a, b, dims, precision=lax.Precision.HIGHEST,
                           preferred_element_type=jnp.float32)


def _pack_w_in_kernel(a_ref, b_ref, o_ref, ba_ref, *, first_shifted, shift):
    j = pl.program_id(0)

    @pl.when(j < first_shifted)
    def _():
        o_ref[...] = a_ref[...].T.astype(o_ref.dtype)

    @pl.when(j >= first_shifted)
    def _():
        rows = jnp.concatenate([a_ref[shift:, :], b_ref[:shift, :]], axis=0)
        o_ref[...] = rows.T.astype(o_ref.dtype)

    @pl.when(j == first_shifted)
    def _():
        head = a_ref[:LANES, :]
        row = lax.broadcasted_iota(jnp.int32, head.shape, 0)
        ba_ref[...] = jnp.where(row < shift, head, 0.0).T.astype(ba_ref.dtype)


def _pack_w_in(w_in_t, n_ba, *, tn):
    n_blocks = PROJ_COLS // tn
    first_shifted = COL_SQ // tn
    assert COL_SQ % tn == 0 and w_in_t.shape[0] == PROJ_COLS + n_ba and n_ba % SUBLANES == 0
    return pl.pallas_call(
        functools.partial(_pack_w_in_kernel, first_shifted=first_shifted, shift=n_ba),
        grid=(n_blocks,),
        in_specs=[pl.BlockSpec((tn, D_MODEL), lambda j: (j, 0)),
                  pl.BlockSpec((tn, D_MODEL), lambda j: (jnp.maximum(j, first_shifted) + 1, 0))],
        out_specs=[pl.BlockSpec((D_MODEL, tn), lambda j: (0, j)),
                   pl.BlockSpec((D_MODEL, LANES), lambda j: (0, 0))],
        out_shape=[jax.ShapeDtypeStruct((D_MODEL, PROJ_COLS), jnp.bfloat16),
                   jax.ShapeDtypeStruct((D_MODEL, LANES), jnp.bfloat16)],
        compiler_params=pltpu.CompilerParams(
            dimension_semantics=("arbitrary",), vmem_limit_bytes=VMEM_LIMIT),
        name="pack_w_in",
    )(w_in_t, w_in_t)


def _rms_scale(x):
    return lax.rsqrt(jnp.mean(x * x, axis=-1, keepdims=True) + EPS)


def _silu(x):
    return x * jax.nn.sigmoid(x)


def _inproj_kernel(*refs, row_chunk, conv_tiles):
    if conv_tiles:
        (x_ref, nw_ref, w_ref, wba_ref, cw_ref, hist_ref,
         o_ref, ba_ref, tail_ref, xn_ref, xe_ref, carry_ref) = refs
    else:
        x_ref, nw_ref, w_ref, wba_ref, o_ref, ba_ref, xn_ref = refs
    i = pl.program_id(0)
    j = pl.program_id(1)
    tm, tn = o_ref.shape

    def normalize(rows):
        x = x_ref[rows, :]
        xn_ref[rows, :] = (x * _rms_scale(x) * nw_ref[...]).astype(jnp.bfloat16)

    def plain():
        o_ref[...] = jnp.dot(xn_ref[...], w_ref[...], preferred_element_type=jnp.float32)

    if not conv_tiles:
        @pl.when(j == 0)
        def _():
            def body(c, carry):
                normalize(pl.ds(pl.multiple_of(c * row_chunk, row_chunk), row_chunk))
                return carry
            lax.fori_loop(0, tm // row_chunk, body, 0)
            ba_ref[...] = jnp.dot(xn_ref[...], wba_ref[...], preferred_element_type=jnp.float32)
        plain()
        return
    pl.when(j >= conv_tiles)(plain)

    def conv_tile(first):
        @pl.when(i == 0)
        def _():
            carry_ref[pl.ds(j, 1)] = hist_ref[...].reshape(1, SUBLANES, tn)
        xe_ref[0:SUBLANES, :] = carry_ref[pl.ds(j, 1)].reshape(SUBLANES, tn)
        rb = tm // INPROJ_ROW_BLOCKS

        def raw_block(r):
            rows = slice(r * rb, (r + 1) * rb)
            if first:
                normalize(rows)
                ba_ref[rows, :] = jnp.dot(xn_ref[rows, :], wba_ref[...], preferred_element_type=jnp.float32)
            return jnp.dot(xn_ref[rows, :], w_ref[...], preferred_element_type=jnp.float32)

        nxt = raw_block(0)
        for r in range(INPROJ_ROW_BLOCKS):
            raw = nxt
            if r + 1 < INPROJ_ROW_BLOCKS:
                nxt = raw_block(r + 1)
            base = SUBLANES + r * rb
            xe_ref[base:base + rb, :] = raw
            conv = None
            for s in range(GDN_CONV):
                term = xe_ref[base - s:base - s + rb, :] * cw_ref[GDN_CONV - 1 - s:GDN_CONV - s, :]
                conv = term if conv is None else conv + term
            o_ref[r * rb:(r + 1) * rb, :] = _silu(conv)
        tail = xe_ref[tm:tm + SUBLANES, :]
        carry_ref[pl.ds(j, 1)] = tail.reshape(1, SUBLANES, tn)
        tail_ref[...] = tail

    pl.when(j == 0)(functools.partial(conv_tile, True))
    pl.when((j > 0) & (j < conv_tiles))(functools.partial(conv_tile, False))


def _inproj(x, nw, w, wba, conv=None, *, tm, tn, row_chunk):
    rows = x.shape[0]
    conv_tiles = GDN_QKV // tn if conv else 0
    in_specs = [
        pl.BlockSpec((tm, D_MODEL), lambda i, j: (i, 0)),
        pl.BlockSpec((1, D_MODEL), lambda i, j: (0, 0)),
        pl.BlockSpec((D_MODEL, tn), lambda i, j: (0, j)),
        pl.BlockSpec((D_MODEL, LANES), lambda i, j: (0, 0)),
    ]
    out_specs = [pl.BlockSpec((tm, tn), lambda i, j: (i, j)),
                 pl.BlockSpec((tm, LANES), lambda i, j: (i, 0))]
    out_shape = [jax.ShapeDtypeStruct((rows, PROJ_COLS), jnp.float32),
                 jax.ShapeDtypeStruct((rows, LANES), jnp.float32)]
    scratch = [pltpu.VMEM((tm, D_MODEL), jnp.bfloat16)]
    args = [x, nw, w, wba]
    if conv:
        conv_col = lambda i, j: (0, jnp.minimum(j, conv_tiles - 1))
        in_specs += [pl.BlockSpec((GDN_CONV, tn), conv_col), pl.BlockSpec((SUBLANES, tn), conv_col)]
        out_specs.append(pl.BlockSpec((SUBLANES, tn), lambda i, j: (i, jnp.minimum(j, conv_tiles - 1))))
        out_shape.append(jax.ShapeDtypeStruct((rows // tm * SUBLANES, GDN_QKV), jnp.float32))
        scratch += [pltpu.VMEM((SUBLANES + tm, tn), jnp.float32),
                    pltpu.VMEM((conv_tiles, SUBLANES, tn), jnp.float32)]
        args += list(conv)
    return pl.pallas_call(
        functools.partial(_inproj_kernel, row_chunk=row_chunk, conv_tiles=conv_tiles),
        grid=(rows // tm, PROJ_COLS // tn),
        in_specs=in_specs,
        out_specs=out_specs,
        out_shape=out_shape,
        scratch_shapes=scratch,
        compiler_params=pltpu.CompilerParams(
            dimension_semantics=("arbitrary", "arbitrary"), vmem_limit_bytes=VMEM_LIMIT),
        name="inproj_conv" if conv else "inproj",
    )(*args)


def _tri_inverse(lms, ri, ci):
    shift = INV_BASE.bit_length() - 1
    eye = (ri == ci).astype(jnp.float32)
    in_block = (ri >> shift) == (ci >> shift)
    ps = [jnp.where(in_block, lm, 0.0) for lm in lms]
    ts = [eye - p for p in ps]
    for _ in range(shift - 1):
        ps = [_dot_inv(p, p) for p in ps]
        ts = [t + _dot_inv(t, p) for t, p in zip(ts, ps)]
    size = INV_BASE
    while size < CHUNK:
        shift += 1
        in_pair = (ri >> shift) == (ci >> shift)
        off_mask = in_pair & jnp.logical_not(in_block)
        tos = [_dot_inv(t, jnp.where(off_mask, lm, 0.0)) for t, lm in zip(ts, lms)]
        ts = [t - _dot_inv(to, t) for t, to in zip(ts, tos)]
        in_block = in_pair
        size *= 2
    return ts


def _gdn_kernel(x_ref, ba_ref, hist_ref, s0_ref, cw_ref, alog_ref, dtb_ref, gnw_ref,
                o_ref, sout_ref, xe_ref, s_ref, *, nb, seq, group, carry, pad_rows, preconv):
    step = pl.program_id(0)
    rows = nb * seq
    n_chunks = rows // CHUNK
    n_groups = CHUNK // group
    gshift = group.bit_length() - 1

    if carry:
        @pl.when(step == 0)
        def _():
            s_ref[...] = s0_ref[0]
    if not preconv:
        if carry:
            @pl.when(step == 0)
            def _():
                xe_ref[:, 0:SUBLANES, :] = hist_ref[...]
        else:
            xe_ref[:, 0:SUBLANES, :] = hist_ref[...]
        xe_ref[:, SUBLANES:SUBLANES + seq, :] = x_ref[:, :, 0:GDN_QKV]

    seq_rows = min(seq, CHUNK)
    seqs_per_chunk = CHUNK // seq_rows

    def chunk_rows(ref, c, row_off, cols):
        if seq >= CHUNK:
            start = row_off + c * CHUNK
            return ref[0:1, start:start + CHUNK, cols]
        b0 = c * seqs_per_chunk
        return ref[b0:b0 + seqs_per_chunk, row_off:row_off + seq, cols]

    def conv_chunk(col, c):
        cols = slice(col, col + HEAD_DIM)
        if preconv:
            return chunk_rows(x_ref, c, 0, cols).reshape(CHUNK, HEAD_DIM)
        acc = None
        for s in range(GDN_CONV):
            term = chunk_rows(xe_ref, c, SUBLANES - s, cols) * cw_ref[GDN_CONV - 1 - s:GDN_CONV - s, cols]
            acc = term if acc is None else acc + term
        return _silu(acc).reshape(CHUNK, HEAD_DIM)

    ri = lax.broadcasted_iota(jnp.int32, (CHUNK, CHUNK), 0)
    ci = lax.broadcasted_iota(jnp.int32, (CHUNK, CHUNK), 1)
    same = (ri >> gshift) == (ci >> gshift)
    m_incl = same & (ri >= ci)
    m_strict = same & (ri > ci)
    f_incl = m_incl.astype(jnp.float32)
    f_same = same.astype(jnp.float32)
    lane = lax.broadcasted_iota(jnp.int32, (CHUNK, LANES), 1)
    row_in_chunk = lax.broadcasted_iota(jnp.int32, (CHUNK, LANES), 0)

    pre = []
    for c in range(n_chunks):
        bac = chunk_rows(ba_ref, c, 0, slice(0, LANES)).reshape(CHUNK, LANES)
        beta_all = jax.nn.sigmoid(bac)
        sp_in = bac + dtb_ref[...]
        softplus = jnp.maximum(sp_in, 0.0) + jnp.log1p(jnp.exp(-jnp.abs(sp_in)))
        g_all = -jnp.exp(alog_ref[...]) * softplus
        if pad_rows and c == 0:
            valid = row_in_chunk >= pad_rows
            beta_all = jnp.where(valid, beta_all, 0.0)
            g_all = jnp.where(valid, g_all, 0.0)
        g_all = jnp.where((lane >= GDN_HEADS) & (lane < 2 * GDN_HEADS), g_all, 0.0)
        gc_col = _dot_exact(f_incl, g_all)
        if n_groups == 1:
            gtot_col = jnp.broadcast_to(gc_col[CHUNK - 1:CHUNK, :], (CHUNK, LANES))
        else:
            gtot_col = _dot_exact(f_same, g_all)
        gc_row = gc_col.T

        for h in range(GDN_HEADS):
            qh = conv_chunk(h * HEAD_DIM, c)
            kh = conv_chunk(GDN_WIDTH + h * HEAD_DIM, c)
            vh = conv_chunk(2 * GDN_WIDTH + h * HEAD_DIM, c)
            qh = qh * lax.rsqrt(jnp.sum(qh * qh, -1, keepdims=True) + EPS) * (HEAD_DIM ** -0.5)
            kh = kh * lax.rsqrt(jnp.sum(kh * kh, -1, keepdims=True) + EPS)
            bcast = lambda col: jnp.broadcast_to(col, (CHUNK, HEAD_DIM))
            gcc = bcast(gc_col[:, GDN_HEADS + h:GDN_HEADS + h + 1])
            gtc = bcast(gtot_col[:, GDN_HEADS + h:GDN_HEADS + h + 1])
            beta = bcast(beta_all[:, h:h + 1])
            gcr = gc_row[GDN_HEADS + h:GDN_HEADS + h + 1, :]
            decay = jnp.exp(jnp.where(m_incl, gcc - gcr, NEG))
            kb = kh * beta
            egc = jnp.exp(gcc)
            pre.append(dict(
                c=c, h=h,
                lm=jnp.where(m_strict, _dot_nt(kb, kh) * decay, 0.0),
                qk=_dot_nt(qh, kh) * decay,
                rhs=jnp.concatenate([vh * beta, kb * egc], axis=1),
                qg=qh * egc,
                kd_t=(kh * jnp.exp(gtc - gcc)).T,
                gl=jnp.exp(gtc)))

    inverses = _tri_inverse([p["lm"] for p in pre], ri, ci)
    sols = [_dot_inv(t, p["rhs"]) for t, p in zip(inverses, pre)]

    for c in range(n_chunks):
        r0 = c * CHUNK
        items = [(p, sol) for p, sol in zip(pre, sols) if p["c"] == c]
        state = lambda h, b: s_ref[h] if carry else s0_ref[c * n_groups + b, h]
        ws, qs = [], []
        for p, sol in items:
            w = sol[:, HEAD_DIM:]
            ws_parts, qs_parts = [], []
            for b in range(n_groups):
                g0 = b * group
                wq = jnp.concatenate([w[g0:g0 + group], p["qg"][g0:g0 + group]], axis=0)
                res = _dot(wq, state(p["h"], b))
                ws_parts.append(res[:group])
                qs_parts.append(res[group:])
            ws.append(ws_parts[0] if n_groups == 1 else jnp.concatenate(ws_parts, axis=0))
            qs.append(qs_parts[0] if n_groups == 1 else jnp.concatenate(qs_parts, axis=0))
        v_new = [sol[:, :HEAD_DIM] - w for (p, sol), w in zip(items, ws)]
        o = [a + _dot(p["qk"], vn) for a, (p, sol), vn in zip(qs, items, v_new)]
        for (p, sol), vn in zip(items, v_new):
            h = p["h"]
            for b in range(n_groups):
                g0 = b * group
                kd_b = p["kd_t"] if n_groups == 1 else jnp.where((ci >> gshift) == b, p["kd_t"], 0.0)
                st = state(h, b) * p["gl"][g0:g0 + 1, :] + _dot(kd_b, vn)
                if carry:
                    s_ref[h] = st
                else:
                    sout_ref[c * n_groups + b, h] = st
        for (p, sol), oh in zip(items, o):
            h = p["h"]
            z = chunk_rows(x_ref, c, 0, slice(COL_Z + h * HEAD_DIM, COL_Z + (h + 1) * HEAD_DIM)).reshape(CHUNK, HEAD_DIM)
            y = oh * lax.rsqrt(jnp.mean(oh * oh, -1, keepdims=True) + EPS) * gnw_ref[...] * _silu(z)
            o_ref[r0:r0 + CHUNK, h * HEAD_DIM:(h + 1) * HEAD_DIM] = y.astype(o_ref.dtype)

    if carry:
        if not preconv:
            xe_ref[:, 0:SUBLANES, :] = xe_ref[:, seq:seq + SUBLANES, :]

        @pl.when(step == pl.num_programs(0) - 1)
        def _():
            sout_ref[0] = s_ref[...]


def _gdn(x3, ba3, x_idx, hist, hist_idx, s0, cw, alog_row, dtb_row, gnw, *,
         n_steps, nb, seq, group, carry, pad_rows, preconv):
    rows = nb * seq
    if carry:
        state_spec = pl.BlockSpec((1, GDN_HEADS, HEAD_DIM, HEAD_DIM), lambda s: (0, 0, 0, 0))
        state_shape = (1, GDN_HEADS, HEAD_DIM, HEAD_DIM)
    else:
        n_states = rows // group
        state_spec = pl.BlockSpec((None, n_states, GDN_HEADS, HEAD_DIM, HEAD_DIM), lambda s: (0, s, 0, 0, 0))
        state_shape = (1, n_steps * n_states, GDN_HEADS, HEAD_DIM, HEAD_DIM)
    full = lambda shape: pl.BlockSpec(shape, lambda s: (0,) * len(shape))
    return pl.pallas_call(
        functools.partial(_gdn_kernel, nb=nb, seq=seq, group=group, carry=carry, pad_rows=pad_rows,
                          preconv=preconv),
        grid=(n_steps,),
        in_specs=[
            pl.BlockSpec((nb, seq, GDN_COLS), x_idx),
            pl.BlockSpec((nb, seq, LANES), x_idx),
            pl.BlockSpec((nb, SUBLANES, GDN_QKV), hist_idx),
            state_spec,
            full((GDN_CONV, GDN_QKV)),
            full((1, LANES)),
            full((1, LANES)),
            full((1, HEAD_DIM)),
        ],
        out_specs=[
            pl.BlockSpec((rows, GDN_WIDTH), lambda s: (s, 0)),
            state_spec,
        ],
        out_shape=[
            jax.ShapeDtypeStruct((n_steps * rows, GDN_WIDTH), jnp.bfloat16),
            jax.ShapeDtypeStruct(state_shape, jnp.float32),
        ],
        scratch_shapes=[
            pltpu.VMEM((nb, SUBLANES + (SUBLANES if preconv else seq), GDN_QKV), jnp.float32),
            pltpu.VMEM((GDN_HEADS, HEAD_DIM, HEAD_DIM), jnp.float32),
        ],
        compiler_params=pltpu.CompilerParams(
            dimension_semantics=("arbitrary",), vmem_limit_bytes=VMEM_LIMIT),
        name="gdn_seq" if carry else "gdn_batch",
    )(x3, ba3, hist, s0, cw, alog_row, dtb_row, gnw)


def _t5_bucket_np(dist):
    n = np.maximum(dist, 0)
    exact = N_BUCKETS // 2
    large = exact + (np.log(np.maximum(n, 1).astype(np.float32) / exact)
                     / math.log(MAX_DISTANCE / exact) * (N_BUCKETS - exact)).astype(np.int32)
    return np.where(n < exact, n, np.minimum(large, N_BUCKETS - 1)).astype(np.int32)


def _bucket_ids(dist, valid):
    return np.where(valid, _t5_bucket_np(dist), -1).astype(np.int32)


def _bias_kernel(table_ref, *refs):
    n = len(refs) // 2
    for ids_ref, out_ref in zip(refs[:n], refs[n:]):
        ids = ids_ref[...]
        nq = ids.shape[0]
        for head in range(SWA_HEADS):
            def body(b, acc):
                return jnp.where(ids == b, table_ref[b, head], acc)
            acc = lax.fori_loop(0, N_BUCKETS, body, jnp.full(ids.shape, NEG, jnp.float32))
            kh, g = divmod(head, SWA_GROUP)
            out_ref[kh, g * nq:(g + 1) * nq, :] = acc


def _bias_tables(rel_table, id_arrays):
    out_shapes = [jax.ShapeDtypeStruct((SWA_KV_HEADS, SWA_GROUP * a.shape[0], a.shape[1]), jnp.float32)
                  for a in id_arrays]
    vmem = pl.BlockSpec(memory_space=pltpu.VMEM)
    return pl.pallas_call(
        _bias_kernel,
        in_specs=[pl.BlockSpec(memory_space=pltpu.SMEM)] + [vmem] * len(id_arrays),
        out_specs=[vmem] * len(id_arrays),
        out_shape=out_shapes,
        name="swa_bias",
    )(rel_table, *[jnp.asarray(a) for a in id_arrays])


def _attend(problems):
    scale = HEAD_DIM ** -0.5
    scores = [[_dot_nt(q, k) * scale + b for k, b in zip(keys, biases)]
              for q, keys, _, biases, _ in problems]
    maxes = []
    for (_, _, _, _, sink), segs in zip(problems, scores):
        m = sink
        for s in segs:
            m = jnp.maximum(m, jnp.max(s, axis=-1, keepdims=True))
        maxes.append(m)
    probs = [[jnp.exp(s - m) for s in segs] for segs, m in zip(scores, maxes)]
    outs = []
    for (_, _, values, _, sink), ps, m in zip(problems, probs, maxes):
        acc = None
        for p, v in zip(ps, values):
            v_ones = jnp.concatenate([v, jnp.ones((v.shape[0], HEAD_DIM), v.dtype)], axis=1)
            pv = _dot(p, v_ones)
            acc = pv if acc is None else acc + pv
        den = acc[:, HEAD_DIM:] + jnp.exp(sink - m)
        outs.append(acc[:, :HEAD_DIM] / den)
    return outs


def _group_queries(q_rows, kh):
    return jnp.concatenate(
        [q_rows((kh * SWA_GROUP + g) * HEAD_DIM, (kh * SWA_GROUP + g + 1) * HEAD_DIM)
         for g in range(SWA_GROUP)], axis=0)


def _pad_keys(rows):
    return jnp.concatenate([rows, jnp.zeros((WINDOW - rows.shape[0], rows.shape[1]), rows.dtype)], axis=0)


def _k_cols(kh):
    return slice(kh * HEAD_DIM, (kh + 1) * HEAD_DIM)


def _v_cols(kh):
    return slice((SWA_KV_HEADS + kh) * HEAD_DIM, (SWA_KV_HEADS + kh + 1) * HEAD_DIM)


def _swa_prompt_kernel(q_ref, kvc_ref, kvp_ref, kvm_ref, bcur_ref, bprev_ref, bm0_ref, bfar_ref,
                       sink_ref, o_ref):
    first = pl.program_id(0) == 0
    problems = []
    for blk in range(SWA_Q_BLOCKS):
        rows = slice(blk * WINDOW, (blk + 1) * WINDOW)
        prev_ref, prev_rows = (kvp_ref, slice(0, WINDOW)) if blk == 0 else (
            kvc_ref, slice((blk - 1) * WINDOW, blk * WINDOW))
        for kh in range(SWA_KV_HEADS):
            ks, vs = _k_cols(kh), _v_cols(kh)
            first_block = first if blk == 0 else False
            b_prev = jnp.where(first_block, NEG, bprev_ref[kh])
            b_meta = jnp.where(first_block, bm0_ref[kh], bfar_ref[kh])
            problems.append((_group_queries(lambda a, b, rows=rows: q_ref[rows, a:b], kh),
                             [kvc_ref[rows, ks], prev_ref[prev_rows, ks], _pad_keys(kvm_ref[:, ks])],
                             [kvc_ref[rows, vs], prev_ref[prev_rows, vs], _pad_keys(kvm_ref[:, vs])],
                             [bcur_ref[kh], b_prev, b_meta], sink_ref[kh]))
    for i, o in enumerate(_attend(problems)):
        blk, kh = divmod(i, SWA_KV_HEADS)
        for g in range(SWA_GROUP):
            head = kh * SWA_GROUP + g
            o_ref[blk * WINDOW:(blk + 1) * WINDOW, head * HEAD_DIM:(head + 1) * HEAD_DIM] = (
                o[g * WINDOW:(g + 1) * WINDOW].astype(o_ref.dtype))


def _swa_meta_kernel(q_ref, kv_ref, bias_ref, sink_ref, o_ref):
    problems = [(_group_queries(lambda a, b: q_ref[:, a:b], kh), [kv_ref[:, _k_cols(kh)]],
                 [kv_ref[:, _v_cols(kh)]], [bias_ref[kh]], sink_ref[kh])
                for kh in range(SWA_KV_HEADS)]
    for kh, o in enumerate(_attend(problems)):
        for g in range(SWA_GROUP):
            head = kh * SWA_GROUP + g
            o_ref[:, head * HEAD_DIM:(head + 1) * HEAD_DIM] = (
                o[g * N_META:(g + 1) * N_META].astype(o_ref.dtype))


def _swa_sample_kernel(q_ref, kvn_ref, win_ref, meta_ref, bwin_ref, bsmall_ref, sink_ref,
                       o_ref, wout_ref, *, nb, seq):
    cached = lambda ref, b, slot, n: ref[b, pl.ds(slot, n, stride=KV_SLOTS), :]
    keep = (WINDOW - seq) * KV_SLOTS
    wout_ref[:, 0:keep, :] = win_ref[:, seq * KV_SLOTS:WINDOW * KV_SLOTS, :]
    for slot in range(KV_SLOTS):
        wout_ref[:, pl.ds(keep + slot, seq, stride=KV_SLOTS), :] = (
            kvn_ref[:, :, slot * HEAD_DIM:(slot + 1) * HEAD_DIM])
    problems = []
    for b in range(nb):
        for kh in range(SWA_KV_HEADS):
            ks, vs = _k_cols(kh), _v_cols(kh)
            k_small = _pad_keys(jnp.concatenate([cached(meta_ref, b, kh, N_META), kvn_ref[b, :, ks]], axis=0))
            v_small = _pad_keys(jnp.concatenate(
                [cached(meta_ref, b, SWA_KV_HEADS + kh, N_META), kvn_ref[b, :, vs]], axis=0))
            problems.append((_group_queries(lambda a, c, b=b: q_ref[b, :, a:c], kh),
                             [cached(win_ref, b, kh, WINDOW), k_small],
                             [cached(win_ref, b, SWA_KV_HEADS + kh, WINDOW), v_small],
                             [bwin_ref[kh], bsmall_ref[kh]], sink_ref[kh]))
    outs = _attend(problems)
    for head in range(SWA_HEADS):
        kh, g = divmod(head, SWA_GROUP)
        rows = [outs[b * SWA_KV_HEADS + kh][g * seq:(g + 1) * seq] for b in range(nb)]
        o_ref[:, head * HEAD_DIM:(head + 1) * HEAD_DIM] = jnp.concatenate(rows, axis=0).astype(o_ref.dtype)


def _outproj_kernel(g_ref, s_ref, h_ref, wo_ref, nw_ref, o_ref):
    mix = (jnp.dot(g_ref[...], wo_ref[0:GDN_WIDTH, :], preferred_element_type=jnp.float32)
           + jnp.dot(s_ref[...], wo_ref[GDN_WIDTH:, :], preferred_element_type=jnp.float32))
    o_ref[...] = h_ref[...] + mix * _rms_scale(mix) * nw_ref[...]


def _outproj(g, s, h, wo, nw, *, tm):
    rows = h.shape[0]
    return pl.pallas_call(
        _outproj_kernel,
        grid=(rows // tm,),
        in_specs=[
            pl.BlockSpec((tm, GDN_WIDTH), lambda i: (i, 0)),
            pl.BlockSpec((tm, SWA_WIDTH), lambda i: (i, 0)),
            pl.BlockSpec((tm, D_MODEL), lambda i: (i, 0)),
            pl.BlockSpec((D_MODEL, D_MODEL), lambda i: (0, 0)),
            pl.BlockSpec((1, D_MODEL), lambda i: (0, 0)),
        ],
        out_specs=pl.BlockSpec((tm, D_MODEL), lambda i: (i, 0)),
        out_shape=jax.ShapeDtypeStruct((rows, D_MODEL), jnp.float32),
        compiler_params=pltpu.CompilerParams(
            dimension_semantics=("arbitrary",), vmem_limit_bytes=VMEM_LIMIT),
        name="outproj",
    )(g, s, h, wo, nw)


def _ffn_kernel(*refs, batch, tm, tf):
    if batch:
        (h_ref, nw_pre_ref, wg_ref, wu_ref, cw_ref, wd_ref, nw_post_ref, hist_ref,
         y_ref, graw_ref, xn_ref, xe_ref) = refs
    else:
        (h_ref, nw_pre_ref, wg_ref, wu_ref, cw_ref, wd_ref, nw_post_ref, hist_ref,
         y_ref, graw_ref, xn_ref, xe_ref, carry_ref) = refs
    i = pl.program_id(0)
    j = pl.program_id(1)
    last_j = pl.num_programs(1) - 1
    rb = tm // FFN_ROW_BLOCKS

    if batch:
        xe_ref[:, 0:SUBLANES, :] = hist_ref[...]
    else:
        @pl.when(i == 0)
        def _():
            carry_ref[pl.ds(j, 1)] = hist_ref[...].reshape(1, SUBLANES, tf)
        xe_ref[:, 0:SUBLANES, :] = carry_ref[pl.ds(j, 1)]

    def step(first, last):
        def gate_up(r):
            rows = slice(r * rb, (r + 1) * rb)
            if first:
                h = h_ref[rows, :]
                xn_ref[rows, :] = (h * _rms_scale(h) * nw_pre_ref[...]).astype(jnp.bfloat16)
            xn = xn_ref[rows, :]
            return (jnp.dot(xn, wg_ref[...], preferred_element_type=jnp.float32),
                    jnp.dot(xn, wu_ref[...], preferred_element_type=jnp.float32))

        nxt = gate_up(0)
        for r in range(FFN_ROW_BLOCKS):
            rows = slice(r * rb, (r + 1) * rb)
            gate, up = nxt
            if r + 1 < FFN_ROW_BLOCKS:
                nxt = gate_up(r + 1)
            if batch:
                seqs = slice(r * rb // SUBLANES, (r + 1) * rb // SUBLANES)
                graw_ref[rows, :] = gate
                xe_ref[seqs, SUBLANES:2 * SUBLANES, :] = gate.reshape(rb // SUBLANES, SUBLANES, tf)
                taps = [xe_ref[seqs, SUBLANES - s:2 * SUBLANES - s, :] for s in range(FFN_CONV)]
            else:
                base = SUBLANES + r * rb
                xe_ref[:, base:base + rb, :] = gate.reshape(1, rb, tf)
                taps = [xe_ref[:, base - s:base - s + rb, :] for s in range(FFN_CONV)]
            conv = None
            for s, tap in enumerate(taps):
                term = tap * cw_ref[FFN_CONV - 1 - s:FFN_CONV - s, :]
                conv = term if conv is None else conv + term
            act = (_silu(conv.reshape(rb, tf)) * up).astype(jnp.bfloat16)
            down = jnp.dot(act, wd_ref[...], preferred_element_type=jnp.float32)
            if first:
                y_ref[rows, :] = down
            elif last:
                y = y_ref[rows, :] + down
                y_ref[rows, :] = h_ref[rows, :] + y * _rms_scale(y) * nw_post_ref[...]
            else:
                y_ref[rows, :] += down
        if not batch:
            tail = xe_ref[:, tm:tm + SUBLANES, :]
            carry_ref[pl.ds(j, 1)] = tail
            graw_ref[...] = tail.reshape(SUBLANES, tf)

    pl.when(j == 0)(functools.partial(step, True, False))
    pl.when((j > 0) & (j < last_j))(functools.partial(step, False, False))
    pl.when(j == last_j)(functools.partial(step, False, True))


def _ffn(h, nw_pre, wg, wu, cw, wd, nw_post, hist, *, batch, tm, tf):
    rows = h.shape[0]
    nj = D_FF // tf
    in_specs = [
        pl.BlockSpec((tm, D_MODEL), lambda i, j: (i, 0)),
        pl.BlockSpec((1, D_MODEL), lambda i, j: (0, 0)),
        pl.BlockSpec((D_MODEL, tf), lambda i, j: (0, j)),
        pl.BlockSpec((D_MODEL, tf), lambda i, j: (0, j)),
        pl.BlockSpec((FFN_CONV, tf), lambda i, j: (0, j)),
        pl.BlockSpec((tf, D_MODEL), lambda i, j: (j, 0)),
        pl.BlockSpec((1, D_MODEL), lambda i, j: (0, 0)),
    ]
    args = [h, nw_pre, wg, wu, cw, wd, nw_post, hist]
    scratch = [pltpu.VMEM((tm, D_MODEL), jnp.bfloat16)]
    if batch:
        in_specs.append(pl.BlockSpec((tm // SUBLANES, SUBLANES, tf), lambda i, j: (i, 0, j)))
        graw_spec = pl.BlockSpec((tm, tf), lambda i, j: (i, j))
        graw_shape = jax.ShapeDtypeStruct((rows, D_FF), jnp.float32)
        scratch.append(pltpu.VMEM((tm // SUBLANES, 2 * SUBLANES, tf), jnp.float32))
    else:
        in_specs.append(pl.BlockSpec((SUBLANES, tf), lambda i, j: (0, j)))
        graw_spec = pl.BlockSpec((SUBLANES, tf), lambda i, j: (i, j))
        graw_shape = jax.ShapeDtypeStruct((rows // tm * SUBLANES, D_FF), jnp.float32)
        scratch.append(pltpu.VMEM((1, SUBLANES + tm, tf), jnp.float32))
        scratch.append(pltpu.VMEM((nj, SUBLANES, tf), jnp.float32))
    return pl.pallas_call(
        functools.partial(_ffn_kernel, batch=batch, tm=tm, tf=tf),
        grid=(rows // tm, nj),
        in_specs=in_specs,
        out_specs=[pl.BlockSpec((tm, D_MODEL), lambda i, j: (i, 0)), graw_spec],
        out_shape=[jax.ShapeDtypeStruct((rows, D_MODEL), jnp.float32), graw_shape],
        scratch_shapes=scratch,
        compiler_params=pltpu.CompilerParams(
            dimension_semantics=("arbitrary", "arbitrary"), vmem_limit_bytes=VMEM_LIMIT),
        name="ffn_batch" if batch else "ffn_seq",
    )(*args)


def kernel(x_prompt, x_sample, cache_swa_meta_kv, cache_swa_window_kv, state_gdn_conv, state_gdn, state_ffn_conv, meta_tokens, rel_bias_table, w_in, gdn_conv_w, gdn_a_log, gdn_dt_bias, gdn_norm_w, swa_sinks, w_out, norm_mix_pre, norm_mix_post, norm_ffn_pre, norm_ffn_post, ffn_w_gate, ffn_w_up, ffn_conv_w, ffn_w_down):
    f32, bf16 = jnp.float32, jnp.bfloat16
    seq = x_prompt.shape[1]
    dec_b, dec_t = x_sample.shape[0], x_sample.shape[1]
    n_dec = dec_b * dec_t
    assert x_prompt.shape[0] == 1 and seq % CHUNK == 0 and dec_t == SUBLANES and n_dec % CHUNK == 0

    w_in_p, w_ba = _pack_w_in(jnp.transpose(w_in[0]), 2 * GDN_HEADS, tn=512)
    wo = w_out[0].astype(bf16)
    wg = ffn_w_gate[0].astype(bf16)
    wu = ffn_w_up[0].astype(bf16)
    wd = ffn_w_down[0].astype(bf16)
    lane_pad = lambda v: jnp.pad(v.reshape(1, GDN_HEADS), ((0, 0), (GDN_HEADS, LANES - 2 * GDN_HEADS)))
    alog_row = lane_pad(gdn_a_log[0])
    dtb_row = lane_pad(gdn_dt_bias[0])
    gnw = gdn_norm_w[0].reshape(1, HEAD_DIM)

    pad_rows = CHUNK - N_META
    n_small = n_dec + CHUNK
    x_big = x_prompt.reshape(seq, D_MODEL)
    x_small = jnp.concatenate(
        [x_sample.reshape(n_dec, D_MODEL), jnp.zeros((pad_rows, D_MODEL), f32), meta_tokens.astype(f32)], axis=0)
    nw = norm_mix_pre[0].reshape(1, D_MODEL)
    cw = gdn_conv_w[0]
    proj_small, ba_small = _inproj(x_small, nw, w_in_p, w_ba, tm=n_small, tn=512, row_chunk=128)
    proj_big, ba_big, qkv_tail = _inproj(
        x_big, nw, w_in_p, w_ba, (cw, proj_small[n_small - SUBLANES:, :GDN_QKV]),
        tm=1024, tn=512, row_chunk=128)

    small_chunks = proj_small.reshape(n_small // CHUNK, CHUNK, PROJ_COLS)
    small_groups = proj_small.reshape(n_small // SUBLANES, SUBLANES, PROJ_COLS)
    last_chunk = n_small // CHUNK - 1
    gdn_meta, s_meta = _gdn(
        small_chunks, ba_small.reshape(n_small // CHUNK, CHUNK, LANES), lambda s: (last_chunk, 0, 0),
        jnp.zeros((1, SUBLANES, GDN_QKV), f32), lambda s: (0, 0, 0),
        jnp.zeros((1, GDN_HEADS, HEAD_DIM, HEAD_DIM), f32), cw, alog_row, dtb_row, gnw,
        n_steps=1, nb=1, seq=CHUNK, group=CHUNK, carry=True, pad_rows=pad_rows, preconv=False)
    gdn_big, s_prompt = _gdn(
        proj_big.reshape(1, seq, PROJ_COLS), ba_big.reshape(1, seq, LANES), lambda s: (0, s, 0),
        jnp.zeros((1, SUBLANES, GDN_QKV), f32), lambda s: (0, 0, 0),
        s_meta, cw, alog_row, dtb_row, gnw,
        n_steps=seq // (GDN_SEQ_CHUNKS * CHUNK), nb=1, seq=GDN_SEQ_CHUNKS * CHUNK, group=CHUNK,
        carry=True, pad_rows=0, preconv=True)
    hist_gdn = jnp.pad(state_gdn_conv[0], ((0, 0), (SUBLANES - (GDN_CONV - 1), 0), (0, 0)))
    nb_gdn = CHUNK // dec_t
    gdn_small, s_sample = _gdn(
        small_groups, ba_small.reshape(n_small // SUBLANES, SUBLANES, LANES), lambda s: (s, 0, 0),
        hist_gdn, lambda s: (s, 0, 0),
        state_gdn, cw, alog_row, dtb_row, gnw,
        n_steps=dec_b // nb_gdn, nb=nb_gdn, seq=dec_t, group=dec_t, carry=False, pad_rows=0, preconv=False)

    qi = np.arange(WINDOW)[:, None]
    kj = np.arange(WINDOW)[None, :]
    mi = np.arange(N_META)[None, :]
    ti = np.arange(dec_t)[:, None]
    new_keys = kj - N_META
    id_arrays = [
        _bucket_ids(qi - kj, qi >= kj),
        _bucket_ids(qi - kj + WINDOW, kj > qi),
        _bucket_ids(qi + N_META - kj, kj < N_META),
        _bucket_ids(qi + N_META - kj + WINDOW, kj < N_META),
        _bucket_ids(ti + WINDOW - kj, kj > ti),
        _bucket_ids(np.where(new_keys < 0, PAST_LEN + ti - kj, ti - new_keys),
                    (new_keys < 0) | ((new_keys <= ti) & (new_keys < dec_t))),
        _bucket_ids(mi.T - mi, mi.T >= mi),
    ]
    bcur, bprev, bm0, bfar, bwin, bsmall, bmm = _bias_tables(rel_bias_table, id_arrays)
    sink_rows = lambda q: jnp.repeat(swa_sinks[0].reshape(SWA_KV_HEADS, SWA_GROUP), q, axis=1)[..., None]

    sq_blk = COL_SQ // SWA_WIDTH
    kv_blk = COL_KV // KV_WIDTH
    meta_blk = (n_small - N_META) // N_META
    full3 = lambda a: pl.BlockSpec(a.shape, lambda j: (0, 0, 0))
    sink_p = sink_rows(WINDOW)
    swa_rows = SWA_Q_BLOCKS * WINDOW
    swa_big = pl.pallas_call(
        _swa_prompt_kernel,
        grid=(seq // swa_rows,),
        in_specs=[
            pl.BlockSpec((swa_rows, SWA_WIDTH), lambda j: (j, sq_blk)),
            pl.BlockSpec((swa_rows, KV_WIDTH), lambda j: (j, kv_blk)),
            pl.BlockSpec((WINDOW, KV_WIDTH), lambda j: (jnp.maximum(j * SWA_Q_BLOCKS - 1, 0), kv_blk)),
            pl.BlockSpec((N_META, KV_WIDTH), lambda j: (meta_blk, kv_blk)),
            full3(bcur), full3(bprev), full3(bm0), full3(bfar), full3(sink_p),
        ],
        out_specs=pl.BlockSpec((swa_rows, SWA_WIDTH), lambda j: (j, 0)),
        out_shape=jax.ShapeDtypeStruct((seq, SWA_WIDTH), bf16),
        compiler_params=pltpu.CompilerParams(
            dimension_semantics=("arbitrary",), vmem_limit_bytes=VMEM_LIMIT),
        name="swa_prompt",
    )(proj_big, proj_big, proj_big, proj_small, bcur, bprev, bm0, bfar, sink_p)

    nb_swa = 8
    sink_s = sink_rows(dec_t)
    win = cache_swa_window_kv.reshape(dec_b, WINDOW * KV_SLOTS, HEAD_DIM)
    meta_kv = cache_swa_meta_kv.reshape(dec_b, N_META * KV_SLOTS, HEAD_DIM)
    swa_small, win_new = pl.pallas_call(
        functools.partial(_swa_sample_kernel, nb=nb_swa, seq=dec_t),
        grid=(dec_b // nb_swa,),
        in_specs=[
            pl.BlockSpec((nb_swa, dec_t, SWA_WIDTH), lambda j: (j, 0, sq_blk)),
            pl.BlockSpec((nb_swa, dec_t, KV_WIDTH), lambda j: (j, 0, kv_blk)),
            pl.BlockSpec((nb_swa, WINDOW * KV_SLOTS, HEAD_DIM), lambda j: (j, 0, 0)),
            pl.BlockSpec((nb_swa, N_META * KV_SLOTS, HEAD_DIM), lambda j: (j, 0, 0)),
            full3(bwin), full3(bsmall), full3(sink_s),
        ],
        out_specs=[pl.BlockSpec((nb_swa * dec_t, SWA_WIDTH), lambda j: (j, 0)),
                   pl.BlockSpec((nb_swa, WINDOW * KV_SLOTS, HEAD_DIM), lambda j: (j, 0, 0))],
        out_shape=[jax.ShapeDtypeStruct((n_dec, SWA_WIDTH), bf16),
                   jax.ShapeDtypeStruct((dec_b, WINDOW * KV_SLOTS, HEAD_DIM), f32)],
        compiler_params=pltpu.CompilerParams(
            dimension_semantics=("arbitrary",), vmem_limit_bytes=VMEM_LIMIT),
        name="swa_sample",
    )(small_groups, small_groups, win, meta_kv, bwin, bsmall, sink_s)

    sink_m = sink_rows(N_META)
    swa_meta = pl.pallas_call(
        _swa_meta_kernel,
        grid=(1,),
        in_specs=[
            pl.BlockSpec((N_META, SWA_WIDTH), lambda j: (meta_blk, sq_blk)),
            pl.BlockSpec((N_META, KV_WIDTH), lambda j: (meta_blk, kv_blk)),
            full3(bmm), full3(sink_m),
        ],
        out_specs=pl.BlockSpec((N_META, SWA_WIDTH), lambda j: (0, 0)),
        out_shape=jax.ShapeDtypeStruct((N_META, SWA_WIDTH), bf16),
        name="swa_meta",
    )(proj_small, proj_small, bmm, sink_m)

    nw_post = norm_mix_post[0].reshape(1, D_MODEL)
    nf_pre = norm_ffn_pre[0].reshape(1, D_MODEL)
    nf_post = norm_ffn_post[0].reshape(1, D_MODEL)
    fcw = ffn_conv_w[0]
    gdn_small_all = jnp.concatenate([gdn_small, gdn_meta], axis=0)
    swa_small_all = jnp.concatenate([swa_small, jnp.zeros((pad_rows, SWA_WIDTH), bf16), swa_meta], axis=0)
    h_small = _outproj(gdn_small_all, swa_small_all, x_small, wo, nw_post, tm=n_small // 2)
    hist_ffn = jnp.pad(state_ffn_conv[0], ((0, CHUNK // SUBLANES), (SUBLANES - (FFN_CONV - 1), 0), (0, 0)))
    y_small, g_small = _ffn(h_small, nf_pre, wg, wu, fcw, wd, nf_post, hist_ffn,
                            batch=True, tm=n_small // 2, tf=512)
    h_big = _outproj(gdn_big, swa_big, x_big, wo, nw_post, tm=512)
    y_big, g_tail = _ffn(h_big, nf_pre, wg, wu, fcw, wd, nf_post, g_small[n_small - SUBLANES:],
                         batch=False, tm=1024, tf=512)

    kv_shape = lambda n: (1, n, 2, SWA_KV_HEADS, HEAD_DIM)
    kv_small = proj_small[:, COL_KV:COL_KV + KV_WIDTH]
    y_prompt = y_big.reshape(1, seq, D_MODEL)
    y_sample = y_small[:n_dec].reshape(dec_b, dec_t, D_MODEL)
    p_meta_kv = kv_small[n_small - N_META:].reshape(kv_shape(N_META))[None]
    p_window_kv = proj_big[seq - WINDOW:, COL_KV:COL_KV + KV_WIDTH].reshape(kv_shape(WINDOW))[None]
    p_gdn_conv = qkv_tail[qkv_tail.shape[0] - (GDN_CONV - 1):].reshape(1, 1, GDN_CONV - 1, GDN_QKV)
    p_gdn_state = s_prompt[None]
    p_ffn_conv = g_tail[g_tail.shape[0] - (FFN_CONV - 1):].reshape(1, 1, FFN_CONV - 1, D_FF)
    s_window_kv = win_new.reshape(1, dec_b, WINDOW, 2, SWA_KV_HEADS, HEAD_DIM)
    s_gdn_conv = proj_small[:n_dec, :GDN_QKV].reshape(dec_b, dec_t, GDN_QKV)[:, dec_t - (GDN_CONV - 1):][None]
    s_gdn_state = s_sample
    s_ffn_conv = g_small[:n_dec].reshape(dec_b, dec_t, D_FF)[:, dec_t - (FFN_CONV - 1):][None]
    return (y_prompt, y_sample, p_meta_kv, p_window_kv, p_gdn_conv, p_gdn_state, p_ffn_conv,
            s_window_kv, s_gdn_conv, s_gdn_state, s_ffn_conv)
```

```python
import functools
import math

import numpy as np
import jax
import jax.numpy as jnp
from jax import lax
from jax.experimental import pallas as pl
from jax.experimental.pallas import tpu as pltpu

D_MODEL = 2048
HEAD_DIM = 128
GDN_HEADS = 8
GDN_WIDTH = GDN_HEADS * HEAD_DIM
GDN_QKV = 3 * GDN_WIDTH
SWA_HEADS = 8
SWA_KV_HEADS = 2
SWA_GROUP = SWA_HEADS // SWA_KV_HEADS
SWA_WIDTH = SWA_HEADS * HEAD_DIM
KV_SLOTS = 2 * SWA_KV_HEADS
KV_WIDTH = KV_SLOTS * HEAD_DIM
WINDOW = 128
N_META = 16
N_BUCKETS = 32
MAX_DISTANCE = 128
GDN_CONV = 4
FFN_CONV = 3
D_FF = 5632
EPS = 1e-6
PAST_LEN = 16384

SUBLANES = 8
LANES = 128

CHUNK = 128
assert CHUNK == HEAD_DIM == LANES
INV_BASE = 16
GDN_SEQ_CHUNKS = 2
INPROJ_ROW_BLOCKS = 2
FFN_ROW_BLOCKS = 2

COL_Z = GDN_QKV
COL_SQ = COL_Z + GDN_WIDTH
COL_KV = COL_SQ + SWA_WIDTH
PROJ_COLS = COL_KV + KV_WIDTH
GDN_COLS = COL_SQ

SWA_Q_BLOCKS = 4
NEG = -1e30
VMEM_LIMIT = 56 * 1024 * 1024

_NT = (((1,), (1,)), ((), ()))


def _dot(a, b):
    return jnp.dot(a.astype(jnp.bfloat16), b.astype(jnp.bfloat16), preferred_element_type=jnp.float32)


def _dot_nt(a, b):
    return lax.dot_general(a.astype(jnp.bfloat16), b.astype(jnp.bfloat16), _NT,
                           preferred_element_type=jnp.float32)


_dot_inv = _dot


def _dot_exact(a, b, dims=None):
    if dims is None:
        return jnp.dot(a, b, precision=lax.Precision.HIGHEST, preferred_element_type=jnp.float32)
    return lax.dot_general(a, b, dims, precision=lax.Precision.HIGHEST,
                           preferred_element_type=jnp.float32)


def _pack_w_in_kernel(a_ref, b_ref, o_ref, ba_ref, *, first_shifted, shift):
    j = pl.program_id(0)

    @pl.when(j < first_shifted)
    def _():
        o_ref[...] = a_ref[...].T.astype(o_ref.dtype)

    @pl.when(j >= first_shifted)
    def _():
        rows = jnp.concatenate([a_ref[shift:, :], b_ref[:shift, :]], axis=0)
        o_ref[...] = rows.T.astype(o_ref.dtype)

    @pl.when(j == first_shifted)
    def _():
        head = a_ref[:LANES, :]
        row = lax.broadcasted_iota(jnp.int32, head.shape, 0)
        ba_ref[...] = jnp.where(row < shift, head, 0.0).T.astype(ba_ref.dtype)


def _pack_w_in(w_in_t, n_ba, *, tn):
    n_blocks = PROJ_COLS // tn
    first_shifted = COL_SQ // tn
    assert COL_SQ % tn == 0 and w_in_t.shape[0] == PROJ_COLS + n_ba and n_ba % SUBLANES == 0
    return pl.pallas_call(
        functools.partial(_pack_w_in_kernel, first_shifted=first_shifted, shift=n_ba),
        grid=(n_blocks,),
        in_specs=[pl.BlockSpec((tn, D_MODEL), lambda j: (j, 0)),
                  pl.BlockSpec((tn, D_MODEL), lambda j: (jnp.maximum(j, first_shifted) + 1, 0))],
        out_specs=[pl.BlockSpec((D_MODEL, tn), lambda j: (0, j)),
                   pl.BlockSpec((D_MODEL, LANES), lambda j: (0, 0))],
        out_shape=[jax.ShapeDtypeStruct((D_MODEL, PROJ_COLS), jnp.bfloat16),
                   jax.ShapeDtypeStruct((D_MODEL, LANES), jnp.bfloat16)],
        compiler_params=pltpu.CompilerParams(
            dimension_semantics=("arbitrary",), vmem_limit_bytes=VMEM_LIMIT),
        name="pack_w_in",
    )(w_in_t, w_in_t)


def _rms_scale(x):
    return lax.rsqrt(jnp.mean(x * x, axis=-1, keepdims=True) + EPS)


def _silu(x):
    return x * jax.nn.sigmoid(x)


def _inproj_kernel(*refs, row_chunk, conv_tiles):
    if conv_tiles:
        (x_ref, nw_ref, w_ref, wba_ref, cw_ref, hist_ref,
         o_ref, ba_ref, tail_ref, xn_ref, xe_ref, carry_ref) = refs
    else:
        x_ref, nw_ref, w_ref, wba_ref, o_ref, ba_ref, xn_ref = refs
    i = pl.program_id(0)
    j = pl.program_id(1)
    tm, tn = o_ref.shape

    def normalize(rows):
        x = x_ref[rows, :]
        xn_ref[rows, :] = (x * _rms_scale(x) * nw_ref[...]).astype(jnp.bfloat16)

    def plain():
        o_ref[...] = jnp.dot(xn_ref[...], w_ref[...], preferred_element_type=jnp.float32)

    if not conv_tiles:
        @pl.when(j == 0)
        def _():
            def body(c, carry):
                normalize(pl.ds(pl.multiple_of(c * row_chunk, row_chunk), row_chunk))
                return carry
            lax.fori_loop(0, tm // row_chunk, body, 0)
            ba_ref[...] = jnp.dot(xn_ref[...], wba_ref[...], preferred_element_type=jnp.float32)
        plain()
        return
    pl.when(j >= conv_tiles)(plain)

    def conv_tile(first):
        @pl.when(i == 0)
        def _():
            carry_ref[pl.ds(j, 1)] = hist_ref[...].reshape(1, SUBLANES, tn)
        xe_ref[0:SUBLANES, :] = carry_ref[pl.ds(j, 1)].reshape(SUBLANES, tn)
        rb = tm // INPROJ_ROW_BLOCKS

        def raw_block(r):
            rows = slice(r * rb, (r + 1) * rb)
            if first:
                normalize(rows)
                ba_ref[rows, :] = jnp.dot(xn_ref[rows, :], wba_ref[...], preferred_element_type=jnp.float32)
            return jnp.dot(xn_ref[rows, :], w_ref[...], preferred_element_type=jnp.float32)

        nxt = raw_block(0)
        for r in range(INPROJ_ROW_BLOCKS):
            raw = nxt
            if r + 1 < INPROJ_ROW_BLOCKS:
                nxt = raw_block(r + 1)
            base = SUBLANES + r * rb
            xe_ref[base:base + rb, :] = raw
            conv = None
            for s in range(GDN_CONV):
                term = xe_ref[base - s:base - s + rb, :] * cw_ref[GDN_CONV - 1 - s:GDN_CONV - s, :]
                conv = term if conv is None else conv + term
            o_ref[r * rb:(r + 1) * rb, :] = _silu(conv)
        tail = xe_ref[tm:tm + SUBLANES, :]
        carry_ref[pl.ds(j, 1)] = tail.reshape(1, SUBLANES, tn)
        tail_ref[...] = tail

    pl.when(j == 0)(functools.partial(conv_tile, True))
    pl.when((j > 0) & (j < conv_tiles))(functools.partial(conv_tile, False))


def _inproj(x, nw, w, wba, conv=None, *, tm, tn, row_chunk):
    rows = x.shape[0]
    conv_tiles = GDN_QKV // tn if conv else 0
    in_specs = [
        pl.BlockSpec((tm, D_MODEL), lambda i, j: (i, 0)),
        pl.BlockSpec((1, D_MODEL), lambda i, j: (0, 0)),
        pl.BlockSpec((D_MODEL, tn), lambda i, j: (0, j)),
        pl.BlockSpec((D_MODEL, LANES), lambda i, j: (0, 0)),
    ]
    out_specs = [pl.BlockSpec((tm, tn), lambda i, j: (i, j)),
                 pl.BlockSpec((tm, LANES), lambda i, j: (i, 0))]
    out_shape = [jax.ShapeDtypeStruct((rows, PROJ_COLS), jnp.float32),
                 jax.ShapeDtypeStruct((rows, LANES), jnp.float32)]
    scratch = [pltpu.VMEM((tm, D_MODEL), jnp.bfloat16)]
    args = [x, nw, w, wba]
    if conv:
        conv_col = lambda i, j: (0, jnp.minimum(j, conv_tiles - 1))
        in_specs += [pl.BlockSpec((GDN_CONV, tn), conv_col), pl.BlockSpec((SUBLANES, tn), conv_col)]
        out_specs.append(pl.BlockSpec((SUBLANES, tn), lambda i, j: (i, jnp.minimum(j, conv_tiles - 1))))
        out_shape.append(jax.ShapeDtypeStruct((rows // tm * SUBLANES, GDN_QKV), jnp.float32))
        scratch += [pltpu.VMEM((SUBLANES + tm, tn), jnp.float32),
                    pltpu.VMEM((conv_tiles, SUBLANES, tn), jnp.float32)]
        args += list(conv)
    return pl.pallas_call(
        functools.partial(_inproj_kernel, row_chunk=row_chunk, conv_tiles=conv_tiles),
        grid=(rows // tm, PROJ_COLS // tn),
        in_specs=in_specs,
        out_specs=out_specs,
        out_shape=out_shape,
        scratch_shapes=scratch,
        compiler_params=pltpu.CompilerParams(
            dimension_semantics=("arbitrary", "arbitrary"), vmem_limit_bytes=VMEM_LIMIT),
        name="inproj_conv" if conv else "inproj",
    )(*args)


def _tri_inverse(lms, ri, ci):
    shift = INV_BASE.bit_length() - 1
    eye = (ri == ci).astype(jnp.float32)
    in_block = (ri >> shift) == (ci >> shift)
    ps = [jnp.where(in_block, lm, 0.0) for lm in lms]
    ts = [eye - p for p in ps]
    for _ in range(shift - 1):
        ps = [_dot_inv(p, p) for p in ps]
        ts = [t + _dot_inv(t, p) for t, p in zip(ts, ps)]
    size = INV_BASE
    while size < CHUNK:
        shift += 1
        in_pair = (ri >> shift) == (ci >> shift)
        off_mask = in_pair & jnp.logical_not(in_block)
        tos = [_dot_inv(t, jnp.where(off_mask, lm, 0.0)) for t, lm in zip(ts, lms)]
        ts = [t - _dot_inv(to, t) for t, to in zip(ts, tos)]
        in_block = in_pair
        size *= 2
    return ts


def _gdn_kernel(x_ref, ba_ref, hist_ref, s0_ref, cw_ref, alog_ref, dtb_ref, gnw_ref,
                o_ref, sout_ref, xe_ref, s_ref, *, nb, seq, group, carry, pad_rows, preconv):
    step = pl.program_id(0)
    rows = nb * seq
    n_chunks = rows // CHUNK
    n_groups = CHUNK // group
    gshift = group.bit_length() - 1

    if carry:
        @pl.when(step == 0)
        def _():
            s_ref[...] = s0_ref[0]
    if not preconv:
        if carry:
            @pl.when(step == 0)
            def _():
                xe_ref[:, 0:SUBLANES, :] = hist_ref[...]
        else:
            xe_ref[:, 0:SUBLANES, :] = hist_ref[...]
        xe_ref[:, SUBLANES:SUBLANES + seq, :] = x_ref[:, :, 0:GDN_QKV]

    seq_rows = min(seq, CHUNK)
    seqs_per_chunk = CHUNK // seq_rows

    def chunk_rows(ref, c, row_off, cols):
        if seq >= CHUNK:
            start = row_off + c * CHUNK
            return ref[0:1, start:start + CHUNK, cols]
        b0 = c * seqs_per_chunk
        return ref[b0:b0 + seqs_per_chunk, row_off:row_off + seq, cols]

    def conv_chunk(col, c):
        cols = slice(col, col + HEAD_DIM)
        if preconv:
            return chunk_rows(x_ref, c, 0, cols).reshape(CHUNK, HEAD_DIM)
        acc = None
        for s in range(GDN_CONV):
            term = chunk_rows(xe_ref, c, SUBLANES - s, cols) * cw_ref[GDN_CONV - 1 - s:GDN_CONV - s, cols]
            acc = term if acc is None else acc + term
        return _silu(acc).reshape(CHUNK, HEAD_DIM)

    ri = lax.broadcasted_iota(jnp.int32, (CHUNK, CHUNK), 0)
    ci = lax.broadcasted_iota(jnp.int32, (CHUNK, CHUNK), 1)
    same = (ri >> gshift) == (ci >> gshift)
    m_incl = same & (ri >= ci)
    m_strict = same & (ri > ci)
    f_incl = m_incl.astype(jnp.float32)
    f_same = same.astype(jnp.float32)
    lane = lax.broadcasted_iota(jnp.int32, (CHUNK, LANES), 1)
    row_in_chunk = lax.broadcasted_iota(jnp.int32, (CHUNK, LANES), 0)

    pre = []
    for c in range(n_chunks):
        bac = chunk_rows(ba_ref, c, 0, slice(0, LANES)).reshape(CHUNK, LANES)
        beta_all = jax.nn.sigmoid(bac)
        sp_in = bac + dtb_ref[...]
        softplus = jnp.maximum(sp_in, 0.0) + jnp.log1p(jnp.exp(-jnp.abs(sp_in)))
        g_all = -jnp.exp(alog_ref[...]) * softplus
        if pad_rows and c == 0:
            valid = row_in_chunk >= pad_rows
            beta_all = jnp.where(valid, beta_all, 0.0)
            g_all = jnp.where(valid, g_all, 0.0)
        g_all = jnp.where((lane >= GDN_HEADS) & (lane < 2 * GDN_HEADS), g_all, 0.0)
        gc_col = _dot_exact(f_incl, g_all)
        if n_groups == 1:
            gtot_col = jnp.broadcast_to(gc_col[CHUNK - 1:CHUNK, :], (CHUNK, LANES))
        else:
            gtot_col = _dot_exact(f_same, g_all)
        gc_row = gc_col.T

        for h in range(GDN_HEADS):
            qh = conv_chunk(h * HEAD_DIM, c)
            kh = conv_chunk(GDN_WIDTH + h * HEAD_DIM, c)
            vh = conv_chunk(2 * GDN_WIDTH + h * HEAD_DIM, c)
            qh = qh * lax.rsqrt(jnp.sum(qh * qh, -1, keepdims=True) + EPS) * (HEAD_DIM ** -0.5)
            kh = kh * lax.rsqrt(jnp.sum(kh * kh, -1, keepdims=True) + EPS)
            bcast = lambda col: jnp.broadcast_to(col, (CHUNK, HEAD_DIM))
            gcc = bcast(gc_col[:, GDN_HEADS + h:GDN_HEADS + h + 1])
            gtc = bcast(gtot_col[:, GDN_HEADS + h:GDN_HEADS + h + 1])
            beta = bcast(beta_all[:, h:h + 1])
            gcr = gc_row[GDN_HEADS + h:GDN_HEADS + h + 1, :]
            decay = jnp.exp(jnp.where(m_incl, gcc - gcr, NEG))
            kb = kh * beta
            egc = jnp.exp(gcc)
            pre.append(dict(
                c=c, h=h,
                lm=jnp.where(m_strict, _dot_nt(kb, kh) * decay, 0.0),
                qk=_dot_nt(qh, kh) * decay,
                rhs=jnp.concatenate([vh * beta, kb * egc], axis=1),
                qg=qh * egc,
                kd_t=(kh * jnp.exp(gtc - gcc)).T,
                gl=jnp.exp(gtc)))

    inverses = _tri_inverse([p["lm"] for p in pre], ri, ci)
    sols = [_dot_inv(t, p["rhs"]) for t, p in zip(inverses, pre)]

    for c in range(n_chunks):
        r0 = c * CHUNK
        items = [(p, sol) for p, sol in zip(pre, sols) if p["c"] == c]
        state = lambda h, b: s_ref[h] if carry else s0_ref[c * n_groups + b, h]
        ws, qs = [], []
        for p, sol in items:
            w = sol[:, HEAD_DIM:]
            ws_parts, qs_parts = [], []
            for b in range(n_groups):
                g0 = b * group
                wq = jnp.concatenate([w[g0:g0 + group], p["qg"][g0:g0 + group]], axis=0)
                res = _dot(wq, state(p["h"], b))
                ws_parts.append(res[:group])
                qs_parts.append(res[group:])
            ws.append(ws_parts[0] if n_groups == 1 else jnp.concatenate(ws_parts, axis=0))
            qs.append(qs_parts[0] if n_groups == 1 else jnp.concatenate(qs_parts, axis=0))
        v_new = [sol[:, :HEAD_DIM] - w for (p, sol), w in zip(items, ws)]
        o = [a + _dot(p["qk"], vn) for a, (p, sol), vn in zip(qs, items, v_new)]
        for (p, sol), vn in zip(items, v_new):
            h = p["h"]
            for b in range(n_groups):
                g0 = b * group
                kd_b = p["kd_t"] if n_groups == 1 else jnp.where((ci >> gshift) == b, p["kd_t"], 0.0)
                st = state(h, b) * p["gl"][g0:g0 + 1, :] + _dot(kd_b, vn)
                if carry:
                    s_ref[h] = st
                else:
                    sout_ref[c * n_groups + b, h] = st
        for (p, sol), oh in zip(items, o):
            h = p["h"]
            z = chunk_rows(x_ref, c, 0, slice(COL_Z + h * HEAD_DIM, COL_Z + (h + 1) * HEAD_DIM)).reshape(CHUNK, HEAD_DIM)
            y = oh * lax.rsqrt(jnp.mean(oh * oh, -1, keepdims=True) + EPS) * gnw_ref[...] * _silu(z)
            o_ref[r0:r0 + CHUNK, h * HEAD_DIM:(h + 1) * HEAD_DIM] = y.astype(o_ref.dtype)

    if carry:
        if not preconv:
            xe_ref[:, 0:SUBLANES, :] = xe_ref[:, seq:seq + SUBLANES, :]

        @pl.when(step == pl.num_programs(0) - 1)
        def _():
            sout_ref[0] = s_ref[...]


def _gdn(x3, ba3, x_idx, hist, hist_idx, s0, cw, alog_row, dtb_row, gnw, *,
         n_steps, nb, seq, group, carry, pad_rows, preconv):
    rows = nb * seq
    if carry:
        state_spec = pl.BlockSpec((1, GDN_HEADS, HEAD_DIM, HEAD_DIM), lambda s: (0, 0, 0, 0))
        state_shape = (1, GDN_HEADS, HEAD_DIM, HEAD_DIM)
    else:
        n_states = rows // group
        state_spec = pl.BlockSpec((None, n_states, GDN_HEADS, HEAD_DIM, HEAD_DIM), lambda s: (0, s, 0, 0, 0))
        state_shape = (1, n_steps * n_states, GDN_HEADS, HEAD_DIM, HEAD_DIM)
    full = lambda shape: pl.BlockSpec(shape, lambda s: (0,) * len(shape))
    return pl.pallas_call(
        functools.partial(_gdn_kernel, nb=nb, seq=seq, group=group, carry=carry, pad_rows=pad_rows,
                          preconv=preconv),
        grid=(n_steps,),
        in_specs=[
            pl.BlockSpec((nb, seq, GDN_COLS), x_idx),
            pl.BlockSpec((nb, seq, LANES), x_idx),
            pl.BlockSpec((nb, SUBLANES, GDN_QKV), hist_idx),
            state_spec,
            full((GDN_CONV, GDN_QKV)),
            full((1, LANES)),
            full((1, LANES)),
            full((1, HEAD_DIM)),
        ],
        out_specs=[
            pl.BlockSpec((rows, GDN_WIDTH), lambda s: (s, 0)),
            state_spec,
        ],
        out_shape=[
            jax.ShapeDtypeStruct((n_steps * rows, GDN_WIDTH), jnp.bfloat16),
            jax.ShapeDtypeStruct(state_shape, jnp.float32),
        ],
        scratch_shapes=[
            pltpu.VMEM((nb, SUBLANES + (SUBLANES if preconv else seq), GDN_QKV), jnp.float32),
            pltpu.VMEM((GDN_HEADS, HEAD_DIM, HEAD_DIM), jnp.float32),
        ],
        compiler_params=pltpu.CompilerParams(
            dimension_semantics=("arbitrary",), vmem_limit_bytes=VMEM_LIMIT),
        name="gdn_seq" if carry else "gdn_batch",
    )(x3, ba3, hist, s0, cw, alog_row, dtb_row, gnw)


def _t5_bucket_np(dist):
    n = np.maximum(dist, 0)
    exact = N_BUCKETS // 2
    large = exact + (np.log(np.maximum(n, 1).astype(np.float32) / exact)
                     / math.log(MAX_DISTANCE / exact) * (N_BUCKETS - exact)).astype(np.int32)
    return np.where(n < exact, n, np.minimum(large, N_BUCKETS - 1)).astype(np.int32)


def _bucket_ids(dist, valid):
    return np.where(valid, _t5_bucket_np(dist), -1).astype(np.int32)


def _bias_kernel(table_ref, *refs):
    n = len(refs) // 2
    for ids_ref, out_ref in zip(refs[:n], refs[n:]):
        nq = ids_ref.shape[0]
        rows_per_pass = min(nq, 2 * SUBLANES)
        for r0 in range(0, nq, rows_per_pass):
            ids = ids_ref[r0:r0 + rows_per_pass, :]

            def body(b, accs):
                hit = ids == b
                return tuple(jnp.where(hit, table_ref[b, head], acc) for head, acc in enumerate(accs))

            init = tuple(jnp.full(ids.shape, NEG, jnp.float32) for _ in range(SWA_HEADS))
            for head, acc in enumerate(lax.fori_loop(0, N_BUCKETS, body, init)):
                kh, g = divmod(head, SWA_GROUP)
                out_ref[kh, g * nq + r0:g * nq + r0 + rows_per_pass, :] = acc


def _bias_tables(rel_table, id_arrays):
    out_shapes = [jax.ShapeDtypeStruct((SWA_KV_HEADS, SWA_GROUP * a.shape[0], a.shape[1]), jnp.float32)
                  for a in id_arrays]
    vmem = pl.BlockSpec(memory_space=pltpu.VMEM)
    return pl.pallas_call(
        _bias_kernel,
        in_specs=[pl.BlockSpec(memory_space=pltpu.SMEM)] + [vmem] * len(id_arrays),
        out_specs=[vmem] * len(id_arrays),
        out_shape=out_shapes,
        name="swa_bias",
    )(rel_table, *[jnp.asarray(a) for a in id_arrays])


def _attend(problems):
    scale = HEAD_DIM ** -0.5
    scores = [[_dot_nt(q, k) * scale + b for k, b in zip(keys, biases)]
              for q, keys, _, biases, _ in problems]
    maxes = []
    for (_, _, _, _, sink), segs in zip(problems, scores):
        m = sink
        for s in segs:
            m = jnp.maximum(m, jnp.max(s, axis=-1, keepdims=True))
        maxes.append(m)
    probs = [[jnp.exp(s - m) for s in segs] for segs, m in zip(scores, maxes)]
    outs = []
    for (_, _, values, _, sink), ps, m in zip(problems, probs, maxes):
        acc = None
        for p, v in zip(ps, values):
            v_ones = jnp.concatenate([v, jnp.ones((v.shape[0], HEAD_DIM), v.dtype)], axis=1)
            pv = _dot(p, v_ones)
            acc = pv if acc is None else acc + pv
        den = acc[:, HEAD_DIM:] + jnp.exp(sink - m)
        outs.append(acc[:, :HEAD_DIM] / den)
    return outs


def _group_queries(q_rows, kh):
    return jnp.concatenate(
        [q_rows((kh * SWA_GROUP + g) * HEAD_DIM, (kh * SWA_GROUP + g + 1) * HEAD_DIM)
         for g in range(SWA_GROUP)], axis=0)


def _pad_keys(rows):
    return jnp.concatenate([rows, jnp.zeros((WINDOW - rows.shape[0], rows.shape[1]), rows.dtype)], axis=0)


def _k_cols(kh):
    return slice(kh * HEAD_DIM, (kh + 1) * HEAD_DIM)


def _v_cols(kh):
    return slice((SWA_KV_HEADS + kh) * HEAD_DIM, (SWA_KV_HEADS + kh + 1) * HEAD_DIM)


def _swa_prompt_kernel(q_ref, kvc_ref, kvp_ref, kvm_ref, bcur_ref, bprev_ref, bm0_ref, bfar_ref,
                       sink_ref, o_ref):
    first = pl.program_id(0) == 0
    problems = []
    for blk in range(SWA_Q_BLOCKS):
        rows = slice(blk * WINDOW, (blk + 1) * WINDOW)
        prev_ref, prev_rows = (kvp_ref, slice(0, WINDOW)) if blk == 0 else (
            kvc_ref, slice((blk - 1) * WINDOW, blk * WINDOW))
        for kh in range(SWA_KV_HEADS):
            ks, vs = _k_cols(kh), _v_cols(kh)
            first_block = first if blk == 0 else False
            b_prev = jnp.where(first_block, NEG, bprev_ref[kh])
            b_meta = jnp.where(first_block, bm0_ref[kh], bfar_ref[kh])
            problems.append((_group_queries(lambda a, b, rows=rows: q_ref[rows, a:b], kh),
                             [kvc_ref[rows, ks], prev_ref[prev_rows, ks], _pad_keys(kvm_ref[:, ks])],
                             [kvc_ref[rows, vs], prev_ref[prev_rows, vs], _pad_keys(kvm_ref[:, vs])],
                             [bcur_ref[kh], b_prev, b_meta], sink_ref[kh]))
    for i, o in enumerate(_attend(problems)):
        blk, kh = divmod(i, SWA_KV_HEADS)
        for g in range(SWA_GROUP):
            head = kh * SWA_GROUP + g
            o_ref[blk * WINDOW:(blk + 1) * WINDOW, head * HEAD_DIM:(head + 1) * HEAD_DIM] = (
                o[g * WINDOW:(g + 1) * WINDOW].astype(o_ref.dtype))


def _swa_meta_kernel(q_ref, kv_ref, bias_ref, sink_ref, o_ref):
    problems = [(_group_queries(lambda a, b: q_ref[:, a:b], kh), [kv_ref[:, _k_cols(kh)]],
                 [kv_ref[:, _v_cols(kh)]], [bias_ref[kh]], sink_ref[kh])
                for kh in range(SWA_KV_HEADS)]
    for kh, o in enumerate(_attend(problems)):
        for g in range(SWA_GROUP):
            head = kh * SWA_GROUP + g
            o_ref[:, head * HEAD_DIM:(head + 1) * HEAD_DIM] = (
                o[g * N_META:(g + 1) * N_META].astype(o_ref.dtype))


def _swa_sample_kernel(q_ref, kvn_ref, win_ref, meta_ref, bwin_ref, bsmall_ref, sink_ref,
                       o_ref, wout_ref, *, nb, seq):
    cached = lambda ref, b, slot, n: ref[b, pl.ds(slot, n, stride=KV_SLOTS), :]
    keep = (WINDOW - seq) * KV_SLOTS
    wout_ref[:, 0:keep, :] = win_ref[:, seq * KV_SLOTS:WINDOW * KV_SLOTS, :]
    for slot in range(KV_SLOTS):
        wout_ref[:, pl.ds(keep + slot, seq, stride=KV_SLOTS), :] = (
            kvn_ref[:, :, slot * HEAD_DIM:(slot + 1) * HEAD_DIM])
    problems = []
    for b in range(nb):
        for kh in range(SWA_KV_HEADS):
            ks, vs = _k_cols(kh), _v_cols(kh)
            k_small = _pad_keys(jnp.concatenate([cached(meta_ref, b, kh, N_META), kvn_ref[b, :, ks]], axis=0))
            v_small = _pad_keys(jnp.concatenate(
                [cached(meta_ref, b, SWA_KV_HEADS + kh, N_META), kvn_ref[b, :, vs]], axis=0))
            problems.append((_group_queries(lambda a, c, b=b: q_ref[b, :, a:c], kh),
                             [cached(win_ref, b, kh, WINDOW), k_small],
                             [cached(win_ref, b, SWA_KV_HEADS + kh, WINDOW), v_small],
                             [bwin_ref[kh], bsmall_ref[kh]], sink_ref[kh]))
    outs = _attend(problems)
    for head in range(SWA_HEADS):
        kh, g = divmod(head, SWA_GROUP)
        rows = [outs[b * SWA_KV_HEADS + kh][g * seq:(g + 1) * seq] for b in range(nb)]
        o_ref[:, head * HEAD_DIM:(head + 1) * HEAD_DIM] = jnp.concatenate(rows, axis=0).astype(o_ref.dtype)


def _outproj_kernel(g_ref, s_ref, h_ref, wo_ref, nw_ref, o_ref):
    mix = (jnp.dot(g_ref[...], wo_ref[0:GDN_WIDTH, :], preferred_element_type=jnp.float32)
           + jnp.dot(s_ref[...], wo_ref[GDN_WIDTH:, :], preferred_element_type=jnp.float32))
    o_ref[...] = h_ref[...] + mix * _rms_scale(mix) * nw_ref[...]


def _outproj(g, s, h, wo, nw, *, tm):
    rows = h.shape[0]
    return pl.pallas_call(
        _outproj_kernel,
        grid=(rows // tm,),
        in_specs=[
            pl.BlockSpec((tm, GDN_WIDTH), lambda i: (i, 0)),
            pl.BlockSpec((tm, SWA_WIDTH), lambda i: (i, 0)),
            pl.BlockSpec((tm, D_MODEL), lambda i: (i, 0)),
            pl.BlockSpec((D_MODEL, D_MODEL), lambda i: (0, 0)),
            pl.BlockSpec((1, D_MODEL), lambda i: (0, 0)),
        ],
        out_specs=pl.BlockSpec((tm, D_MODEL), lambda i: (i, 0)),
        out_shape=jax.ShapeDtypeStruct((rows, D_MODEL), jnp.float32),
        compiler_params=pltpu.CompilerParams(
            dimension_semantics=("arbitrary",), vmem_limit_bytes=VMEM_LIMIT),
        name="outproj",
    )(g, s, h, wo, nw)


def _ffn_kernel(*refs, batch, tm, tf):
    if batch:
        (h_ref, nw_pre_ref, wg_ref, wu_ref, cw_ref, wd_ref, nw_post_ref, hist_ref,
         y_ref, graw_ref, xn_ref, xe_ref) = refs
    else:
        (h_ref, nw_pre_ref, wg_ref, wu_ref, cw_ref, wd_ref, nw_post_ref, hist_ref,
         y_ref, graw_ref, xn_ref, xe_ref, carry_ref) = refs
    i = pl.program_id(0)
    j = pl.program_id(1)
    last_j = pl.num_programs(1) - 1
    rb = tm // FFN_ROW_BLOCKS

    if batch:
        xe_ref[:, 0:SUBLANES, :] = hist_ref[...]
    else:
        @pl.when(i == 0)
        def _():
            carry_ref[pl.ds(j, 1)] = hist_ref[...].reshape(1, SUBLANES, tf)
        xe_ref[:, 0:SUBLANES, :] = carry_ref[pl.ds(j, 1)]

    def step(first, last):
        def gate_up(r):
            rows = slice(r * rb, (r + 1) * rb)
            if first:
                h = h_ref[rows, :]
                xn_ref[rows, :] = (h * _rms_scale(h) * nw_pre_ref[...]).astype(jnp.bfloat16)
            xn = xn_ref[rows, :]
            return (jnp.dot(xn, wg_ref[...], preferred_element_type=jnp.float32),
                    jnp.dot(xn, wu_ref[...], preferred_element_type=jnp.float32))

        nxt = gate_up(0)
        for r in range(FFN_ROW_BLOCKS):
            rows = slice(r * rb, (r + 1) * rb)
            gate, up = nxt
            if r + 1 < FFN_ROW_BLOCKS:
                nxt = gate_up(r + 1)
            if batch:
                seqs = slice(r * rb // SUBLANES, (r + 1) * rb // SUBLANES)
                graw_ref[rows, :] = gate
                xe_ref[seqs, SUBLANES:2 * SUBLANES, :] = gate.reshape(rb // SUBLANES, SUBLANES, tf)
                taps = [xe_ref[seqs, SUBLANES - s:2 * SUBLANES - s, :] for s in range(FFN_CONV)]
            else:
                base = SUBLANES + r * rb
                xe_ref[:, base:base + rb, :] = gate.reshape(1, rb, tf)
                taps = [xe_ref[:, base - s:base - s + rb, :] for s in range(FFN_CONV)]
            conv = None
            for s, tap in enumerate(taps):
                term = tap * cw_ref[FFN_CONV - 1 - s:FFN_CONV - s, :]
                conv = term if conv is None else conv + term
            act = (_silu(conv.reshape(rb, tf)) * up).astype(jnp.bfloat16)
            down = jnp.dot(act, wd_ref[...], preferred_element_type=jnp.float32)
            if first:
                y_ref[rows, :] = down
            elif last:
                y = y_ref[rows, :] + down
                y_ref[rows, :] = h_ref[rows, :] + y * _rms_scale(y) * nw_post_ref[...]
            else:
                y_ref[rows, :] += down
        if not batch:
            tail = xe_ref[:, tm:tm + SUBLANES, :]
            carry_ref[pl.ds(j, 1)] = tail
            graw_ref[...] = tail.reshape(SUBLANES, tf)

    pl.when(j == 0)(functools.partial(step, True, False))
    pl.when((j > 0) & (j < last_j))(functools.partial(step, False, False))
    pl.when(j == last_j)(functools.partial(step, False, True))


def _ffn(h, nw_pre, wg, wu, cw, wd, nw_post, hist, *, batch, tm, tf):
    rows = h.shape[0]
    nj = D_FF // tf
    in_specs = [
        pl.BlockSpec((tm, D_MODEL), lambda i, j: (i, 0)),
        pl.BlockSpec((1, D_MODEL), lambda i, j: (0, 0)),
        pl.BlockSpec((D_MODEL, tf), lambda i, j: (0, j)),
        pl.BlockSpec((D_MODEL, tf), lambda i, j: (0, j)),
        pl.BlockSpec((FFN_CONV, tf), lambda i, j: (0, j)),
        pl.BlockSpec((tf, D_MODEL), lambda i, j: (j, 0)),
        pl.BlockSpec((1, D_MODEL), lambda i, j: (0, 0)),
    ]
    args = [h, nw_pre, wg, wu, cw, wd, nw_post, hist]
    scratch = [pltpu.VMEM((tm, D_MODEL), jnp.bfloat16)]
    if batch:
        in_specs.append(pl.BlockSpec((tm // SUBLANES, SUBLANES, tf), lambda i, j: (i, 0, j)))
        graw_spec = pl.BlockSpec((tm, tf), lambda i, j: (i, j))
        graw_shape = jax.ShapeDtypeStruct((rows, D_FF), jnp.float32)
        scratch.append(pltpu.VMEM((tm // SUBLANES, 2 * SUBLANES, tf), jnp.float32))
    else:
        in_specs.append(pl.BlockSpec((SUBLANES, tf), lambda i, j: (0, j)))
        graw_spec = pl.BlockSpec((SUBLANES, tf), lambda i, j: (i, j))
        graw_shape = jax.ShapeDtypeStruct((rows // tm * SUBLANES, D_FF), jnp.float32)
        scratch.append(pltpu.VMEM((1, SUBLANES + tm, tf), jnp.float32))
        scratch.append(pltpu.VMEM((nj, SUBLANES, tf), jnp.float32))
    return pl.pallas_call(
        functools.partial(_ffn_kernel, batch=batch, tm=tm, tf=tf),
        grid=(rows // tm, nj),
        in_specs=in_specs,
        out_specs=[pl.BlockSpec((tm, D_MODEL), lambda i, j: (i, 0)), graw_spec],
        out_shape=[jax.ShapeDtypeStruct((rows, D_MODEL), jnp.float32), graw_shape],
        scratch_shapes=scratch,
        compiler_params=pltpu.CompilerParams(
            dimension_semantics=("arbitrary", "arbitrary"), vmem_limit_bytes=VMEM_LIMIT),
        name="ffn_batch" if batch else "ffn_seq",
    )(*args)


def kernel(x_prompt, x_sample, cache_swa_meta_kv, cache_swa_window_kv, state_gdn_conv, state_gdn, state_ffn_conv, meta_tokens, rel_bias_table, w_in, gdn_conv_w, gdn_a_log, gdn_dt_bias, gdn_norm_w, swa_sinks, w_out, norm_mix_pre, norm_mix_post, norm_ffn_pre, norm_ffn_post, ffn_w_gate, ffn_w_up, ffn_conv_w, ffn_w_down):
    f32, bf16 = jnp.float32, jnp.bfloat16
    seq = x_prompt.shape[1]
    dec_b, dec_t = x_sample.shape[0], x_sample.shape[1]
    n_dec = dec_b * dec_t
    assert x_prompt.shape[0] == 1 and seq % CHUNK == 0 and dec_t == SUBLANES and n_dec % CHUNK == 0

    w_in_p, w_ba = _pack_w_in(jnp.transpose(w_in[0]), 2 * GDN_HEADS, tn=512)
    wo = w_out[0].astype(bf16)
    wg = ffn_w_gate[0].astype(bf16)
    wu = ffn_w_up[0].astype(bf16)
    wd = ffn_w_down[0].astype(bf16)
    lane_pad = lambda v: jnp.pad(v.reshape(1, GDN_HEADS), ((0, 0), (GDN_HEADS, LANES - 2 * GDN_HEADS)))
    alog_row = lane_pad(gdn_a_log[0])
    dtb_row = lane_pad(gdn_dt_bias[0])
    gnw = gdn_norm_w[0].reshape(1, HEAD_DIM)

    pad_rows = CHUNK - N_META
    n_small = n_dec + CHUNK
    x_big = x_prompt.reshape(seq, D_MODEL)
    x_small = jnp.concatenate(
        [x_sample.reshape(n_dec, D_MODEL), jnp.zeros((pad_rows, D_MODEL), f32), meta_tokens.astype(f32)], axis=0)
    nw = norm_mix_pre[0].reshape(1, D_MODEL)
    cw = gdn_conv_w[0]
    proj_small, ba_small = _inproj(x_small, nw, w_in_p, w_ba, tm=n_small, tn=512, row_chunk=128)
    proj_big, ba_big, qkv_tail = _inproj(
        x_big, nw, w_in_p, w_ba, (cw, proj_small[n_small - SUBLANES:, :GDN_QKV]),
        tm=1024, tn=512, row_chunk=128)

    small_chunks = proj_small.reshape(n_small // CHUNK, CHUNK, PROJ_COLS)
    small_groups = proj_small.reshape(n_small // SUBLANES, SUBLANES, PROJ_COLS)
    last_chunk = n_small // CHUNK - 1
    gdn_meta, s_meta = _gdn(
        small_chunks, ba_small.reshape(n_small // CHUNK, CHUNK, LANES), lambda s: (last_chunk, 0, 0),
        jnp.zeros((1, SUBLANES, GDN_QKV), f32), lambda s: (0, 0, 0),
        jnp.zeros((1, GDN_HEADS, HEAD_DIM, HEAD_DIM), f32), cw, alog_row, dtb_row, gnw,
        n_steps=1, nb=1, seq=CHUNK, group=CHUNK, carry=True, pad_rows=pad_rows, preconv=False)
    gdn_big, s_prompt = _gdn(
        proj_big.reshape(1, seq, PROJ_COLS), ba_big.reshape(1, seq, LANES), lambda s: (0, s, 0),
        jnp.zeros((1, SUBLANES, GDN_QKV), f32), lambda s: (0, 0, 0),
        s_meta, cw, alog_row, dtb_row, gnw,
        n_steps=seq // (GDN_SEQ_CHUNKS * CHUNK), nb=1, seq=GDN_SEQ_CHUNKS * CHUNK, group=CHUNK,
        carry=True, pad_rows=0, preconv=True)
    hist_gdn = jnp.pad(state_gdn_conv[0], ((0, 0), (SUBLANES - (GDN_CONV - 1), 0), (0, 0)))
    nb_gdn = CHUNK // dec_t
    gdn_small, s_sample = _gdn(
        small_groups, ba_small.reshape(n_small // SUBLANES, SUBLANES, LANES), lambda s: (s, 0, 0),
        hist_gdn, lambda s: (s, 0, 0),
        state_gdn, cw, alog_row, dtb_row, gnw,
        n_steps=dec_b // nb_gdn, nb=nb_gdn, seq=dec_t, group=dec_t, carry=False, pad_rows=0, preconv=False)

    qi = np.arange(WINDOW)[:, None]
    kj = np.arange(WINDOW)[None, :]
    mi = np.arange(N_META)[None, :]
    ti = np.arange(dec_t)[:, None]
    new_keys = kj - N_META
    id_arrays = [
        _bucket_ids(qi - kj, qi >= kj),
        _bucket_ids(qi - kj + WINDOW, kj > qi),
        _bucket_ids(qi + N_META - kj, kj < N_META),
        _bucket_ids(qi + N_META - kj + WINDOW, kj < N_META),
        _bucket_ids(ti + WINDOW - kj, kj > ti),
        _bucket_ids(np.where(new_keys < 0, PAST_LEN + ti - kj, ti - new_keys),
                    (new_keys < 0) | ((new_keys <= ti) & (new_keys < dec_t))),
        _bucket_ids(mi.T - mi, mi.T >= mi),
    ]
    bcur, bprev, bm0, bfar, bwin, bsmall, bmm = _bias_tables(rel_bias_table, id_arrays)
    sink_rows = lambda q: jnp.repeat(swa_sinks[0].reshape(SWA_KV_HEADS, SWA_GROUP), q, axis=1)[..., None]

    sq_blk = COL_SQ // SWA_WIDTH
    kv_blk = COL_KV // KV_WIDTH
    meta_blk = (n_small - N_META) // N_META
    full3 = lambda a: pl.BlockSpec(a.shape, lambda j: (0, 0, 0))
    sink_p = sink_rows(WINDOW)
    swa_rows = SWA_Q_BLOCKS * WINDOW
    swa_big = pl.pallas_call(
        _swa_prompt_kernel,
        grid=(seq // swa_rows,),
        in_specs=[
            pl.BlockSpec((swa_rows, SWA_WIDTH), lambda j: (j, sq_blk)),
            pl.BlockSpec((swa_rows, KV_WIDTH), lambda j: (j, kv_blk)),
            pl.BlockSpec((WINDOW, KV_WIDTH), lambda j: (jnp.maximum(j * SWA_Q_BLOCKS - 1, 0), kv_blk)),
            pl.BlockSpec((N_META, KV_WIDTH), lambda j: (meta_blk, kv_blk)),
            full3(bcur), full3(bprev), full3(bm0), full3(bfar), full3(sink_p),
        ],
        out_specs=pl.BlockSpec((swa_rows, SWA_WIDTH), lambda j: (j, 0)),
        out_shape=jax.ShapeDtypeStruct((seq, SWA_WIDTH), bf16),
        compiler_params=pltpu.CompilerParams(
            dimension_semantics=("arbitrary",), vmem_limit_bytes=VMEM_LIMIT),
        name="swa_prompt",
    )(proj_big, proj_big, proj_big, proj_small, bcur, bprev, bm0, bfar, sink_p)

    nb_swa = 8
    sink_s = sink_rows(dec_t)
    win = cache_swa_window_kv.reshape(dec_b, WINDOW * KV_SLOTS, HEAD_DIM)
    meta_kv = cache_swa_meta_kv.reshape(dec_b, N_META * KV_SLOTS, HEAD_DIM)
    swa_small, win_new = pl.pallas_call(
        functools.partial(_swa_sample_kernel, nb=nb_swa, seq=dec_t),
        grid=(dec_b // nb_swa,),
        in_specs=[
            pl.BlockSpec((nb_swa, dec_t, SWA_WIDTH), lambda j: (j, 0, sq_blk)),
            pl.BlockSpec((nb_swa, dec_t, KV_WIDTH), lambda j: (j, 0, kv_blk)),
            pl.BlockSpec((nb_swa, WINDOW * KV_SLOTS, HEAD_DIM), lambda j: (j, 0, 0)),
            pl.BlockSpec((nb_swa, N_META * KV_SLOTS, HEAD_DIM), lambda j: (j, 0, 0)),
            full3(bwin), full3(bsmall), full3(sink_s),
        ],
        out_specs=[pl.BlockSpec((nb_swa * dec_t, SWA_WIDTH), lambda j: (j, 0)),
                   pl.BlockSpec((nb_swa, WINDOW * KV_SLOTS, HEAD_DIM), lambda j: (j, 0, 0))],
        out_shape=[jax.ShapeDtypeStruct((n_dec, SWA_WIDTH), bf16),
                   jax.ShapeDtypeStruct((dec_b, WINDOW * KV_SLOTS, HEAD_DIM), f32)],
        compiler_params=pltpu.CompilerParams(
            dimension_semantics=("arbitrary",), vmem_limit_bytes=VMEM_LIMIT),
        name="swa_sample",
    )(small_groups, small_groups, win, meta_kv, bwin, bsmall, sink_s)

    sink_m = sink_rows(N_META)
    swa_meta = pl.pallas_call(
        _swa_meta_kernel,
        grid=(1,),
        in_specs=[
            pl.BlockSpec((N_META, SWA_WIDTH), lambda j: (meta_blk, sq_blk)),
            pl.BlockSpec((N_META, KV_WIDTH), lambda j: (meta_blk, kv_blk)),
            full3(bmm), full3(sink_m),
        ],
        out_specs=pl.BlockSpec((N_META, SWA_WIDTH), lambda j: (0, 0)),
        out_shape=jax.ShapeDtypeStruct((N_META, SWA_WIDTH), bf16),
        name="swa_meta",
    )(proj_small, proj_small, bmm, sink_m)

    nw_post = norm_mix_post[0].reshape(1, D_MODEL)
    nf_pre = norm_ffn_pre[0].reshape(1, D_MODEL)
    nf_post = norm_ffn_post[0].reshape(1, D_MODEL)
    fcw = ffn_conv_w[0]
    gdn_small_all = jnp.concatenate([gdn_small, gdn_meta], axis=0)
    swa_small_all = jnp.concatenate([swa_small, jnp.zeros((pad_rows, SWA_WIDTH), bf16), swa_meta], axis=0)
    h_small = _outproj(gdn_small_all, swa_small_all, x_small, wo, nw_post, tm=n_small // 2)
    hist_ffn = jnp.pad(state_ffn_conv[0], ((0, CHUNK // SUBLANES), (SUBLANES - (FFN_CONV - 1), 0), (0, 0)))
    y_small, g_small = _ffn(h_small, nf_pre, wg, wu, fcw, wd, nf_post, hist_ffn,
                            batch=True, tm=n_small // 2, tf=512)
    h_big = _outproj(gdn_big, swa_big, x_big, wo, nw_post, tm=512)
    y_big, g_tail = _ffn(h_big, nf_pre, wg, wu, fcw, wd, nf_post, g_small[n_small - SUBLANES:],
                         batch=False, tm=1024, tf=512)

    kv_shape = lambda n: (1, n, 2, SWA_KV_HEADS, HEAD_DIM)
    kv_small = proj_small[:, COL_KV:COL_KV + KV_WIDTH]
    y_prompt = y_big.reshape(1, seq, D_MODEL)
    y_sample = y_small[:n_dec].reshape(dec_b, dec_t, D_MODEL)
    p_meta_kv = kv_small[n_small - N_META:].reshape(kv_shape(N_META))[None]
    p_window_kv = proj_big[seq - WINDOW:, COL_KV:COL_KV + KV_WIDTH].reshape(kv_shape(WINDOW))[None]
    p_gdn_conv = qkv_tail[qkv_tail.shape[0] - (GDN_CONV - 1):].reshape(1, 1, GDN_CONV - 1, GDN_QKV)
    p_gdn_state = s_prompt[None]
    p_ffn_conv = g_tail[g_tail.shape[0] - (FFN_CONV - 1):].reshape(1, 1, FFN_CONV - 1, D_FF)
    s_window_kv = win_new.reshape(1, dec_b, WINDOW, 2, SWA_KV_HEADS, HEAD_DIM)
    s_gdn_conv = proj_small[:n_dec, :GDN_QKV].reshape(dec_b, dec_t, GDN_QKV)[:, dec_t - (GDN_CONV - 1):][None]
    s_gdn_state = s_sample
    s_ffn_conv = g_small[:n_dec].reshape(dec_b, dec_t, D_FF)[:, dec_t - (FFN_CONV - 1):][None]
    return (y_prompt, y_sample, p_meta_kv, p_window_kv, p_gdn_conv, p_gdn_state, p_ffn_conv,
            s_window_kv, s_gdn_conv, s_gdn_state, s_ffn_conv)
```

```python
import functools
import math

import numpy as np
import jax
import jax.numpy as jnp
from jax import lax
from jax.experimental import pallas as pl
from jax.experimental.pallas import tpu as pltpu

D_MODEL = 2048
HEAD_DIM = 128
GDN_HEADS = 8
GDN_WIDTH = GDN_HEADS * HEAD_DIM
GDN_QKV = 3 * GDN_WIDTH
SWA_HEADS = 8
SWA_KV_HEADS = 2
SWA_GROUP = SWA_HEADS // SWA_KV_HEADS
SWA_WIDTH = SWA_HEADS * HEAD_DIM
KV_SLOTS = 2 * SWA_KV_HEADS
KV_WIDTH = KV_SLOTS * HEAD_DIM
WINDOW = 128
N_META = 16
N_BUCKETS = 32
MAX_DISTANCE = 128
GDN_CONV = 4
FFN_CONV = 3
D_FF = 5632
EPS = 1e-6
PAST_LEN = 16384

SUBLANES = 8
LANES = 128

CHUNK = 128
assert CHUNK == HEAD_DIM == LANES
INV_BASE = 16
GDN_SEQ_CHUNKS = 2
INPROJ_ROW_BLOCKS = 2
FFN_ROW_BLOCKS = 2

COL_Z = GDN_QKV
COL_SQ = COL_Z + GDN_WIDTH
COL_KV = COL_SQ + SWA_WIDTH
PROJ_COLS = COL_KV + KV_WIDTH
GDN_COLS = COL_SQ

SWA_Q_BLOCKS = 4
NEG = -1e30
VMEM_LIMIT = 56 * 1024 * 1024

_NT = (((1,), (1,)), ((), ()))


def _dot(a, b):
    return jnp.dot(a.astype(jnp.bfloat16), b.astype(jnp.bfloat16), preferred_element_type=jnp.float32)


def _dot_nt(a, b):
    return lax.dot_general(a.astype(jnp.bfloat16), b.astype(jnp.bfloat16), _NT,
                           preferred_element_type=jnp.float32)


_dot_inv = _dot


def _dot_exact(a, b, dims=None):
    if dims is None:
        return jnp.dot(a, b, precision=lax.Precision.HIGHEST, preferred_element_type=jnp.float32)
    return lax.dot_general(a, b, dims, precision=lax.Precision.HIGHEST,
                           preferred_element_type=jnp.float32)


def _pack_w_in_kernel(a_ref, b_ref, o_ref, ba_ref, *, first_shifted, shift):
    j = pl.program_id(0)

    @pl.when(j < first_shifted)
    def _():
        o_ref[...] = a_ref[...].T.astype(o_ref.dtype)

    @pl.when(j >= first_shifted)
    def _():
        rows = jnp.concatenate([a_ref[shift:, :], b_ref[:shift, :]], axis=0)
        o_ref[...] = rows.T.astype(o_ref.dtype)

    @pl.when(j == first_shifted)
    def _():
        head = a_ref[:LANES, :]
        row = lax.broadcasted_iota(jnp.int32, head.shape, 0)
        ba_ref[...] = jnp.where(row < shift, head, 0.0).T.astype(ba_ref.dtype)


def _pack_w_in(w_in_t, n_ba, *, tn):
    n_blocks = PROJ_COLS // tn
    first_shifted = COL_SQ // tn
    assert COL_SQ % tn == 0 and w_in_t.shape[0] == PROJ_COLS + n_ba and n_ba % SUBLANES == 0
    return pl.pallas_call(
        functools.partial(_pack_w_in_kernel, first_shifted=first_shifted, shift=n_ba),
        grid=(n_blocks,),
        in_specs=[pl.BlockSpec((tn, D_MODEL), lambda j: (j, 0)),
                  pl.BlockSpec((tn, D_MODEL), lambda j: (jnp.maximum(j, first_shifted) + 1, 0))],
        out_specs=[pl.BlockSpec((D_MODEL, tn), lambda j: (0, j)),
                   pl.BlockSpec((D_MODEL, LANES), lambda j: (0, 0))],
        out_shape=[jax.ShapeDtypeStruct((D_MODEL, PROJ_COLS), jnp.bfloat16),
                   jax.ShapeDtypeStruct((D_MODEL, LANES), jnp.bfloat16)],
        compiler_params=pltpu.CompilerParams(
            dimension_semantics=("arbitrary",), vmem_limit_bytes=VMEM_LIMIT),
        name="pack_w_in",
    )(w_in_t, w_in_t)


def _rms_scale(x):
    return lax.rsqrt(jnp.mean(x * x, axis=-1, keepdims=True) + EPS)


def _silu(x):
    return x * jax.nn.sigmoid(x)


def _inproj_kernel(*refs, row_chunk, conv_tiles):
    if conv_tiles:
        (x_ref, nw_ref, w_ref, wba_ref, cw_ref, hist_ref,
         o_ref, ba_ref, tail_ref, xn_ref, xe_ref, carry_ref) = refs
    else:
        x_ref, nw_ref, w_ref, wba_ref, o_ref, ba_ref, xn_ref = refs
    i = pl.program_id(0)
    j = pl.program_id(1)
    tm, tn = o_ref.shape

    def normalize(rows):
        x = x_ref[rows, :]
        xn_ref[rows, :] = (x * _rms_scale(x) * nw_ref[...]).astype(jnp.bfloat16)

    def plain():
        o_ref[...] = jnp.dot(xn_ref[...], w_ref[...], preferred_element_type=jnp.float32)

    if not conv_tiles:
        @pl.when(j == 0)
        def _():
            def body(c, carry):
                normalize(pl.ds(pl.multiple_of(c * row_chunk, row_chunk), row_chunk))
                return carry
            lax.fori_loop(0, tm // row_chunk, body, 0)
            ba_ref[...] = jnp.dot(xn_ref[...], wba_ref[...], preferred_element_type=jnp.float32)
        plain()
        return
    pl.when(j >= conv_tiles)(plain)

    def conv_tile(first):
        @pl.when(i == 0)
        def _():
            carry_ref[pl.ds(j, 1)] = hist_ref[...].reshape(1, SUBLANES, tn)
        xe_ref[0:SUBLANES, :] = carry_ref[pl.ds(j, 1)].reshape(SUBLANES, tn)
        rb = tm // INPROJ_ROW_BLOCKS

        def raw_block(r):
            rows = slice(r * rb, (r + 1) * rb)
            if first:
                normalize(rows)
                ba_ref[rows, :] = jnp.dot(xn_ref[rows, :], wba_ref[...], preferred_element_type=jnp.float32)
            return jnp.dot(xn_ref[rows, :], w_ref[...], preferred_element_type=jnp.float32)

        nxt = raw_block(0)
        for r in range(INPROJ_ROW_BLOCKS):
            raw = nxt
            if r + 1 < INPROJ_ROW_BLOCKS:
                nxt = raw_block(r + 1)
            base = SUBLANES + r * rb
            xe_ref[base:base + rb, :] = raw
            conv = None
            for s in range(GDN_CONV):
                term = xe_ref[base - s:base - s + rb, :] * cw_ref[GDN_CONV - 1 - s:GDN_CONV - s, :]
                conv = term if conv is None else conv + term
            o_ref[r * rb:(r + 1) * rb, :] = conv
        tail = xe_ref[tm:tm + SUBLANES, :]
        carry_ref[pl.ds(j, 1)] = tail.reshape(1, SUBLANES, tn)
        tail_ref[...] = tail

    pl.when(j == 0)(functools.partial(conv_tile, True))
    pl.when((j > 0) & (j < conv_tiles))(functools.partial(conv_tile, False))


def _inproj(x, nw, w, wba, conv=None, *, tm, tn, row_chunk):
    rows = x.shape[0]
    conv_tiles = GDN_QKV // tn if conv else 0
    in_specs = [
        pl.BlockSpec((tm, D_MODEL), lambda i, j: (i, 0)),
        pl.BlockSpec((1, D_MODEL), lambda i, j: (0, 0)),
        pl.BlockSpec((D_MODEL, tn), lambda i, j: (0, j)),
        pl.BlockSpec((D_MODEL, LANES), lambda i, j: (0, 0)),
    ]
    out_specs = [pl.BlockSpec((tm, tn), lambda i, j: (i, j)),
                 pl.BlockSpec((tm, LANES), lambda i, j: (i, 0))]
    out_shape = [jax.ShapeDtypeStruct((rows, PROJ_COLS), jnp.float32),
                 jax.ShapeDtypeStruct((rows, LANES), jnp.float32)]
    scratch = [pltpu.VMEM((tm, D_MODEL), jnp.bfloat16)]
    args = [x, nw, w, wba]
    if conv:
        conv_col = lambda i, j: (0, jnp.minimum(j, conv_tiles - 1))
        in_specs += [pl.BlockSpec((GDN_CONV, tn), conv_col), pl.BlockSpec((SUBLANES, tn), conv_col)]
        out_specs.append(pl.BlockSpec((SUBLANES, tn), lambda i, j: (i, jnp.minimum(j, conv_tiles - 1))))
        out_shape.append(jax.ShapeDtypeStruct((rows // tm * SUBLANES, GDN_QKV), jnp.float32))
        scratch += [pltpu.VMEM((SUBLANES + tm, tn), jnp.float32),
                    pltpu.VMEM((conv_tiles, SUBLANES, tn), jnp.float32)]
        args += list(conv)
    return pl.pallas_call(
        functools.partial(_inproj_kernel, row_chunk=row_chunk, conv_tiles=conv_tiles),
        grid=(rows // tm, PROJ_COLS // tn),
        in_specs=in_specs,
        out_specs=out_specs,
        out_shape=out_shape,
        scratch_shapes=scratch,
        compiler_params=pltpu.CompilerParams(
            dimension_semantics=("arbitrary", "arbitrary"), vmem_limit_bytes=VMEM_LIMIT),
        name="inproj_conv" if conv else "inproj",
    )(*args)


def _tri_inverse(lms, ri, ci):
    shift = INV_BASE.bit_length() - 1
    eye = (ri == ci).astype(jnp.float32)
    in_block = (ri >> shift) == (ci >> shift)
    ps = [jnp.where(in_block, lm, 0.0) for lm in lms]
    ts = [eye - p for p in ps]
    for _ in range(shift - 1):
        ps = [_dot_inv(p, p) for p in ps]
        ts = [t + _dot_inv(t, p) for t, p in zip(ts, ps)]
    size = INV_BASE
    while size < CHUNK:
        shift += 1
        in_pair = (ri >> shift) == (ci >> shift)
        off_mask = in_pair & jnp.logical_not(in_block)
        tos = [_dot_inv(t, jnp.where(off_mask, lm, 0.0)) for t, lm in zip(ts, lms)]
        ts = [t - _dot_inv(to, t) for t, to in zip(ts, tos)]
        in_block = in_pair
        size *= 2
    return ts


def _gdn_kernel(x_ref, ba_ref, hist_ref, s0_ref, cw_ref, alog_ref, dtb_ref, gnw_ref,
                o_ref, sout_ref, xe_ref, s_ref, *, nb, seq, group, carry, pad_rows, preconv):
    step = pl.program_id(0)
    rows = nb * seq
    n_chunks = rows // CHUNK
    n_groups = CHUNK // group
    gshift = group.bit_length() - 1

    if carry:
        @pl.when(step == 0)
        def _():
            s_ref[...] = s0_ref[0]
    if not preconv:
        if carry:
            @pl.when(step == 0)
            def _():
                xe_ref[:, 0:SUBLANES, :] = hist_ref[...]
        else:
            xe_ref[:, 0:SUBLANES, :] = hist_ref[...]
        xe_ref[:, SUBLANES:SUBLANES + seq, :] = x_ref[:, :, 0:GDN_QKV]

    seq_rows = min(seq, CHUNK)
    seqs_per_chunk = CHUNK // seq_rows

    def chunk_rows(ref, c, row_off, cols):
        if seq >= CHUNK:
            start = row_off + c * CHUNK
            return ref[0:1, start:start + CHUNK, cols]
        b0 = c * seqs_per_chunk
        return ref[b0:b0 + seqs_per_chunk, row_off:row_off + seq, cols]

    def conv_chunk(col, c):
        cols = slice(col, col + HEAD_DIM)
        if preconv:
            return _silu(chunk_rows(x_ref, c, 0, cols).reshape(CHUNK, HEAD_DIM))
        acc = None
        for s in range(GDN_CONV):
            term = chunk_rows(xe_ref, c, SUBLANES - s, cols) * cw_ref[GDN_CONV - 1 - s:GDN_CONV - s, cols]
            acc = term if acc is None else acc + term
        return _silu(acc).reshape(CHUNK, HEAD_DIM)

    ri = lax.broadcasted_iota(jnp.int32, (CHUNK, CHUNK), 0)
    ci = lax.broadcasted_iota(jnp.int32, (CHUNK, CHUNK), 1)
    same = (ri >> gshift) == (ci >> gshift)
    m_incl = same & (ri >= ci)
    m_strict = same & (ri > ci)
    f_incl = m_incl.astype(jnp.float32)
    f_same = same.astype(jnp.float32)
    lane = lax.broadcasted_iota(jnp.int32, (CHUNK, LANES), 1)
    row_in_chunk = lax.broadcasted_iota(jnp.int32, (CHUNK, LANES), 0)

    pre = []
    for c in range(n_chunks):
        bac = chunk_rows(ba_ref, c, 0, slice(0, LANES)).reshape(CHUNK, LANES)
        beta_all = jax.nn.sigmoid(bac)
        sp_in = bac + dtb_ref[...]
        softplus = jnp.maximum(sp_in, 0.0) + jnp.log1p(jnp.exp(-jnp.abs(sp_in)))
        g_all = -jnp.exp(alog_ref[...]) * softplus
        if pad_rows and c == 0:
            valid = row_in_chunk >= pad_rows
            beta_all = jnp.where(valid, beta_all, 0.0)
            g_all = jnp.where(valid, g_all, 0.0)
        g_all = jnp.where((lane >= GDN_HEADS) & (lane < 2 * GDN_HEADS), g_all, 0.0)
        gc_col = _dot_exact(f_incl, g_all)
        if n_groups == 1:
            gtot_col = jnp.broadcast_to(gc_col[CHUNK - 1:CHUNK, :], (CHUNK, LANES))
        else:
            gtot_col = _dot_exact(f_same, g_all)
        gc_row = gc_col.T

        for h in range(GDN_HEADS):
            qh = conv_chunk(h * HEAD_DIM, c)
            kh = conv_chunk(GDN_WIDTH + h * HEAD_DIM, c)
            vh = conv_chunk(2 * GDN_WIDTH + h * HEAD_DIM, c)
            qh = qh * lax.rsqrt(jnp.sum(qh * qh, -1, keepdims=True) + EPS) * (HEAD_DIM ** -0.5)
            kh = kh * lax.rsqrt(jnp.sum(kh * kh, -1, keepdims=True) + EPS)
            bcast = lambda col: jnp.broadcast_to(col, (CHUNK, HEAD_DIM))
            gcc = bcast(gc_col[:, GDN_HEADS + h:GDN_HEADS + h + 1])
            gtc = bcast(gtot_col[:, GDN_HEADS + h:GDN_HEADS + h + 1])
            beta = bcast(beta_all[:, h:h + 1])
            gcr = gc_row[GDN_HEADS + h:GDN_HEADS + h + 1, :]
            decay = jnp.exp(jnp.where(m_incl, gcc - gcr, NEG))
            kb = kh * beta
            egc = jnp.exp(gcc)
            pre.append(dict(
                c=c, h=h,
                lm=jnp.where(m_strict, _dot_nt(kb, kh) * decay, 0.0),
                qk=_dot_nt(qh, kh) * decay,
                rhs=jnp.concatenate([vh * beta, kb * egc], axis=1),
                qg=qh * egc,
                kd_t=(kh * jnp.exp(gtc - gcc)).T,
                gl=jnp.exp(gtc)))

    inverses = _tri_inverse([p["lm"] for p in pre], ri, ci)
    sols = [_dot_inv(t, p["rhs"]) for t, p in zip(inverses, pre)]

    for c in range(n_chunks):
        r0 = c * CHUNK
        items = [(p, sol) for p, sol in zip(pre, sols) if p["c"] == c]
        state = lambda h, b: s_ref[h] if carry else s0_ref[c * n_groups + b, h]
        ws, qs = [], []
        for p, sol in items:
            w = sol[:, HEAD_DIM:]
            ws_parts, qs_parts = [], []
            for b in range(n_groups):
                g0 = b * group
                wq = jnp.concatenate([w[g0:g0 + group], p["qg"][g0:g0 + group]], axis=0)
                res = _dot(wq, state(p["h"], b))
                ws_parts.append(res[:group])
                qs_parts.append(res[group:])
            ws.append(ws_parts[0] if n_groups == 1 else jnp.concatenate(ws_parts, axis=0))
            qs.append(qs_parts[0] if n_groups == 1 else jnp.concatenate(qs_parts, axis=0))
        v_new = [sol[:, :HEAD_DIM] - w for (p, sol), w in zip(items, ws)]
        o = [a + _dot(p["qk"], vn) for a, (p, sol), vn in zip(qs, items, v_new)]
        for (p, sol), vn in zip(items, v_new):
            h = p["h"]
            for b in range(n_groups):
                g0 = b * group
                kd_b = p["kd_t"] if n_groups == 1 else jnp.where((ci >> gshift) == b, p["kd_t"], 0.0)
                st = state(h, b) * p["gl"][g0:g0 + 1, :] + _dot(kd_b, vn)
                if carry:
                    s_ref[h] = st
                else:
                    sout_ref[c * n_groups + b, h] = st
        for (p, sol), oh in zip(items, o):
            h = p["h"]
            z = chunk_rows(x_ref, c, 0, slice(COL_Z + h * HEAD_DIM, COL_Z + (h + 1) * HEAD_DIM)).reshape(CHUNK, HEAD_DIM)
            y = oh * lax.rsqrt(jnp.mean(oh * oh, -1, keepdims=True) + EPS) * gnw_ref[...] * _silu(z)
            o_ref[r0:r0 + CHUNK, h * HEAD_DIM:(h + 1) * HEAD_DIM] = y.astype(o_ref.dtype)

    if carry:
        if not preconv:
            xe_ref[:, 0:SUBLANES, :] = xe_ref[:, seq:seq + SUBLANES, :]

        @pl.when(step == pl.num_programs(0) - 1)
        def _():
            sout_ref[0] = s_ref[...]


def _gdn(x3, ba3, x_idx, hist, hist_idx, s0, cw, alog_row, dtb_row, gnw, *,
         n_steps, nb, seq, group, carry, pad_rows, preconv):
    rows = nb * seq
    if carry:
        state_spec = pl.BlockSpec((1, GDN_HEADS, HEAD_DIM, HEAD_DIM), lambda s: (0, 0, 0, 0))
        state_shape = (1, GDN_HEADS, HEAD_DIM, HEAD_DIM)
    else:
        n_states = rows // group
        state_spec = pl.BlockSpec((None, n_states, GDN_HEADS, HEAD_DIM, HEAD_DIM), lambda s: (0, s, 0, 0, 0))
        state_shape = (1, n_steps * n_states, GDN_HEADS, HEAD_DIM, HEAD_DIM)
    full = lambda shape: pl.BlockSpec(shape, lambda s: (0,) * len(shape))
    return pl.pallas_call(
        functools.partial(_gdn_kernel, nb=nb, seq=seq, group=group, carry=carry, pad_rows=pad_rows,
                          preconv=preconv),
        grid=(n_steps,),
        in_specs=[
            pl.BlockSpec((nb, seq, GDN_COLS), x_idx),
            pl.BlockSpec((nb, seq, LANES), x_idx),
            pl.BlockSpec((nb, SUBLANES, GDN_QKV), hist_idx),
            state_spec,
            full((GDN_CONV, GDN_QKV)),
            full((1, LANES)),
            full((1, LANES)),
            full((1, HEAD_DIM)),
        ],
        out_specs=[
            pl.BlockSpec((rows, GDN_WIDTH), lambda s: (s, 0)),
            state_spec,
        ],
        out_shape=[
            jax.ShapeDtypeStruct((n_steps * rows, GDN_WIDTH), jnp.bfloat16),
            jax.ShapeDtypeStruct(state_shape, jnp.float32),
        ],
        scratch_shapes=[
            pltpu.VMEM((nb, SUBLANES + (SUBLANES if preconv else seq), GDN_QKV), jnp.float32),
            pltpu.VMEM((GDN_HEADS, HEAD_DIM, HEAD_DIM), jnp.float32),
        ],
        compiler_params=pltpu.CompilerParams(
            dimension_semantics=("arbitrary",), vmem_limit_bytes=VMEM_LIMIT),
        name="gdn_seq" if carry else "gdn_batch",
    )(x3, ba3, hist, s0, cw, alog_row, dtb_row, gnw)


def _t5_bucket_np(dist):
    n = np.maximum(dist, 0)
    exact = N_BUCKETS // 2
    large = exact + (np.log(np.maximum(n, 1).astype(np.float32) / exact)
                     / math.log(MAX_DISTANCE / exact) * (N_BUCKETS - exact)).astype(np.int32)
    return np.where(n < exact, n, np.minimum(large, N_BUCKETS - 1)).astype(np.int32)


def _bucket_ids(dist, valid):
    return np.where(valid, _t5_bucket_np(dist), -1).astype(np.int32)


def _bias_kernel(table_ref, *refs):
    n = len(refs) // 2
    for ids_ref, out_ref in zip(refs[:n], refs[n:]):
        nq = ids_ref.shape[0]
        rows_per_pass = min(nq, 2 * SUBLANES)
        for r0 in range(0, nq, rows_per_pass):
            ids = ids_ref[r0:r0 + rows_per_pass, :]

            def body(b, accs):
                hit = ids == b
                return tuple(jnp.where(hit, table_ref[b, head], acc) for head, acc in enumerate(accs))

            init = tuple(jnp.full(ids.shape, NEG, jnp.float32) for _ in range(SWA_HEADS))
            for head, acc in enumerate(lax.fori_loop(0, N_BUCKETS, body, init)):
                kh, g = divmod(head, SWA_GROUP)
                out_ref[kh, g * nq + r0:g * nq + r0 + rows_per_pass, :] = acc


def _bias_tables(rel_table, id_arrays):
    out_shapes = [jax.ShapeDtypeStruct((SWA_KV_HEADS, SWA_GROUP * a.shape[0], a.shape[1]), jnp.float32)
                  for a in id_arrays]
    vmem = pl.BlockSpec(memory_space=pltpu.VMEM)
    return pl.pallas_call(
        _bias_kernel,
        in_specs=[pl.BlockSpec(memory_space=pltpu.SMEM)] + [vmem] * len(id_arrays),
        out_specs=[vmem] * len(id_arrays),
        out_shape=out_shapes,
        name="swa_bias",
    )(rel_table, *[jnp.asarray(a) for a in id_arrays])


def _attend(problems):
    scale = HEAD_DIM ** -0.5
    scores = [[_dot_nt(q, k) * scale + b for k, b in zip(keys, biases)]
              for q, keys, _, biases, _ in problems]
    maxes = []
    for (_, _, _, _, sink), segs in zip(problems, scores):
        m = sink
        for s in segs:
            m = jnp.maximum(m, jnp.max(s, axis=-1, keepdims=True))
        maxes.append(m)
    probs = [[jnp.exp(s - m) for s in segs] for segs, m in zip(scores, maxes)]
    outs = []
    for (_, _, values, _, sink), ps, m in zip(problems, probs, maxes):
        acc = None
        for p, v in zip(ps, values):
            v_ones = jnp.concatenate([v, jnp.ones((v.shape[0], HEAD_DIM), v.dtype)], axis=1)
            pv = _dot(p, v_ones)
            acc = pv if acc is None else acc + pv
        den = acc[:, HEAD_DIM:] + jnp.exp(sink - m)
        outs.append(acc[:, :HEAD_DIM] / den)
    return outs


def _group_queries(q_rows, kh):
    return jnp.concatenate(
        [q_rows((kh * SWA_GROUP + g) * HEAD_DIM, (kh * SWA_GROUP + g + 1) * HEAD_DIM)
         for g in range(SWA_GROUP)], axis=0)


def _pad_keys(rows):
    return jnp.concatenate([rows, jnp.zeros((WINDOW - rows.shape[0], rows.shape[1]), rows.dtype)], axis=0)


def _k_cols(kh):
    return slice(kh * HEAD_DIM, (kh + 1) * HEAD_DIM)


def _v_cols(kh):
    return slice((SWA_KV_HEADS + kh) * HEAD_DIM, (SWA_KV_HEADS + kh + 1) * HEAD_DIM)


def _swa_prompt_kernel(q_ref, kvc_ref, kvp_ref, kvm_ref, bcur_ref, bprev_ref, bm0_ref, bfar_ref,
                       sink_ref, o_ref):
    first = pl.program_id(0) == 0
    problems = []
    for blk in range(SWA_Q_BLOCKS):
        rows = slice(blk * WINDOW, (blk + 1) * WINDOW)
        prev_ref, prev_rows = (kvp_ref, slice(0, WINDOW)) if blk == 0 else (
            kvc_ref, slice((blk - 1) * WINDOW, blk * WINDOW))
        for kh in range(SWA_KV_HEADS):
            ks, vs = _k_cols(kh), _v_cols(kh)
            first_block = first if blk == 0 else False
            b_prev = jnp.where(first_block, NEG, bprev_ref[kh])
            b_meta = jnp.where(first_block, bm0_ref[kh], bfar_ref[kh])
            problems.append((_group_queries(lambda a, b, rows=rows: q_ref[rows, a:b], kh),
                             [kvc_ref[rows, ks], prev_ref[prev_rows, ks], _pad_keys(kvm_ref[:, ks])],
                             [kvc_ref[rows, vs], prev_ref[prev_rows, vs], _pad_keys(kvm_ref[:, vs])],
                             [bcur_ref[kh], b_prev, b_meta], sink_ref[kh]))
    for i, o in enumerate(_attend(problems)):
        blk, kh = divmod(i, SWA_KV_HEADS)
        for g in range(SWA_GROUP):
            head = kh * SWA_GROUP + g
            o_ref[blk * WINDOW:(blk + 1) * WINDOW, head * HEAD_DIM:(head + 1) * HEAD_DIM] = (
                o[g * WINDOW:(g + 1) * WINDOW].astype(o_ref.dtype))


def _swa_meta_kernel(q_ref, kv_ref, bias_ref, sink_ref, o_ref):
    problems = [(_group_queries(lambda a, b: q_ref[:, a:b], kh), [kv_ref[:, _k_cols(kh)]],
                 [kv_ref[:, _v_cols(kh)]], [bias_ref[kh]], sink_ref[kh])
                for kh in range(SWA_KV_HEADS)]
    for kh, o in enumerate(_attend(problems)):
        for g in range(SWA_GROUP):
            head = kh * SWA_GROUP + g
            o_ref[:, head * HEAD_DIM:(head + 1) * HEAD_DIM] = (
                o[g * N_META:(g + 1) * N_META].astype(o_ref.dtype))


def _swa_sample_kernel(q_ref, kvn_ref, win_ref, meta_ref, bwin_ref, bsmall_ref, sink_ref,
                       o_ref, wout_ref, *, nb, seq):
    cached = lambda ref, b, slot, n: ref[b, pl.ds(slot, n, stride=KV_SLOTS), :]
    keep = (WINDOW - seq) * KV_SLOTS
    wout_ref[:, 0:keep, :] = win_ref[:, seq * KV_SLOTS:WINDOW * KV_SLOTS, :]
    for slot in range(KV_SLOTS):
        wout_ref[:, pl.ds(keep + slot, seq, stride=KV_SLOTS), :] = (
            kvn_ref[:, :, slot * HEAD_DIM:(slot + 1) * HEAD_DIM])
    problems = []
    for b in range(nb):
        for kh in range(SWA_KV_HEADS):
            ks, vs = _k_cols(kh), _v_cols(kh)
            k_small = _pad_keys(jnp.concatenate([cached(meta_ref, b, kh, N_META), kvn_ref[b, :, ks]], axis=0))
            v_small = _pad_keys(jnp.concatenate(
                [cached(meta_ref, b, SWA_KV_HEADS + kh, N_META), kvn_ref[b, :, vs]], axis=0))
            problems.append((_group_queries(lambda a, c, b=b: q_ref[b, :, a:c], kh),
                             [cached(win_ref, b, kh, WINDOW), k_small],
                             [cached(win_ref, b, SWA_KV_HEADS + kh, WINDOW), v_small],
                             [bwin_ref[kh], bsmall_ref[kh]], sink_ref[kh]))
    outs = _attend(problems)
    for head in range(SWA_HEADS):
        kh, g = divmod(head, SWA_GROUP)
        rows = [outs[b * SWA_KV_HEADS + kh][g * seq:(g + 1) * seq] for b in range(nb)]
        o_ref[:, head * HEAD_DIM:(head + 1) * HEAD_DIM] = jnp.concatenate(rows, axis=0).astype(o_ref.dtype)


def _outproj_kernel(g_ref, s_ref, h_ref, wo_ref, nw_ref, o_ref):
    mix = (jnp.dot(g_ref[...], wo_ref[0:GDN_WIDTH, :], preferred_element_type=jnp.float32)
           + jnp.dot(s_ref[...], wo_ref[GDN_WIDTH:, :], preferred_element_type=jnp.float32))
    o_ref[...] = h_ref[...] + mix * _rms_scale(mix) * nw_ref[...]


def _outproj(g, s, h, wo, nw, *, tm):
    rows = h.shape[0]
    return pl.pallas_call(
        _outproj_kernel,
        grid=(rows // tm,),
        in_specs=[
            pl.BlockSpec((tm, GDN_WIDTH), lambda i: (i, 0)),
            pl.BlockSpec((tm, SWA_WIDTH), lambda i: (i, 0)),
            pl.BlockSpec((tm, D_MODEL), lambda i: (i, 0)),
            pl.BlockSpec((D_MODEL, D_MODEL), lambda i: (0, 0)),
            pl.BlockSpec((1, D_MODEL), lambda i: (0, 0)),
        ],
        out_specs=pl.BlockSpec((tm, D_MODEL), lambda i: (i, 0)),
        out_shape=jax.ShapeDtypeStruct((rows, D_MODEL), jnp.float32),
        compiler_params=pltpu.CompilerParams(
            dimension_semantics=("arbitrary",), vmem_limit_bytes=VMEM_LIMIT),
        name="outproj",
    )(g, s, h, wo, nw)


def _ffn_kernel(*refs, batch, tm, tf):
    if batch:
        (h_ref, nw_pre_ref, wg_ref, wu_ref, cw_ref, wd_ref, nw_post_ref, hist_ref,
         y_ref, graw_ref, xn_ref, xe_ref) = refs
    else:
        (h_ref, nw_pre_ref, wg_ref, wu_ref, cw_ref, wd_ref, nw_post_ref, hist_ref,
         y_ref, graw_ref, xn_ref, xe_ref, carry_ref) = refs
    i = pl.program_id(0)
    j = pl.program_id(1)
    last_j = pl.num_programs(1) - 1
    rb = tm // FFN_ROW_BLOCKS

    if batch:
        xe_ref[:, 0:SUBLANES, :] = hist_ref[...]
    else:
        @pl.when(i == 0)
        def _():
            carry_ref[pl.ds(j, 1)] = hist_ref[...].reshape(1, SUBLANES, tf)
        xe_ref[:, 0:SUBLANES, :] = carry_ref[pl.ds(j, 1)]

    def step(first, last):
        def gate_up(r):
            rows = slice(r * rb, (r + 1) * rb)
            if first:
                h = h_ref[rows, :]
                xn_ref[rows, :] = (h * _rms_scale(h) * nw_pre_ref[...]).astype(jnp.bfloat16)
            xn = xn_ref[rows, :]
            return (jnp.dot(xn, wg_ref[...], preferred_element_type=jnp.float32),
                    jnp.dot(xn, wu_ref[...], preferred_element_type=jnp.float32))

        nxt = gate_up(0)
        for r in range(FFN_ROW_BLOCKS):
            rows = slice(r * rb, (r + 1) * rb)
            gate, up = nxt
            if r + 1 < FFN_ROW_BLOCKS:
                nxt = gate_up(r + 1)
            if batch:
                seqs = slice(r * rb // SUBLANES, (r + 1) * rb // SUBLANES)
                graw_ref[rows, :] = gate
                xe_ref[seqs, SUBLANES:2 * SUBLANES, :] = gate.reshape(rb // SUBLANES, SUBLANES, tf)
                taps = [xe_ref[seqs, SUBLANES - s:2 * SUBLANES - s, :] for s in range(FFN_CONV)]
            else:
                base = SUBLANES + r * rb
                xe_ref[:, base:base + rb, :] = gate.reshape(1, rb, tf)
                taps = [xe_ref[:, base - s:base - s + rb, :] for s in range(FFN_CONV)]
            conv = None
            for s, tap in enumerate(taps):
                term = tap * cw_ref[FFN_CONV - 1 - s:FFN_CONV - s, :]
                conv = term if conv is None else conv + term
            act = (_silu(conv.reshape(rb, tf)) * up).astype(jnp.bfloat16)
            down = jnp.dot(act, wd_ref[...], preferred_element_type=jnp.float32)
            if first:
                y_ref[rows, :] = down
            elif last:
                y = y_ref[rows, :] + down
                y_ref[rows, :] = h_ref[rows, :] + y * _rms_scale(y) * nw_post_ref[...]
            else:
                y_ref[rows, :] += down
        if not batch:
            tail = xe_ref[:, tm:tm + SUBLANES, :]
            carry_ref[pl.ds(j, 1)] = tail
            graw_ref[...] = tail.reshape(SUBLANES, tf)

    pl.when(j == 0)(functools.partial(step, True, False))
    pl.when((j > 0) & (j < last_j))(functools.partial(step, False, False))
    pl.when(j == last_j)(functools.partial(step, False, True))


def _ffn(h, nw_pre, wg, wu, cw, wd, nw_post, hist, *, batch, tm, tf):
    rows = h.shape[0]
    nj = D_FF // tf
    in_specs = [
        pl.BlockSpec((tm, D_MODEL), lambda i, j: (i, 0)),
        pl.BlockSpec((1, D_MODEL), lambda i, j: (0, 0)),
        pl.BlockSpec((D_MODEL, tf), lambda i, j: (0, j)),
        pl.BlockSpec((D_MODEL, tf), lambda i, j: (0, j)),
        pl.BlockSpec((FFN_CONV, tf), lambda i, j: (0, j)),
        pl.BlockSpec((tf, D_MODEL), lambda i, j: (j, 0)),
        pl.BlockSpec((1, D_MODEL), lambda i, j: (0, 0)),
    ]
    args = [h, nw_pre, wg, wu, cw, wd, nw_post, hist]
    scratch = [pltpu.VMEM((tm, D_MODEL), jnp.bfloat16)]
    if batch:
        in_specs.append(pl.BlockSpec((tm // SUBLANES, SUBLANES, tf), lambda i, j: (i, 0, j)))
        graw_spec = pl.BlockSpec((tm, tf), lambda i, j: (i, j))
        graw_shape = jax.ShapeDtypeStruct((rows, D_FF), jnp.float32)
        scratch.append(pltpu.VMEM((tm // SUBLANES, 2 * SUBLANES, tf), jnp.float32))
    else:
        in_specs.append(pl.BlockSpec((SUBLANES, tf), lambda i, j: (0, j)))
        graw_spec = pl.BlockSpec((SUBLANES, tf), lambda i, j: (i, j))
        graw_shape = jax.ShapeDtypeStruct((rows // tm * SUBLANES, D_FF), jnp.float32)
        scratch.append(pltpu.VMEM((1, SUBLANES + tm, tf), jnp.float32))
        scratch.append(pltpu.VMEM((nj, SUBLANES, tf), jnp.float32))
    return pl.pallas_call(
        functools.partial(_ffn_kernel, batch=batch, tm=tm, tf=tf),
        grid=(rows // tm, nj),
        in_specs=in_specs,
        out_specs=[pl.BlockSpec((tm, D_MODEL), lambda i, j: (i, 0)), graw_spec],
        out_shape=[jax.ShapeDtypeStruct((rows, D_MODEL), jnp.float32), graw_shape],
        scratch_shapes=scratch,
        compiler_params=pltpu.CompilerParams(
            dimension_semantics=("arbitrary", "arbitrary"), vmem_limit_bytes=VMEM_LIMIT),
        name="ffn_batch" if batch else "ffn_seq",
    )(*args)


def kernel(x_prompt, x_sample, cache_swa_meta_kv, cache_swa_window_kv, state_gdn_conv, state_gdn, state_ffn_conv, meta_tokens, rel_bias_table, w_in, gdn_conv_w, gdn_a_log, gdn_dt_bias, gdn_norm_w, swa_sinks, w_out, norm_mix_pre, norm_mix_post, norm_ffn_pre, norm_ffn_post, ffn_w_gate, ffn_w_up, ffn_conv_w, ffn_w_down):
    f32, bf16 = jnp.float32, jnp.bfloat16
    seq = x_prompt.shape[1]
    dec_b, dec_t = x_sample.shape[0], x_sample.shape[1]
    n_dec = dec_b * dec_t
    assert x_prompt.shape[0] == 1 and seq % CHUNK == 0 and dec_t == SUBLANES and n_dec % CHUNK == 0

    w_in_p, w_ba = _pack_w_in(jnp.transpose(w_in[0]), 2 * GDN_HEADS, tn=512)
    wo = w_out[0].astype(bf16)
    wg = ffn_w_gate[0].astype(bf16)
    wu = ffn_w_up[0].astype(bf16)
    wd = ffn_w_down[0].astype(bf16)
    lane_pad = lambda v: jnp.pad(v.reshape(1, GDN_HEADS), ((0, 0), (GDN_HEADS, LANES - 2 * GDN_HEADS)))
    alog_row = lane_pad(gdn_a_log[0])
    dtb_row = lane_pad(gdn_dt_bias[0])
    gnw = gdn_norm_w[0].reshape(1, HEAD_DIM)

    pad_rows = CHUNK - N_META
    n_small = n_dec + CHUNK
    x_big = x_prompt.reshape(seq, D_MODEL)
    x_small = jnp.concatenate(
        [x_sample.reshape(n_dec, D_MODEL), jnp.zeros((pad_rows, D_MODEL), f32), meta_tokens.astype(f32)], axis=0)
    nw = norm_mix_pre[0].reshape(1, D_MODEL)
    cw = gdn_conv_w[0]
    proj_small, ba_small = _inproj(x_small, nw, w_in_p, w_ba, tm=n_small, tn=512, row_chunk=128)
    proj_big, ba_big, qkv_tail = _inproj(
        x_big, nw, w_in_p, w_ba, (cw, proj_small[n_small - SUBLANES:, :GDN_QKV]),
        tm=1024, tn=512, row_chunk=128)

    small_chunks = proj_small.reshape(n_small // CHUNK, CHUNK, PROJ_COLS)
    small_groups = proj_small.reshape(n_small // SUBLANES, SUBLANES, PROJ_COLS)
    last_chunk = n_small // CHUNK - 1
    gdn_meta, s_meta = _gdn(
        small_chunks, ba_small.reshape(n_small // CHUNK, CHUNK, LANES), lambda s: (last_chunk, 0, 0),
        jnp.zeros((1, SUBLANES, GDN_QKV), f32), lambda s: (0, 0, 0),
        jnp.zeros((1, GDN_HEADS, HEAD_DIM, HEAD_DIM), f32), cw, alog_row, dtb_row, gnw,
        n_steps=1, nb=1, seq=CHUNK, group=CHUNK, carry=True, pad_rows=pad_rows, preconv=False)
    gdn_big, s_prompt = _gdn(
        proj_big.reshape(1, seq, PROJ_COLS), ba_big.reshape(1, seq, LANES), lambda s: (0, s, 0),
        jnp.zeros((1, SUBLANES, GDN_QKV), f32), lambda s: (0, 0, 0),
        s_meta, cw, alog_row, dtb_row, gnw,
        n_steps=seq // (GDN_SEQ_CHUNKS * CHUNK), nb=1, seq=GDN_SEQ_CHUNKS * CHUNK, group=CHUNK,
        carry=True, pad_rows=0, preconv=True)
    hist_gdn = jnp.pad(state_gdn_conv[0], ((0, 0), (SUBLANES - (GDN_CONV - 1), 0), (0, 0)))
    nb_gdn = CHUNK // dec_t
    gdn_small, s_sample = _gdn(
        small_groups, ba_small.reshape(n_small // SUBLANES, SUBLANES, LANES), lambda s: (s, 0, 0),
        hist_gdn, lambda s: (s, 0, 0),
        state_gdn, cw, alog_row, dtb_row, gnw,
        n_steps=dec_b // nb_gdn, nb=nb_gdn, seq=dec_t, group=dec_t, carry=False, pad_rows=0, preconv=False)

    qi = np.arange(WINDOW)[:, None]
    kj = np.arange(WINDOW)[None, :]
    mi = np.arange(N_META)[None, :]
    ti = np.arange(dec_t)[:, None]
    new_keys = kj - N_META
    id_arrays = [
        _bucket_ids(qi - kj, qi >= kj),
        _bucket_ids(qi - kj + WINDOW, kj > qi),
        _bucket_ids(qi + N_META - kj, kj < N_META),
        _bucket_ids(qi + N_META - kj + WINDOW, kj < N_META),
        _bucket_ids(ti + WINDOW - kj, kj > ti),
        _bucket_ids(np.where(new_keys < 0, PAST_LEN + ti - kj, ti - new_keys),
                    (new_keys < 0) | ((new_keys <= ti) & (new_keys < dec_t))),
        _bucket_ids(mi.T - mi, mi.T >= mi),
    ]
    bcur, bprev, bm0, bfar, bwin, bsmall, bmm = _bias_tables(rel_bias_table, id_arrays)
    sink_rows = lambda q: jnp.repeat(swa_sinks[0].reshape(SWA_KV_HEADS, SWA_GROUP), q, axis=1)[..., None]

    sq_blk = COL_SQ // SWA_WIDTH
    kv_blk = COL_KV // KV_WIDTH
    meta_blk = (n_small - N_META) // N_META
    full3 = lambda a: pl.BlockSpec(a.shape, lambda j: (0, 0, 0))
    sink_p = sink_rows(WINDOW)
    swa_rows = SWA_Q_BLOCKS * WINDOW
    swa_big = pl.pallas_call(
        _swa_prompt_kernel,
        grid=(seq // swa_rows,),
        in_specs=[
            pl.BlockSpec((swa_rows, SWA_WIDTH), lambda j: (j, sq_blk)),
            pl.BlockSpec((swa_rows, KV_WIDTH), lambda j: (j, kv_blk)),
            pl.BlockSpec((WINDOW, KV_WIDTH), lambda j: (jnp.maximum(j * SWA_Q_BLOCKS - 1, 0), kv_blk)),
            pl.BlockSpec((N_META, KV_WIDTH), lambda j: (meta_blk, kv_blk)),
            full3(bcur), full3(bprev), full3(bm0), full3(bfar), full3(sink_p),
        ],
        out_specs=pl.BlockSpec((swa_rows, SWA_WIDTH), lambda j: (j, 0)),
        out_shape=jax.ShapeDtypeStruct((seq, SWA_WIDTH), bf16),
        compiler_params=pltpu.CompilerParams(
            dimension_semantics=("arbitrary",), vmem_limit_bytes=VMEM_LIMIT),
        name="swa_prompt",
    )(proj_big, proj_big, proj_big, proj_small, bcur, bprev, bm0, bfar, sink_p)

    nb_swa = 8
    sink_s = sink_rows(dec_t)
    win = cache_swa_window_kv.reshape(dec_b, WINDOW * KV_SLOTS, HEAD_DIM)
    meta_kv = cache_swa_meta_kv.reshape(dec_b, N_META * KV_SLOTS, HEAD_DIM)
    swa_small, win_new = pl.pallas_call(
        functools.partial(_swa_sample_kernel, nb=nb_swa, seq=dec_t),
        grid=(dec_b // nb_swa,),
        in_specs=[
            pl.BlockSpec((nb_swa, dec_t, SWA_WIDTH), lambda j: (j, 0, sq_blk)),
            pl.BlockSpec((nb_swa, dec_t, KV_WIDTH), lambda j: (j, 0, kv_blk)),
            pl.BlockSpec((nb_swa, WINDOW * KV_SLOTS, HEAD_DIM), lambda j: (j, 0, 0)),
            pl.BlockSpec((nb_swa, N_META * KV_SLOTS, HEAD_DIM), lambda j: (j, 0, 0)),
            full3(bwin), full3(bsmall), full3(sink_s),
        ],
        out_specs=[pl.BlockSpec((nb_swa * dec_t, SWA_WIDTH), lambda j: (j, 0)),
                   pl.BlockSpec((nb_swa, WINDOW * KV_SLOTS, HEAD_DIM), lambda j: (j, 0, 0))],
        out_shape=[jax.ShapeDtypeStruct((n_dec, SWA_WIDTH), bf16),
                   jax.ShapeDtypeStruct((dec_b, WINDOW * KV_SLOTS, HEAD_DIM), f32)],
        compiler_params=pltpu.CompilerParams(
            dimension_semantics=("arbitrary",), vmem_limit_bytes=VMEM_LIMIT),
        name="swa_sample",
    )(small_groups, small_groups, win, meta_kv, bwin, bsmall, sink_s)

    sink_m = sink_rows(N_META)
    swa_meta = pl.pallas_call(
        _swa_meta_kernel,
        grid=(1,),
        in_specs=[
            pl.BlockSpec((N_META, SWA_WIDTH), lambda j: (meta_blk, sq_blk)),
            pl.BlockSpec((N_META, KV_WIDTH), lambda j: (meta_blk, kv_blk)),
            full3(bmm), full3(sink_m),
        ],
        out_specs=pl.BlockSpec((N_META, SWA_WIDTH), lambda j: (0, 0)),
        out_shape=jax.ShapeDtypeStruct((N_META, SWA_WIDTH), bf16),
        name="swa_meta",
    )(proj_small, proj_small, bmm, sink_m)

    nw_post = norm_mix_post[0].reshape(1, D_MODEL)
    nf_pre = norm_ffn_pre[0].reshape(1, D_MODEL)
    nf_post = norm_ffn_post[0].reshape(1, D_MODEL)
    fcw = ffn_conv_w[0]
    gdn_small_all = jnp.concatenate([gdn_small, gdn_meta], axis=0)
    swa_small_all = jnp.concatenate([swa_small, jnp.zeros((pad_rows, SWA_WIDTH), bf16), swa_meta], axis=0)
    h_small = _outproj(gdn_small_all, swa_small_all, x_small, wo, nw_post, tm=n_small // 2)
    hist_ffn = jnp.pad(state_ffn_conv[0], ((0, CHUNK // SUBLANES), (SUBLANES - (FFN_CONV - 1), 0), (0, 0)))
    y_small, g_small = _ffn(h_small, nf_pre, wg, wu, fcw, wd, nf_post, hist_ffn,
                            batch=True, tm=n_small // 2, tf=512)
    h_big = _outproj(gdn_big, swa_big, x_big, wo, nw_post, tm=512)
    y_big, g_tail = _ffn(h_big, nf_pre, wg, wu, fcw, wd, nf_post, g_small[n_small - SUBLANES:],
                         batch=False, tm=1024, tf=512)

    kv_shape = lambda n: (1, n, 2, SWA_KV_HEADS, HEAD_DIM)
    kv_small = proj_small[:, COL_KV:COL_KV + KV_WIDTH]
    y_prompt = y_big.reshape(1, seq, D_MODEL)
    y_sample = y_small[:n_dec].reshape(dec_b, dec_t, D_MODEL)
    p_meta_kv = kv_small[n_small - N_META:].reshape(kv_shape(N_META))[None]
    p_window_kv = proj_big[seq - WINDOW:, COL_KV:COL_KV + KV_WIDTH].reshape(kv_shape(WINDOW))[None]
    p_gdn_conv = qkv_tail[qkv_tail.shape[0] - (GDN_CONV - 1):].reshape(1, 1, GDN_CONV - 1, GDN_QKV)
    p_gdn_state = s_prompt[None]
    p_ffn_conv = g_tail[g_tail.shape[0] - (FFN_CONV - 1):].reshape(1, 1, FFN_CONV - 1, D_FF)
    s_window_kv = win_new.reshape(1, dec_b, WINDOW, 2, SWA_KV_HEADS, HEAD_DIM)
    s_gdn_conv = proj_small[:n_dec, :GDN_QKV].reshape(dec_b, dec_t, GDN_QKV)[:, dec_t - (GDN_CONV - 1):][None]
    s_gdn_state = s_sample
    s_ffn_conv = g_small[:n_dec].reshape(dec_b, dec_t, D_FF)[:, dec_t - (FFN_CONV - 1):][None]
    return (y_prompt, y_sample, p_meta_kv, p_window_kv, p_gdn_conv, p_gdn_state, p_ffn_conv,
            s_window_kv, s_gdn_conv, s_gdn_state, s_ffn_conv)
```

```python
import functools
import math

import numpy as np
import jax
import jax.numpy as jnp
from jax import lax
from jax.experimental import pallas as pl
from jax.experimental.pallas import tpu as pltpu

D_MODEL = 2048
HEAD_DIM = 128
GDN_HEADS = 8
GDN_WIDTH = GDN_HEADS * HEAD_DIM
GDN_QKV = 3 * GDN_WIDTH
SWA_HEADS = 8
SWA_KV_HEADS = 2
SWA_GROUP = SWA_HEADS // SWA_KV_HEADS
SWA_WIDTH = SWA_HEADS * HEAD_DIM
KV_SLOTS = 2 * SWA_KV_HEADS
KV_WIDTH = KV_SLOTS * HEAD_DIM
WINDOW = 128
N_META = 16
N_BUCKETS = 32
MAX_DISTANCE = 128
GDN_CONV = 4
FFN_CONV = 3
D_FF = 5632
EPS = 1e-6
PAST_LEN = 16384

SUBLANES = 8
LANES = 128

CHUNK = 128
assert CHUNK == HEAD_DIM == LANES
INV_BASE = 16
GDN_SEQ_CHUNKS = 2
INPROJ_ROW_BLOCKS = 2
FFN_ROW_BLOCKS = 2

COL_Z = GDN_QKV
COL_SQ = COL_Z + GDN_WIDTH
COL_KV = COL_SQ + SWA_WIDTH
PROJ_COLS = COL_KV + KV_WIDTH
GDN_COLS = COL_SQ

SWA_Q_BLOCKS = 4
NEG = -1e30
VMEM_LIMIT = 56 * 1024 * 1024

_NT = (((1,), (1,)), ((), ()))


def _dot(a, b):
    return jnp.dot(a.astype(jnp.bfloat16), b.astype(jnp.bfloat16), preferred_element_type=jnp.float32)


def _dot_nt(a, b):
    return lax.dot_general(a.astype(jnp.bfloat16), b.astype(jnp.bfloat16), _NT,
                           preferred_element_type=jnp.float32)


_dot_inv = _dot


def _dot_exact(a, b, dims=None):
    if dims is None:
        return jnp.dot(a, b, precision=lax.Precision.HIGHEST, preferred_element_type=jnp.float32)
    return lax.dot_general(a, b, dims, precision=lax.Precision.HIGHEST,
                           preferred_element_type=jnp.float32)


def _pack_w_in_kernel(a_ref, b_ref, o_ref, ba_ref, *, first_shifted, shift):
    j = pl.program_id(0)

    @pl.when(j < first_shifted)
    def _():
        o_ref[...] = a_ref[...].T.astype(o_ref.dtype)

    @pl.when(j >= first_shifted)
    def _():
        rows = jnp.concatenate([a_ref[shift:, :], b_ref[:shift, :]], axis=0)
        o_ref[...] = rows.T.astype(o_ref.dtype)

    @pl.when(j == first_shifted)
    def _():
        head = a_ref[:LANES, :]
        row = lax.broadcasted_iota(jnp.int32, head.shape, 0)
        ba_ref[...] = jnp.where(row < shift, head, 0.0).T.astype(ba_ref.dtype)


def _pack_w_in(w_in_t, n_ba, *, tn):
    n_blocks = PROJ_COLS // tn
    first_shifted = COL_SQ // tn
    assert COL_SQ % tn == 0 and w_in_t.shape[0] == PROJ_COLS + n_ba and n_ba % SUBLANES == 0
    return pl.pallas_call(
        functools.partial(_pack_w_in_kernel, first_shifted=first_shifted, shift=n_ba),
        grid=(n_blocks,),
        in_specs=[pl.BlockSpec((tn, D_MODEL), lambda j: (j, 0)),
                  pl.BlockSpec((tn, D_MODEL), lambda j: (jnp.maximum(j, first_shifted) + 1, 0))],
        out_specs=[pl.BlockSpec((D_MODEL, tn), lambda j: (0, j)),
                   pl.BlockSpec((D_MODEL, LANES), lambda j: (0, 0))],
        out_shape=[jax.ShapeDtypeStruct((D_MODEL, PROJ_COLS), jnp.bfloat16),
                   jax.ShapeDtypeStruct((D_MODEL, LANES), jnp.bfloat16)],
        compiler_params=pltpu.CompilerParams(
            dimension_semantics=("arbitrary",), vmem_limit_bytes=VMEM_LIMIT),
        name="pack_w_in",
    )(w_in_t, w_in_t)


def _rms_scale(x):
    return lax.rsqrt(jnp.mean(x * x, axis=-1, keepdims=True) + EPS)


def _silu(x):
    return x * (0.5 * jnp.tanh(0.5 * x) + 0.5)


def _inproj_kernel(*refs, row_chunk, conv_tiles):
    if conv_tiles:
        (x_ref, nw_ref, w_ref, wba_ref, cw_ref, hist_ref,
         o_ref, ba_ref, tail_ref, xn_ref, xe_ref, carry_ref) = refs
    else:
        x_ref, nw_ref, w_ref, wba_ref, o_ref, ba_ref, xn_ref = refs
    i = pl.program_id(0)
    j = pl.program_id(1)
    tm, tn = o_ref.shape

    def normalize(rows):
        x = x_ref[rows, :]
        xn_ref[rows, :] = (x * _rms_scale(x) * nw_ref[...]).astype(jnp.bfloat16)

    def plain():
        o_ref[...] = jnp.dot(xn_ref[...], w_ref[...], preferred_element_type=jnp.float32)

    if not conv_tiles:
        @pl.when(j == 0)
        def _():
            def body(c, carry):
                normalize(pl.ds(pl.multiple_of(c * row_chunk, row_chunk), row_chunk))
                return carry
            lax.fori_loop(0, tm // row_chunk, body, 0)
            ba_ref[...] = jnp.dot(xn_ref[...], wba_ref[...], preferred_element_type=jnp.float32)
        plain()
        return
    pl.when(j >= conv_tiles)(plain)

    def conv_tile(first):
        @pl.when(i == 0)
        def _():
            carry_ref[pl.ds(j, 1)] = hist_ref[...].reshape(1, SUBLANES, tn)
        xe_ref[0:SUBLANES, :] = carry_ref[pl.ds(j, 1)].reshape(SUBLANES, tn)
        rb = tm // INPROJ_ROW_BLOCKS

        def raw_block(r):
            rows = slice(r * rb, (r + 1) * rb)
            if first:
                normalize(rows)
                ba_ref[rows, :] = jnp.dot(xn_ref[rows, :], wba_ref[...], preferred_element_type=jnp.float32)
            return jnp.dot(xn_ref[rows, :], w_ref[...], preferred_element_type=jnp.float32)

        nxt = raw_block(0)
        for r in range(INPROJ_ROW_BLOCKS):
            raw = nxt
            if r + 1 < INPROJ_ROW_BLOCKS:
                nxt = raw_block(r + 1)
            base = SUBLANES + r * rb
            xe_ref[base:base + rb, :] = raw
            conv = None
            for s in range(GDN_CONV):
                term = xe_ref[base - s:base - s + rb, :] * cw_ref[GDN_CONV - 1 - s:GDN_CONV - s, :]
                conv = term if conv is None else conv + term
            o_ref[r * rb:(r + 1) * rb, :] = conv
        tail = xe_ref[tm:tm + SUBLANES, :]
        carry_ref[pl.ds(j, 1)] = tail.reshape(1, SUBLANES, tn)
        tail_ref[...] = tail

    pl.when(j == 0)(functools.partial(conv_tile, True))
    pl.when((j > 0) & (j < conv_tiles))(functools.partial(conv_tile, False))


def _inproj(x, nw, w, wba, conv=None, *, tm, tn, row_chunk):
    rows = x.shape[0]
    conv_tiles = GDN_QKV // tn if conv else 0
    in_specs = [
        pl.BlockSpec((tm, D_MODEL), lambda i, j: (i, 0)),
        pl.BlockSpec((1, D_MODEL), lambda i, j: (0, 0)),
        pl.BlockSpec((D_MODEL, tn), lambda i, j: (0, j)),
        pl.BlockSpec((D_MODEL, LANES), lambda i, j: (0, 0)),
    ]
    out_specs = [pl.BlockSpec((tm, tn), lambda i, j: (i, j)),
                 pl.BlockSpec((tm, LANES), lambda i, j: (i, 0))]
    out_shape = [jax.ShapeDtypeStruct((rows, PROJ_COLS), jnp.float32),
                 jax.ShapeDtypeStruct((rows, LANES), jnp.float32)]
    scratch = [pltpu.VMEM((tm, D_MODEL), jnp.bfloat16)]
    args = [x, nw, w, wba]
    if conv:
        conv_col = lambda i, j: (0, jnp.minimum(j, conv_tiles - 1))
        in_specs += [pl.BlockSpec((GDN_CONV, tn), conv_col), pl.BlockSpec((SUBLANES, tn), conv_col)]
        out_specs.append(pl.BlockSpec((SUBLANES, tn), lambda i, j: (i, jnp.minimum(j, conv_tiles - 1))))
        out_shape.append(jax.ShapeDtypeStruct((rows // tm * SUBLANES, GDN_QKV), jnp.float32))
        scratch += [pltpu.VMEM((SUBLANES + tm, tn), jnp.float32),
                    pltpu.VMEM((conv_tiles, SUBLANES, tn), jnp.float32)]
        args += list(conv)
    return pl.pallas_call(
        functools.partial(_inproj_kernel, row_chunk=row_chunk, conv_tiles=conv_tiles),
        grid=(rows // tm, PROJ_COLS // tn),
        in_specs=in_specs,
        out_specs=out_specs,
        out_shape=out_shape,
        scratch_shapes=scratch,
        compiler_params=pltpu.CompilerParams(
            dimension_semantics=("arbitrary", "arbitrary"), vmem_limit_bytes=VMEM_LIMIT),
        name="inproj_conv" if conv else "inproj",
    )(*args)


def _tri_inverse(lms, ri, ci):
    shift = INV_BASE.bit_length() - 1
    eye = (ri == ci).astype(jnp.float32)
    in_block = (ri >> shift) == (ci >> shift)
    ps = [jnp.where(in_block, lm, 0.0) for lm in lms]
    ts = [eye - p for p in ps]
    for _ in range(shift - 1):
        ps = [_dot_inv(p, p) for p in ps]
        ts = [t + _dot_inv(t, p) for t, p in zip(ts, ps)]
    size = INV_BASE
    while size < CHUNK:
        shift += 1
        in_pair = (ri >> shift) == (ci >> shift)
        off_mask = in_pair & jnp.logical_not(in_block)
        tos = [_dot_inv(t, jnp.where(off_mask, lm, 0.0)) for t, lm in zip(ts, lms)]
        ts = [t - _dot_inv(to, t) for t, to in zip(ts, tos)]
        in_block = in_pair
        size *= 2
    return ts


def _gdn_kernel(x_ref, ba_ref, hist_ref, s0_ref, cw_ref, alog_ref, dtb_ref, gnw_ref,
                o_ref, sout_ref, xe_ref, s_ref, *, nb, seq, group, carry, pad_rows, preconv):
    step = pl.program_id(0)
    rows = nb * seq
    n_chunks = rows // CHUNK
    n_groups = CHUNK // group
    gshift = group.bit_length() - 1

    if carry:
        @pl.when(step == 0)
        def _():
            s_ref[...] = s0_ref[0]
    if not preconv:
        if carry:
            @pl.when(step == 0)
            def _():
                xe_ref[:, 0:SUBLANES, :] = hist_ref[...]
        else:
            xe_ref[:, 0:SUBLANES, :] = hist_ref[...]
        xe_ref[:, SUBLANES:SUBLANES + seq, :] = x_ref[:, :, 0:GDN_QKV]

    seq_rows = min(seq, CHUNK)
    seqs_per_chunk = CHUNK // seq_rows

    def chunk_rows(ref, c, row_off, cols):
        if seq >= CHUNK:
            start = row_off + c * CHUNK
            return ref[0:1, start:start + CHUNK, cols]
        b0 = c * seqs_per_chunk
        return ref[b0:b0 + seqs_per_chunk, row_off:row_off + seq, cols]

    def conv_chunk(col, c):
        cols = slice(col, col + HEAD_DIM)
        if preconv:
            return _silu(chunk_rows(x_ref, c, 0, cols).reshape(CHUNK, HEAD_DIM))
        acc = None
        for s in range(GDN_CONV):
            term = chunk_rows(xe_ref, c, SUBLANES - s, cols) * cw_ref[GDN_CONV - 1 - s:GDN_CONV - s, cols]
            acc = term if acc is None else acc + term
        return _silu(acc).reshape(CHUNK, HEAD_DIM)

    ri = lax.broadcasted_iota(jnp.int32, (CHUNK, CHUNK), 0)
    ci = lax.broadcasted_iota(jnp.int32, (CHUNK, CHUNK), 1)
    same = (ri >> gshift) == (ci >> gshift)
    m_incl = same & (ri >= ci)
    m_strict = same & (ri > ci)
    f_incl = m_incl.astype(jnp.float32)
    f_same = same.astype(jnp.float32)
    lane = lax.broadcasted_iota(jnp.int32, (CHUNK, LANES), 1)
    row_in_chunk = lax.broadcasted_iota(jnp.int32, (CHUNK, LANES), 0)

    pre = []
    for c in range(n_chunks):
        bac = chunk_rows(ba_ref, c, 0, slice(0, LANES)).reshape(CHUNK, LANES)
        beta_all = jax.nn.sigmoid(bac)
        sp_in = bac + dtb_ref[...]
        softplus = jnp.maximum(sp_in, 0.0) + jnp.log1p(jnp.exp(-jnp.abs(sp_in)))
        g_all = -jnp.exp(alog_ref[...]) * softplus
        if pad_rows and c == 0:
            valid = row_in_chunk >= pad_rows
            beta_all = jnp.where(valid, beta_all, 0.0)
            g_all = jnp.where(valid, g_all, 0.0)
        g_all = jnp.where((lane >= GDN_HEADS) & (lane < 2 * GDN_HEADS), g_all, 0.0)
        gc_col = _dot_exact(f_incl, g_all)
        if n_groups == 1:
            gtot_col = jnp.broadcast_to(gc_col[CHUNK - 1:CHUNK, :], (CHUNK, LANES))
        else:
            gtot_col = _dot_exact(f_same, g_all)
        gc_row = gc_col.T

        for h in range(GDN_HEADS):
            qh = conv_chunk(h * HEAD_DIM, c)
            kh = conv_chunk(GDN_WIDTH + h * HEAD_DIM, c)
            vh = conv_chunk(2 * GDN_WIDTH + h * HEAD_DIM, c)
            qh = qh * lax.rsqrt(jnp.sum(qh * qh, -1, keepdims=True) + EPS) * (HEAD_DIM ** -0.5)
            kh = kh * lax.rsqrt(jnp.sum(kh * kh, -1, keepdims=True) + EPS)
            bcast = lambda col: jnp.broadcast_to(col, (CHUNK, HEAD_DIM))
            gcc = bcast(gc_col[:, GDN_HEADS + h:GDN_HEADS + h + 1])
            gtc = bcast(gtot_col[:, GDN_HEADS + h:GDN_HEADS + h + 1])
            beta = bcast(beta_all[:, h:h + 1])
            gcr = gc_row[GDN_HEADS + h:GDN_HEADS + h + 1, :]
            decay = jnp.exp(jnp.where(m_incl, gcc - gcr, NEG))
            kb = kh * beta
            egc = jnp.exp(gcc)
            pre.append(dict(
                c=c, h=h,
                lm=jnp.where(m_strict, _dot_nt(kb, kh) * decay, 0.0),
                qk=_dot_nt(qh, kh) * decay,
                rhs=jnp.concatenate([vh * beta, kb * egc], axis=1),
                qg=qh * egc,
                kd=kh * jnp.exp(gtc - gcc),
                gl=jnp.exp(gtc)))

    inverses = _tri_inverse([p["lm"] for p in pre], ri, ci)
    sols = [_dot_inv(t, p["rhs"]) for t, p in zip(inverses, pre)]

    for c in range(n_chunks):
        r0 = c * CHUNK
        items = [(p, sol) for p, sol in zip(pre, sols) if p["c"] == c]
        state = lambda h, b: s_ref[h] if carry else s0_ref[c * n_groups + b, h]
        ws, qs = [], []
        for p, sol in items:
            w = sol[:, HEAD_DIM:]
            ws_parts, qs_parts = [], []
            for b in range(n_groups):
                g0 = b * group
                wq = jnp.concatenate([w[g0:g0 + group], p["qg"][g0:g0 + group]], axis=0)
                res = _dot(wq, state(p["h"], b))
                ws_parts.append(res[:group])
                qs_parts.append(res[group:])
            ws.append(ws_parts[0] if n_groups == 1 else jnp.concatenate(ws_parts, axis=0))
            qs.append(qs_parts[0] if n_groups == 1 else jnp.concatenate(qs_parts, axis=0))
        v_new = [sol[:, :HEAD_DIM] - w for (p, sol), w in zip(items, ws)]
        o = [a + _dot(p["qk"], vn) for a, (p, sol), vn in zip(qs, items, v_new)]
        for (p, sol), vn in zip(items, v_new):
            h = p["h"]
            kd_t = None if n_groups == 1 else p["kd"].T
            for b in range(n_groups):
                g0 = b * group
                if n_groups == 1:
                    upd = lax.dot_general(p["kd"].astype(jnp.bfloat16), vn.astype(jnp.bfloat16),
                                          (((0,), (0,)), ((), ())), preferred_element_type=jnp.float32)
                else:
                    upd = _dot(jnp.where((ci >> gshift) == b, kd_t, 0.0), vn)
                st = state(h, b) * p["gl"][g0:g0 + 1, :] + upd
                if carry:
                    s_ref[h] = st
                else:
                    sout_ref[c * n_groups + b, h] = st
        for (p, sol), oh in zip(items, o):
            h = p["h"]
            z = chunk_rows(x_ref, c, 0, slice(COL_Z + h * HEAD_DIM, COL_Z + (h + 1) * HEAD_DIM)).reshape(CHUNK, HEAD_DIM)
            y = oh * lax.rsqrt(jnp.mean(oh * oh, -1, keepdims=True) + EPS) * gnw_ref[...] * _silu(z)
            o_ref[r0:r0 + CHUNK, h * HEAD_DIM:(h + 1) * HEAD_DIM] = y.astype(o_ref.dtype)

    if carry:
        if not preconv:
            xe_ref[:, 0:SUBLANES, :] = xe_ref[:, seq:seq + SUBLANES, :]

        @pl.when(step == pl.num_programs(0) - 1)
        def _():
            sout_ref[0] = s_ref[...]


def _gdn(x3, ba3, x_idx, hist, hist_idx, s0, cw, alog_row, dtb_row, gnw, *,
         n_steps, nb, seq, group, carry, pad_rows, preconv):
    rows = nb * seq
    if carry:
        state_spec = pl.BlockSpec((1, GDN_HEADS, HEAD_DIM, HEAD_DIM), lambda s: (0, 0, 0, 0))
        state_shape = (1, GDN_HEADS, HEAD_DIM, HEAD_DIM)
    else:
        n_states = rows // group
        state_spec = pl.BlockSpec((None, n_states, GDN_HEADS, HEAD_DIM, HEAD_DIM), lambda s: (0, s, 0, 0, 0))
        state_shape = (1, n_steps * n_states, GDN_HEADS, HEAD_DIM, HEAD_DIM)
    full = lambda shape: pl.BlockSpec(shape, lambda s: (0,) * len(shape))
    return pl.pallas_call(
        functools.partial(_gdn_kernel, nb=nb, seq=seq, group=group, carry=carry, pad_rows=pad_rows,
                          preconv=preconv),
        grid=(n_steps,),
        in_specs=[
            pl.BlockSpec((nb, seq, GDN_COLS), x_idx),
            pl.BlockSpec((nb, seq, LANES), x_idx),
            pl.BlockSpec((nb, SUBLANES, GDN_QKV), hist_idx),
            state_spec,
            full((GDN_CONV, GDN_QKV)),
            full((1, LANES)),
            full((1, LANES)),
            full((1, HEAD_DIM)),
        ],
        out_specs=[
            pl.BlockSpec((rows, GDN_WIDTH), lambda s: (s, 0)),
            state_spec,
        ],
        out_shape=[
            jax.ShapeDtypeStruct((n_steps * rows, GDN_WIDTH), jnp.bfloat16),
            jax.ShapeDtypeStruct(state_shape, jnp.float32),
        ],
        scratch_shapes=[
            pltpu.VMEM((nb, SUBLANES + (SUBLANES if preconv else seq), GDN_QKV), jnp.float32),
            pltpu.VMEM((GDN_HEADS, HEAD_DIM, HEAD_DIM), jnp.float32),
        ],
        compiler_params=pltpu.CompilerParams(
            dimension_semantics=("arbitrary",), vmem_limit_bytes=VMEM_LIMIT),
        name="gdn_seq" if carry else "gdn_batch",
    )(x3, ba3, hist, s0, cw, alog_row, dtb_row, gnw)


def _t5_bucket_np(dist):
    n = np.maximum(dist, 0)
    exact = N_BUCKETS // 2
    large = exact + (np.log(np.maximum(n, 1).astype(np.float32) / exact)
                     / math.log(MAX_DISTANCE / exact) * (N_BUCKETS - exact)).astype(np.int32)
    return np.where(n < exact, n, np.minimum(large, N_BUCKETS - 1)).astype(np.int32)


def _bucket_ids(dist, valid):
    return np.where(valid, _t5_bucket_np(dist), -1).astype(np.int32)


def _bias_kernel(table_ref, *refs):
    n = len(refs) // 2
    for ids_ref, out_ref in zip(refs[:n], refs[n:]):
        nq = ids_ref.shape[0]
        rows_per_pass = min(nq, 2 * SUBLANES)
        for r0 in range(0, nq, rows_per_pass):
            ids = ids_ref[r0:r0 + rows_per_pass, :]

            def body(b, accs):
                hit = ids == b
                return tuple(jnp.where(hit, table_ref[b, head], acc) for head, acc in enumerate(accs))

            init = tuple(jnp.full(ids.shape, NEG, jnp.float32) for _ in range(SWA_HEADS))
            for head, acc in enumerate(lax.fori_loop(0, N_BUCKETS, body, init)):
                kh, g = divmod(head, SWA_GROUP)
                out_ref[kh, g * nq + r0:g * nq + r0 + rows_per_pass, :] = acc


def _bias_tables(rel_table, id_arrays):
    out_shapes = [jax.ShapeDtypeStruct((SWA_KV_HEADS, SWA_GROUP * a.shape[0], a.shape[1]), jnp.float32)
                  for a in id_arrays]
    vmem = pl.BlockSpec(memory_space=pltpu.VMEM)
    return pl.pallas_call(
        _bias_kernel,
        in_specs=[pl.BlockSpec(memory_space=pltpu.SMEM)] + [vmem] * len(id_arrays),
        out_specs=[vmem] * len(id_arrays),
        out_shape=out_shapes,
        name="swa_bias",
    )(rel_table, *[jnp.asarray(a) for a in id_arrays])


def _attend(problems):
    scale = HEAD_DIM ** -0.5
    scores = [[_dot_nt(q, k) * scale + b for k, b in zip(keys, biases)]
              for q, keys, _, biases, _ in problems]
    maxes = []
    for (_, _, _, _, sink), segs in zip(problems, scores):
        m = sink
        for s in segs:
            m = jnp.maximum(m, jnp.max(s, axis=-1, keepdims=True))
        maxes.append(m)
    probs = [[jnp.exp(s - m) for s in segs] for segs, m in zip(scores, maxes)]
    outs = []
    for (_, _, values, _, sink), ps, m in zip(problems, probs, maxes):
        acc = None
        for p, v in zip(ps, values):
            v_ones = jnp.concatenate([v, jnp.ones((v.shape[0], HEAD_DIM), v.dtype)], axis=1)
            pv = _dot(p, v_ones)
            acc = pv if acc is None else acc + pv
        den = acc[:, HEAD_DIM:] + jnp.exp(sink - m)
        outs.append(acc[:, :HEAD_DIM] / den)
    return outs


def _group_queries(q_rows, kh):
    return jnp.concatenate(
        [q_rows((kh * SWA_GROUP + g) * HEAD_DIM, (kh * SWA_GROUP + g + 1) * HEAD_DIM)
         for g in range(SWA_GROUP)], axis=0)


def _pad_keys(rows):
    return jnp.concatenate([rows, jnp.zeros((WINDOW - rows.shape[0], rows.shape[1]), rows.dtype)], axis=0)


def _k_cols(kh):
    return slice(kh * HEAD_DIM, (kh + 1) * HEAD_DIM)


def _v_cols(kh):
    return slice((SWA_KV_HEADS + kh) * HEAD_DIM, (SWA_KV_HEADS + kh + 1) * HEAD_DIM)


def _swa_prompt_kernel(q_ref, kvc_ref, kvp_ref, kvm_ref, bcur_ref, bprev_ref, bm0_ref, bfar_ref,
                       sink_ref, o_ref):
    first = pl.program_id(0) == 0
    problems = []
    for blk in range(SWA_Q_BLOCKS):
        rows = slice(blk * WINDOW, (blk + 1) * WINDOW)
        prev_ref, prev_rows = (kvp_ref, slice(0, WINDOW)) if blk == 0 else (
            kvc_ref, slice((blk - 1) * WINDOW, blk * WINDOW))
        for kh in range(SWA_KV_HEADS):
            ks, vs = _k_cols(kh), _v_cols(kh)
            first_block = first if blk == 0 else False
            b_prev = jnp.where(first_block, NEG, bprev_ref[kh])
            b_meta = jnp.where(first_block, bm0_ref[kh], bfar_ref[kh])
            problems.append((_group_queries(lambda a, b, rows=rows: q_ref[rows, a:b], kh),
                             [kvc_ref[rows, ks], prev_ref[prev_rows, ks], _pad_keys(kvm_ref[:, ks])],
                             [kvc_ref[rows, vs], prev_ref[prev_rows, vs], _pad_keys(kvm_ref[:, vs])],
                             [bcur_ref[kh], b_prev, b_meta], sink_ref[kh]))
    for i, o in enumerate(_attend(problems)):
        blk, kh = divmod(i, SWA_KV_HEADS)
        for g in range(SWA_GROUP):
            head = kh * SWA_GROUP + g
            o_ref[blk * WINDOW:(blk + 1) * WINDOW, head * HEAD_DIM:(head + 1) * HEAD_DIM] = (
                o[g * WINDOW:(g + 1) * WINDOW].astype(o_ref.dtype))


def _swa_meta_kernel(q_ref, kv_ref, bias_ref, sink_ref, o_ref):
    problems = [(_group_queries(lambda a, b: q_ref[:, a:b], kh), [kv_ref[:, _k_cols(kh)]],
                 [kv_ref[:, _v_cols(kh)]], [bias_ref[kh]], sink_ref[kh])
                for kh in range(SWA_KV_HEADS)]
    for kh, o in enumerate(_attend(problems)):
        for g in range(SWA_GROUP):
            head = kh * SWA_GROUP + g
            o_ref[:, head * HEAD_DIM:(head + 1) * HEAD_DIM] = (
                o[g * N_META:(g + 1) * N_META].astype(o_ref.dtype))


def _swa_sample_kernel(q_ref, kvn_ref, win_ref, meta_ref, bwin_ref, bsmall_ref, sink_ref,
                       o_ref, wout_ref, *, nb, seq):
    cached = lambda ref, b, slot, n: ref[b, pl.ds(slot, n, stride=KV_SLOTS), :]
    keep = (WINDOW - seq) * KV_SLOTS
    wout_ref[:, 0:keep, :] = win_ref[:, seq * KV_SLOTS:WINDOW * KV_SLOTS, :]
    for slot in range(KV_SLOTS):
        wout_ref[:, pl.ds(keep + slot, seq, stride=KV_SLOTS), :] = (
            kvn_ref[:, :, slot * HEAD_DIM:(slot + 1) * HEAD_DIM])
    problems = []
    for b in range(nb):
        for kh in range(SWA_KV_HEADS):
            ks, vs = _k_cols(kh), _v_cols(kh)
            k_small = _pad_keys(jnp.concatenate([cached(meta_ref, b, kh, N_META), kvn_ref[b, :, ks]], axis=0))
            v_small = _pad_keys(jnp.concatenate(
                [cached(meta_ref, b, SWA_KV_HEADS + kh, N_META), kvn_ref[b, :, vs]], axis=0))
            problems.append((_group_queries(lambda a, c, b=b: q_ref[b, :, a:c], kh),
                             [cached(win_ref, b, kh, WINDOW), k_small],
                             [cached(win_ref, b, SWA_KV_HEADS + kh, WINDOW), v_small],
                             [bwin_ref[kh], bsmall_ref[kh]], sink_ref[kh]))
    outs = _attend(problems)
    for head in range(SWA_HEADS):
        kh, g = divmod(head, SWA_GROUP)
        rows = [outs[b * SWA_KV_HEADS + kh][g * seq:(g + 1) * seq] for b in range(nb)]
        o_ref[:, head * HEAD_DIM:(head + 1) * HEAD_DIM] = jnp.concatenate(rows, axis=0).astype(o_ref.dtype)


def _outproj_kernel(g_ref, s_ref, h_ref, wo_ref, nw_ref, o_ref):
    both = jnp.concatenate([g_ref[...], s_ref[...]], axis=1)
    mix = jnp.dot(both, wo_ref[...], preferred_element_type=jnp.float32)
    o_ref[...] = h_ref[...] + mix * _rms_scale(mix) * nw_ref[...]


def _outproj(g, s, h, wo, nw, *, tm):
    rows = h.shape[0]
    return pl.pallas_call(
        _outproj_kernel,
        grid=(rows // tm,),
        in_specs=[
            pl.BlockSpec((tm, GDN_WIDTH), lambda i: (i, 0)),
            pl.BlockSpec((tm, SWA_WIDTH), lambda i: (i, 0)),
            pl.BlockSpec((tm, D_MODEL), lambda i: (i, 0)),
            pl.BlockSpec((D_MODEL, D_MODEL), lambda i: (0, 0)),
            pl.BlockSpec((1, D_MODEL), lambda i: (0, 0)),
        ],
        out_specs=pl.BlockSpec((tm, D_MODEL), lambda i: (i, 0)),
        out_shape=jax.ShapeDtypeStruct((rows, D_MODEL), jnp.float32),
        compiler_params=pltpu.CompilerParams(
            dimension_semantics=("arbitrary",), vmem_limit_bytes=VMEM_LIMIT),
        name="outproj",
    )(g, s, h, wo, nw)


def _ffn_kernel(*refs, batch, tm, tf):
    if batch:
        (h_ref, nw_pre_ref, wg_ref, wu_ref, cw_ref, wd_ref, nw_post_ref, hist_ref,
         y_ref, graw_ref, xn_ref, xe_ref) = refs
    else:
        (h_ref, nw_pre_ref, wg_ref, wu_ref, cw_ref, wd_ref, nw_post_ref, hist_ref,
         y_ref, graw_ref, xn_ref, xe_ref, carry_ref) = refs
    i = pl.program_id(0)
    j = pl.program_id(1)
    last_j = pl.num_programs(1) - 1
    rb = tm // FFN_ROW_BLOCKS

    if batch:
        xe_ref[:, 0:SUBLANES, :] = hist_ref[...]
    else:
        @pl.when(i == 0)
        def _():
            carry_ref[pl.ds(j, 1)] = hist_ref[...].reshape(1, SUBLANES, tf)
        xe_ref[:, 0:SUBLANES, :] = carry_ref[pl.ds(j, 1)]

    def step(first, last):
        def gate_up(r):
            rows = slice(r * rb, (r + 1) * rb)
            if first:
                h = h_ref[rows, :]
                xn_ref[rows, :] = (h * _rms_scale(h) * nw_pre_ref[...]).astype(jnp.bfloat16)
            xn = xn_ref[rows, :]
            return (jnp.dot(xn, wg_ref[...], preferred_element_type=jnp.float32),
                    jnp.dot(xn, wu_ref[...], preferred_element_type=jnp.float32))

        nxt = gate_up(0)
        for r in range(FFN_ROW_BLOCKS):
            rows = slice(r * rb, (r + 1) * rb)
            gate, up = nxt
            if r + 1 < FFN_ROW_BLOCKS:
                nxt = gate_up(r + 1)
            if batch:
                seqs = slice(r * rb // SUBLANES, (r + 1) * rb // SUBLANES)
                graw_ref[rows, :] = gate
                xe_ref[seqs, SUBLANES:2 * SUBLANES, :] = gate.reshape(rb // SUBLANES, SUBLANES, tf)
                taps = [xe_ref[seqs, SUBLANES - s:2 * SUBLANES - s, :] for s in range(FFN_CONV)]
            else:
                base = SUBLANES + r * rb
                xe_ref[:, base:base + rb, :] = gate.reshape(1, rb, tf)
                taps = [xe_ref[:, base - s:base - s + rb, :] for s in range(FFN_CONV)]
            conv = None
            for s, tap in enumerate(taps):
                term = tap * cw_ref[FFN_CONV - 1 - s:FFN_CONV - s, :]
                conv = term if conv is None else conv + term
            act = (_silu(conv.reshape(rb, tf)) * up).astype(jnp.bfloat16)
            down = jnp.dot(act, wd_ref[...], preferred_element_type=jnp.float32)
            if first:
                y_ref[rows, :] = down
            elif last:
                y = y_ref[rows, :] + down
                y_ref[rows, :] = h_ref[rows, :] + y * _rms_scale(y) * nw_post_ref[...]
            else:
                y_ref[rows, :] += down
        if not batch:
            tail = xe_ref[:, tm:tm + SUBLANES, :]
            carry_ref[pl.ds(j, 1)] = tail
            graw_ref[...] = tail.reshape(SUBLANES, tf)

    pl.when(j == 0)(functools.partial(step, True, False))
    pl.when((j > 0) & (j < last_j))(functools.partial(step, False, False))
    pl.when(j == last_j)(functools.partial(step, False, True))


def _ffn(h, nw_pre, wg, wu, cw, wd, nw_post, hist, *, batch, tm, tf, out_rows=None):
    rows = h.shape[0]
    nj = D_FF // tf
    out_rows = rows if out_rows is None else out_rows
    in_specs = [
        pl.BlockSpec((tm, D_MODEL), lambda i, j: (i, 0)),
        pl.BlockSpec((1, D_MODEL), lambda i, j: (0, 0)),
        pl.BlockSpec((D_MODEL, tf), lambda i, j: (0, j)),
        pl.BlockSpec((D_MODEL, tf), lambda i, j: (0, j)),
        pl.BlockSpec((FFN_CONV, tf), lambda i, j: (0, j)),
        pl.BlockSpec((tf, D_MODEL), lambda i, j: (j, 0)),
        pl.BlockSpec((1, D_MODEL), lambda i, j: (0, 0)),
    ]
    args = [h, nw_pre, wg, wu, cw, wd, nw_post, hist]
    scratch = [pltpu.VMEM((tm, D_MODEL), jnp.bfloat16)]
    if batch:
        in_specs.append(pl.BlockSpec((tm // SUBLANES, SUBLANES, tf), lambda i, j: (i, 0, j)))
        graw_spec = pl.BlockSpec((tm, tf), lambda i, j: (i, j))
        graw_shape = jax.ShapeDtypeStruct((rows, D_FF), jnp.float32)
        scratch.append(pltpu.VMEM((tm // SUBLANES, 2 * SUBLANES, tf), jnp.float32))
    else:
        in_specs.append(pl.BlockSpec((SUBLANES, tf), lambda i, j: (0, j)))
        graw_spec = pl.BlockSpec((SUBLANES, tf), lambda i, j: (i, j))
        graw_shape = jax.ShapeDtypeStruct((rows // tm * SUBLANES, D_FF), jnp.float32)
        scratch.append(pltpu.VMEM((1, SUBLANES + tm, tf), jnp.float32))
        scratch.append(pltpu.VMEM((nj, SUBLANES, tf), jnp.float32))
    return pl.pallas_call(
        functools.partial(_ffn_kernel, batch=batch, tm=tm, tf=tf),
        grid=(rows // tm, nj),
        in_specs=in_specs,
        out_specs=[pl.BlockSpec((tm, D_MODEL), lambda i, j: (i, 0)), graw_spec],
        out_shape=[jax.ShapeDtypeStruct((out_rows, D_MODEL), jnp.float32), graw_shape],
        scratch_shapes=scratch,
        compiler_params=pltpu.CompilerParams(
            dimension_semantics=("arbitrary", "arbitrary"), vmem_limit_bytes=VMEM_LIMIT),
        name="ffn_batch" if batch else "ffn_seq",
    )(*args)


def kernel(x_prompt, x_sample, cache_swa_meta_kv, cache_swa_window_kv, state_gdn_conv, state_gdn, state_ffn_conv, meta_tokens, rel_bias_table, w_in, gdn_conv_w, gdn_a_log, gdn_dt_bias, gdn_norm_w, swa_sinks, w_out, norm_mix_pre, norm_mix_post, norm_ffn_pre, norm_ffn_post, ffn_w_gate, ffn_w_up, ffn_conv_w, ffn_w_down):
    f32, bf16 = jnp.float32, jnp.bfloat16
    seq = x_prompt.shape[1]
    dec_b, dec_t = x_sample.shape[0], x_sample.shape[1]
    n_dec = dec_b * dec_t
    assert x_prompt.shape[0] == 1 and seq % CHUNK == 0 and dec_t == SUBLANES and n_dec % CHUNK == 0

    w_in_p, w_ba = _pack_w_in(jnp.transpose(w_in[0]), 2 * GDN_HEADS, tn=512)
    wo = w_out[0].astype(bf16)
    wg = ffn_w_gate[0].astype(bf16)
    wu = ffn_w_up[0].astype(bf16)
    wd = ffn_w_down[0].astype(bf16)
    lane_pad = lambda v: jnp.pad(v.reshape(1, GDN_HEADS), ((0, 0), (GDN_HEADS, LANES - 2 * GDN_HEADS)))
    alog_row = lane_pad(gdn_a_log[0])
    dtb_row = lane_pad(gdn_dt_bias[0])
    gnw = gdn_norm_w[0].reshape(1, HEAD_DIM)

    pad_rows = CHUNK - N_META
    n_small = n_dec + CHUNK
    x_big = x_prompt.reshape(seq, D_MODEL)
    x_small = jnp.concatenate(
        [x_sample.reshape(n_dec, D_MODEL), jnp.zeros((pad_rows, D_MODEL), f32), meta_tokens.astype(f32)], axis=0)
    nw = norm_mix_pre[0].reshape(1, D_MODEL)
    cw = gdn_conv_w[0]
    proj_small, ba_small = _inproj(x_small, nw, w_in_p, w_ba, tm=n_small, tn=512, row_chunk=128)
    proj_big, ba_big, qkv_tail = _inproj(
        x_big, nw, w_in_p, w_ba, (cw, proj_small[n_small - SUBLANES:, :GDN_QKV]),
        tm=1024, tn=512, row_chunk=128)

    small_chunks = proj_small.reshape(n_small // CHUNK, CHUNK, PROJ_COLS)
    small_groups = proj_small.reshape(n_small // SUBLANES, SUBLANES, PROJ_COLS)
    last_chunk = n_small // CHUNK - 1
    gdn_meta, s_meta = _gdn(
        small_chunks, ba_small.reshape(n_small // CHUNK, CHUNK, LANES), lambda s: (last_chunk, 0, 0),
        jnp.zeros((1, SUBLANES, GDN_QKV), f32), lambda s: (0, 0, 0),
        jnp.zeros((1, GDN_HEADS, HEAD_DIM, HEAD_DIM), f32), cw, alog_row, dtb_row, gnw,
        n_steps=1, nb=1, seq=CHUNK, group=CHUNK, carry=True, pad_rows=pad_rows, preconv=False)
    gdn_big, s_prompt = _gdn(
        proj_big.reshape(1, seq, PROJ_COLS), ba_big.reshape(1, seq, LANES), lambda s: (0, s, 0),
        jnp.zeros((1, SUBLANES, GDN_QKV), f32), lambda s: (0, 0, 0),
        s_meta, cw, alog_row, dtb_row, gnw,
        n_steps=seq // (GDN_SEQ_CHUNKS * CHUNK), nb=1, seq=GDN_SEQ_CHUNKS * CHUNK, group=CHUNK,
        carry=True, pad_rows=0, preconv=True)
    hist_gdn = jnp.pad(state_gdn_conv[0], ((0, 0), (SUBLANES - (GDN_CONV - 1), 0), (0, 0)))
    nb_gdn = CHUNK // dec_t
    gdn_small, s_sample = _gdn(
        small_groups, ba_small.reshape(n_small // SUBLANES, SUBLANES, LANES), lambda s: (s, 0, 0),
        hist_gdn, lambda s: (s, 0, 0),
        state_gdn, cw, alog_row, dtb_row, gnw,
        n_steps=dec_b // nb_gdn, nb=nb_gdn, seq=dec_t, group=dec_t, carry=False, pad_rows=0, preconv=False)

    qi = np.arange(WINDOW)[:, None]
    kj = np.arange(WINDOW)[None, :]
    mi = np.arange(N_META)[None, :]
    ti = np.arange(dec_t)[:, None]
    new_keys = kj - N_META
    id_arrays = [
        _bucket_ids(qi - kj, qi >= kj),
        _bucket_ids(qi - kj + WINDOW, kj > qi),
        _bucket_ids(qi + N_META - kj, kj < N_META),
        _bucket_ids(qi + N_META - kj + WINDOW, kj < N_META),
        _bucket_ids(ti + WINDOW - kj, kj > ti),
        _bucket_ids(np.where(new_keys < 0, PAST_LEN + ti - kj, ti - new_keys),
                    (new_keys < 0) | ((new_keys <= ti) & (new_keys < dec_t))),
        _bucket_ids(mi.T - mi, mi.T >= mi),
    ]
    bcur, bprev, bm0, bfar, bwin, bsmall, bmm = _bias_tables(rel_bias_table, id_arrays)
    sink_rows = lambda q: jnp.repeat(swa_sinks[0].reshape(SWA_KV_HEADS, SWA_GROUP), q, axis=1)[..., None]

    sq_blk = COL_SQ // SWA_WIDTH
    kv_blk = COL_KV // KV_WIDTH
    meta_blk = (n_small - N_META) // N_META
    full3 = lambda a: pl.BlockSpec(a.shape, lambda j: (0, 0, 0))
    sink_p = sink_rows(WINDOW)
    swa_rows = SWA_Q_BLOCKS * WINDOW
    swa_big = pl.pallas_call(
        _swa_prompt_kernel,
        grid=(seq // swa_rows,),
        in_specs=[
            pl.BlockSpec((swa_rows, SWA_WIDTH), lambda j: (j, sq_blk)),
            pl.BlockSpec((swa_rows, KV_WIDTH), lambda j: (j, kv_blk)),
            pl.BlockSpec((WINDOW, KV_WIDTH), lambda j: (jnp.maximum(j * SWA_Q_BLOCKS - 1, 0), kv_blk)),
            pl.BlockSpec((N_META, KV_WIDTH), lambda j: (meta_blk, kv_blk)),
            full3(bcur), full3(bprev), full3(bm0), full3(bfar), full3(sink_p),
        ],
        out_specs=pl.BlockSpec((swa_rows, SWA_WIDTH), lambda j: (j, 0)),
        out_shape=jax.ShapeDtypeStruct((seq, SWA_WIDTH), bf16),
        compiler_params=pltpu.CompilerParams(
            dimension_semantics=("arbitrary",), vmem_limit_bytes=VMEM_LIMIT),
        name="swa_prompt",
    )(proj_big, proj_big, proj_big, proj_small, bcur, bprev, bm0, bfar, sink_p)

    nb_swa = 8
    sink_s = sink_rows(dec_t)
    win = cache_swa_window_kv.reshape(dec_b, WINDOW * KV_SLOTS, HEAD_DIM)
    meta_kv = cache_swa_meta_kv.reshape(dec_b, N_META * KV_SLOTS, HEAD_DIM)
    swa_small, win_new = pl.pallas_call(
        functools.partial(_swa_sample_kernel, nb=nb_swa, seq=dec_t),
        grid=(dec_b // nb_swa,),
        in_specs=[
            pl.BlockSpec((nb_swa, dec_t, SWA_WIDTH), lambda j: (j, 0, sq_blk)),
            pl.BlockSpec((nb_swa, dec_t, KV_WIDTH), lambda j: (j, 0, kv_blk)),
            pl.BlockSpec((nb_swa, WINDOW * KV_SLOTS, HEAD_DIM), lambda j: (j, 0, 0)),
            pl.BlockSpec((nb_swa, N_META * KV_SLOTS, HEAD_DIM), lambda j: (j, 0, 0)),
            full3(bwin), full3(bsmall), full3(sink_s),
        ],
        out_specs=[pl.BlockSpec((nb_swa * dec_t, SWA_WIDTH), lambda j: (j, 0)),
                   pl.BlockSpec((nb_swa, WINDOW * KV_SLOTS, HEAD_DIM), lambda j: (j, 0, 0))],
        out_shape=[jax.ShapeDtypeStruct((n_dec, SWA_WIDTH), bf16),
                   jax.ShapeDtypeStruct((dec_b, WINDOW * KV_SLOTS, HEAD_DIM), f32)],
        compiler_params=pltpu.CompilerParams(
            dimension_semantics=("arbitrary",), vmem_limit_bytes=VMEM_LIMIT),
        name="swa_sample",
    )(small_groups, small_groups, win, meta_kv, bwin, bsmall, sink_s)

    sink_m = sink_rows(N_META)
    swa_meta = pl.pallas_call(
        _swa_meta_kernel,
        grid=(1,),
        in_specs=[
            pl.BlockSpec((N_META, SWA_WIDTH), lambda j: (meta_blk, sq_blk)),
            pl.BlockSpec((N_META, KV_WIDTH), lambda j: (meta_blk, kv_blk)),
            full3(bmm), full3(sink_m),
        ],
        out_specs=pl.BlockSpec((N_META, SWA_WIDTH), lambda j: (0, 0)),
        out_shape=jax.ShapeDtypeStruct((N_META, SWA_WIDTH), bf16),
        name="swa_meta",
    )(proj_small, proj_small, bmm, sink_m)

    nw_post = norm_mix_post[0].reshape(1, D_MODEL)
    nf_pre = norm_ffn_pre[0].reshape(1, D_MODEL)
    nf_post = norm_ffn_post[0].reshape(1, D_MODEL)
    fcw = ffn_conv_w[0]
    gdn_small_all = jnp.concatenate([gdn_small, gdn_meta], axis=0)
    swa_small_all = jnp.concatenate([swa_small, jnp.zeros((pad_rows, SWA_WIDTH), bf16), swa_meta], axis=0)
    h_small = _outproj(gdn_small_all, swa_small_all, x_small, wo, nw_post, tm=n_small // 2)
    hist_ffn = jnp.pad(state_ffn_conv[0], ((0, CHUNK // SUBLANES), (SUBLANES - (FFN_CONV - 1), 0), (0, 0)))
    y_small, g_small = _ffn(h_small, nf_pre, wg, wu, fcw, wd, nf_post, hist_ffn,
                            batch=True, tm=n_small // 2, tf=512, out_rows=n_dec)
    h_big = _outproj(gdn_big, swa_big, x_big, wo, nw_post, tm=512)
    y_big, g_tail = _ffn(h_big, nf_pre, wg, wu, fcw, wd, nf_post, g_small[n_small - SUBLANES:],
                         batch=False, tm=1024, tf=512)

    kv_shape = lambda n: (1, n, 2, SWA_KV_HEADS, HEAD_DIM)
    kv_small = proj_small[:, COL_KV:COL_KV + KV_WIDTH]
    y_prompt = y_big.reshape(1, seq, D_MODEL)
    y_sample = y_small.reshape(dec_b, dec_t, D_MODEL)
    p_meta_kv = kv_small[n_small - N_META:].reshape(kv_shape(N_META))[None]
    p_window_kv = proj_big[seq - WINDOW:, COL_KV:COL_KV + KV_WIDTH].reshape(kv_shape(WINDOW))[None]
    p_gdn_conv = qkv_tail[qkv_tail.shape[0] - (GDN_CONV - 1):].reshape(1, 1, GDN_CONV - 1, GDN_QKV)
    p_gdn_state = s_prompt[None]
    p_ffn_conv = g_tail[g_tail.shape[0] - (FFN_CONV - 1):].reshape(1, 1, FFN_CONV - 1, D_FF)
    s_window_kv = win_new.reshape(1, dec_b, WINDOW, 2, SWA_KV_HEADS, HEAD_DIM)
    s_gdn_conv = proj_small[:n_dec, :GDN_QKV].reshape(dec_b, dec_t, GDN_QKV)[:, dec_t - (GDN_CONV - 1):][None]
    s_gdn_state = s_sample
    s_ffn_conv = g_small[:n_dec].reshape(dec_b, dec_t, D_FF)[:, dec_t - (FFN_CONV - 1):][None]
    return (y_prompt, y_sample, p_meta_kv, p_window_kv, p_gdn_conv, p_gdn_state, p_ffn_conv,
            s_window_kv, s_gdn_conv, s_gdn_state, s_ffn_conv)
```

```python
import functools
import math

import numpy as np
import jax
import jax.numpy as jnp
from jax import lax
from jax.experimental import pallas as pl
from jax.experimental.pallas import tpu as pltpu

D_MODEL = 2048
HEAD_DIM = 128
GDN_HEADS = 8
GDN_WIDTH = GDN_HEADS * HEAD_DIM
GDN_QKV = 3 * GDN_WIDTH
SWA_HEADS = 8
SWA_KV_HEADS = 2
SWA_GROUP = SWA_HEADS // SWA_KV_HEADS
SWA_WIDTH = SWA_HEADS * HEAD_DIM
KV_SLOTS = 2 * SWA_KV_HEADS
KV_WIDTH = KV_SLOTS * HEAD_DIM
WINDOW = 128
N_META = 16
N_BUCKETS = 32
MAX_DISTANCE = 128
GDN_CONV = 4
FFN_CONV = 3
D_FF = 5632
EPS = 1e-6
PAST_LEN = 16384

SUBLANES = 8
LANES = 128

CHUNK = 128
assert CHUNK == HEAD_DIM == LANES
INV_BASE = 16
GDN_SEQ_CHUNKS = 2
INPROJ_ROW_BLOCKS = 2
FFN_ROW_BLOCKS = 2

COL_Z = GDN_QKV
COL_SQ = COL_Z + GDN_WIDTH
COL_KV = COL_SQ + SWA_WIDTH
PROJ_COLS = COL_KV + KV_WIDTH
GDN_COLS = COL_SQ

SWA_Q_BLOCKS = 4
NEG = -1e30
VMEM_LIMIT = 56 * 1024 * 1024

_NT = (((1,), (1,)), ((), ()))


def _dot(a, b):
    return jnp.dot(a.astype(jnp.bfloat16), b.astype(jnp.bfloat16), preferred_element_type=jnp.float32)


def _dot_nt(a, b):
    return lax.dot_general(a.astype(jnp.bfloat16), b.astype(jnp.bfloat16), _NT,
                           preferred_element_type=jnp.float32)


_dot_inv = _dot


def _dot_exact(a, b, dims=None):
    if dims is None:
        return jnp.dot(a, b, precision=lax.Precision.HIGHEST, preferred_element_type=jnp.float32)
    return lax.dot_general(a, b, dims, precision=lax.Precision.HIGHEST,
                           preferred_element_type=jnp.float32)


def _pack_w_in_kernel(a_ref, b_ref, o_ref, ba_ref, *, first_shifted, shift):
    j = pl.program_id(0)

    @pl.when(j < first_shifted)
    def _():
        o_ref[...] = a_ref[...].T.astype(o_ref.dtype)

    @pl.when(j >= first_shifted)
    def _():
        rows = jnp.concatenate([a_ref[shift:, :], b_ref[:shift, :]], axis=0)
        o_ref[...] = rows.T.astype(o_ref.dtype)

    @pl.when(j == first_shifted)
    def _():
        head = a_ref[:LANES, :]
        row = lax.broadcasted_iota(jnp.int32, head.shape, 0)
        ba_ref[...] = jnp.where(row < shift, head, 0.0).T.astype(ba_ref.dtype)


def _pack_w_in(w_in_t, n_ba, *, tn):
    n_blocks = PROJ_COLS // tn
    first_shifted = COL_SQ // tn
    assert COL_SQ % tn == 0 and w_in_t.shape[0] == PROJ_COLS + n_ba and n_ba % SUBLANES == 0
    return pl.pallas_call(
        functools.partial(_pack_w_in_kernel, first_shifted=first_shifted, shift=n_ba),
        grid=(n_blocks,),
        in_specs=[pl.BlockSpec((tn, D_MODEL), lambda j: (j, 0)),
                  pl.BlockSpec((tn, D_MODEL), lambda j: (jnp.maximum(j, first_shifted) + 1, 0))],
        out_specs=[pl.BlockSpec((D_MODEL, tn), lambda j: (0, j)),
                   pl.BlockSpec((D_MODEL, LANES), lambda j: (0, 0))],
        out_shape=[jax.ShapeDtypeStruct((D_MODEL, PROJ_COLS), jnp.bfloat16),
                   jax.ShapeDtypeStruct((D_MODEL, LANES), jnp.bfloat16)],
        compiler_params=pltpu.CompilerParams(
            dimension_semantics=("arbitrary",), vmem_limit_bytes=VMEM_LIMIT),
        name="pack_w_in",
    )(w_in_t, w_in_t)


def _rms_scale(x):
    return lax.rsqrt(jnp.mean(x * x, axis=-1, keepdims=True) + EPS)


def _silu(x):
    return x * (0.5 * jnp.tanh(0.5 * x) + 0.5)


def _inproj_kernel(*refs, row_chunk, conv_tiles):
    if conv_tiles:
        (x_ref, nw_ref, w_ref, wba_ref, cw_ref, hist_ref,
         o_ref, ba_ref, tail_ref, xn_ref, xe_ref, carry_ref) = refs
    else:
        x_ref, nw_ref, w_ref, wba_ref, o_ref, ba_ref, xn_ref = refs
    i = pl.program_id(0)
    j = pl.program_id(1)
    tm, tn = o_ref.shape

    def normalize(rows):
        x = x_ref[rows, :]
        xn_ref[rows, :] = (x * _rms_scale(x) * nw_ref[...]).astype(jnp.bfloat16)

    def plain():
        o_ref[...] = jnp.dot(xn_ref[...], w_ref[...], preferred_element_type=jnp.float32)

    if not conv_tiles:
        @pl.when(j == 0)
        def _():
            def body(c, carry):
                normalize(pl.ds(pl.multiple_of(c * row_chunk, row_chunk), row_chunk))
                return carry
            lax.fori_loop(0, tm // row_chunk, body, 0)
            ba_ref[...] = jnp.dot(xn_ref[...], wba_ref[...], preferred_element_type=jnp.float32)
        plain()
        return
    pl.when(j >= conv_tiles)(plain)

    def conv_tile(first):
        @pl.when(i == 0)
        def _():
            carry_ref[pl.ds(j, 1)] = hist_ref[...].reshape(1, SUBLANES, tn)
        xe_ref[0:SUBLANES, :] = carry_ref[pl.ds(j, 1)].reshape(SUBLANES, tn)
        rb = tm // INPROJ_ROW_BLOCKS

        def raw_block(r):
            rows = slice(r * rb, (r + 1) * rb)
            if first:
                normalize(rows)
                ba_ref[rows, :] = jnp.dot(xn_ref[rows, :], wba_ref[...], preferred_element_type=jnp.float32)
            return jnp.dot(xn_ref[rows, :], w_ref[...], preferred_element_type=jnp.float32)

        nxt = raw_block(0)
        for r in range(INPROJ_ROW_BLOCKS):
            raw = nxt
            if r + 1 < INPROJ_ROW_BLOCKS:
                nxt = raw_block(r + 1)
            base = SUBLANES + r * rb
            xe_ref[base:base + rb, :] = raw
            conv = None
            for s in range(GDN_CONV):
                term = xe_ref[base - s:base - s + rb, :] * cw_ref[GDN_CONV - 1 - s:GDN_CONV - s, :]
                conv = term if conv is None else conv + term
            o_ref[r * rb:(r + 1) * rb, :] = conv
        tail = xe_ref[tm:tm + SUBLANES, :]
        carry_ref[pl.ds(j, 1)] = tail.reshape(1, SUBLANES, tn)
        tail_ref[...] = tail

    pl.when(j == 0)(functools.partial(conv_tile, True))
    pl.when((j > 0) & (j < conv_tiles))(functools.partial(conv_tile, False))


def _inproj(x, nw, w, wba, conv=None, *, tm, tn, row_chunk):
    rows = x.shape[0]
    conv_tiles = GDN_QKV // tn if conv else 0
    in_specs = [
        pl.BlockSpec((tm, D_MODEL), lambda i, j: (i, 0)),
        pl.BlockSpec((1, D_MODEL), lambda i, j: (0, 0)),
        pl.BlockSpec((D_MODEL, tn), lambda i, j: (0, j)),
        pl.BlockSpec((D_MODEL, LANES), lambda i, j: (0, 0)),
    ]
    out_specs = [pl.BlockSpec((tm, tn), lambda i, j: (i, j)),
                 pl.BlockSpec((tm, LANES), lambda i, j: (i, 0))]
    out_shape = [jax.ShapeDtypeStruct((rows, PROJ_COLS), jnp.float32),
                 jax.ShapeDtypeStruct((rows, LANES), jnp.float32)]
    scratch = [pltpu.VMEM((tm, D_MODEL), jnp.bfloat16)]
    args = [x, nw, w, wba]
    if conv:
        conv_col = lambda i, j: (0, jnp.minimum(j, conv_tiles - 1))
        in_specs += [pl.BlockSpec((GDN_CONV, tn), conv_col), pl.BlockSpec((SUBLANES, tn), conv_col)]
        out_specs.append(pl.BlockSpec((SUBLANES, tn), lambda i, j: (i, jnp.minimum(j, conv_tiles - 1))))
        out_shape.append(jax.ShapeDtypeStruct((rows // tm * SUBLANES, GDN_QKV), jnp.float32))
        scratch += [pltpu.VMEM((SUBLANES + tm, tn), jnp.float32),
                    pltpu.VMEM((conv_tiles, SUBLANES, tn), jnp.float32)]
        args += list(conv)
    return pl.pallas_call(
        functools.partial(_inproj_kernel, row_chunk=row_chunk, conv_tiles=conv_tiles),
        grid=(rows // tm, PROJ_COLS // tn),
        in_specs=in_specs,
        out_specs=out_specs,
        out_shape=out_shape,
        scratch_shapes=scratch,
        compiler_params=pltpu.CompilerParams(
            dimension_semantics=("arbitrary", "arbitrary"), vmem_limit_bytes=VMEM_LIMIT),
        name="inproj_conv" if conv else "inproj",
    )(*args)


def _tri_inverse(lms, ri, ci):
    shift = INV_BASE.bit_length() - 1
    eye = (ri == ci).astype(jnp.float32)
    in_block = (ri >> shift) == (ci >> shift)
    ps = [jnp.where(in_block, lm, 0.0) for lm in lms]
    ts = [eye - p for p in ps]
    for _ in range(shift - 1):
        ps = [_dot_inv(p, p) for p in ps]
        ts = [t + _dot_inv(t, p) for t, p in zip(ts, ps)]
    size = INV_BASE
    while size < CHUNK:
        shift += 1
        in_pair = (ri >> shift) == (ci >> shift)
        off_mask = in_pair & jnp.logical_not(in_block)
        tos = [_dot_inv(t, jnp.where(off_mask, lm, 0.0)) for t, lm in zip(ts, lms)]
        ts = [t - _dot_inv(to, t) for t, to in zip(ts, tos)]
        in_block = in_pair
        size *= 2
    return ts


def _gdn_kernel(x_ref, ba_ref, hist_ref, s0_ref, cw_ref, alog_ref, dtb_ref, gnw_ref,
                o_ref, sout_ref, xe_ref, s_ref, *, nb, seq, group, carry, pad_rows, preconv):
    step = pl.program_id(0)
    rows = nb * seq
    n_chunks = rows // CHUNK
    n_groups = CHUNK // group
    gshift = group.bit_length() - 1

    if carry:
        @pl.when(step == 0)
        def _():
            s_ref[...] = s0_ref[0]
    if not preconv:
        if carry:
            @pl.when(step == 0)
            def _():
                xe_ref[:, 0:SUBLANES, :] = hist_ref[...]
        else:
            xe_ref[:, 0:SUBLANES, :] = hist_ref[...]
        xe_ref[:, SUBLANES:SUBLANES + seq, :] = x_ref[:, :, 0:GDN_QKV]

    seq_rows = min(seq, CHUNK)
    seqs_per_chunk = CHUNK // seq_rows

    def chunk_rows(ref, c, row_off, cols):
        if seq >= CHUNK:
            start = row_off + c * CHUNK
            return ref[0:1, start:start + CHUNK, cols]
        b0 = c * seqs_per_chunk
        return ref[b0:b0 + seqs_per_chunk, row_off:row_off + seq, cols]

    def conv_chunk(col, c):
        cols = slice(col, col + HEAD_DIM)
        if preconv:
            return _silu(chunk_rows(x_ref, c, 0, cols).reshape(CHUNK, HEAD_DIM))
        acc = None
        for s in range(GDN_CONV):
            term = chunk_rows(xe_ref, c, SUBLANES - s, cols) * cw_ref[GDN_CONV - 1 - s:GDN_CONV - s, cols]
            acc = term if acc is None else acc + term
        return _silu(acc).reshape(CHUNK, HEAD_DIM)

    ri = lax.broadcasted_iota(jnp.int32, (CHUNK, CHUNK), 0)
    ci = lax.broadcasted_iota(jnp.int32, (CHUNK, CHUNK), 1)
    same = (ri >> gshift) == (ci >> gshift)
    m_incl = same & (ri >= ci)
    m_strict = same & (ri > ci)
    f_incl = m_incl.astype(jnp.float32)
    f_same = same.astype(jnp.float32)
    lane = lax.broadcasted_iota(jnp.int32, (CHUNK, LANES), 1)
    row_in_chunk = lax.broadcasted_iota(jnp.int32, (CHUNK, LANES), 0)

    pre = []
    for c in range(n_chunks):
        bac = chunk_rows(ba_ref, c, 0, slice(0, LANES)).reshape(CHUNK, LANES)
        beta_all = jax.nn.sigmoid(bac)
        sp_in = bac + dtb_ref[...]
        softplus = jnp.maximum(sp_in, 0.0) + jnp.log1p(jnp.exp(-jnp.abs(sp_in)))
        g_all = -jnp.exp(alog_ref[...]) * softplus
        if pad_rows and c == 0:
            valid = row_in_chunk >= pad_rows
            beta_all = jnp.where(valid, beta_all, 0.0)
            g_all = jnp.where(valid, g_all, 0.0)
        g_all = jnp.where((lane >= GDN_HEADS) & (lane < 2 * GDN_HEADS), g_all, 0.0)
        gc_col = _dot_exact(f_incl, g_all)
        if n_groups == 1:
            gtot_col = jnp.broadcast_to(gc_col[CHUNK - 1:CHUNK, :], (CHUNK, LANES))
        else:
            gtot_col = _dot_exact(f_same, g_all)
        gc_row = gc_col.T

        for h in range(GDN_HEADS):
            qh = conv_chunk(h * HEAD_DIM, c)
            kh = conv_chunk(GDN_WIDTH + h * HEAD_DIM, c)
            vh = conv_chunk(2 * GDN_WIDTH + h * HEAD_DIM, c)
            qh = qh * lax.rsqrt(jnp.sum(qh * qh, -1, keepdims=True) + EPS) * (HEAD_DIM ** -0.5)
            kh = kh * lax.rsqrt(jnp.sum(kh * kh, -1, keepdims=True) + EPS)
            bcast = lambda col: jnp.broadcast_to(col, (CHUNK, HEAD_DIM))
            gcc = bcast(gc_col[:, GDN_HEADS + h:GDN_HEADS + h + 1])
            gtc = bcast(gtot_col[:, GDN_HEADS + h:GDN_HEADS + h + 1])
            beta = bcast(beta_all[:, h:h + 1])
            gcr = gc_row[GDN_HEADS + h:GDN_HEADS + h + 1, :]
            decay = jnp.exp(jnp.where(m_incl, gcc - gcr, NEG))
            kb = kh * beta
            egc = jnp.exp(gcc)
            pre.append(dict(
                c=c, h=h,
                lm=jnp.where(m_strict, _dot_nt(kb, kh) * decay, 0.0),
                qk=_dot_nt(qh, kh) * decay,
                rhs=jnp.concatenate([vh * beta, kb * egc], axis=1),
                qg=qh * egc,
                kd_t=(kh * jnp.exp(gtc - gcc)).T,
                gl=jnp.exp(gtc)))

    inverses = _tri_inverse([p["lm"] for p in pre], ri, ci)
    sols = [_dot_inv(t, p["rhs"]) for t, p in zip(inverses, pre)]

    for c in range(n_chunks):
        r0 = c * CHUNK
        items = [(p, sol) for p, sol in zip(pre, sols) if p["c"] == c]
        state = lambda h, b: s_ref[h] if carry else s0_ref[c * n_groups + b, h]
        ws, qs = [], []
        for p, sol in items:
            w = sol[:, HEAD_DIM:]
            ws_parts, qs_parts = [], []
            for b in range(n_groups):
                g0 = b * group
                wq = jnp.concatenate([w[g0:g0 + group], p["qg"][g0:g0 + group]], axis=0)
                res = _dot(wq, state(p["h"], b))
                ws_parts.append(res[:group])
                qs_parts.append(res[group:])
            ws.append(ws_parts[0] if n_groups == 1 else jnp.concatenate(ws_parts, axis=0))
            qs.append(qs_parts[0] if n_groups == 1 else jnp.concatenate(qs_parts, axis=0))
        v_new = [sol[:, :HEAD_DIM] - w for (p, sol), w in zip(items, ws)]
        o = [a + _dot(p["qk"], vn) for a, (p, sol), vn in zip(qs, items, v_new)]
        for (p, sol), vn in zip(items, v_new):
            h = p["h"]
            for b in range(n_groups):
                g0 = b * group
                kd_b = p["kd_t"] if n_groups == 1 else jnp.where((ci >> gshift) == b, p["kd_t"], 0.0)
                st = state(h, b) * p["gl"][g0:g0 + 1, :] + _dot(kd_b, vn)
                if carry:
                    s_ref[h] = st
                else:
                    sout_ref[c * n_groups + b, h] = st
        for (p, sol), oh in zip(items, o):
            h = p["h"]
            z = chunk_rows(x_ref, c, 0, slice(COL_Z + h * HEAD_DIM, COL_Z + (h + 1) * HEAD_DIM)).reshape(CHUNK, HEAD_DIM)
            y = oh * lax.rsqrt(jnp.mean(oh * oh, -1, keepdims=True) + EPS) * gnw_ref[...] * _silu(z)
            o_ref[r0:r0 + CHUNK, h * HEAD_DIM:(h + 1) * HEAD_DIM] = y.astype(o_ref.dtype)

    if carry:
        if not preconv:
            xe_ref[:, 0:SUBLANES, :] = xe_ref[:, seq:seq + SUBLANES, :]

        @pl.when(step == pl.num_programs(0) - 1)
        def _():
            sout_ref[0] = s_ref[...]


def _gdn(x3, ba3, x_idx, hist, hist_idx, s0, cw, alog_row, dtb_row, gnw, *,
         n_steps, nb, seq, group, carry, pad_rows, preconv):
    rows = nb * seq
    if carry:
        state_spec = pl.BlockSpec((1, GDN_HEADS, HEAD_DIM, HEAD_DIM), lambda s: (0, 0, 0, 0))
        state_shape = (1, GDN_HEADS, HEAD_DIM, HEAD_DIM)
    else:
        n_states = rows // group
        state_spec = pl.BlockSpec((None, n_states, GDN_HEADS, HEAD_DIM, HEAD_DIM), lambda s: (0, s, 0, 0, 0))
        state_shape = (1, n_steps * n_states, GDN_HEADS, HEAD_DIM, HEAD_DIM)
    full = lambda shape: pl.BlockSpec(shape, lambda s: (0,) * len(shape))
    return pl.pallas_call(
        functools.partial(_gdn_kernel, nb=nb, seq=seq, group=group, carry=carry, pad_rows=pad_rows,
                          preconv=preconv),
        grid=(n_steps,),
        in_specs=[
            pl.BlockSpec((nb, seq, GDN_COLS), x_idx),
            pl.BlockSpec((nb, seq, LANES), x_idx),
            pl.BlockSpec((nb, SUBLANES, GDN_QKV), hist_idx),
            state_spec,
            full((GDN_CONV, GDN_QKV)),
            full((1, LANES)),
            full((1, LANES)),
            full((1, HEAD_DIM)),
        ],
        out_specs=[
            pl.BlockSpec((rows, GDN_WIDTH), lambda s: (s, 0)),
            state_spec,
        ],
        out_shape=[
            jax.ShapeDtypeStruct((n_steps * rows, GDN_WIDTH), jnp.bfloat16),
            jax.ShapeDtypeStruct(state_shape, jnp.float32),
        ],
        scratch_shapes=[
            pltpu.VMEM((nb, SUBLANES + (SUBLANES if preconv else seq), GDN_QKV), jnp.float32),
            pltpu.VMEM((GDN_HEADS, HEAD_DIM, HEAD_DIM), jnp.float32),
        ],
        compiler_params=pltpu.CompilerParams(
            dimension_semantics=("arbitrary",), vmem_limit_bytes=VMEM_LIMIT),
        name="gdn_seq" if carry else "gdn_batch",
    )(x3, ba3, hist, s0, cw, alog_row, dtb_row, gnw)


def _t5_bucket_np(dist):
    n = np.maximum(dist, 0)
    exact = N_BUCKETS // 2
    large = exact + (np.log(np.maximum(n, 1).astype(np.float32) / exact)
                     / math.log(MAX_DISTANCE / exact) * (N_BUCKETS - exact)).astype(np.int32)
    return np.where(n < exact, n, np.minimum(large, N_BUCKETS - 1)).astype(np.int32)


def _bucket_ids(dist, valid):
    return np.where(valid, _t5_bucket_np(dist), -1).astype(np.int32)


def _bias_kernel(table_ref, *refs):
    n = len(refs) // 2
    for ids_ref, out_ref in zip(refs[:n], refs[n:]):
        nq = ids_ref.shape[0]
        rows_per_pass = min(nq, 2 * SUBLANES)
        for r0 in range(0, nq, rows_per_pass):
            ids = ids_ref[r0:r0 + rows_per_pass, :]

            def body(b, accs):
                hit = ids == b
                return tuple(jnp.where(hit, table_ref[b, head], acc) for head, acc in enumerate(accs))

            init = tuple(jnp.full(ids.shape, NEG, jnp.float32) for _ in range(SWA_HEADS))
            for head, acc in enumerate(lax.fori_loop(0, N_BUCKETS, body, init)):
                kh, g = divmod(head, SWA_GROUP)
                out_ref[kh, g * nq + r0:g * nq + r0 + rows_per_pass, :] = acc


def _bias_tables(rel_table, id_arrays):
    out_shapes = [jax.ShapeDtypeStruct((SWA_KV_HEADS, SWA_GROUP * a.shape[0], a.shape[1]), jnp.float32)
                  for a in id_arrays]
    vmem = pl.BlockSpec(memory_space=pltpu.VMEM)
    return pl.pallas_call(
        _bias_kernel,
        in_specs=[pl.BlockSpec(memory_space=pltpu.SMEM)] + [vmem] * len(id_arrays),
        out_specs=[vmem] * len(id_arrays),
        out_shape=out_shapes,
        name="swa_bias",
    )(rel_table, *[jnp.asarray(a) for a in id_arrays])


def _attend(problems):
    scale = HEAD_DIM ** -0.5
    scores = [[_dot_nt(q, k) * scale + b for k, b in zip(keys, biases)]
              for q, keys, _, biases, _ in problems]
    maxes = []
    for (_, _, _, _, sink), segs in zip(problems, scores):
        m = sink
        for s in segs:
            m = jnp.maximum(m, jnp.max(s, axis=-1, keepdims=True))
        maxes.append(m)
    probs = [[jnp.exp(s - m) for s in segs] for segs, m in zip(scores, maxes)]
    outs = []
    for (_, _, values, _, sink), ps, m in zip(problems, probs, maxes):
        acc = None
        for p, v in zip(ps, values):
            v_ones = jnp.concatenate([v, jnp.ones((v.shape[0], HEAD_DIM), v.dtype)], axis=1)
            pv = _dot(p, v_ones)
            acc = pv if acc is None else acc + pv
        den = acc[:, HEAD_DIM:] + jnp.exp(sink - m)
        outs.append(acc[:, :HEAD_DIM] / den)
    return outs


def _group_queries(q_rows, kh):
    return jnp.concatenate(
        [q_rows((kh * SWA_GROUP + g) * HEAD_DIM, (kh * SWA_GROUP + g + 1) * HEAD_DIM)
         for g in range(SWA_GROUP)], axis=0)


def _pad_keys(rows):
    return jnp.concatenate([rows, jnp.zeros((WINDOW - rows.shape[0], rows.shape[1]), rows.dtype)], axis=0)


def _k_cols(kh):
    return slice(kh * HEAD_DIM, (kh + 1) * HEAD_DIM)


def _v_cols(kh):
    return slice((SWA_KV_HEADS + kh) * HEAD_DIM, (SWA_KV_HEADS + kh + 1) * HEAD_DIM)


def _swa_prompt_kernel(q_ref, kvc_ref, kvp_ref, kvm_ref, bcur_ref, bprev_ref, bm0_ref, bfar_ref,
                       sink_ref, o_ref):
    first = pl.program_id(0) == 0
    problems = []
    for blk in range(SWA_Q_BLOCKS):
        rows = slice(blk * WINDOW, (blk + 1) * WINDOW)
        prev_ref, prev_rows = (kvp_ref, slice(0, WINDOW)) if blk == 0 else (
            kvc_ref, slice((blk - 1) * WINDOW, blk * WINDOW))
        for kh in range(SWA_KV_HEADS):
            ks, vs = _k_cols(kh), _v_cols(kh)
            first_block = first if blk == 0 else False
            b_prev = jnp.where(first_block, NEG, bprev_ref[kh])
            b_meta = jnp.where(first_block, bm0_ref[kh], bfar_ref[kh])
            problems.append((_group_queries(lambda a, b, rows=rows: q_ref[rows, a:b], kh),
                             [kvc_ref[rows, ks], prev_ref[prev_rows, ks], _pad_keys(kvm_ref[:, ks])],
                             [kvc_ref[rows, vs], prev_ref[prev_rows, vs], _pad_keys(kvm_ref[:, vs])],
                             [bcur_ref[kh], b_prev, b_meta], sink_ref[kh]))
    for i, o in enumerate(_attend(problems)):
        blk, kh = divmod(i, SWA_KV_HEADS)
        for g in range(SWA_GROUP):
            head = kh * SWA_GROUP + g
            o_ref[blk * WINDOW:(blk + 1) * WINDOW, head * HEAD_DIM:(head + 1) * HEAD_DIM] = (
                o[g * WINDOW:(g + 1) * WINDOW].astype(o_ref.dtype))


def _swa_meta_kernel(q_ref, kv_ref, bias_ref, sink_ref, o_ref):
    problems = [(_group_queries(lambda a, b: q_ref[:, a:b], kh), [kv_ref[:, _k_cols(kh)]],
                 [kv_ref[:, _v_cols(kh)]], [bias_ref[kh]], sink_ref[kh])
                for kh in range(SWA_KV_HEADS)]
    for kh, o in enumerate(_attend(problems)):
        for g in range(SWA_GROUP):
            head = kh * SWA_GROUP + g
            o_ref[:, head * HEAD_DIM:(head + 1) * HEAD_DIM] = (
                o[g * N_META:(g + 1) * N_META].astype(o_ref.dtype))


def _swa_sample_kernel(q_ref, kvn_ref, win_ref, meta_ref, bwin_ref, bsmall_ref, sink_ref,
                       o_ref, wout_ref, *, nb, seq):
    cached = lambda ref, b, slot, n: ref[b, pl.ds(slot, n, stride=KV_SLOTS), :]
    keep = (WINDOW - seq) * KV_SLOTS
    wout_ref[:, 0:keep, :] = win_ref[:, seq * KV_SLOTS:WINDOW * KV_SLOTS, :]
    for slot in range(KV_SLOTS):
        wout_ref[:, pl.ds(keep + slot, seq, stride=KV_SLOTS), :] = (
            kvn_ref[:, :, slot * HEAD_DIM:(slot + 1) * HEAD_DIM])
    problems = []
    for b in range(nb):
        for kh in range(SWA_KV_HEADS):
            ks, vs = _k_cols(kh), _v_cols(kh)
            k_small = _pad_keys(jnp.concatenate([cached(meta_ref, b, kh, N_META), kvn_ref[b, :, ks]], axis=0))
            v_small = _pad_keys(jnp.concatenate(
                [cached(meta_ref, b, SWA_KV_HEADS + kh, N_META), kvn_ref[b, :, vs]], axis=0))
            problems.append((_group_queries(lambda a, c, b=b: q_ref[b, :, a:c], kh),
                             [cached(win_ref, b, kh, WINDOW), k_small],
                             [cached(win_ref, b, SWA_KV_HEADS + kh, WINDOW), v_small],
                             [bwin_ref[kh], bsmall_ref[kh]], sink_ref[kh]))
    outs = _attend(problems)
    for head in range(SWA_HEADS):
        kh, g = divmod(head, SWA_GROUP)
        rows = [outs[b * SWA_KV_HEADS + kh][g * seq:(g + 1) * seq] for b in range(nb)]
        o_ref[:, head * HEAD_DIM:(head + 1) * HEAD_DIM] = jnp.concatenate(rows, axis=0).astype(o_ref.dtype)


def _outproj_kernel(g_ref, s_ref, h_ref, wo_ref, nw_ref, o_ref):
    both = jnp.concatenate([g_ref[...], s_ref[...]], axis=1)
    mix = jnp.dot(both, wo_ref[...], preferred_element_type=jnp.float32)
    o_ref[...] = h_ref[...] + mix * _rms_scale(mix) * nw_ref[...]


def _outproj(g, s, h, wo, nw, *, tm):
    rows = h.shape[0]
    return pl.pallas_call(
        _outproj_kernel,
        grid=(rows // tm,),
        in_specs=[
            pl.BlockSpec((tm, GDN_WIDTH), lambda i: (i, 0)),
            pl.BlockSpec((tm, SWA_WIDTH), lambda i: (i, 0)),
            pl.BlockSpec((tm, D_MODEL), lambda i: (i, 0)),
            pl.BlockSpec((D_MODEL, D_MODEL), lambda i: (0, 0)),
            pl.BlockSpec((1, D_MODEL), lambda i: (0, 0)),
        ],
        out_specs=pl.BlockSpec((tm, D_MODEL), lambda i: (i, 0)),
        out_shape=jax.ShapeDtypeStruct((rows, D_MODEL), jnp.float32),
        compiler_params=pltpu.CompilerParams(
            dimension_semantics=("arbitrary",), vmem_limit_bytes=VMEM_LIMIT),
        name="outproj",
    )(g, s, h, wo, nw)


def _ffn_kernel(*refs, batch, tm, tf):
    if batch:
        (h_ref, nw_pre_ref, wg_ref, wu_ref, cw_ref, wd_ref, nw_post_ref, hist_ref,
         y_ref, graw_ref, xn_ref, xe_ref) = refs
    else:
        (h_ref, nw_pre_ref, wg_ref, wu_ref, cw_ref, wd_ref, nw_post_ref, hist_ref,
         y_ref, graw_ref, xn_ref, xe_ref, carry_ref) = refs
    i = pl.program_id(0)
    j = pl.program_id(1)
    last_j = pl.num_programs(1) - 1
    rb = tm // FFN_ROW_BLOCKS

    if batch:
        xe_ref[:, 0:SUBLANES, :] = hist_ref[...]
    else:
        @pl.when(i == 0)
        def _():
            carry_ref[pl.ds(j, 1)] = hist_ref[...].reshape(1, SUBLANES, tf)
        xe_ref[:, 0:SUBLANES, :] = carry_ref[pl.ds(j, 1)]

    def step(first, last):
        def gate_up(r):
            rows = slice(r * rb, (r + 1) * rb)
            if first:
                h = h_ref[rows, :]
                xn_ref[rows, :] = (h * _rms_scale(h) * nw_pre_ref[...]).astype(jnp.bfloat16)
            xn = xn_ref[rows, :]
            return (jnp.dot(xn, wg_ref[...], preferred_element_type=jnp.float32),
                    jnp.dot(xn, wu_ref[...], preferred_element_type=jnp.float32))

        nxt = gate_up(0)
        for r in range(FFN_ROW_BLOCKS):
            rows = slice(r * rb, (r + 1) * rb)
            gate, up = nxt
            if r + 1 < FFN_ROW_BLOCKS:
                nxt = gate_up(r + 1)
            if batch:
                seqs = slice(r * rb // SUBLANES, (r + 1) * rb // SUBLANES)
                graw_ref[rows, :] = gate
                xe_ref[seqs, SUBLANES:2 * SUBLANES, :] = gate.reshape(rb // SUBLANES, SUBLANES, tf)
                taps = [xe_ref[seqs, SUBLANES - s:2 * SUBLANES - s, :] for s in range(FFN_CONV)]
            else:
                base = SUBLANES + r * rb
                xe_ref[:, base:base + rb, :] = gate.reshape(1, rb, tf)
                taps = [xe_ref[:, base - s:base - s + rb, :] for s in range(FFN_CONV)]
            conv = None
            for s, tap in enumerate(taps):
                term = tap * cw_ref[FFN_CONV - 1 - s:FFN_CONV - s, :]
                conv = term if conv is None else conv + term
            act = (_silu(conv.reshape(rb, tf)) * up).astype(jnp.bfloat16)
            down = jnp.dot(act, wd_ref[...], preferred_element_type=jnp.float32)
            if first:
                y_ref[rows, :] = down
            elif last:
                y = y_ref[rows, :] + down
                y_ref[rows, :] = h_ref[rows, :] + y * _rms_scale(y) * nw_post_ref[...]
            else:
                y_ref[rows, :] += down
        if not batch:
            tail = xe_ref[:, tm:tm + SUBLANES, :]
            carry_ref[pl.ds(j, 1)] = tail
            graw_ref[...] = tail.reshape(SUBLANES, tf)

    pl.when(j == 0)(functools.partial(step, True, False))
    pl.when((j > 0) & (j < last_j))(functools.partial(step, False, False))
    pl.when(j == last_j)(functools.partial(step, False, True))


def _ffn(h, nw_pre, wg, wu, cw, wd, nw_post, hist, *, batch, tm, tf, out_rows=None):
    rows = h.shape[0]
    nj = D_FF // tf
    out_rows = rows if out_rows is None else out_rows
    in_specs = [
        pl.BlockSpec((tm, D_MODEL), lambda i, j: (i, 0)),
        pl.BlockSpec((1, D_MODEL), lambda i, j: (0, 0)),
        pl.BlockSpec((D_MODEL, tf), lambda i, j: (0, j)),
        pl.BlockSpec((D_MODEL, tf), lambda i, j: (0, j)),
        pl.BlockSpec((FFN_CONV, tf), lambda i, j: (0, j)),
        pl.BlockSpec((tf, D_MODEL), lambda i, j: (j, 0)),
        pl.BlockSpec((1, D_MODEL), lambda i, j: (0, 0)),
    ]
    args = [h, nw_pre, wg, wu, cw, wd, nw_post, hist]
    scratch = [pltpu.VMEM((tm, D_MODEL), jnp.bfloat16)]
    if batch:
        in_specs.append(pl.BlockSpec((tm // SUBLANES, SUBLANES, tf), lambda i, j: (i, 0, j)))
        graw_spec = pl.BlockSpec((tm, tf), lambda i, j: (i, j))
        graw_shape = jax.ShapeDtypeStruct((rows, D_FF), jnp.float32)
        scratch.append(pltpu.VMEM((tm // SUBLANES, 2 * SUBLANES, tf), jnp.float32))
    else:
        in_specs.append(pl.BlockSpec((SUBLANES, tf), lambda i, j: (0, j)))
        graw_spec = pl.BlockSpec((SUBLANES, tf), lambda i, j: (i, j))
        graw_shape = jax.ShapeDtypeStruct((rows // tm * SUBLANES, D_FF), jnp.float32)
        scratch.append(pltpu.VMEM((1, SUBLANES + tm, tf), jnp.float32))
        scratch.append(pltpu.VMEM((nj, SUBLANES, tf), jnp.float32))
    return pl.pallas_call(
        functools.partial(_ffn_kernel, batch=batch, tm=tm, tf=tf),
        grid=(rows // tm, nj),
        in_specs=in_specs,
        out_specs=[pl.BlockSpec((tm, D_MODEL), lambda i, j: (i, 0)), graw_spec],
        out_shape=[jax.ShapeDtypeStruct((out_rows, D_MODEL), jnp.float32), graw_shape],
        scratch_shapes=scratch,
        compiler_params=pltpu.CompilerParams(
            dimension_semantics=("arbitrary", "arbitrary"), vmem_limit_bytes=VMEM_LIMIT),
        name="ffn_batch" if batch else "ffn_seq",
    )(*args)


def kernel(x_prompt, x_sample, cache_swa_meta_kv, cache_swa_window_kv, state_gdn_conv, state_gdn, state_ffn_conv, meta_tokens, rel_bias_table, w_in, gdn_conv_w, gdn_a_log, gdn_dt_bias, gdn_norm_w, swa_sinks, w_out, norm_mix_pre, norm_mix_post, norm_ffn_pre, norm_ffn_post, ffn_w_gate, ffn_w_up, ffn_conv_w, ffn_w_down):
    f32, bf16 = jnp.float32, jnp.bfloat16
    seq = x_prompt.shape[1]
    dec_b, dec_t = x_sample.shape[0], x_sample.shape[1]
    n_dec = dec_b * dec_t
    assert x_prompt.shape[0] == 1 and seq % CHUNK == 0 and dec_t == SUBLANES and n_dec % CHUNK == 0

    w_in_p, w_ba = _pack_w_in(jnp.transpose(w_in[0]), 2 * GDN_HEADS, tn=512)
    wo = w_out[0].astype(bf16)
    wg = ffn_w_gate[0].astype(bf16)
    wu = ffn_w_up[0].astype(bf16)
    wd = ffn_w_down[0].astype(bf16)
    lane_pad = lambda v: jnp.pad(v.reshape(1, GDN_HEADS), ((0, 0), (GDN_HEADS, LANES - 2 * GDN_HEADS)))
    alog_row = lane_pad(gdn_a_log[0])
    dtb_row = lane_pad(gdn_dt_bias[0])
    gnw = gdn_norm_w[0].reshape(1, HEAD_DIM)

    pad_rows = CHUNK - N_META
    n_small = n_dec + CHUNK
    x_big = x_prompt.reshape(seq, D_MODEL)
    x_small = jnp.concatenate(
        [x_sample.reshape(n_dec, D_MODEL), jnp.zeros((pad_rows, D_MODEL), f32), meta_tokens.astype(f32)], axis=0)
    nw = norm_mix_pre[0].reshape(1, D_MODEL)
    cw = gdn_conv_w[0]
    proj_small, ba_small = _inproj(x_small, nw, w_in_p, w_ba, tm=n_small, tn=512, row_chunk=128)
    proj_big, ba_big, qkv_tail = _inproj(
        x_big, nw, w_in_p, w_ba, (cw, proj_small[n_small - SUBLANES:, :GDN_QKV]),
        tm=1024, tn=512, row_chunk=128)

    small_chunks = proj_small.reshape(n_small // CHUNK, CHUNK, PROJ_COLS)
    small_groups = proj_small.reshape(n_small // SUBLANES, SUBLANES, PROJ_COLS)
    last_chunk = n_small // CHUNK - 1
    gdn_meta, s_meta = _gdn(
        small_chunks, ba_small.reshape(n_small // CHUNK, CHUNK, LANES), lambda s: (last_chunk, 0, 0),
        jnp.zeros((1, SUBLANES, GDN_QKV), f32), lambda s: (0, 0, 0),
        jnp.zeros((1, GDN_HEADS, HEAD_DIM, HEAD_DIM), f32), cw, alog_row, dtb_row, gnw,
        n_steps=1, nb=1, seq=CHUNK, group=CHUNK, carry=True, pad_rows=pad_rows, preconv=False)
    gdn_big, s_prompt = _gdn(
        proj_big.reshape(1, seq, PROJ_COLS), ba_big.reshape(1, seq, LANES), lambda s: (0, s, 0),
        jnp.zeros((1, SUBLANES, GDN_QKV), f32), lambda s: (0, 0, 0),
        s_meta, cw, alog_row, dtb_row, gnw,
        n_steps=seq // (GDN_SEQ_CHUNKS * CHUNK), nb=1, seq=GDN_SEQ_CHUNKS * CHUNK, group=CHUNK,
        carry=True, pad_rows=0, preconv=True)
    hist_gdn = jnp.pad(state_gdn_conv[0], ((0, 0), (SUBLANES - (GDN_CONV - 1), 0), (0, 0)))
    nb_gdn = CHUNK // dec_t
    gdn_small, s_sample = _gdn(
        small_groups, ba_small.reshape(n_small // SUBLANES, SUBLANES, LANES), lambda s: (s, 0, 0),
        hist_gdn, lambda s: (s, 0, 0),
        state_gdn, cw, alog_row, dtb_row, gnw,
        n_steps=dec_b // nb_gdn, nb=nb_gdn, seq=dec_t, group=dec_t, carry=False, pad_rows=0, preconv=False)

    qi = np.arange(WINDOW)[:, None]
    kj = np.arange(WINDOW)[None, :]
    mi = np.arange(N_META)[None, :]
    ti = np.arange(dec_t)[:, None]
    new_keys = kj - N_META
    id_arrays = [
        _bucket_ids(qi - kj, qi >= kj),
        _bucket_ids(qi - kj + WINDOW, kj > qi),
        _bucket_ids(qi + N_META - kj, kj < N_META),
        _bucket_ids(qi + N_META - kj + WINDOW, kj < N_META),
        _bucket_ids(ti + WINDOW - kj, kj > ti),
        _bucket_ids(np.where(new_keys < 0, PAST_LEN + ti - kj, ti - new_keys),
                    (new_keys < 0) | ((new_keys <= ti) & (new_keys < dec_t))),
        _bucket_ids(mi.T - mi, mi.T >= mi),
    ]
    bcur, bprev, bm0, bfar, bwin, bsmall, bmm = _bias_tables(rel_bias_table, id_arrays)
    sink_rows = lambda q: jnp.repeat(swa_sinks[0].reshape(SWA_KV_HEADS, SWA_GROUP), q, axis=1)[..., None]

    sq_blk = COL_SQ // SWA_WIDTH
    kv_blk = COL_KV // KV_WIDTH
    meta_blk = (n_small - N_META) // N_META
    full3 = lambda a: pl.BlockSpec(a.shape, lambda j: (0, 0, 0))
    sink_p = sink_rows(WINDOW)
    swa_rows = SWA_Q_BLOCKS * WINDOW
    swa_big = pl.pallas_call(
        _swa_prompt_kernel,
        grid=(seq // swa_rows,),
        in_specs=[
            pl.BlockSpec((swa_rows, SWA_WIDTH), lambda j: (j, sq_blk)),
            pl.BlockSpec((swa_rows, KV_WIDTH), lambda j: (j, kv_blk)),
            pl.BlockSpec((WINDOW, KV_WIDTH), lambda j: (jnp.maximum(j * SWA_Q_BLOCKS - 1, 0), kv_blk)),
            pl.BlockSpec((N_META, KV_WIDTH), lambda j: (meta_blk, kv_blk)),
            full3(bcur), full3(bprev), full3(bm0), full3(bfar), full3(sink_p),
        ],
        out_specs=pl.BlockSpec((swa_rows, SWA_WIDTH), lambda j: (j, 0)),
        out_shape=jax.ShapeDtypeStruct((seq, SWA_WIDTH), bf16),
        compiler_params=pltpu.CompilerParams(
            dimension_semantics=("arbitrary",), vmem_limit_bytes=VMEM_LIMIT),
        name="swa_prompt",
    )(proj_big, proj_big, proj_big, proj_small, bcur, bprev, bm0, bfar, sink_p)

    nb_swa = 8
    sink_s = sink_rows(dec_t)
    win = cache_swa_window_kv.reshape(dec_b, WINDOW * KV_SLOTS, HEAD_DIM)
    meta_kv = cache_swa_meta_kv.reshape(dec_b, N_META * KV_SLOTS, HEAD_DIM)
    swa_small, win_new = pl.pallas_call(
        functools.partial(_swa_sample_kernel, nb=nb_swa, seq=dec_t),
        grid=(dec_b // nb_swa,),
        in_specs=[
            pl.BlockSpec((nb_swa, dec_t, SWA_WIDTH), lambda j: (j, 0, sq_blk)),
            pl.BlockSpec((nb_swa, dec_t, KV_WIDTH), lambda j: (j, 0, kv_blk)),
            pl.BlockSpec((nb_swa, WINDOW * KV_SLOTS, HEAD_DIM), lambda j: (j, 0, 0)),
            pl.BlockSpec((nb_swa, N_META * KV_SLOTS, HEAD_DIM), lambda j: (j, 0, 0)),
            full3(bwin), full3(bsmall), full3(sink_s),
        ],
        out_specs=[pl.BlockSpec((nb_swa * dec_t, SWA_WIDTH), lambda j: (j, 0)),
                   pl.BlockSpec((nb_swa, WINDOW * KV_SLOTS, HEAD_DIM), lambda j: (j, 0, 0))],
        out_shape=[jax.ShapeDtypeStruct((n_dec, SWA_WIDTH), bf16),
                   jax.ShapeDtypeStruct((dec_b, WINDOW * KV_SLOTS, HEAD_DIM), f32)],
        compiler_params=pltpu.CompilerParams(
            dimension_semantics=("arbitrary",), vmem_limit_bytes=VMEM_LIMIT),
        name="swa_sample",
    )(small_groups, small_groups, win, meta_kv, bwin, bsmall, sink_s)

    sink_m = sink_rows(N_META)
    swa_meta = pl.pallas_call(
        _swa_meta_kernel,
        grid=(1,),
        in_specs=[
            pl.BlockSpec((N_META, SWA_WIDTH), lambda j: (meta_blk, sq_blk)),
            pl.BlockSpec((N_META, KV_WIDTH), lambda j: (meta_blk, kv_blk)),
            full3(bmm), full3(sink_m),
        ],
        out_specs=pl.BlockSpec((N_META, SWA_WIDTH), lambda j: (0, 0)),
        out_shape=jax.ShapeDtypeStruct((N_META, SWA_WIDTH), bf16),
        name="swa_meta",
    )(proj_small, proj_small, bmm, sink_m)

    nw_post = norm_mix_post[0].reshape(1, D_MODEL)
    nf_pre = norm_ffn_pre[0].reshape(1, D_MODEL)
    nf_post = norm_ffn_post[0].reshape(1, D_MODEL)
    fcw = ffn_conv_w[0]
    gdn_small_all = jnp.concatenate([gdn_small, gdn_meta], axis=0)
    swa_small_all = jnp.concatenate([swa_small, jnp.zeros((pad_rows, SWA_WIDTH), bf16), swa_meta], axis=0)
    h_small = _outproj(gdn_small_all, swa_small_all, x_small, wo, nw_post, tm=n_small // 2)
    hist_ffn = jnp.pad(state_ffn_conv[0], ((0, CHUNK // SUBLANES), (SUBLANES - (FFN_CONV - 1), 0), (0, 0)))
    y_small, g_small = _ffn(h_small, nf_pre, wg, wu, fcw, wd, nf_post, hist_ffn,
                            batch=True, tm=n_small // 2, tf=512, out_rows=n_dec)
    h_big = _outproj(gdn_big, swa_big, x_big, wo, nw_post, tm=512)
    y_big, g_tail = _ffn(h_big, nf_pre, wg, wu, fcw, wd, nf_post, g_small[n_small - SUBLANES:],
                         batch=False, tm=1024, tf=512)

    kv_shape = lambda n: (1, n, 2, SWA_KV_HEADS, HEAD_DIM)
    kv_small = proj_small[:, COL_KV:COL_KV + KV_WIDTH]
    y_prompt = y_big.reshape(1, seq, D_MODEL)
    y_sample = y_small.reshape(dec_b, dec_t, D_MODEL)
    p_meta_kv = kv_small[n_small - N_META:].reshape(kv_shape(N_META))[None]
    p_window_kv = proj_big[seq - WINDOW:, COL_KV:COL_KV + KV_WIDTH].reshape(kv_shape(WINDOW))[None]
    p_gdn_conv = qkv_tail[qkv_tail.shape[0] - (GDN_CONV - 1):].reshape(1, 1, GDN_CONV - 1, GDN_QKV)
    p_gdn_state = s_prompt[None]
    p_ffn_conv = g_tail[g_tail.shape[0] - (FFN_CONV - 1):].reshape(1, 1, FFN_CONV - 1, D_FF)
    s_window_kv = win_new.reshape(1, dec_b, WINDOW, 2, SWA_KV_HEADS, HEAD_DIM)
    s_gdn_conv = proj_small[:n_dec, :GDN_QKV].reshape(dec_b, dec_t, GDN_QKV)[:, dec_t - (GDN_CONV - 1):][None]
    s_gdn_state = s_sample
    s_ffn_conv = g_small[:n_dec].reshape(dec_b, dec_t, D_FF)[:, dec_t - (FFN_CONV - 1):][None]
    return (y_prompt, y_sample, p_meta_kv, p_window_kv, p_gdn_conv, p_gdn_state, p_ffn_conv,
            s_window_kv, s_gdn_conv, s_gdn_state, s_ffn_conv)
```

```python
import functools
import math

import numpy as np
import jax
import jax.numpy as jnp
from jax import lax
from jax.experimental import pallas as pl
from jax.experimental.pallas import tpu as pltpu

D_MODEL = 2048
HEAD_DIM = 128
GDN_HEADS = 8
GDN_WIDTH = GDN_HEADS * HEAD_DIM
GDN_QKV = 3 * GDN_WIDTH
SWA_HEADS = 8
SWA_KV_HEADS = 2
SWA_GROUP = SWA_HEADS // SWA_KV_HEADS
SWA_WIDTH = SWA_HEADS * HEAD_DIM
KV_SLOTS = 2 * SWA_KV_HEADS
KV_WIDTH = KV_SLOTS * HEAD_DIM
WINDOW = 128
N_META = 16
N_BUCKETS = 32
MAX_DISTANCE = 128
GDN_CONV = 4
FFN_CONV = 3
D_FF = 5632
EPS = 1e-6
PAST_LEN = 16384

SUBLANES = 8
LANES = 128

CHUNK = 128
assert CHUNK == HEAD_DIM == LANES
INV_BASE = 16
GDN_SEQ_CHUNKS = 2
INPROJ_ROW_BLOCKS = 2
FFN_ROW_BLOCKS = 2

COL_Z = GDN_QKV
COL_SQ = COL_Z + GDN_WIDTH
COL_KV = COL_SQ + SWA_WIDTH
PROJ_COLS = COL_KV + KV_WIDTH
GDN_COLS = COL_SQ

SWA_Q_BLOCKS = 4
NEG = -1e30
VMEM_LIMIT = 56 * 1024 * 1024

_NT = (((1,), (1,)), ((), ()))


def _dot(a, b):
    return jnp.dot(a.astype(jnp.bfloat16), b.astype(jnp.bfloat16), preferred_element_type=jnp.float32)


def _dot_nt(a, b):
    return lax.dot_general(a.astype(jnp.bfloat16), b.astype(jnp.bfloat16), _NT,
                           preferred_element_type=jnp.float32)


_dot_inv = _dot


def _dot_exact(a, b, dims=None):
    if dims is None:
        return jnp.dot(a, b, precision=lax.Precision.HIGHEST, preferred_element_type=jnp.float32)
    return lax.dot_general(a, b, dims, precision=lax.Precision.HIGHEST,
                           preferred_element_type=jnp.float32)


def _pack_w_in_kernel(a_ref, b_ref, o_ref, ba_ref, *, first_shifted, shift):
    j = pl.program_id(0)

    @pl.when(j < first_shifted)
    def _():
        o_ref[...] = a_ref[...].T.astype(o_ref.dtype)

    @pl.when(j >= first_shifted)
    def _():
        rows = jnp.concatenate([a_ref[shift:, :], b_ref[:shift, :]], axis=0)
        o_ref[...] = rows.T.astype(o_ref.dtype)

    @pl.when(j == first_shifted)
    def _():
        head = a_ref[:LANES, :]
        row = lax.broadcasted_iota(jnp.int32, head.shape, 0)
        ba_ref[...] = jnp.where(row < shift, head, 0.0).T.astype(ba_ref.dtype)


def _pack_w_in(w_in_t, n_ba, *, tn):
    n_blocks = PROJ_COLS // tn
    first_shifted = COL_SQ // tn
    assert COL_SQ % tn == 0 and w_in_t.shape[0] == PROJ_COLS + n_ba and n_ba % SUBLANES == 0
    return pl.pallas_call(
        functools.partial(_pack_w_in_kernel, first_shifted=first_shifted, shift=n_ba),
        grid=(n_blocks,),
        in_specs=[pl.BlockSpec((tn, D_MODEL), lambda j: (j, 0)),
                  pl.BlockSpec((tn, D_MODEL), lambda j: (jnp.maximum(j, first_shifted) + 1, 0))],
        out_specs=[pl.BlockSpec((D_MODEL, tn), lambda j: (0, j)),
                   pl.BlockSpec((D_MODEL, LANES), lambda j: (0, 0))],
        out_shape=[jax.ShapeDtypeStruct((D_MODEL, PROJ_COLS), jnp.bfloat16),
                   jax.ShapeDtypeStruct((D_MODEL, LANES), jnp.bfloat16)],
        compiler_params=pltpu.CompilerParams(
            dimension_semantics=("arbitrary",), vmem_limit_bytes=VMEM_LIMIT),
        name="pack_w_in",
    )(w_in_t, w_in_t)


def _rms_scale(x):
    return lax.rsqrt(jnp.mean(x * x, axis=-1, keepdims=True) + EPS)


def _silu(x):
    return x * (0.5 * jnp.tanh(0.5 * x) + 0.5)


def _inproj_kernel(*refs, row_chunk, conv_tiles):
    if conv_tiles:
        (x_ref, nw_ref, w_ref, wba_ref, cw_ref, hist_ref,
         o_ref, ba_ref, tail_ref, xn_ref, xe_ref, carry_ref) = refs
    else:
        x_ref, nw_ref, w_ref, wba_ref, o_ref, ba_ref, xn_ref = refs
    i = pl.program_id(0)
    j = pl.program_id(1)
    tm, tn = o_ref.shape

    def normalize(rows):
        x = x_ref[rows, :]
        xn_ref[rows, :] = (x * _rms_scale(x) * nw_ref[...]).astype(jnp.bfloat16)

    def plain():
        o_ref[...] = jnp.dot(xn_ref[...], w_ref[...], preferred_element_type=jnp.float32)

    if not conv_tiles:
        @pl.when(j == 0)
        def _():
            def body(c, carry):
                normalize(pl.ds(pl.multiple_of(c * row_chunk, row_chunk), row_chunk))
                return carry
            lax.fori_loop(0, tm // row_chunk, body, 0)
            ba_ref[...] = jnp.dot(xn_ref[...], wba_ref[...], preferred_element_type=jnp.float32)
        plain()
        return
    pl.when(j >= conv_tiles)(plain)

    def conv_tile(first):
        @pl.when(i == 0)
        def _():
            carry_ref[pl.ds(j, 1)] = hist_ref[...].reshape(1, SUBLANES, tn)
        xe_ref[0:SUBLANES, :] = carry_ref[pl.ds(j, 1)].reshape(SUBLANES, tn)
        rb = tm // INPROJ_ROW_BLOCKS

        def raw_block(r):
            rows = slice(r * rb, (r + 1) * rb)
            if first:
                normalize(rows)
                ba_ref[rows, :] = jnp.dot(xn_ref[rows, :], wba_ref[...], preferred_element_type=jnp.float32)
            return jnp.dot(xn_ref[rows, :], w_ref[...], preferred_element_type=jnp.float32)

        nxt = raw_block(0)
        for r in range(INPROJ_ROW_BLOCKS):
            raw = nxt
            if r + 1 < INPROJ_ROW_BLOCKS:
                nxt = raw_block(r + 1)
            base = SUBLANES + r * rb
            xe_ref[base:base + rb, :] = raw
            conv = None
            for s in range(GDN_CONV):
                term = xe_ref[base - s:base - s + rb, :] * cw_ref[GDN_CONV - 1 - s:GDN_CONV - s, :]
                conv = term if conv is None else conv + term
            o_ref[r * rb:(r + 1) * rb, :] = conv
        tail = xe_ref[tm:tm + SUBLANES, :]
        carry_ref[pl.ds(j, 1)] = tail.reshape(1, SUBLANES, tn)
        tail_ref[...] = tail

    pl.when(j == 0)(functools.partial(conv_tile, True))
    pl.when((j > 0) & (j < conv_tiles))(functools.partial(conv_tile, False))


def _inproj(x, nw, w, wba, conv=None, *, tm, tn, row_chunk):
    rows = x.shape[0]
    conv_tiles = GDN_QKV // tn if conv else 0
    in_specs = [
        pl.BlockSpec((tm, D_MODEL), lambda i, j: (i, 0)),
        pl.BlockSpec((1, D_MODEL), lambda i, j: (0, 0)),
        pl.BlockSpec((D_MODEL, tn), lambda i, j: (0, j)),
        pl.BlockSpec((D_MODEL, LANES), lambda i, j: (0, 0)),
    ]
    out_specs = [pl.BlockSpec((tm, tn), lambda i, j: (i, j)),
                 pl.BlockSpec((tm, LANES), lambda i, j: (i, 0))]
    out_shape = [jax.ShapeDtypeStruct((rows, PROJ_COLS), jnp.float32),
                 jax.ShapeDtypeStruct((rows, LANES), jnp.float32)]
    scratch = [pltpu.VMEM((tm, D_MODEL), jnp.bfloat16)]
    args = [x, nw, w, wba]
    if conv:
        conv_col = lambda i, j: (0, jnp.minimum(j, conv_tiles - 1))
        in_specs += [pl.BlockSpec((GDN_CONV, tn), conv_col), pl.BlockSpec((SUBLANES, tn), conv_col)]
        out_specs.append(pl.BlockSpec((SUBLANES, tn), lambda i, j: (i, jnp.minimum(j, conv_tiles - 1))))
        out_shape.append(jax.ShapeDtypeStruct((rows // tm * SUBLANES, GDN_QKV), jnp.float32))
        scratch += [pltpu.VMEM((SUBLANES + tm, tn), jnp.float32),
                    pltpu.VMEM((conv_tiles, SUBLANES, tn), jnp.float32)]
        args += list(conv)
    return pl.pallas_call(
        functools.partial(_inproj_kernel, row_chunk=row_chunk, conv_tiles=conv_tiles),
        grid=(rows // tm, PROJ_COLS // tn),
        in_specs=in_specs,
        out_specs=out_specs,
        out_shape=out_shape,
        scratch_shapes=scratch,
        compiler_params=pltpu.CompilerParams(
            dimension_semantics=("arbitrary", "arbitrary"), vmem_limit_bytes=VMEM_LIMIT),
        name="inproj_conv" if conv else "inproj",
    )(*args)


def _tri_inverse(lms, ri, ci):
    shift = INV_BASE.bit_length() - 1
    eye = (ri == ci).astype(jnp.float32)
    in_block = (ri >> shift) == (ci >> shift)
    ps = [jnp.where(in_block, lm, 0.0) for lm in lms]
    ts = [eye - p for p in ps]
    for _ in range(shift - 1):
        ps = [_dot_inv(p, p) for p in ps]
        ts = [t + _dot_inv(t, p) for t, p in zip(ts, ps)]
    size = INV_BASE
    while size < CHUNK:
        shift += 1
        in_pair = (ri >> shift) == (ci >> shift)
        off_mask = in_pair & jnp.logical_not(in_block)
        tos = [_dot_inv(t, jnp.where(off_mask, lm, 0.0)) for t, lm in zip(ts, lms)]
        ts = [t - _dot_inv(to, t) for t, to in zip(ts, tos)]
        in_block = in_pair
        size *= 2
    return ts


def _gdn_kernel(x_ref, ba_ref, hist_ref, s0_ref, cw_ref, alog_ref, dtb_ref, gnw_ref,
                o_ref, sout_ref, xe_ref, s_ref, *, nb, seq, group, carry, pad_rows, preconv):
    step = pl.program_id(0)
    rows = nb * seq
    n_chunks = rows // CHUNK
    n_groups = CHUNK // group
    gshift = group.bit_length() - 1

    if carry:
        @pl.when(step == 0)
        def _():
            s_ref[...] = s0_ref[0]
    if not preconv:
        if carry:
            @pl.when(step == 0)
            def _():
                xe_ref[:, 0:SUBLANES, :] = hist_ref[...]
        else:
            xe_ref[:, 0:SUBLANES, :] = hist_ref[...]
        xe_ref[:, SUBLANES:SUBLANES + seq, :] = x_ref[:, :, 0:GDN_QKV]

    seq_rows = min(seq, CHUNK)
    seqs_per_chunk = CHUNK // seq_rows

    def chunk_rows(ref, c, row_off, cols):
        if seq >= CHUNK:
            start = row_off + c * CHUNK
            return ref[0:1, start:start + CHUNK, cols]
        b0 = c * seqs_per_chunk
        return ref[b0:b0 + seqs_per_chunk, row_off:row_off + seq, cols]

    def conv_chunk(col, c):
        cols = slice(col, col + HEAD_DIM)
        if preconv:
            return _silu(chunk_rows(x_ref, c, 0, cols).reshape(CHUNK, HEAD_DIM))
        acc = None
        for s in range(GDN_CONV):
            term = chunk_rows(xe_ref, c, SUBLANES - s, cols) * cw_ref[GDN_CONV - 1 - s:GDN_CONV - s, cols]
            acc = term if acc is None else acc + term
        return _silu(acc).reshape(CHUNK, HEAD_DIM)

    ri = lax.broadcasted_iota(jnp.int32, (CHUNK, CHUNK), 0)
    ci = lax.broadcasted_iota(jnp.int32, (CHUNK, CHUNK), 1)
    same = (ri >> gshift) == (ci >> gshift)
    m_incl = same & (ri >= ci)
    m_strict = same & (ri > ci)
    f_incl = m_incl.astype(jnp.float32)
    f_same = same.astype(jnp.float32)
    lane = lax.broadcasted_iota(jnp.int32, (CHUNK, LANES), 1)
    row_in_chunk = lax.broadcasted_iota(jnp.int32, (CHUNK, LANES), 0)

    pre = []
    for c in range(n_chunks):
        bac = chunk_rows(ba_ref, c, 0, slice(0, LANES)).reshape(CHUNK, LANES)
        beta_all = jax.nn.sigmoid(bac)
        sp_in = bac + dtb_ref[...]
        softplus = jnp.maximum(sp_in, 0.0) + jnp.log1p(jnp.exp(-jnp.abs(sp_in)))
        g_all = -jnp.exp(alog_ref[...]) * softplus
        if pad_rows and c == 0:
            valid = row_in_chunk >= pad_rows
            beta_all = jnp.where(valid, beta_all, 0.0)
            g_all = jnp.where(valid, g_all, 0.0)
        g_all = jnp.where((lane >= GDN_HEADS) & (lane < 2 * GDN_HEADS), g_all, 0.0)
        gc_col = _dot_exact(f_incl, g_all)
        if n_groups == 1:
            gtot_col = jnp.broadcast_to(gc_col[CHUNK - 1:CHUNK, :], (CHUNK, LANES))
        else:
            gtot_col = _dot_exact(f_same, g_all)
        gc_row = gc_col.T

        for h in range(GDN_HEADS):
            qh = conv_chunk(h * HEAD_DIM, c)
            kh = conv_chunk(GDN_WIDTH + h * HEAD_DIM, c)
            vh = conv_chunk(2 * GDN_WIDTH + h * HEAD_DIM, c)
            qh = qh * lax.rsqrt(jnp.sum(qh * qh, -1, keepdims=True) + EPS) * (HEAD_DIM ** -0.5)
            kh = kh * lax.rsqrt(jnp.sum(kh * kh, -1, keepdims=True) + EPS)
            bcast = lambda col: jnp.broadcast_to(col, (CHUNK, HEAD_DIM))
            gcc = bcast(gc_col[:, GDN_HEADS + h:GDN_HEADS + h + 1])
            gtc = bcast(gtot_col[:, GDN_HEADS + h:GDN_HEADS + h + 1])
            beta = bcast(beta_all[:, h:h + 1])
            gcr = gc_row[GDN_HEADS + h:GDN_HEADS + h + 1, :]
            decay = jnp.exp(jnp.where(m_incl, gcc - gcr, NEG))
            kb = kh * beta
            egc = jnp.exp(gcc)
            pre.append(dict(
                c=c, h=h,
                lm=jnp.where(m_strict, _dot_nt(kb, kh) * decay, 0.0),
                qk=_dot_nt(qh, kh) * decay,
                rhs=jnp.concatenate([vh * beta, kb * egc], axis=1),
                qg=qh * egc,
                kd_t=(kh * jnp.exp(gtc - gcc)).T,
                gl=jnp.exp(gtc)))

    inverses = _tri_inverse([p["lm"] for p in pre], ri, ci)
    sols = [_dot_inv(t, p["rhs"]) for t, p in zip(inverses, pre)]

    for c in range(n_chunks):
        r0 = c * CHUNK
        items = [(p, sol) for p, sol in zip(pre, sols) if p["c"] == c]
        state = lambda h, b: s_ref[h] if carry else s0_ref[c * n_groups + b, h]
        ws, qs = [], []
        for p, sol in items:
            w = sol[:, HEAD_DIM:]
            ws_parts, qs_parts = [], []
            for b in range(n_groups):
                g0 = b * group
                wq = jnp.concatenate([w[g0:g0 + group], p["qg"][g0:g0 + group]], axis=0)
                res = _dot(wq, state(p["h"], b))
                ws_parts.append(res[:group])
                qs_parts.append(res[group:])
            ws.append(ws_parts[0] if n_groups == 1 else jnp.concatenate(ws_parts, axis=0))
            qs.append(qs_parts[0] if n_groups == 1 else jnp.concatenate(qs_parts, axis=0))
        v_new = [sol[:, :HEAD_DIM] - w for (p, sol), w in zip(items, ws)]
        o = [a + _dot(p["qk"], vn) for a, (p, sol), vn in zip(qs, items, v_new)]
        for (p, sol), vn in zip(items, v_new):
            h = p["h"]
            for b in range(n_groups):
                g0 = b * group
                kd_b = p["kd_t"] if n_groups == 1 else jnp.where((ci >> gshift) == b, p["kd_t"], 0.0)
                st = state(h, b) * p["gl"][g0:g0 + 1, :] + _dot(kd_b, vn)
                if carry:
                    s_ref[h] = st
                else:
                    sout_ref[c * n_groups + b, h] = st
        for (p, sol), oh in zip(items, o):
            h = p["h"]
            z = chunk_rows(x_ref, c, 0, slice(COL_Z + h * HEAD_DIM, COL_Z + (h + 1) * HEAD_DIM)).reshape(CHUNK, HEAD_DIM)
            y = oh * lax.rsqrt(jnp.mean(oh * oh, -1, keepdims=True) + EPS) * gnw_ref[...] * _silu(z)
            o_ref[r0:r0 + CHUNK, h * HEAD_DIM:(h + 1) * HEAD_DIM] = y.astype(o_ref.dtype)

    if carry:
        if not preconv:
            xe_ref[:, 0:SUBLANES, :] = xe_ref[:, seq:seq + SUBLANES, :]

        @pl.when(step == pl.num_programs(0) - 1)
        def _():
            sout_ref[0] = s_ref[...]


def _gdn(x3, ba3, x_idx, hist, hist_idx, s0, cw, alog_row, dtb_row, gnw, *,
         n_steps, nb, seq, group, carry, pad_rows, preconv):
    rows = nb * seq
    if carry:
        state_spec = pl.BlockSpec((1, GDN_HEADS, HEAD_DIM, HEAD_DIM), lambda s: (0, 0, 0, 0))
        state_shape = (1, GDN_HEADS, HEAD_DIM, HEAD_DIM)
    else:
        n_states = rows // group
        state_spec = pl.BlockSpec((None, n_states, GDN_HEADS, HEAD_DIM, HEAD_DIM), lambda s: (0, s, 0, 0, 0))
        state_shape = (1, n_steps * n_states, GDN_HEADS, HEAD_DIM, HEAD_DIM)
    full = lambda shape: pl.BlockSpec(shape, lambda s: (0,) * len(shape))
    return pl.pallas_call(
        functools.partial(_gdn_kernel, nb=nb, seq=seq, group=group, carry=carry, pad_rows=pad_rows,
                          preconv=preconv),
        grid=(n_steps,),
        in_specs=[
            pl.BlockSpec((nb, seq, GDN_COLS), x_idx),
            pl.BlockSpec((nb, seq, LANES), x_idx),
            pl.BlockSpec((nb, SUBLANES, GDN_QKV), hist_idx),
            state_spec,
            full((GDN_CONV, GDN_QKV)),
            full((1, LANES)),
            full((1, LANES)),
            full((1, HEAD_DIM)),
        ],
        out_specs=[
            pl.BlockSpec((rows, GDN_WIDTH), lambda s: (s, 0)),
            state_spec,
        ],
        out_shape=[
            jax.ShapeDtypeStruct((n_steps * rows, GDN_WIDTH), jnp.bfloat16),
            jax.ShapeDtypeStruct(state_shape, jnp.float32),
        ],
        scratch_shapes=[
            pltpu.VMEM((nb, SUBLANES + (SUBLANES if preconv else seq), GDN_QKV), jnp.float32),
            pltpu.VMEM((GDN_HEADS, HEAD_DIM, HEAD_DIM), jnp.float32),
        ],
        compiler_params=pltpu.CompilerParams(
            dimension_semantics=("arbitrary",), vmem_limit_bytes=VMEM_LIMIT),
        name="gdn_seq" if carry else "gdn_batch",
    )(x3, ba3, hist, s0, cw, alog_row, dtb_row, gnw)


def _t5_bucket_np(dist):
    n = np.maximum(dist, 0)
    exact = N_BUCKETS // 2
    large = exact + (np.log(np.maximum(n, 1).astype(np.float32) / exact)
                     / math.log(MAX_DISTANCE / exact) * (N_BUCKETS - exact)).astype(np.int32)
    return np.where(n < exact, n, np.minimum(large, N_BUCKETS - 1)).astype(np.int32)


def _bucket_ids(dist, valid):
    return np.where(valid, _t5_bucket_np(dist), -1).astype(np.int32)


def _bias_kernel(table_ref, *refs):
    n = len(refs) // 2
    for ids_ref, out_ref in zip(refs[:n], refs[n:]):
        nq = ids_ref.shape[0]
        rows_per_pass = min(nq, 2 * SUBLANES)
        for r0 in range(0, nq, rows_per_pass):
            ids = ids_ref[r0:r0 + rows_per_pass, :]

            def body(b, accs):
                hit = ids == b
                return tuple(jnp.where(hit, table_ref[b, head], acc) for head, acc in enumerate(accs))

            init = tuple(jnp.full(ids.shape, NEG, jnp.float32) for _ in range(SWA_HEADS))
            for head, acc in enumerate(lax.fori_loop(0, N_BUCKETS, body, init)):
                kh, g = divmod(head, SWA_GROUP)
                out_ref[kh, g * nq + r0:g * nq + r0 + rows_per_pass, :] = acc


def _bias_tables(rel_table, id_arrays):
    out_shapes = [jax.ShapeDtypeStruct((SWA_KV_HEADS, SWA_GROUP * a.shape[0], a.shape[1]), jnp.float32)
                  for a in id_arrays]
    vmem = pl.BlockSpec(memory_space=pltpu.VMEM)
    return pl.pallas_call(
        _bias_kernel,
        in_specs=[pl.BlockSpec(memory_space=pltpu.SMEM)] + [vmem] * len(id_arrays),
        out_specs=[vmem] * len(id_arrays),
        out_shape=out_shapes,
        name="swa_bias",
    )(rel_table, *[jnp.asarray(a) for a in id_arrays])


def _attend(problems):
    scale = HEAD_DIM ** -0.5
    scores = [[_dot_nt(q, k) * scale + b for k, b in zip(keys, biases)]
              for q, keys, _, biases, _ in problems]
    maxes = []
    for (_, _, _, _, sink), segs in zip(problems, scores):
        m = sink
        for s in segs:
            m = jnp.maximum(m, jnp.max(s, axis=-1, keepdims=True))
        maxes.append(m)
    probs = [[jnp.exp(s - m) for s in segs] for segs, m in zip(scores, maxes)]
    outs = []
    for (_, _, values, _, sink), ps, m in zip(problems, probs, maxes):
        acc = None
        for p, v in zip(ps, values):
            v_ones = jnp.concatenate([v, jnp.ones((v.shape[0], HEAD_DIM), v.dtype)], axis=1)
            pv = _dot(p, v_ones)
            acc = pv if acc is None else acc + pv
        den = acc[:, HEAD_DIM:] + jnp.exp(sink - m)
        outs.append(acc[:, :HEAD_DIM] / den)
    return outs


def _group_queries(q_rows, kh):
    return jnp.concatenate(
        [q_rows((kh * SWA_GROUP + g) * HEAD_DIM, (kh * SWA_GROUP + g + 1) * HEAD_DIM)
         for g in range(SWA_GROUP)], axis=0)


def _pad_keys(rows):
    return jnp.concatenate([rows, jnp.zeros((WINDOW - rows.shape[0], rows.shape[1]), rows.dtype)], axis=0)


def _k_cols(kh):
    return slice(kh * HEAD_DIM, (kh + 1) * HEAD_DIM)


def _v_cols(kh):
    return slice((SWA_KV_HEADS + kh) * HEAD_DIM, (SWA_KV_HEADS + kh + 1) * HEAD_DIM)


def _swa_prompt_kernel(q_ref, kvc_ref, kvp_ref, kvm_ref, bcur_ref, bprev_ref, bm0_ref, bfar_ref,
                       sink_ref, o_ref):
    first = pl.program_id(0) == 0
    problems = []
    for blk in range(SWA_Q_BLOCKS):
        rows = slice(blk * WINDOW, (blk + 1) * WINDOW)
        prev_ref, prev_rows = (kvp_ref, slice(0, WINDOW)) if blk == 0 else (
            kvc_ref, slice((blk - 1) * WINDOW, blk * WINDOW))
        for kh in range(SWA_KV_HEADS):
            ks, vs = _k_cols(kh), _v_cols(kh)
            first_block = first if blk == 0 else False
            b_prev = jnp.where(first_block, NEG, bprev_ref[kh])
            b_meta = jnp.where(first_block, bm0_ref[kh], bfar_ref[kh])
            problems.append((_group_queries(lambda a, b, rows=rows: q_ref[rows, a:b], kh),
                             [kvc_ref[rows, ks], prev_ref[prev_rows, ks], _pad_keys(kvm_ref[:, ks])],
                             [kvc_ref[rows, vs], prev_ref[prev_rows, vs], _pad_keys(kvm_ref[:, vs])],
                             [bcur_ref[kh], b_prev, b_meta], sink_ref[kh]))
    for i, o in enumerate(_attend(problems)):
        blk, kh = divmod(i, SWA_KV_HEADS)
        for g in range(SWA_GROUP):
            head = kh * SWA_GROUP + g
            o_ref[blk * WINDOW:(blk + 1) * WINDOW, head * HEAD_DIM:(head + 1) * HEAD_DIM] = (
                o[g * WINDOW:(g + 1) * WINDOW].astype(o_ref.dtype))


def _swa_meta_kernel(q_ref, kv_ref, bias_ref, sink_ref, o_ref):
    problems = [(_group_queries(lambda a, b: q_ref[:, a:b], kh), [kv_ref[:, _k_cols(kh)]],
                 [kv_ref[:, _v_cols(kh)]], [bias_ref[kh]], sink_ref[kh])
                for kh in range(SWA_KV_HEADS)]
    for kh, o in enumerate(_attend(problems)):
        for g in range(SWA_GROUP):
            head = kh * SWA_GROUP + g
            o_ref[:, head * HEAD_DIM:(head + 1) * HEAD_DIM] = (
                o[g * N_META:(g + 1) * N_META].astype(o_ref.dtype))


def _swa_sample_kernel(q_ref, kvn_ref, win_ref, meta_ref, bwin_ref, bsmall_ref, sink_ref,
                       o_ref, wout_ref, *, nb, seq):
    cached = lambda ref, b, slot, n: ref[b, pl.ds(slot, n, stride=KV_SLOTS), :]
    keep = (WINDOW - seq) * KV_SLOTS
    wout_ref[:, 0:keep, :] = win_ref[:, seq * KV_SLOTS:WINDOW * KV_SLOTS, :]
    for slot in range(KV_SLOTS):
        wout_ref[:, pl.ds(keep + slot, seq, stride=KV_SLOTS), :] = (
            kvn_ref[:, :, slot * HEAD_DIM:(slot + 1) * HEAD_DIM])
    problems = []
    for b in range(nb):
        for kh in range(SWA_KV_HEADS):
            ks, vs = _k_cols(kh), _v_cols(kh)
            k_small = _pad_keys(jnp.concatenate([cached(meta_ref, b, kh, N_META), kvn_ref[b, :, ks]], axis=0))
            v_small = _pad_keys(jnp.concatenate(
                [cached(meta_ref, b, SWA_KV_HEADS + kh, N_META), kvn_ref[b, :, vs]], axis=0))
            problems.append((_group_queries(lambda a, c, b=b: q_ref[b, :, a:c], kh),
                             [cached(win_ref, b, kh, WINDOW), k_small],
                             [cached(win_ref, b, SWA_KV_HEADS + kh, WINDOW), v_small],
                             [bwin_ref[kh], bsmall_ref[kh]], sink_ref[kh]))
    outs = _attend(problems)
    for head in range(SWA_HEADS):
        kh, g = divmod(head, SWA_GROUP)
        rows = [outs[b * SWA_KV_HEADS + kh][g * seq:(g + 1) * seq] for b in range(nb)]
        o_ref[:, head * HEAD_DIM:(head + 1) * HEAD_DIM] = jnp.concatenate(rows, axis=0).astype(o_ref.dtype)


def _outproj_kernel(g_ref, s_ref, h_ref, wo_ref, nw_ref, o_ref):
    both = jnp.concatenate([g_ref[...], s_ref[...]], axis=1)
    mix = jnp.dot(both, wo_ref[...], preferred_element_type=jnp.float32)
    o_ref[...] = h_ref[...] + mix * _rms_scale(mix) * nw_ref[...]


def _outproj(g, s, h, wo, nw, *, tm):
    rows = h.shape[0]
    return pl.pallas_call(
        _outproj_kernel,
        grid=(rows // tm,),
        in_specs=[
            pl.BlockSpec((tm, GDN_WIDTH), lambda i: (i, 0)),
            pl.BlockSpec((tm, SWA_WIDTH), lambda i: (i, 0)),
            pl.BlockSpec((tm, D_MODEL), lambda i: (i, 0)),
            pl.BlockSpec((D_MODEL, D_MODEL), lambda i: (0, 0)),
            pl.BlockSpec((1, D_MODEL), lambda i: (0, 0)),
        ],
        out_specs=pl.BlockSpec((tm, D_MODEL), lambda i: (i, 0)),
        out_shape=jax.ShapeDtypeStruct((rows, D_MODEL), jnp.float32),
        compiler_params=pltpu.CompilerParams(
            dimension_semantics=("arbitrary",), vmem_limit_bytes=VMEM_LIMIT),
        name="outproj",
    )(g, s, h, wo, nw)


def _ffn_kernel(*refs, batch, tm, tf):
    if batch:
        (h_ref, nw_pre_ref, wg_ref, wu_ref, cw_ref, wd_ref, nw_post_ref, hist_ref,
         y_ref, graw_ref, xn_ref, xe_ref) = refs
    else:
        (h_ref, nw_pre_ref, wg_ref, wu_ref, cw_ref, wd_ref, nw_post_ref, hist_ref,
         y_ref, graw_ref, xn_ref, xe_ref, carry_ref) = refs
    i = pl.program_id(0)
    j = pl.program_id(1)
    last_j = pl.num_programs(1) - 1
    rb = tm // FFN_ROW_BLOCKS

    if batch:
        xe_ref[:, 0:SUBLANES, :] = hist_ref[...]
    else:
        @pl.when(i == 0)
        def _():
            carry_ref[pl.ds(j, 1)] = hist_ref[...].reshape(1, SUBLANES, tf)
        xe_ref[:, 0:SUBLANES, :] = carry_ref[pl.ds(j, 1)]

    def step(first, last):
        def gate_up(r):
            rows = slice(r * rb, (r + 1) * rb)
            if first:
                h = h_ref[rows, :]
                xn_ref[rows, :] = (h * _rms_scale(h) * nw_pre_ref[...]).astype(jnp.bfloat16)
            xn = xn_ref[rows, :]
            return (jnp.dot(xn, wg_ref[...], preferred_element_type=jnp.float32),
                    jnp.dot(xn, wu_ref[...], preferred_element_type=jnp.float32))

        nxt = gate_up(0)
        for r in range(FFN_ROW_BLOCKS):
            rows = slice(r * rb, (r + 1) * rb)
            gate, up = nxt
            if r + 1 < FFN_ROW_BLOCKS:
                nxt = gate_up(r + 1)
            if batch:
                seqs = slice(r * rb // SUBLANES, (r + 1) * rb // SUBLANES)
                graw_ref[rows, :] = gate
                xe_ref[seqs, SUBLANES:2 * SUBLANES, :] = gate.reshape(rb // SUBLANES, SUBLANES, tf)
                taps = [xe_ref[seqs, SUBLANES - s:2 * SUBLANES - s, :] for s in range(FFN_CONV)]
            else:
                base = SUBLANES + r * rb
                xe_ref[:, base:base + rb, :] = gate.reshape(1, rb, tf)
                taps = [xe_ref[:, base - s:base - s + rb, :] for s in range(FFN_CONV)]
            conv = None
            for s, tap in enumerate(taps):
                term = tap * cw_ref[FFN_CONV - 1 - s:FFN_CONV - s, :]
                conv = term if conv is None else conv + term
            act = (_silu(conv.reshape(rb, tf)) * up).astype(jnp.bfloat16)
            down = jnp.dot(act, wd_ref[...], preferred_element_type=jnp.float32)
            if first:
                y_ref[rows, :] = down
            elif last:
                y = y_ref[rows, :] + down
                y_ref[rows, :] = h_ref[rows, :] + y * _rms_scale(y) * nw_post_ref[...]
            else:
                y_ref[rows, :] += down
        if not batch:
            tail = xe_ref[:, tm:tm + SUBLANES, :]
            carry_ref[pl.ds(j, 1)] = tail
            graw_ref[...] = tail.reshape(SUBLANES, tf)

    pl.when(j == 0)(functools.partial(step, True, False))
    pl.when((j > 0) & (j < last_j))(functools.partial(step, False, False))
    pl.when(j == last_j)(functools.partial(step, False, True))


def _ffn(h, nw_pre, wg, wu, cw, wd, nw_post, hist, *, batch, tm, tf):
    rows = h.shape[0]
    nj = D_FF // tf
    in_specs = [
        pl.BlockSpec((tm, D_MODEL), lambda i, j: (i, 0)),
        pl.BlockSpec((1, D_MODEL), lambda i, j: (0, 0)),
        pl.BlockSpec((D_MODEL, tf), lambda i, j: (0, j)),
        pl.BlockSpec((D_MODEL, tf), lambda i, j: (0, j)),
        pl.BlockSpec((FFN_CONV, tf), lambda i, j: (0, j)),
        pl.BlockSpec((tf, D_MODEL), lambda i, j: (j, 0)),
        pl.BlockSpec((1, D_MODEL), lambda i, j: (0, 0)),
    ]
    args = [h, nw_pre, wg, wu, cw, wd, nw_post, hist]
    scratch = [pltpu.VMEM((tm, D_MODEL), jnp.bfloat16)]
    if batch:
        in_specs.append(pl.BlockSpec((tm // SUBLANES, SUBLANES, tf), lambda i, j: (i, 0, j)))
        graw_spec = pl.BlockSpec((tm, tf), lambda i, j: (i, j))
        graw_shape = jax.ShapeDtypeStruct((rows, D_FF), jnp.float32)
        scratch.append(pltpu.VMEM((tm // SUBLANES, 2 * SUBLANES, tf), jnp.float32))
    else:
        in_specs.append(pl.BlockSpec((SUBLANES, tf), lambda i, j: (0, j)))
        graw_spec = pl.BlockSpec((SUBLANES, tf), lambda i, j: (i, j))
        graw_shape = jax.ShapeDtypeStruct((rows // tm * SUBLANES, D_FF), jnp.float32)
        scratch.append(pltpu.VMEM((1, SUBLANES + tm, tf), jnp.float32))
        scratch.append(pltpu.VMEM((nj, SUBLANES, tf), jnp.float32))
    return pl.pallas_call(
        functools.partial(_ffn_kernel, batch=batch, tm=tm, tf=tf),
        grid=(rows // tm, nj),
        in_specs=in_specs,
        out_specs=[pl.BlockSpec((tm, D_MODEL), lambda i, j: (i, 0)), graw_spec],
        out_shape=[jax.ShapeDtypeStruct((rows, D_MODEL), jnp.float32), graw_shape],
        scratch_shapes=scratch,
        compiler_params=pltpu.CompilerParams(
            dimension_semantics=("arbitrary", "arbitrary"), vmem_limit_bytes=VMEM_LIMIT),
        name="ffn_batch" if batch else "ffn_seq",
    )(*args)


def kernel(x_prompt, x_sample, cache_swa_meta_kv, cache_swa_window_kv, state_gdn_conv, state_gdn, state_ffn_conv, meta_tokens, rel_bias_table, w_in, gdn_conv_w, gdn_a_log, gdn_dt_bias, gdn_norm_w, swa_sinks, w_out, norm_mix_pre, norm_mix_post, norm_ffn_pre, norm_ffn_post, ffn_w_gate, ffn_w_up, ffn_conv_w, ffn_w_down):
    f32, bf16 = jnp.float32, jnp.bfloat16
    seq = x_prompt.shape[1]
    dec_b, dec_t = x_sample.shape[0], x_sample.shape[1]
    n_dec = dec_b * dec_t
    assert x_prompt.shape[0] == 1 and seq % CHUNK == 0 and dec_t == SUBLANES and n_dec % CHUNK == 0

    w_in_p, w_ba = _pack_w_in(jnp.transpose(w_in[0]), 2 * GDN_HEADS, tn=512)
    wo = w_out[0].astype(bf16)
    wg = ffn_w_gate[0].astype(bf16)
    wu = ffn_w_up[0].astype(bf16)
    wd = ffn_w_down[0].astype(bf16)
    lane_pad = lambda v: jnp.pad(v.reshape(1, GDN_HEADS), ((0, 0), (GDN_HEADS, LANES - 2 * GDN_HEADS)))
    alog_row = lane_pad(gdn_a_log[0])
    dtb_row = lane_pad(gdn_dt_bias[0])
    gnw = gdn_norm_w[0].reshape(1, HEAD_DIM)

    pad_rows = CHUNK - N_META
    n_small = n_dec + CHUNK
    x_big = x_prompt.reshape(seq, D_MODEL)
    x_small = jnp.concatenate(
        [x_sample.reshape(n_dec, D_MODEL), jnp.zeros((pad_rows, D_MODEL), f32), meta_tokens.astype(f32)], axis=0)
    nw = norm_mix_pre[0].reshape(1, D_MODEL)
    cw = gdn_conv_w[0]
    proj_small, ba_small = _inproj(x_small, nw, w_in_p, w_ba, tm=n_small, tn=512, row_chunk=128)
    proj_big, ba_big, qkv_tail = _inproj(
        x_big, nw, w_in_p, w_ba, (cw, proj_small[n_small - SUBLANES:, :GDN_QKV]),
        tm=1024, tn=512, row_chunk=128)

    small_chunks = proj_small.reshape(n_small // CHUNK, CHUNK, PROJ_COLS)
    small_groups = proj_small.reshape(n_small // SUBLANES, SUBLANES, PROJ_COLS)
    last_chunk = n_small // CHUNK - 1
    gdn_meta, s_meta = _gdn(
        small_chunks, ba_small.reshape(n_small // CHUNK, CHUNK, LANES), lambda s: (last_chunk, 0, 0),
        jnp.zeros((1, SUBLANES, GDN_QKV), f32), lambda s: (0, 0, 0),
        jnp.zeros((1, GDN_HEADS, HEAD_DIM, HEAD_DIM), f32), cw, alog_row, dtb_row, gnw,
        n_steps=1, nb=1, seq=CHUNK, group=CHUNK, carry=True, pad_rows=pad_rows, preconv=False)
    gdn_big, s_prompt = _gdn(
        proj_big.reshape(1, seq, PROJ_COLS), ba_big.reshape(1, seq, LANES), lambda s: (0, s, 0),
        jnp.zeros((1, SUBLANES, GDN_QKV), f32), lambda s: (0, 0, 0),
        s_meta, cw, alog_row, dtb_row, gnw,
        n_steps=seq // (GDN_SEQ_CHUNKS * CHUNK), nb=1, seq=GDN_SEQ_CHUNKS * CHUNK, group=CHUNK,
        carry=True, pad_rows=0, preconv=True)
    hist_gdn = jnp.pad(state_gdn_conv[0], ((0, 0), (SUBLANES - (GDN_CONV - 1), 0), (0, 0)))
    nb_gdn = CHUNK // dec_t
    gdn_small, s_sample = _gdn(
        small_groups, ba_small.reshape(n_small // SUBLANES, SUBLANES, LANES), lambda s: (s, 0, 0),
        hist_gdn, lambda s: (s, 0, 0),
        state_gdn, cw, alog_row, dtb_row, gnw,
        n_steps=dec_b // nb_gdn, nb=nb_gdn, seq=dec_t, group=dec_t, carry=False, pad_rows=0, preconv=False)

    qi = np.arange(WINDOW)[:, None]
    kj = np.arange(WINDOW)[None, :]
    mi = np.arange(N_META)[None, :]
    ti = np.arange(dec_t)[:, None]
    new_keys = kj - N_META
    id_arrays = [
        _bucket_ids(qi - kj, qi >= kj),
        _bucket_ids(qi - kj + WINDOW, kj > qi),
        _bucket_ids(qi + N_META - kj, kj < N_META),
        _bucket_ids(qi + N_META - kj + WINDOW, kj < N_META),
        _bucket_ids(ti + WINDOW - kj, kj > ti),
        _bucket_ids(np.where(new_keys < 0, PAST_LEN + ti - kj, ti - new_keys),
                    (new_keys < 0) | ((new_keys <= ti) & (new_keys < dec_t))),
        _bucket_ids(mi.T - mi, mi.T >= mi),
    ]
    bcur, bprev, bm0, bfar, bwin, bsmall, bmm = _bias_tables(rel_bias_table, id_arrays)
    sink_rows = lambda q: jnp.repeat(swa_sinks[0].reshape(SWA_KV_HEADS, SWA_GROUP), q, axis=1)[..., None]

    sq_blk = COL_SQ // SWA_WIDTH
    kv_blk = COL_KV // KV_WIDTH
    meta_blk = (n_small - N_META) // N_META
    full3 = lambda a: pl.BlockSpec(a.shape, lambda j: (0, 0, 0))
    sink_p = sink_rows(WINDOW)
    swa_rows = SWA_Q_BLOCKS * WINDOW
    swa_big = pl.pallas_call(
        _swa_prompt_kernel,
        grid=(seq // swa_rows,),
        in_specs=[
            pl.BlockSpec((swa_rows, SWA_WIDTH), lambda j: (j, sq_blk)),
            pl.BlockSpec((swa_rows, KV_WIDTH), lambda j: (j, kv_blk)),
            pl.BlockSpec((WINDOW, KV_WIDTH), lambda j: (jnp.maximum(j * SWA_Q_BLOCKS - 1, 0), kv_blk)),
            pl.BlockSpec((N_META, KV_WIDTH), lambda j: (meta_blk, kv_blk)),
            full3(bcur), full3(bprev), full3(bm0), full3(bfar), full3(sink_p),
        ],
        out_specs=pl.BlockSpec((swa_rows, SWA_WIDTH), lambda j: (j, 0)),
        out_shape=jax.ShapeDtypeStruct((seq, SWA_WIDTH), bf16),
        compiler_params=pltpu.CompilerParams(
            dimension_semantics=("arbitrary",), vmem_limit_bytes=VMEM_LIMIT),
        name="swa_prompt",
    )(proj_big, proj_big, proj_big, proj_small, bcur, bprev, bm0, bfar, sink_p)

    nb_swa = 8
    sink_s = sink_rows(dec_t)
    win = cache_swa_window_kv.reshape(dec_b, WINDOW * KV_SLOTS, HEAD_DIM)
    meta_kv = cache_swa_meta_kv.reshape(dec_b, N_META * KV_SLOTS, HEAD_DIM)
    swa_small, win_new = pl.pallas_call(
        functools.partial(_swa_sample_kernel, nb=nb_swa, seq=dec_t),
        grid=(dec_b // nb_swa,),
        in_specs=[
            pl.BlockSpec((nb_swa, dec_t, SWA_WIDTH), lambda j: (j, 0, sq_blk)),
            pl.BlockSpec((nb_swa, dec_t, KV_WIDTH), lambda j: (j, 0, kv_blk)),
            pl.BlockSpec((nb_swa, WINDOW * KV_SLOTS, HEAD_DIM), lambda j: (j, 0, 0)),
            pl.BlockSpec((nb_swa, N_META * KV_SLOTS, HEAD_DIM), lambda j: (j, 0, 0)),
            full3(bwin), full3(bsmall), full3(sink_s),
        ],
        out_specs=[pl.BlockSpec((nb_swa * dec_t, SWA_WIDTH), lambda j: (j, 0)),
                   pl.BlockSpec((nb_swa, WINDOW * KV_SLOTS, HEAD_DIM), lambda j: (j, 0, 0))],
        out_shape=[jax.ShapeDtypeStruct((n_dec, SWA_WIDTH), bf16),
                   jax.ShapeDtypeStruct((dec_b, WINDOW * KV_SLOTS, HEAD_DIM), f32)],
        compiler_params=pltpu.CompilerParams(
            dimension_semantics=("arbitrary",), vmem_limit_bytes=VMEM_LIMIT),
        name="swa_sample",
    )(small_groups, small_groups, win, meta_kv, bwin, bsmall, sink_s)

    sink_m = sink_rows(N_META)
    swa_meta = pl.pallas_call(
        _swa_meta_kernel,
        grid=(1,),
        in_specs=[
            pl.BlockSpec((N_META, SWA_WIDTH), lambda j: (meta_blk, sq_blk)),
            pl.BlockSpec((N_META, KV_WIDTH), lambda j: (meta_blk, kv_blk)),
            full3(bmm), full3(sink_m),
        ],
        out_specs=pl.BlockSpec((N_META, SWA_WIDTH), lambda j: (0, 0)),
        out_shape=jax.ShapeDtypeStruct((N_META, SWA_WIDTH), bf16),
        name="swa_meta",
    )(proj_small, proj_small, bmm, sink_m)

    nw_post = norm_mix_post[0].reshape(1, D_MODEL)
    nf_pre = norm_ffn_pre[0].reshape(1, D_MODEL)
    nf_post = norm_ffn_post[0].reshape(1, D_MODEL)
    fcw = ffn_conv_w[0]
    gdn_small_all = jnp.concatenate([gdn_small, gdn_meta], axis=0)
    swa_small_all = jnp.concatenate([swa_small, jnp.zeros((pad_rows, SWA_WIDTH), bf16), swa_meta], axis=0)
    h_small = _outproj(gdn_small_all, swa_small_all, x_small, wo, nw_post, tm=n_small // 2)
    hist_ffn = jnp.pad(state_ffn_conv[0], ((0, CHUNK // SUBLANES), (SUBLANES - (FFN_CONV - 1), 0), (0, 0)))
    y_small, g_small = _ffn(h_small, nf_pre, wg, wu, fcw, wd, nf_post, hist_ffn,
                            batch=True, tm=n_small // 2, tf=512)
    h_big = _outproj(gdn_big, swa_big, x_big, wo, nw_post, tm=512)
    y_big, g_tail = _ffn(h_big, nf_pre, wg, wu, fcw, wd, nf_post, g_small[n_small - SUBLANES:],
                         batch=False, tm=1024, tf=512)

    kv_shape = lambda n: (1, n, 2, SWA_KV_HEADS, HEAD_DIM)
    kv_small = proj_small[:, COL_KV:COL_KV + KV_WIDTH]
    y_prompt = y_big.reshape(1, seq, D_MODEL)
    y_sample = y_small[:n_dec].reshape(dec_b, dec_t, D_MODEL)
    p_meta_kv = kv_small[n_small - N_META:].reshape(kv_shape(N_META))[None]
    p_window_kv = proj_big[seq - WINDOW:, COL_KV:COL_KV + KV_WIDTH].reshape(kv_shape(WINDOW))[None]
    p_gdn_conv = qkv_tail[qkv_tail.shape[0] - (GDN_CONV - 1):].reshape(1, 1, GDN_CONV - 1, GDN_QKV)
    p_gdn_state = s_prompt[None]
    p_ffn_conv = g_tail[g_tail.shape[0] - (FFN_CONV - 1):].reshape(1, 1, FFN_CONV - 1, D_FF)
    s_window_kv = win_new.reshape(1, dec_b, WINDOW, 2, SWA_KV_HEADS, HEAD_DIM)
    s_gdn_conv = proj_small[:n_dec, :GDN_QKV].reshape(dec_b, dec_t, GDN_QKV)[:, dec_t - (GDN_CONV - 1):][None]
    s_gdn_state = s_sample
    s_ffn_conv = g_small[:n_dec].reshape(dec_b, dec_t, D_FF)[:, dec_t - (FFN_CONV - 1):][None]
    return (y_prompt, y_sample, p_meta_kv, p_window_kv, p_gdn_conv, p_gdn_state, p_ffn_conv,
            s_window_kv, s_gdn_conv, s_gdn_state, s_ffn_conv)
```
